```python
import math
import jax
import jax.numpy as jnp
from jax import lax
import numpy as np

D_MODEL = 1024
BATCH = 4
SEQ = 4096
DEPTH = 2

GRID_W = 64
CTX_LEN = 256
EPS = 1e-6
N_MOD = 6

MIX_WIDTH = D_MODEL
SGU_DIM = MIX_WIDTH // 4
SGU_HEADS = 4
SGU_HEAD_DIM = SGU_DIM // SGU_HEADS
SGU_CHUNK = 128

S5_DIM = MIX_WIDTH // 4
S5_GROUP = 16
S5_GROUPS = S5_DIM // S5_GROUP
S5_STATE = 64

GLA_DIM = MIX_WIDTH - SGU_DIM - S5_DIM
GLA_HEADS = 8
GLA_DV = GLA_DIM // GLA_HEADS
GLA_DK = GLA_DV // 2
GLA_KEY_DIM = GLA_HEADS * GLA_DK
GLA_RANK = 16
GLA_GATE_TEMP = 16.0
GLA_CHUNK = 64

PEER_HEADS = 8
PEER_NKEYS = 128
PEER_EXPERTS = PEER_NKEYS * PEER_NKEYS
PEER_QDIM = 256
PEER_HALF = PEER_QDIM // 2
PEER_TOPK = 16
PEER_TOKEN_BLOCK = 128

IN_SPLITS = (SGU_DIM, SGU_DIM, S5_DIM, GLA_KEY_DIM, GLA_KEY_DIM, GLA_DIM, GLA_DIM, GLA_RANK, GLA_RANK)
IN_WIDTH = sum(IN_SPLITS)

kernel_name = 'hybrid_sgu_s5_gla_peer_flow_block'


def rmsnorm(x, g=None):
    xf = x.astype(jnp.float32)
    y = xf * lax.rsqrt(jnp.mean(xf * xf, axis=-1, keepdims=True) + EPS)
    if g is not None:
        y = y * g.astype(jnp.float32)
    return y.astype(x.dtype)


def modulate(h, shift, scale):
    return h * (1 + scale) + shift


def split_cols(p):
    cuts = [int(i) for i in np.cumsum(IN_SPLITS)[:-1]]
    return jnp.split(p, cuts, axis=-1)


def flip_t(t):
    return jnp.flip(t, axis=1)


def to_col_major(t, rows):
    b, n = t.shape[0], t.shape[1]
    rest = t.shape[2:]
    return jnp.swapaxes(t.reshape(b, rows, GRID_W, *rest), 1, 2).reshape(b, n, *rest)


def from_col_major(t, rows):
    b, n = t.shape[0], t.shape[1]
    rest = t.shape[2:]
    return jnp.swapaxes(t.reshape(b, GRID_W, rows, *rest), 1, 2).reshape(b, n, *rest)


def sgu_mixer(u, v, w_s, b_s):
    bsz, length, _ = u.shape
    n = length // SGU_CHUNK
    u = jax.nn.gelu(u)
    v = rmsnorm(jax.nn.gelu(v).reshape(bsz, n, SGU_CHUNK, SGU_HEADS, SGU_HEAD_DIM))
    mixed = jnp.einsum('hts,bnshd->bnthd', w_s, v) + jnp.swapaxes(b_s, 0, 1)[:, :, None]
    return u * mixed.reshape(bsz, length, SGU_DIM)


def s5_discretise(lam_re, lam_im, b_re, b_im, log_step):
    f32 = jnp.float32
    lam_re = jnp.minimum(lam_re.astype(f32), -1e-4)
    lam_im = lam_im.astype(f32)
    dt = jnp.exp(log_step.astype(f32))[:, None]
    mag = jnp.exp(lam_re * dt)
    a_re = mag * jnp.cos(lam_im * dt)
    a_im = mag * jnp.sin(lam_im * dt)
    den = lam_re * lam_re + lam_im * lam_im
    f_re = ((a_re - 1.0) * lam_re + a_im * lam_im) / den
    f_im = (a_im * lam_re - (a_re - 1.0) * lam_im) / den
    b_re = b_re.astype(f32)
    b_im = b_im.astype(f32)
    bb_re = f_re[..., None] * b_re - f_im[..., None] * b_im
    bb_im = f_re[..., None] * b_im + f_im[..., None] * b_re
    return a_re, a_im, bb_re, bb_im


def complex_affine_combine(e1, e2):
    a1r, a1i, b1r, b1i = e1
    a2r, a2i, b2r, b2i = e2
    return (a2r * a1r - a2i * a1i, a2r * a1i + a2i * a1r,
            a2r * b1r - a2i * b1i + b2r, a2r * b1i + a2i * b1r + b2i)


def s5_scan(u, disc, h0):
    a_re, a_im, bb_re, bb_im = disc
    bu_re = jnp.einsum('blgh,gph->blgp', u, bb_re)
    bu_im = jnp.einsum('blgh,gph->blgp', u, bb_im)
    ar = jnp.broadcast_to(a_re, bu_re.shape)
    ai = jnp.broadcast_to(a_im, bu_re.shape)
    cr, ci, hr, hi = lax.associative_scan(complex_affine_combine, (ar, ai, bu_re, bu_im), axis=1)
    if h0 is not None:
        h0r = h0[0][:, None]
        h0i = h0[1][:, None]
        hr = hr + cr * h0r - ci * h0i
        hi = hi + cr * h0i + ci * h0r
    return hr, hi


def s5_readout(h, c_re, c_im):
    return jnp.einsum('blgp,ghp->blgh', h[0], c_re) - jnp.einsum('blgp,ghp->blgh', h[1], c_im)


def s5_glu(y, w_glu):
    y = y.reshape(y.shape[0], y.shape[1], S5_DIM)
    z = jax.nn.gelu(y) @ w_glu.astype(jnp.float32)
    za, zb = jnp.split(z, 2, axis=-1)
    return za * jax.nn.sigmoid(zb)


def s5_mixer(xc, xl, lam_re, lam_im, b_re, b_im, c_re, c_im, log_step, d_skip, w_glu, ctx_out):
    f32 = jnp.float32

    def groups(t):
        return t.astype(f32).reshape(t.shape[0], t.shape[1], S5_GROUPS, S5_GROUP)

    uc, ul = groups(xc), groups(xl)
    ys_c, ys_l = [], []
    for d in range(2):
        disc = s5_discretise(lam_re[d], lam_im[d], b_re[d], b_im[d], log_step[d])
        cre, cim = c_re[d].astype(f32), c_im[d].astype(f32)
        uc_d = uc if d == 0 else flip_t(uc)
        ul_d = ul if d == 0 else flip_t(ul)
        hc = s5_scan(uc_d, disc, None)
        hl = s5_scan(ul_d, disc, (hc[0][:, -1], hc[1][:, -1]))
        y_l = s5_readout(hl, cre, cim)
        ys_l.append(y_l if d == 0 else flip_t(y_l))
        if ctx_out:
            y_c = s5_readout(hc, cre, cim)
            ys_c.append(y_c if d == 0 else flip_t(y_c))
    dd = d_skip.astype(f32).reshape(S5_GROUPS, S5_GROUP)
    out_l = s5_glu(ys_l[0] + ys_l[1] + dd * ul, w_glu)
    out_c = s5_glu(ys_c[0] + ys_c[1] + dd * uc, w_glu) if ctx_out else None
    return out_l, out_c


def gla_scan(q, k, v, log_a, s0, need_out):
    bsz, length, nh, dk = q.shape
    dv = v.shape[-1]
    n = length // GLA_CHUNK

    def chunks(t):
        return t.reshape(bsz, n, GLA_CHUNK, nh, t.shape[-1])

    q, k, v, log_a = chunks(q), chunks(k), chunks(v), chunks(log_a)
    b = jnp.cumsum(log_a, axis=2)
    b_last = b[:, :, -1]
    chunk_kv = jnp.einsum('bnshk,bnshv->bnhkv', k * jnp.exp(b_last[:, :, None] - b), v)
    decay = jnp.exp(b_last)
    if s0 is None:
        s0 = jnp.zeros((bsz, nh, dk, dv), jnp.float32)

    def step(s, inp):
        dec, kv = inp
        return dec[..., None] * s + kv, s

    s_final, s_in = lax.scan(step, s0, (jnp.swapaxes(decay, 0, 1), jnp.swapaxes(chunk_kv, 0, 1)))
    if not need_out:
        return None, s_final
    s_in = jnp.swapaxes(s_in, 0, 1)
    o_inter = jnp.einsum('bnthk,bnhkv->bnthv', q * jnp.exp(b), s_in)
    b_ref = b[:, :, GLA_CHUNK // 2][:, :, None]
    scores = jnp.einsum('bnthk,bnshk->bnhts', q * jnp.exp(b - b_ref), k * jnp.exp(b_ref - b))
    mask = jnp.tril(jnp.ones((GLA_CHUNK, GLA_CHUNK), dtype=bool))
    scores = jnp.where(mask, scores, 0.0)
    o_intra = jnp.einsum('bnhts,bnshv->bnthv', scores, v)
    return (o_inter + o_intra).reshape(bsz, length, nh, dv), s_final


def gla_gates(z, w_up, b_up):
    f32 = jnp.float32
    za = z.astype(f32) @ w_up.astype(f32) + b_up.astype(f32)
    return (jax.nn.log_sigmoid(za) / GLA_GATE_TEMP).reshape(z.shape[0], z.shape[1], GLA_HEADS, GLA_DK)


def gla_out(o, g, norm_g):
    o = rmsnorm(o).reshape(o.shape[0], o.shape[1], GLA_DIM) * norm_g.astype(jnp.float32)
    return o * jax.nn.silu(g.astype(jnp.float32))


def gla_mixer(cols_c, cols_l, w_gate, b_gate, norm_g, rows, ctx_out):
    f32 = jnp.float32

    def prep(cols):
        q, k, v, g, zf, zb = cols
        q = q.astype(f32).reshape(q.shape[0], q.shape[1], GLA_HEADS, GLA_DK) * (GLA_DK ** -0.5)
        k = k.astype(f32).reshape(k.shape[0], k.shape[1], GLA_HEADS, GLA_DK)
        v = v.astype(f32).reshape(v.shape[0], v.shape[1], GLA_HEADS, GLA_DV)
        af = gla_gates(zf, w_gate[0], b_gate[0])
        ab = gla_gates(zb, w_gate[1], b_gate[1])
        return q, k, v, af, ab, g

    qc, kc, vc, afc, abc, gc = prep(cols_c)
    ql, kl, vl, afl, abl, gl = prep(cols_l)
    ql, kl, vl, afl, abl = [to_col_major(t, rows) for t in (ql, kl, vl, afl, abl)]
    oc_f, sc_f = gla_scan(qc, kc, vc, afc, None, ctx_out)
    oc_b, sc_b = gla_scan(flip_t(qc), flip_t(kc), flip_t(vc), flip_t(abc), None, ctx_out)
    ol_f, _ = gla_scan(ql, kl, vl, afl, sc_f, True)
    ol_b, _ = gla_scan(flip_t(ql), flip_t(kl), flip_t(vl), flip_t(abl), sc_b, True)
    out_l = gla_out(from_col_major(ol_f + flip_t(ol_b), rows), gl, norm_g)
    out_c = gla_out(oc_f + flip_t(oc_b), gc, norm_g) if ctx_out else None
    return out_l, out_c


def peer_ffn(h, w_query, sub_keys, expert_u, expert_v):
    bsz, length, dm = h.shape
    blocks = h.reshape(-1, PEER_TOKEN_BLOCK, dm)
    keys = sub_keys.astype(jnp.float32)

    def block(xb):
        q = (xb @ w_query).astype(jnp.float32).reshape(PEER_TOKEN_BLOCK, PEER_HEADS, 2, PEER_HALF)
        s = jnp.einsum('thpd,phkd->thpk', q, keys)
        s1, i1 = lax.top_k(s[:, :, 0], PEER_TOPK)
        s2, i2 = lax.top_k(s[:, :, 1], PEER_TOPK)
        cand_s = (s1[..., :, None] + s2[..., None, :]).reshape(PEER_TOKEN_BLOCK, PEER_HEADS, PEER_TOPK * PEER_TOPK)
        cand_i = (i1[..., :, None] * PEER_NKEYS + i2[..., None, :]).reshape(PEER_TOKEN_BLOCK, PEER_HEADS, PEER_TOPK * PEER_TOPK)
        top_s, pos = lax.top_k(cand_s, PEER_TOPK)
        idx = jnp.take_along_axis(cand_i, pos, axis=-1)
        gate = jax.nn.softmax(top_s, axis=-1)
        act = jax.nn.gelu(jnp.einsum('td,thkd->thk', xb, expert_u[idx]).astype(jnp.float32))
        return jnp.einsum('thk,thkd->td', (gate * act).astype(xb.dtype), expert_v[idx])

    return lax.map(block, blocks).reshape(bsz, length, dm)


def setup_inputs(seed: int = 0) -> dict:
    key = jax.random.key(seed)
    ks = iter(jax.random.split(key, 40))
    f32 = jnp.float32

    def nrm(shape, scale):
        return jax.random.normal(next(ks), shape, f32) * scale

    L, D = DEPTH, D_MODEL
    n_idx = jnp.arange(S5_STATE, dtype=f32)
    return {
        'x': nrm((BATCH, SEQ, D), 1.0),
        'c': nrm((BATCH, D), 1.0),
        'ctx': nrm((BATCH, CTX_LEN, D), 1.0),
        'c_ctx': nrm((D,), 1.0),
        'w_mod': nrm((L, D, N_MOD * D), 0.5 * D ** -0.5),
        'b_mod': nrm((L, N_MOD * D), 0.02),
        'norm1_g': 1.0 + nrm((L, D), 0.02),
        'norm2_g': 1.0 + nrm((L, D), 0.02),
        'w_in': nrm((L, D, IN_WIDTH), D ** -0.5),
        'w_out': nrm((L, MIX_WIDTH, D), MIX_WIDTH ** -0.5),
        'sgu_w': nrm((L, SGU_HEADS, SGU_CHUNK, SGU_CHUNK), SGU_CHUNK ** -0.5),
        'sgu_b': 1.0 + nrm((L, SGU_HEADS, SGU_CHUNK), 0.1),
        's5_lambda_re': -0.5 + nrm((L, 2, S5_GROUPS, S5_STATE), 0.01),
        's5_lambda_im': math.pi * n_idx + nrm((L, 2, S5_GROUPS, S5_STATE), 0.01),
        's5_b_re': nrm((L, 2, S5_GROUPS, S5_STATE, S5_GROUP), (2 * S5_GROUP) ** -0.5),
        's5_b_im': nrm((L, 2, S5_GROUPS, S5_STATE, S5_GROUP), (2 * S5_GROUP) ** -0.5),
        's5_c_re': nrm((L, 2, S5_GROUPS, S5_GROUP, S5_STATE), (2 * S5_STATE) ** -0.5),
        's5_c_im': nrm((L, 2, S5_GROUPS, S5_GROUP, S5_STATE), (2 * S5_STATE) ** -0.5),
        's5_log_step': jax.random.uniform(next(ks), (L, 2, S5_GROUPS), f32, math.log(1e-3), math.log(1e-1)),
        's5_d': nrm((L, S5_DIM), 1.0),
        's5_w_glu': nrm((L, S5_DIM, 2 * S5_DIM), S5_DIM ** -0.5),
        'gla_w_gate': nrm((L, 2, GLA_RANK, GLA_KEY_DIM), GLA_RANK ** -0.5),
        'gla_b_gate': nrm((L, 2, GLA_KEY_DIM), 0.1),
        'gla_norm_g': 1.0 + nrm((L, GLA_DIM), 0.02),
        'peer_w_query': nrm((L, D, PEER_HEADS * PEER_QDIM), D ** -0.5),
        'peer_sub_keys': nrm((L, 2, PEER_HEADS, PEER_NKEYS, PEER_HALF), PEER_HALF ** -0.5),
        'peer_expert_u': nrm((L, PEER_EXPERTS, D), D ** -0.5),
        'peer_expert_v': nrm((L, PEER_EXPERTS, D), PEER_HEADS ** -0.5),
        'final_norm_g': 1.0 + nrm((D,), 0.02),
    }


def reference(x, c, ctx, c_ctx, w_mod, b_mod, norm1_g, norm2_g, w_in, w_out, sgu_w, sgu_b,
              s5_lambda_re, s5_lambda_im, s5_b_re, s5_b_im, s5_c_re, s5_c_im, s5_log_step, s5_d, s5_w_glu,
              gla_w_gate, gla_b_gate, gla_norm_g, peer_w_query, peer_sub_keys, peer_expert_u, peer_expert_v,
              final_norm_g):
    rows = x.shape[1] // GRID_W
    xl, xc = x, ctx
    for l in range(DEPTH):
        ctx_out = l < DEPTH - 1
        mod_l = [m[:, None, :] for m in jnp.split(jax.nn.silu(c) @ w_mod[l] + b_mod[l], N_MOD, axis=-1)]
        mod_c = jnp.split(jax.nn.silu(c_ctx) @ w_mod[l] + b_mod[l], N_MOD, axis=-1)

        cols_l = split_cols(modulate(rmsnorm(xl, norm1_g[l]), mod_l[0], mod_l[1]) @ w_in[l])
        cols_c = split_cols(modulate(rmsnorm(xc, norm1_g[l]), mod_c[0], mod_c[1]) @ w_in[l])
        s5_l, s5_c = s5_mixer(cols_c[2], cols_l[2], s5_lambda_re[l], s5_lambda_im[l], s5_b_re[l], s5_b_im[l],
                              s5_c_re[l], s5_c_im[l], s5_log_step[l], s5_d[l], s5_w_glu[l], ctx_out)
        gla_l, gla_c = gla_mixer(cols_c[3:], cols_l[3:], gla_w_gate[l], gla_b_gate[l], gla_norm_g[l], rows, ctx_out)
        sgu_l = sgu_mixer(cols_l[0], cols_l[1], sgu_w[l], sgu_b[l])
        y_l = jnp.concatenate([sgu_l.astype(xl.dtype), s5_l.astype(xl.dtype), gla_l.astype(xl.dtype)], axis=-1) @ w_out[l]
        xl = xl + mod_l[2] * y_l

        h_l = modulate(rmsnorm(xl, norm2_g[l]), mod_l[3], mod_l[4])
        xl = xl + mod_l[5] * peer_ffn(h_l, peer_w_query[l], peer_sub_keys[l], peer_expert_u[l], peer_expert_v[l])

        if ctx_out:
            sgu_c = sgu_mixer(cols_c[0], cols_c[1], sgu_w[l], sgu_b[l])
            y_c = jnp.concatenate([sgu_c.astype(xc.dtype), s5_c.astype(xc.dtype), gla_c.astype(xc.dtype)], axis=-1) @ w_out[l]
            xc = xc + mod_c[2] * y_c
            h_c = modulate(rmsnorm(xc, norm2_g[l]), mod_c[3], mod_c[4])
            xc = xc + mod_c[5] * peer_ffn(h_c, peer_w_query[l], peer_sub_keys[l], peer_expert_u[l], peer_expert_v[l])
    return rmsnorm(xl, final_norm_g)
```

```python
import functools
import math

import numpy as np
import jax
import jax.numpy as jnp
from jax import lax
from jax.experimental import pallas as pl
from jax.experimental.pallas import tpu as pltpu

F32 = jnp.float32
BF16 = jnp.bfloat16

EPS = 1e-6
N_MOD = 6
GRID_W = 64

SGU_DIM = 256
SGU_HEADS = 4
SGU_HEAD_DIM = 64
SGU_CHUNK = 128

S5_DIM = 256
S5_GROUP = 16
S5_GROUPS = 16
S5_STATE = 64
S5_LANES = S5_GROUPS * S5_STATE

GLA_DIM = 512
GLA_HEADS = 8
GLA_DV = 64
GLA_DK = 32
GLA_KEY_DIM = 256
GLA_RANK = 16
GLA_GATE_TEMP = 16.0
GLA_CHUNK = 64

PEER_HEADS = 8
PEER_NKEYS = 128
PEER_HALF = 128
PEER_TOPK = 16

IN_WIDTH = 2336
IN_PAD = 2432
LANE = 128

VMEM_LIMIT = 56 * 1024 * 1024

NEG_INF = float("-inf")
POS_INF = float("inf")


def _cparams(sem):
    return pltpu.CompilerParams(dimension_semantics=sem, vmem_limit_bytes=VMEM_LIMIT)


def _gelu(x):
    c = math.sqrt(2.0 / math.pi)
    return 0.5 * x * (1.0 + jnp.tanh(c * (x + 0.044715 * (x * x * x))))


def _sigmoid(x):
    return 1.0 / (1.0 + jnp.exp(-x))


def _dot(a, b):
    return jnp.dot(a, b, preferred_element_type=F32)


def _dot_nt(a, b):
    return lax.dot_general(a, b, (((1,), (1,)), ((), ())), preferred_element_type=F32)


def _dot_tn(a, b):
    return lax.dot_general(a, b, (((0,), (0,)), ((), ())), preferred_element_type=F32)


def _split3(x):
    hi = x.astype(BF16)
    r = x - hi.astype(F32)
    mid = r.astype(BF16)
    lo = (r - mid.astype(F32)).astype(BF16)
    return hi, mid, lo


def _dot_x_exact(x, m):
    hi, mid, lo = _split3(x)
    return _dot(hi, m) + _dot(mid, m) + _dot(lo, m)


def _dot_m_exact(m, x):
    hi, mid, lo = _split3(x)
    return _dot(m, hi) + _dot(m, mid) + _dot(m, lo)


def _full_spec(shape):
    nd = len(shape)
    return pl.BlockSpec(shape, lambda *_: (0,) * nd)


MOD_TILE = 512


def _mod_kernel(c_ref, w_ref, b_ref, o_ref):
    c = c_ref[...]
    a = c * _sigmoid(c)
    o_ref[0] = jnp.dot(a, w_ref[0], preferred_element_type=F32,
                       precision=lax.Precision.HIGHEST) + b_ref[0]


def _mod_call(cc, w_mod, b_mod):
    depth, d, nd = w_mod.shape
    rows = cc.shape[0]
    return pl.pallas_call(
        _mod_kernel,
        grid=(depth, nd // MOD_TILE),
        in_specs=[
            pl.BlockSpec((rows, d), lambda l, j: (0, 0)),
            pl.BlockSpec((1, d, MOD_TILE), lambda l, j: (l, 0, j)),
            pl.BlockSpec((1, 1, MOD_TILE), lambda l, j: (l, 0, j)),
        ],
        out_specs=pl.BlockSpec((1, rows, MOD_TILE), lambda l, j: (l, 0, j)),
        out_shape=jax.ShapeDtypeStruct((depth, rows, nd), F32),
        compiler_params=_cparams(("parallel", "parallel")),
    )(cc, w_mod, b_mod.reshape(depth, 1, nd))


def _stage_a_kernel(x_ref, mod_ref, g1_ref, win_ref, sguw_ref, sgub_ref, ones_ref, gw_ref, gb_ref,
                    sgu_ref, s5x_ref, qk_ref, v_ref, g_ref, la_ref, *, tb):
    x = x_ref[0]
    ms = jnp.mean(x * x, axis=-1, keepdims=True)
    xn = x * lax.rsqrt(ms + EPS) * g1_ref[...]
    h = xn * (1.0 + mod_ref[0, 1:2, :]) + mod_ref[0, 0:1, :]
    cols = _dot(h.astype(BF16), win_ref[...])

    u = _gelu(cols[:, 0:SGU_DIM])
    v = _gelu(cols[:, SGU_DIM:2 * SGU_DIM])
    msq = _dot_x_exact(v * v, ones_ref[...]) * (1.0 / SGU_HEAD_DIM)
    vn = (v * lax.rsqrt(msq + EPS)).astype(BF16)
    head_of_lane = lax.broadcasted_iota(jnp.int32, (SGU_CHUNK, SGU_DIM), 1) // SGU_HEAD_DIM
    for ci in range(tb // SGU_CHUNK):
        rows = slice(ci * SGU_CHUNK, (ci + 1) * SGU_CHUNK)
        vc = vn[rows]
        mixed = sgub_ref[...]
        for hh in range(SGU_HEADS):
            mixed = mixed + jnp.where(head_of_lane == hh, _dot(sguw_ref[hh], vc), 0.0)
        sgu_ref[0, rows, :] = u[rows] * mixed

    s5x_ref[0] = cols[:, 512:768]
    qk_ref[0, :, 0:GLA_KEY_DIM] = cols[:, 768:1024] * (GLA_DK ** -0.5)
    qk_ref[0, :, GLA_KEY_DIM:] = cols[:, 1024:1280]
    v_ref[0] = cols[:, 1280:1792]
    g_ref[0] = cols[:, 1792:2304]

    z = cols[:, 2304:2432].astype(BF16)
    za = _dot(z, gw_ref[...]) + gb_ref[...]
    log_sig = jnp.minimum(za, 0.0) - jnp.log1p(jnp.exp(-jnp.abs(za)))
    la_ref[0] = log_sig * (1.0 / GLA_GATE_TEMP)


def _stage_a(xs, mod, g1, win, sguw, sgub, ones_sgu, gw, gb, tb):
    b, l, d = xs.shape
    kern = functools.partial(_stage_a_kernel, tb=tb)
    tok = lambda w: pl.BlockSpec((1, tb, w), lambda bi, i: (bi, i, 0))
    widths = (SGU_DIM, S5_DIM, 2 * GLA_KEY_DIM, GLA_DIM, GLA_DIM, 2 * GLA_KEY_DIM)
    return pl.pallas_call(
        kern,
        grid=(b, l // tb),
        in_specs=[
            tok(d),
            pl.BlockSpec((1, 8, d), lambda bi, i: (bi, 0, 0)),
            _full_spec(g1.shape), _full_spec(win.shape), _full_spec(sguw.shape), _full_spec(sgub.shape),
            _full_spec(ones_sgu.shape), _full_spec(gw.shape), _full_spec(gb.shape),
        ],
        out_specs=[tok(w) for w in widths],
        out_shape=[jax.ShapeDtypeStruct((b, l, w), F32) for w in widths],
        compiler_params=_cparams(("parallel", "parallel")),
    )(xs, mod, g1, win, sguw, sgub, ones_sgu, gw, gb)


S5_TC = 128


def _s5_kernel(u_ref, b2_ref, ar_ref, ai_ref, c2_ref, y_ref, h_ref, buf_ref, *, nseq):
    tc = S5_TC

    @pl.when(pl.program_id(0) == 0)
    def _():
        h_ref[...] = jnp.zeros_like(h_ref)

    u = u_ref[...].reshape(tc * 2 * nseq, 2 * S5_DIM).astype(BF16)
    buf_ref[...] = _dot(u, b2_ref[...]).reshape(tc, 2 * nseq, 2 * S5_LANES)
    ar = ar_ref[...]
    ai = ai_ref[...]

    def step(t, carry):
        hr, hi = carry
        bur = buf_ref[t, :, 0:S5_LANES]
        bui = buf_ref[t, :, S5_LANES:]
        nhr = ar * hr - ai * hi + bur
        nhi = ar * hi + ai * hr + bui
        buf_ref[t, :, 0:S5_LANES] = nhr
        buf_ref[t, :, S5_LANES:] = nhi
        return nhr, nhi

    hr, hi = lax.fori_loop(0, tc, step, (h_ref[:, 0:S5_LANES], h_ref[:, S5_LANES:]))
    h_ref[:, 0:S5_LANES] = hr
    h_ref[:, S5_LANES:] = hi

    hs = buf_ref[...].reshape(tc * 2 * nseq, 2 * S5_LANES).astype(BF16)
    y2 = _dot(hs, c2_ref[...])
    row = lax.broadcasted_iota(jnp.int32, (tc * 2 * nseq, S5_DIM), 0) % (2 * nseq)
    y = jnp.where(row < nseq, y2[:, 0:S5_DIM], y2[:, S5_DIM:])
    y_ref[...] = y.reshape(tc, 2 * nseq, S5_DIM)


def _s5_call(useq, b2, ar, ai, c2):
    t, rows, _ = useq.shape
    nseq = rows // 2
    kern = functools.partial(_s5_kernel, nseq=nseq)
    return pl.pallas_call(
        kern,
        grid=(t // S5_TC,),
        in_specs=[
            pl.BlockSpec((S5_TC, rows, 2 * S5_DIM), lambda i: (i, 0, 0)),
            _full_spec(b2.shape), _full_spec(ar.shape), _full_spec(ai.shape), _full_spec(c2.shape),
        ],
        out_specs=pl.BlockSpec((S5_TC, rows, S5_DIM), lambda i: (i, 0, 0)),
        out_shape=jax.ShapeDtypeStruct((t, rows, S5_DIM), F32),
        scratch_shapes=[
            pltpu.VMEM((rows, 2 * S5_LANES), F32),
            pltpu.VMEM((S5_TC, rows, 2 * S5_LANES), F32),
        ],
        compiler_params=_cparams(("arbitrary",)),
    )(useq, b2, ar, ai, c2)


def _gla_stream(q, k, v, la, s_t, tri, last_row, ref_row, tri_mask, hm_k, hm_v, hm_s):
    bcum = _dot_m_exact(tri, la)
    blast = bcum[last_row:last_row + 1]
    bref = bcum[ref_row:ref_row + 1]
    qe = q * jnp.exp(bcum)
    qd = q * jnp.exp(bcum - bref)
    kd = k * jnp.exp(bref - bcum)
    kdec = k * jnp.exp(blast - bcum)
    kst = jnp.where(hm_k, jnp.concatenate([kd] * GLA_HEADS, axis=0), 0.0).astype(BF16)
    sc = jnp.where(tri_mask, _dot_nt(qd.astype(BF16), kst), 0.0)
    vbd = jnp.where(hm_v, jnp.concatenate([v] * GLA_HEADS, axis=0), 0.0).astype(BF16)
    o = _dot(sc.astype(BF16), vbd) + _dot_nt(qe.astype(BF16), s_t.astype(BF16))
    kv_t = jnp.where(hm_s, _dot_tn(v.astype(BF16), kdec.astype(BF16)), 0.0)
    s_new = s_t * jnp.exp(blast) + kv_t
    return o, s_new


def _gla_kernel(qkf_ref, vf_ref, laf_ref, qkb_ref, vb_ref, lab_ref, s0_ref, trif_ref, trib_ref,
                of_ref, ob_ref, sout_ref, s_ref, *, nb):
    c = pl.program_id(0)

    @pl.when(c == 0)
    def _():
        s_ref[...] = s0_ref[...]

    ch = GLA_CHUNK
    r_k = lax.broadcasted_iota(jnp.int32, (GLA_HEADS * ch, GLA_KEY_DIM), 0) // ch
    c_k = lax.broadcasted_iota(jnp.int32, (GLA_HEADS * ch, GLA_KEY_DIM), 1) // GLA_DK
    hm_k = r_k == c_k
    r_v = lax.broadcasted_iota(jnp.int32, (GLA_HEADS * ch, GLA_DIM), 0) // ch
    c_v = lax.broadcasted_iota(jnp.int32, (GLA_HEADS * ch, GLA_DIM), 1) // GLA_DV
    hm_v = r_v == c_v
    r_s = lax.broadcasted_iota(jnp.int32, (GLA_DIM, GLA_KEY_DIM), 0) // GLA_DV
    c_s = lax.broadcasted_iota(jnp.int32, (GLA_DIM, GLA_KEY_DIM), 1) // GLA_DK
    hm_s = r_s == c_s
    t_i = lax.broadcasted_iota(jnp.int32, (ch, GLA_HEADS * ch), 0)
    s_i = lax.broadcasted_iota(jnp.int32, (ch, GLA_HEADS * ch), 1) % ch
    mask_f = t_i >= s_i
    mask_b = t_i <= s_i
    trif = trif_ref[...]
    trib = trib_ref[...]

    def body(b, carry):
        qk = qkf_ref[b]
        o, s_new = _gla_stream(qk[:, 0:GLA_KEY_DIM], qk[:, GLA_KEY_DIM:], vf_ref[b], laf_ref[b], s_ref[b, 0],
                               trif, ch - 1, ch // 2, mask_f, hm_k, hm_v, hm_s)
        of_ref[b] = o
        s_ref[b, 0] = s_new
        qk = qkb_ref[b]
        o, s_new = _gla_stream(qk[:, 0:GLA_KEY_DIM], qk[:, GLA_KEY_DIM:], vb_ref[b], lab_ref[b], s_ref[b, 1],
                               trib, 0, ch - 1 - ch // 2, mask_b, hm_k, hm_v, hm_s)
        ob_ref[b] = o
        s_ref[b, 1] = s_new
        return carry

    lax.fori_loop(0, nb, body, 0)

    @pl.when(c == pl.num_programs(0) - 1)
    def _():
        sout_ref[...] = s_ref[...]


def _gla_call(qk, v, la, s0, trif, trib, grid_order):
    b, l, _ = qk.shape
    ch = GLA_CHUNK
    n = l // ch
    if grid_order:
        assert l == ch * GRID_W
        view = lambda a: a.reshape(b, ch, GRID_W * a.shape[-1])
        spec = lambda w, per, off, rev: pl.BlockSpec(
            (b, ch, w), (lambda c: (0, 0, (n - 1 - c) * per + off)) if rev else (lambda c: (0, 0, c * per + off)))
    else:
        view = lambda a: a
        spec = lambda w, per, off, rev: pl.BlockSpec(
            (b, ch, w), (lambda c: (0, n - 1 - c, off)) if rev else (lambda c: (0, c, off)))
    kern = functools.partial(_gla_kernel, nb=b)
    o_shape = jax.ShapeDtypeStruct(view(v).shape, F32)
    of, ob, s_out = pl.pallas_call(
        kern,
        grid=(n,),
        in_specs=[
            spec(2 * GLA_KEY_DIM, 1, 0, False), spec(GLA_DIM, 1, 0, False), spec(GLA_KEY_DIM, 2, 0, False),
            spec(2 * GLA_KEY_DIM, 1, 0, True), spec(GLA_DIM, 1, 0, True), spec(GLA_KEY_DIM, 2, 1, True),
            _full_spec(s0.shape), _full_spec(trif.shape), _full_spec(trib.shape),
        ],
        out_specs=[spec(GLA_DIM, 1, 0, False), spec(GLA_DIM, 1, 0, True), _full_spec(s0.shape)],
        out_shape=[o_shape, o_shape, jax.ShapeDtypeStruct(s0.shape, F32)],
        scratch_shapes=[pltpu.VMEM(s0.shape, F32)],
        compiler_params=_cparams(("arbitrary",)),
    )(view(qk), view(v), view(la), view(qk), view(v), view(la), s0, trif, trib)
    return of.reshape(b, l, GLA_DIM), ob.reshape(b, l, GLA_DIM), s_out


def _stage_e_kernel(x_ref, mod_ref, sgu_ref, yf_ref, yb_ref, s5x_ref, of_ref, ob_ref, g_ref,
                    dskip_ref, wglu_ref, normg_ref, ones_ref, wout_ref, o_ref):
    ys = yf_ref[0] + yb_ref[0] + dskip_ref[...] * s5x_ref[0]
    z = _dot(_gelu(ys).astype(BF16), wglu_ref[...])
    s5o = z[:, 0:S5_DIM] * _sigmoid(z[:, S5_DIM:])
    o = of_ref[0] + ob_ref[0]
    ms = _dot_x_exact(o * o, ones_ref[...]) * (1.0 / GLA_DV)
    g = g_ref[0]
    gl = o * lax.rsqrt(ms + EPS) * normg_ref[...] * (g * _sigmoid(g))
    y = (_dot(sgu_ref[0].astype(BF16), wout_ref[0:SGU_DIM, :])
         + _dot(s5o.astype(BF16), wout_ref[SGU_DIM:SGU_DIM + S5_DIM, :])
         + _dot(gl.astype(BF16), wout_ref[SGU_DIM + S5_DIM:, :]))
    o_ref[0] = x_ref[0] + mod_ref[0, 2:3, :] * y


def _stage_e(xs, mod, sgu, yf, yb, s5x, of, ob, g, dskip, wglu, normg, ones_gla, wout, tb):
    b, l, d = xs.shape
    tok = lambda w: pl.BlockSpec((1, tb, w), lambda bi, i: (bi, i, 0))
    return pl.pallas_call(
        _stage_e_kernel,
        grid=(b, l // tb),
        in_specs=[
            tok(d), pl.BlockSpec((1, 8, d), lambda bi, i: (bi, 0, 0)),
            tok(SGU_DIM), tok(S5_DIM), tok(S5_DIM), tok(S5_DIM), tok(GLA_DIM), tok(GLA_DIM), tok(GLA_DIM),
            _full_spec(dskip.shape), _full_spec(wglu.shape), _full_spec(normg.shape),
            _full_spec(ones_gla.shape), _full_spec(wout.shape),
        ],
        out_specs=tok(d),
        out_shape=jax.ShapeDtypeStruct((b, l, d), F32),
        compiler_params=_cparams(("parallel", "parallel")),
    )(xs, mod, sgu, yf, yb, s5x, of, ob, g, dskip, wglu, normg, ones_gla, wout)


PEER_TBF = 256


def _extract_top(s, vals_ref, tcol):
    key_id = lax.broadcasted_iota(jnp.int32, s.shape, 0).astype(F32)
    work = s
    for it in range(PEER_TOPK):
        m = jnp.max(work, axis=0, keepdims=True)
        first = jnp.min(jnp.where(work == m, key_id, float(PEER_NKEYS)), axis=0, keepdims=True)
        vals_ref[it:it + 1, tcol] = m
        work = jnp.where(key_id == first, NEG_INF, work)
    return work < s


def _stage_f_kernel(x_ref, mod_ref, g2_ref, wqt_ref, keys_ref,
                    ht_ref, e1_ref, t1_ref, e2_ref, s2_ref, qt_ref, v1_ref, v2_ref, *, tb):
    x = x_ref[0]
    ms = jnp.mean(x * x, axis=-1, keepdims=True)
    xn = x * lax.rsqrt(ms + EPS) * g2_ref[...]
    h = xn * (1.0 + mod_ref[0, 4:5, :]) + mod_ref[0, 3:4, :]
    ht = h.T.astype(BF16)
    ht_ref[...] = ht
    qt_ref[...] = _dot(wqt_ref[...], ht)

    def head_body(hh, carry):
        for tc in range(tb // LANE):
            tcol = slice(tc * LANE, (tc + 1) * LANE)
            r1 = pl.multiple_of(hh * (2 * PEER_HALF), 2 * PEER_HALF)
            q1 = qt_ref[pl.ds(r1, PEER_HALF), tcol].astype(BF16)
            q2 = qt_ref[pl.ds(r1 + PEER_HALF, PEER_HALF), tcol].astype(BF16)
            s1 = _dot(keys_ref[hh], q1)
            s2 = _dot(keys_ref[PEER_HEADS + hh], q2)
            sel1 = _extract_top(s1, v1_ref, tcol)
            sel2 = _extract_top(s2, v2_ref, tcol)
            v1 = v1_ref[:, tcol]
            v2 = v2_ref[:, tcol]
            parts = [v1[0:1] + v2]
            for a in range(1, 8):
                parts.append(v1[a:a + 1] + v2[0:8])
            parts.append(v1[8:16] + v2[0:1])
            cand = jnp.concatenate(parts, axis=0)
            work = cand
            theta = jnp.full((1, LANE), NEG_INF, F32)
            cnt = jnp.zeros((1, LANE), F32)
            for _ in range(PEER_TOPK):
                m = jnp.max(work, axis=0, keepdims=True)
                eq = work == m
                theta = jnp.where(cnt < PEER_TOPK, m, theta)
                cnt = cnt + jnp.sum(jnp.where(eq, 1.0, 0.0), axis=0, keepdims=True)
                work = jnp.where(eq, NEG_INF, work)
            cmax = v1[0:1] + v2[0:1]
            zsum = jnp.sum(jnp.where(cand >= theta, jnp.exp(cand - cmax), 0.0), axis=0, keepdims=True)
            t1 = jnp.full(s1.shape, POS_INF, F32)
            for b in range(PEER_TOPK):
                t1 = jnp.where(s1 + v2[b:b + 1] >= theta, v2[b:b + 1], t1)
            t1 = jnp.where(sel1, t1, POS_INF)
            e1_ref[tc, hh] = jnp.where(sel1, jnp.exp(s1 - v1[0:1]), 0.0) / zsum
            t1_ref[tc, hh] = t1
            e2_ref[tc, hh] = jnp.where(sel2, jnp.exp(s2 - v2[0:1]), 0.0)
            s2_ref[tc, hh] = jnp.where(sel2, s2, NEG_INF)
        return carry

    lax.fori_loop(0, PEER_HEADS, head_body, 0)


def _stage_f(xs, mod, g2, wqt, keys, tb):
    b, l, d = xs.shape
    nblk = l // tb
    ntok = b * l
    nch = ntok // LANE
    kern = functools.partial(_stage_f_kernel, tb=tb)
    desc_spec = pl.BlockSpec((tb // LANE, PEER_HEADS, PEER_NKEYS, LANE), lambda bi, i: (bi * nblk + i, 0, 0, 0))
    desc_shape = jax.ShapeDtypeStruct((nch, PEER_HEADS, PEER_NKEYS, LANE), F32)
    return pl.pallas_call(
        kern,
        grid=(b, nblk),
        in_specs=[
            pl.BlockSpec((1, tb, d), lambda bi, i: (bi, i, 0)),
            pl.BlockSpec((1, 8, d), lambda bi, i: (bi, 0, 0)),
            _full_spec(g2.shape), _full_spec(wqt.shape), _full_spec(keys.shape),
        ],
        out_specs=[pl.BlockSpec((d, tb), lambda bi, i: (0, bi * nblk + i)),
                   desc_spec, desc_spec, desc_spec, desc_spec],
        out_shape=[jax.ShapeDtypeStruct((d, ntok), BF16), desc_shape, desc_shape, desc_shape, desc_shape],
        scratch_shapes=[
            pltpu.VMEM((PEER_HEADS * 2 * PEER_HALF, tb), F32),
            pltpu.VMEM((PEER_TOPK, tb), F32),
            pltpu.VMEM((PEER_TOPK, tb), F32),
        ],
        compiler_params=_cparams(("parallel", "parallel")),
    )(xs, mod, g2, wqt, keys)


PEER_TBG = 512
PEER_TE = 1024
PEER_I1_PER_TILE = PEER_TE // PEER_NKEYS


def _stage_g_kernel(ht_ref, e1_ref, t1_ref, e2_ref, s2_ref, u_ref, vt_ref, x_ref, mod_ref,
                    o_ref, acc_ref, at_ref, pt_ref, *, tb):
    i = pl.program_id(1)

    @pl.when(i == 0)
    def _():
        acc_ref[...] = jnp.zeros_like(acc_ref)

    at_ref[...] = _dot(u_ref[...], ht_ref[...])
    half = PEER_NKEYS // 2
    for tc in range(tb // LANE):
        tcol = slice(tc * LANE, (tc + 1) * LANE)

        def i1_body(i1l, carry):
            i1g = i * PEER_I1_PER_TILE + i1l
            for hf in range(2):
                rows2 = slice(hf * half, (hf + 1) * half)
                gate = jnp.zeros((half, LANE), F32)
                for hh in range(PEER_HEADS):
                    e1row = e1_ref[tc, hh, pl.ds(i1g, 1), :]
                    t1row = t1_ref[tc, hh, pl.ds(i1g, 1), :]
                    e2 = e2_ref[tc, hh, rows2, :]
                    s2 = s2_ref[tc, hh, rows2, :]
                    gate = gate + jnp.where(s2 >= t1row, e2 * e1row, 0.0)
                r0 = pl.multiple_of(i1l * PEER_NKEYS + hf * half, half)
                a = at_ref[pl.ds(r0, half), tcol]
                pt_ref[pl.ds(r0, half), tcol] = (gate * _gelu(a)).astype(BF16)
            return carry

        lax.fori_loop(0, PEER_I1_PER_TILE, i1_body, 0)
    acc_ref[...] += _dot(vt_ref[...], pt_ref[...])

    @pl.when(i == pl.num_programs(1) - 1)
    def _():
        o_ref[...] = x_ref[...] + mod_ref[0, 5:6, :] * acc_ref[...].T


def _stage_g(ht, e1, t1, e2, s2, u_bf, vt_bf, xflat, mod, tokens_per_batch, tb):
    d, ntok = ht.shape
    n_exp = u_bf.shape[0]
    blocks_per_batch = tokens_per_batch // tb
    kern = functools.partial(_stage_g_kernel, tb=tb)
    desc_spec = pl.BlockSpec((tb // LANE, PEER_HEADS, PEER_NKEYS, LANE), lambda j, i: (j, 0, 0, 0))
    return pl.pallas_call(
        kern,
        grid=(ntok // tb, n_exp // PEER_TE),
        in_specs=[
            pl.BlockSpec((d, tb), lambda j, i: (0, j)),
            desc_spec, desc_spec, desc_spec, desc_spec,
            pl.BlockSpec((PEER_TE, d), lambda j, i: (i, 0)),
            pl.BlockSpec((d, PEER_TE), lambda j, i: (0, i)),
            pl.BlockSpec((tb, d), lambda j, i: (j, 0)),
            pl.BlockSpec((1, 8, d), lambda j, i: (j // blocks_per_batch, 0, 0)),
        ],
        out_specs=pl.BlockSpec((tb, d), lambda j, i: (j, 0)),
        out_shape=jax.ShapeDtypeStruct((ntok, d), F32),
        scratch_shapes=[
            pltpu.VMEM((d, tb), F32),
            pltpu.VMEM((PEER_TE, tb), F32),
            pltpu.VMEM((PEER_TE, tb), BF16),
        ],
        compiler_params=_cparams(("parallel", "arbitrary")),
    )(ht, e1, t1, e2, s2, u_bf, vt_bf, xflat, mod)


def _final_norm_kernel(x_ref, g_ref, o_ref):
    x = x_ref[...]
    ms = jnp.mean(x * x, axis=-1, keepdims=True)
    o_ref[...] = x * lax.rsqrt(ms + EPS) * g_ref[...]


def _final_norm(xflat, g, tb=512):
    n, d = xflat.shape
    return pl.pallas_call(
        _final_norm_kernel,
        grid=(n // tb,),
        in_specs=[pl.BlockSpec((tb, d), lambda i: (i, 0)), _full_spec(g.shape)],
        out_specs=pl.BlockSpec((tb, d), lambda i: (i, 0)),
        out_shape=jax.ShapeDtypeStruct((n, d), F32),
        compiler_params=_cparams(("parallel",)),
    )(xflat, g)


def _block_ones(n, blk):
    idx = np.arange(n) // blk
    return jnp.asarray((idx[:, None] == idx[None, :]).astype(np.float32), dtype=BF16)


def _s5_discretise(lam_re, lam_im, b_re, b_im, log_step):
    lam_re = jnp.minimum(lam_re.astype(F32), -1e-4)
    lam_im = lam_im.astype(F32)
    dt = jnp.exp(log_step.astype(F32))[:, None]
    mag = jnp.exp(lam_re * dt)
    a_re = mag * jnp.cos(lam_im * dt)
    a_im = mag * jnp.sin(lam_im * dt)
    den = lam_re * lam_re + lam_im * lam_im
    f_re = ((a_re - 1.0) * lam_re + a_im * lam_im) / den
    f_im = (a_im * lam_re - (a_re - 1.0) * lam_im) / den
    b_re = b_re.astype(F32)
    b_im = b_im.astype(F32)
    bb_re = f_re[..., None] * b_re - f_im[..., None] * b_im
    bb_im = f_re[..., None] * b_im + f_im[..., None] * b_re
    return a_re, a_im, bb_re, bb_im


def _group_block_diag(t):
    g, r, c = t.shape
    eye = jnp.eye(g, dtype=t.dtype)
    return (t[:, :, None, :] * eye[:, None, :, None]).reshape(g * r, g * c)


def _s5_params(lam_re, lam_im, b_re, b_im, c_re, c_im, log_step, nseq):
    b_rows, c_cols, ars, ais = [], [], [], []
    for d in range(2):
        a_re, a_im, bb_re, bb_im = _s5_discretise(lam_re[d], lam_im[d], b_re[d], b_im[d], log_step[d])
        bm = jnp.concatenate([_group_block_diag(jnp.swapaxes(bb_re, 1, 2)),
                              _group_block_diag(jnp.swapaxes(bb_im, 1, 2))], axis=1)
        b_rows.append(bm)
        cm = jnp.concatenate([_group_block_diag(jnp.swapaxes(c_re[d].astype(F32), 1, 2)),
                              -_group_block_diag(jnp.swapaxes(c_im[d].astype(F32), 1, 2))], axis=0)
        c_cols.append(cm)
        ars.append(jnp.broadcast_to(a_re.reshape(1, S5_LANES), (nseq, S5_LANES)))
        ais.append(jnp.broadcast_to(a_im.reshape(1, S5_LANES), (nseq, S5_LANES)))
    b2 = jnp.concatenate(b_rows, axis=0).astype(BF16)
    c2 = jnp.concatenate(c_cols, axis=1).astype(BF16)
    return b2, jnp.concatenate(ars, axis=0), jnp.concatenate(ais, axis=0), c2


def _s5_sequences(s5x_c, s5x_l):
    fwd = jnp.concatenate([s5x_c, s5x_l], axis=1)
    bwd = jnp.concatenate([jnp.flip(s5x_c, axis=1), jnp.flip(s5x_l, axis=1)], axis=1)
    z = jnp.zeros_like(fwd)
    rows = jnp.concatenate([jnp.concatenate([fwd, z], axis=-1), jnp.concatenate([z, bwd], axis=-1)], axis=0)
    return jnp.swapaxes(rows, 0, 1)


def _s5_unpack(y, nb, c_len):
    yf = jnp.swapaxes(y[:, :nb], 0, 1)
    yb = jnp.swapaxes(y[:, nb:], 0, 1)
    return (yf[:, c_len:], jnp.flip(yb[:, c_len:], axis=1), yf[:, :c_len], jnp.flip(yb[:, :c_len], axis=1))


def kernel(x, c, ctx, c_ctx, w_mod, b_mod, norm1_g, norm2_g, w_in, w_out, sgu_w, sgu_b, s5_lambda_re, s5_lambda_im, s5_b_re, s5_b_im, s5_c_re, s5_c_im, s5_log_step, s5_d, s5_w_glu, gla_w_gate, gla_b_gate, gla_norm_g, peer_w_query, peer_sub_keys, peer_expert_u, peer_expert_v, final_norm_g):
    nb, seq, d = x.shape
    c_len = ctx.shape[1]
    depth = w_mod.shape[0]

    cc = jnp.concatenate([c, c_ctx[None, :], jnp.zeros((8 - nb - 1, d), F32)], axis=0)
    mods = _mod_call(cc, w_mod, b_mod)

    ones_sgu = _block_ones(SGU_DIM, SGU_HEAD_DIM)
    ones_gla = _block_ones(GLA_DIM, GLA_DV)
    tri_np = np.tril(np.ones((GLA_CHUNK, GLA_CHUNK), np.float32))
    trif = jnp.asarray(tri_np, dtype=BF16)
    trib = jnp.asarray(tri_np.T, dtype=BF16)
    s_zero = jnp.zeros((nb, 2, GLA_DIM, GLA_KEY_DIM), F32)

    xl, xc = x, ctx
    for l in range(depth):
        ctx_out = l < depth - 1
        m6 = mods[l].reshape(8, N_MOD, d)
        mod_l = jnp.pad(m6[:nb], ((0, 0), (0, 2), (0, 0)))
        mod_c = jnp.broadcast_to(jnp.pad(m6[nb], ((0, 2), (0, 0)))[None], (nb, 8, d))

        win = jnp.pad(w_in[l], ((0, 0), (0, IN_PAD - IN_WIDTH))).astype(BF16)
        sguw = sgu_w[l].astype(BF16)
        sgub = jnp.repeat(jnp.swapaxes(sgu_b[l], 0, 1), SGU_HEAD_DIM, axis=1)
        gw = jnp.zeros((LANE, 2 * GLA_KEY_DIM), F32)
        gw = gw.at[0:GLA_RANK, 0:GLA_KEY_DIM].set(gla_w_gate[l, 0])
        gw = gw.at[GLA_RANK:2 * GLA_RANK, GLA_KEY_DIM:].set(gla_w_gate[l, 1]).astype(BF16)
        gb = gla_b_gate[l].reshape(1, 2 * GLA_KEY_DIM)
        g1 = norm1_g[l].reshape(1, d)

        a_l = _stage_a(xl, mod_l, g1, win, sguw, sgub, ones_sgu, gw, gb, tb=256)
        a_c = _stage_a(xc, mod_c, g1, win, sguw, sgub, ones_sgu, gw, gb, tb=256)
        sgu_l, s5x_l, qk_l, v_l, g_l, la_l = a_l
        sgu_c, s5x_c, qk_c, v_c, g_c, la_c = a_c

        b2, ar, ai, c2 = _s5_params(s5_lambda_re[l], s5_lambda_im[l], s5_b_re[l], s5_b_im[l],
                                    s5_c_re[l], s5_c_im[l], s5_log_step[l], nb)
        yseq = _s5_call(_s5_sequences(s5x_c, s5x_l), b2, ar, ai, c2)
        yf_l, yb_l, yf_c, yb_c = _s5_unpack(yseq, nb, c_len)

        of_c, ob_c, s_ctx = _gla_call(qk_c, v_c, la_c, s_zero, trif, trib, grid_order=False)
        of_l, ob_l, _ = _gla_call(qk_l, v_l, la_l, s_ctx, trif, trib, grid_order=True)

        dskip = s5_d[l].reshape(1, S5_DIM)
        wglu = s5_w_glu[l].astype(BF16)
        normg = gla_norm_g[l].reshape(1, GLA_DIM)
        wout = w_out[l].astype(BF16)
        g2 = norm2_g[l].reshape(1, d)
        wqt = jnp.swapaxes(peer_w_query[l], 0, 1).astype(BF16)
        keys = peer_sub_keys[l].reshape(2 * PEER_HEADS, PEER_NKEYS, PEER_HALF).astype(BF16)
        u_bf = peer_expert_u[l].astype(BF16)
        vt_bf = jnp.swapaxes(peer_expert_v[l], 0, 1).astype(BF16)

        xl = _stage_e(xl, mod_l, sgu_l, yf_l, yb_l, s5x_l, of_l, ob_l, g_l,
                      dskip, wglu, normg, ones_gla, wout, tb=256)
        desc = _stage_f(xl, mod_l, g2, wqt, keys, tb=PEER_TBF)
        xl = _stage_g(*desc, u_bf, vt_bf, xl.reshape(nb * seq, d), mod_l, seq, PEER_TBG).reshape(nb, seq, d)

        if ctx_out:
            xc = _stage_e(xc, mod_c, sgu_c, yf_c, yb_c, s5x_c, of_c, ob_c, g_c,
                          dskip, wglu, normg, ones_gla, wout, tb=256)
            desc = _stage_f(xc, mod_c, g2, wqt, keys, tb=PEER_TBF)
            xc = _stage_g(*desc, u_bf, vt_bf, xc.reshape(nb * c_len, d), mod_c, c_len,
                          min(PEER_TBG, c_len)).reshape(nb, c_len, d)

    return _final_norm(xl.reshape(nb * seq, d), final_norm_g.reshape(1, d)).reshape(nb, seq, d)
```

```python
import functools
import math

import numpy as np
import jax
import jax.numpy as jnp
from jax import lax
from jax.experimental import pallas as pl
from jax.experimental.pallas import tpu as pltpu

F32 = jnp.float32
BF16 = jnp.bfloat16

EPS = 1e-6
N_MOD = 6
GRID_W = 64

SGU_DIM = 256
SGU_HEADS = 4
SGU_HEAD_DIM = 64
SGU_CHUNK = 128

S5_DIM = 256
S5_GROUP = 16
S5_GROUPS = 16
S5_STATE = 64
S5_LANES = S5_GROUPS * S5_STATE

GLA_DIM = 512
GLA_HEADS = 8
GLA_DV = 64
GLA_DK = 32
GLA_KEY_DIM = 256
GLA_RANK = 16
GLA_GATE_TEMP = 16.0
GLA_CHUNK = 64

PEER_HEADS = 8
PEER_NKEYS = 128
PEER_HALF = 128
PEER_TOPK = 16

IN_WIDTH = 2336
IN_PAD = 2432
LANE = 128

VMEM_LIMIT = 56 * 1024 * 1024

NEG_INF = float("-inf")
POS_INF = float("inf")


def _cparams(sem):
    return pltpu.CompilerParams(dimension_semantics=sem, vmem_limit_bytes=VMEM_LIMIT)


def _gelu(x):
    c = math.sqrt(2.0 / math.pi)
    return 0.5 * x * (1.0 + jnp.tanh(c * (x + 0.044715 * (x * x * x))))


def _sigmoid(x):
    return 1.0 / (1.0 + jnp.exp(-x))


def _dot(a, b):
    return jnp.dot(a, b, preferred_element_type=F32)


def _dot_nt(a, b):
    return lax.dot_general(a, b, (((1,), (1,)), ((), ())), preferred_element_type=F32)


def _dot_tn(a, b):
    return lax.dot_general(a, b, (((0,), (0,)), ((), ())), preferred_element_type=F32)


def _split3(x):
    hi = x.astype(BF16)
    r = x - hi.astype(F32)
    mid = r.astype(BF16)
    lo = (r - mid.astype(F32)).astype(BF16)
    return hi, mid, lo


def _dot_x_exact(x, m):
    hi, mid, lo = _split3(x)
    return _dot(hi, m) + _dot(mid, m) + _dot(lo, m)


def _dot_m_exact(m, x):
    hi, mid, lo = _split3(x)
    return _dot(m, hi) + _dot(m, mid) + _dot(m, lo)


def _full_spec(shape):
    nd = len(shape)
    return pl.BlockSpec(shape, lambda *_: (0,) * nd)


MOD_TILE = 512


def _mod_kernel(c_ref, w_ref, b_ref, o_ref):
    c = c_ref[...]
    a = c * _sigmoid(c)
    o_ref[0] = jnp.dot(a, w_ref[0], preferred_element_type=F32,
                       precision=lax.Precision.HIGHEST) + b_ref[0]


def _mod_call(cc, w_mod, b_mod):
    depth, d, nd = w_mod.shape
    rows = cc.shape[0]
    return pl.pallas_call(
        _mod_kernel,
        grid=(depth, nd // MOD_TILE),
        in_specs=[
            pl.BlockSpec((rows, d), lambda l, j: (0, 0)),
            pl.BlockSpec((1, d, MOD_TILE), lambda l, j: (l, 0, j)),
            pl.BlockSpec((1, 1, MOD_TILE), lambda l, j: (l, 0, j)),
        ],
        out_specs=pl.BlockSpec((1, rows, MOD_TILE), lambda l, j: (l, 0, j)),
        out_shape=jax.ShapeDtypeStruct((depth, rows, nd), F32),
        compiler_params=_cparams(("parallel", "parallel")),
    )(cc, w_mod, b_mod.reshape(depth, 1, nd))


def _stage_a_kernel(x_ref, mod_ref, g1_ref, win_ref, sguw_ref, sgub_ref, ones_ref, gw_ref, gb_ref,
                    sgu_ref, s5x_ref, qk_ref, v_ref, g_ref, la_ref, *, tb):
    x = x_ref[0]
    ms = jnp.mean(x * x, axis=-1, keepdims=True)
    xn = x * lax.rsqrt(ms + EPS) * g1_ref[...]
    h = xn * (1.0 + mod_ref[0, 1:2, :]) + mod_ref[0, 0:1, :]
    cols = _dot(h.astype(BF16), win_ref[...])

    u = _gelu(cols[:, 0:SGU_DIM])
    v = _gelu(cols[:, SGU_DIM:2 * SGU_DIM])
    msq = _dot_x_exact(v * v, ones_ref[...]) * (1.0 / SGU_HEAD_DIM)
    vn = (v * lax.rsqrt(msq + EPS)).astype(BF16)
    head_of_lane = lax.broadcasted_iota(jnp.int32, (SGU_CHUNK, SGU_DIM), 1) // SGU_HEAD_DIM
    for ci in range(tb // SGU_CHUNK):
        rows = slice(ci * SGU_CHUNK, (ci + 1) * SGU_CHUNK)
        vc = vn[rows]
        mixed = sgub_ref[...]
        for hh in range(SGU_HEADS):
            mixed = mixed + jnp.where(head_of_lane == hh, _dot(sguw_ref[hh], vc), 0.0)
        sgu_ref[0, rows, :] = u[rows] * mixed

    s5x_ref[0] = cols[:, 512:768]
    qk_ref[0, :, 0:GLA_KEY_DIM] = cols[:, 768:1024] * (GLA_DK ** -0.5)
    qk_ref[0, :, GLA_KEY_DIM:] = cols[:, 1024:1280]
    v_ref[0] = cols[:, 1280:1792]
    g_ref[0] = cols[:, 1792:2304]

    z = cols[:, 2304:2432].astype(BF16)
    za = _dot(z, gw_ref[...]) + gb_ref[...]
    log_sig = jnp.minimum(za, 0.0) - jnp.log1p(jnp.exp(-jnp.abs(za)))
    la_ref[0] = log_sig * (1.0 / GLA_GATE_TEMP)


def _stage_a(xs, mod, g1, win, sguw, sgub, ones_sgu, gw, gb, tb):
    b, l, d = xs.shape
    kern = functools.partial(_stage_a_kernel, tb=tb)
    tok = lambda w: pl.BlockSpec((1, tb, w), lambda bi, i: (bi, i, 0))
    widths = (SGU_DIM, S5_DIM, 2 * GLA_KEY_DIM, GLA_DIM, GLA_DIM, 2 * GLA_KEY_DIM)
    return pl.pallas_call(
        kern,
        grid=(b, l // tb),
        in_specs=[
            tok(d),
            pl.BlockSpec((1, 8, d), lambda bi, i: (bi, 0, 0)),
            _full_spec(g1.shape), _full_spec(win.shape), _full_spec(sguw.shape), _full_spec(sgub.shape),
            _full_spec(ones_sgu.shape), _full_spec(gw.shape), _full_spec(gb.shape),
        ],
        out_specs=[tok(w) for w in widths],
        out_shape=[jax.ShapeDtypeStruct((b, l, w), F32) for w in widths],
        compiler_params=_cparams(("parallel", "parallel")),
    )(xs, mod, g1, win, sguw, sgub, ones_sgu, gw, gb)


S5_TC = 128


def _s5_kernel(u_ref, b2_ref, ar_ref, ai_ref, c2_ref, y_ref, h_ref, buf_ref, *, nseq):
    tc = S5_TC

    @pl.when(pl.program_id(0) == 0)
    def _():
        h_ref[...] = jnp.zeros_like(h_ref)

    u = u_ref[...].reshape(tc * 2 * nseq, 2 * S5_DIM).astype(BF16)
    buf_ref[...] = _dot(u, b2_ref[...]).reshape(tc, 2 * nseq, 2 * S5_LANES)
    ar = ar_ref[...]
    ai = ai_ref[...]

    def step(t, carry):
        hr, hi = carry
        bur = buf_ref[t, :, 0:S5_LANES]
        bui = buf_ref[t, :, S5_LANES:]
        nhr = ar * hr - ai * hi + bur
        nhi = ar * hi + ai * hr + bui
        buf_ref[t, :, 0:S5_LANES] = nhr
        buf_ref[t, :, S5_LANES:] = nhi
        return nhr, nhi

    hr, hi = lax.fori_loop(0, tc, step, (h_ref[:, 0:S5_LANES], h_ref[:, S5_LANES:]))
    h_ref[:, 0:S5_LANES] = hr
    h_ref[:, S5_LANES:] = hi

    hs = buf_ref[...].reshape(tc * 2 * nseq, 2 * S5_LANES).astype(BF16)
    y2 = _dot(hs, c2_ref[...])
    row = lax.broadcasted_iota(jnp.int32, (tc * 2 * nseq, S5_DIM), 0) % (2 * nseq)
    y = jnp.where(row < nseq, y2[:, 0:S5_DIM], y2[:, S5_DIM:])
    y_ref[...] = y.reshape(tc, 2 * nseq, S5_DIM)


def _s5_call(useq, b2, ar, ai, c2):
    t, rows, _ = useq.shape
    nseq = rows // 2
    kern = functools.partial(_s5_kernel, nseq=nseq)
    return pl.pallas_call(
        kern,
        grid=(t // S5_TC,),
        in_specs=[
            pl.BlockSpec((S5_TC, rows, 2 * S5_DIM), lambda i: (i, 0, 0)),
            _full_spec(b2.shape), _full_spec(ar.shape), _full_spec(ai.shape), _full_spec(c2.shape),
        ],
        out_specs=pl.BlockSpec((S5_TC, rows, S5_DIM), lambda i: (i, 0, 0)),
        out_shape=jax.ShapeDtypeStruct((t, rows, S5_DIM), F32),
        scratch_shapes=[
            pltpu.VMEM((rows, 2 * S5_LANES), F32),
            pltpu.VMEM((S5_TC, rows, 2 * S5_LANES), F32),
        ],
        compiler_params=_cparams(("arbitrary",)),
    )(useq, b2, ar, ai, c2)


def _gla_stream(q, k, v, la, s_t, tri, last_row, ref_row, tri_mask, hm_k, hm_v, hm_s):
    bcum = _dot_m_exact(tri, la)
    blast = bcum[last_row:last_row + 1]
    bref = bcum[ref_row:ref_row + 1]
    qe = q * jnp.exp(bcum)
    qd = q * jnp.exp(bcum - bref)
    kd = k * jnp.exp(bref - bcum)
    kdec = k * jnp.exp(blast - bcum)
    kst = jnp.where(hm_k, jnp.concatenate([kd] * GLA_HEADS, axis=0), 0.0).astype(BF16)
    sc = jnp.where(tri_mask, _dot_nt(qd.astype(BF16), kst), 0.0)
    vbd = jnp.where(hm_v, jnp.concatenate([v] * GLA_HEADS, axis=0), 0.0).astype(BF16)
    o = _dot(sc.astype(BF16), vbd) + _dot_nt(qe.astype(BF16), s_t.astype(BF16))
    kv_t = jnp.where(hm_s, _dot_tn(v.astype(BF16), kdec.astype(BF16)), 0.0)
    s_new = s_t * jnp.exp(blast) + kv_t
    return o, s_new


def _gla_kernel(qkf_ref, vf_ref, laf_ref, qkb_ref, vb_ref, lab_ref, s0_ref, trif_ref, trib_ref,
                of_ref, ob_ref, sout_ref, s_ref, *, nb):
    c = pl.program_id(0)

    @pl.when(c == 0)
    def _():
        s_ref[...] = s0_ref[...]

    ch = GLA_CHUNK
    r_k = lax.broadcasted_iota(jnp.int32, (GLA_HEADS * ch, GLA_KEY_DIM), 0) // ch
    c_k = lax.broadcasted_iota(jnp.int32, (GLA_HEADS * ch, GLA_KEY_DIM), 1) // GLA_DK
    hm_k = r_k == c_k
    r_v = lax.broadcasted_iota(jnp.int32, (GLA_HEADS * ch, GLA_DIM), 0) // ch
    c_v = lax.broadcasted_iota(jnp.int32, (GLA_HEADS * ch, GLA_DIM), 1) // GLA_DV
    hm_v = r_v == c_v
    r_s = lax.broadcasted_iota(jnp.int32, (GLA_DIM, GLA_KEY_DIM), 0) // GLA_DV
    c_s = lax.broadcasted_iota(jnp.int32, (GLA_DIM, GLA_KEY_DIM), 1) // GLA_DK
    hm_s = r_s == c_s
    t_i = lax.broadcasted_iota(jnp.int32, (ch, GLA_HEADS * ch), 0)
    s_i = lax.broadcasted_iota(jnp.int32, (ch, GLA_HEADS * ch), 1) % ch
    mask_f = t_i >= s_i
    mask_b = t_i <= s_i
    trif = trif_ref[...]
    trib = trib_ref[...]

    def body(b, carry):
        qk = qkf_ref[b]
        o, s_new = _gla_stream(qk[:, 0:GLA_KEY_DIM], qk[:, GLA_KEY_DIM:], vf_ref[b], laf_ref[b], s_ref[b, 0],
                               trif, ch - 1, ch // 2, mask_f, hm_k, hm_v, hm_s)
        of_ref[b] = o
        s_ref[b, 0] = s_new
        qk = qkb_ref[b]
        o, s_new = _gla_stream(qk[:, 0:GLA_KEY_DIM], qk[:, GLA_KEY_DIM:], vb_ref[b], lab_ref[b], s_ref[b, 1],
                               trib, 0, ch - 1 - ch // 2, mask_b, hm_k, hm_v, hm_s)
        ob_ref[b] = o
        s_ref[b, 1] = s_new
        return carry

    lax.fori_loop(0, nb, body, 0)

    @pl.when(c == pl.num_programs(0) - 1)
    def _():
        sout_ref[...] = s_ref[...]


def _gla_call(qk, v, la, s0, trif, trib, grid_order):
    b, l, _ = qk.shape
    ch = GLA_CHUNK
    n = l // ch
    if grid_order:
        assert l == ch * GRID_W
        view = lambda a: a.reshape(b, ch, GRID_W * a.shape[-1])
        spec = lambda w, per, off, rev: pl.BlockSpec(
            (b, ch, w), (lambda c: (0, 0, (n - 1 - c) * per + off)) if rev else (lambda c: (0, 0, c * per + off)))
    else:
        view = lambda a: a
        spec = lambda w, per, off, rev: pl.BlockSpec(
            (b, ch, w), (lambda c: (0, n - 1 - c, off)) if rev else (lambda c: (0, c, off)))
    kern = functools.partial(_gla_kernel, nb=b)
    o_shape = jax.ShapeDtypeStruct(view(v).shape, F32)
    of, ob, s_out = pl.pallas_call(
        kern,
        grid=(n,),
        in_specs=[
            spec(2 * GLA_KEY_DIM, 1, 0, False), spec(GLA_DIM, 1, 0, False), spec(GLA_KEY_DIM, 2, 0, False),
            spec(2 * GLA_KEY_DIM, 1, 0, True), spec(GLA_DIM, 1, 0, True), spec(GLA_KEY_DIM, 2, 1, True),
            _full_spec(s0.shape), _full_spec(trif.shape), _full_spec(trib.shape),
        ],
        out_specs=[spec(GLA_DIM, 1, 0, False), spec(GLA_DIM, 1, 0, True), _full_spec(s0.shape)],
        out_shape=[o_shape, o_shape, jax.ShapeDtypeStruct(s0.shape, F32)],
        scratch_shapes=[pltpu.VMEM(s0.shape, F32)],
        compiler_params=_cparams(("arbitrary",)),
    )(view(qk), view(v), view(la), view(qk), view(v), view(la), s0, trif, trib)
    return of.reshape(b, l, GLA_DIM), ob.reshape(b, l, GLA_DIM), s_out


def _stage_e_kernel(x_ref, mod_ref, sgu_ref, yf_ref, yb_ref, s5x_ref, of_ref, ob_ref, g_ref,
                    dskip_ref, wglu_ref, normg_ref, ones_ref, wout_ref, o_ref):
    ys = yf_ref[0] + yb_ref[0] + dskip_ref[...] * s5x_ref[0]
    z = _dot(_gelu(ys).astype(BF16), wglu_ref[...])
    s5o = z[:, 0:S5_DIM] * _sigmoid(z[:, S5_DIM:])
    o = of_ref[0] + ob_ref[0]
    ms = _dot_x_exact(o * o, ones_ref[...]) * (1.0 / GLA_DV)
    g = g_ref[0]
    gl = o * lax.rsqrt(ms + EPS) * normg_ref[...] * (g * _sigmoid(g))
    y = (_dot(sgu_ref[0].astype(BF16), wout_ref[0:SGU_DIM, :])
         + _dot(s5o.astype(BF16), wout_ref[SGU_DIM:SGU_DIM + S5_DIM, :])
         + _dot(gl.astype(BF16), wout_ref[SGU_DIM + S5_DIM:, :]))
    o_ref[0] = x_ref[0] + mod_ref[0, 2:3, :] * y


def _stage_e(xs, mod, sgu, yf, yb, s5x, of, ob, g, dskip, wglu, normg, ones_gla, wout, tb):
    b, l, d = xs.shape
    tok = lambda w: pl.BlockSpec((1, tb, w), lambda bi, i: (bi, i, 0))
    return pl.pallas_call(
        _stage_e_kernel,
        grid=(b, l // tb),
        in_specs=[
            tok(d), pl.BlockSpec((1, 8, d), lambda bi, i: (bi, 0, 0)),
            tok(SGU_DIM), tok(S5_DIM), tok(S5_DIM), tok(S5_DIM), tok(GLA_DIM), tok(GLA_DIM), tok(GLA_DIM),
            _full_spec(dskip.shape), _full_spec(wglu.shape), _full_spec(normg.shape),
            _full_spec(ones_gla.shape), _full_spec(wout.shape),
        ],
        out_specs=tok(d),
        out_shape=jax.ShapeDtypeStruct((b, l, d), F32),
        compiler_params=_cparams(("parallel", "parallel")),
    )(xs, mod, sgu, yf, yb, s5x, of, ob, g, dskip, wglu, normg, ones_gla, wout)


PEER_TBF = 256
SUBLANES = 8


def _sort_network_16():
    def merge(lo, hi, r):
        step = r * 2
        if step < hi - lo:
            yield from merge(lo, hi, step)
            yield from merge(lo + r, hi, step)
            for i in range(lo + r, hi - r, step):
                yield (i, i + r)
        else:
            yield (lo, lo + r)

    def sort(lo, hi):
        if hi - lo >= 1:
            mid = lo + (hi - lo) // 2
            yield from sort(lo, mid)
            yield from sort(mid + 1, hi)
            yield from merge(lo, hi, 1)

    return tuple(sort(0, PEER_TOPK - 1))


SORT16 = _sort_network_16()
BITONIC16 = tuple((k, k + s) for s in (8, 4, 2, 1) for k in range(PEER_TOPK) if not k & s)


def _compare_exchange(xs, pairs):
    xs = list(xs)
    for i, j in pairs:
        hi = jnp.maximum(xs[i], xs[j])
        lo = jnp.minimum(xs[i], xs[j])
        xs[i], xs[j] = hi, lo
    return xs


def _merge_sublanes(xs):
    for shift in (4, 6, 7):
        rolled = [pltpu.roll(x, shift, 0) for x in xs]
        xs = [jnp.maximum(xs[k], rolled[PEER_TOPK - 1 - k]) for k in range(PEER_TOPK)]
        xs = _compare_exchange(xs, BITONIC16)
    return xs


def _dup_bf16_words(x):
    bits = pltpu.bitcast(x.astype(BF16).astype(F32), jnp.int32)
    return bits | lax.shift_right_logical(bits, 16)


def _stage_f_kernel(x_ref, mod_ref, g2_ref, wqt_ref, keys_ref,
                    ht_ref, e1w_ref, n1w_ref, e2_ref, r2_ref, qt_ref, v1_ref, v2_ref, *, tb):
    x = x_ref[0]
    ms = jnp.mean(x * x, axis=-1, keepdims=True)
    xn = x * lax.rsqrt(ms + EPS) * g2_ref[...]
    h = xn * (1.0 + mod_ref[0, 4:5, :]) + mod_ref[0, 3:4, :]
    ht = h.T.astype(BF16)
    ht_ref[...] = ht
    qt_ref[...] = _dot(wqt_ref[...], ht)
    k_top = PEER_TOPK

    def tiles(s):
        return [s[SUBLANES * k:SUBLANES * (k + 1)] for k in range(PEER_NKEYS // SUBLANES)]

    def head_body(hh, carry):
        for tc in range(tb // LANE):
            tcol = slice(tc * LANE, (tc + 1) * LANE)
            r1 = pl.multiple_of(hh * (2 * PEER_HALF), 2 * PEER_HALF)
            q1 = qt_ref[pl.ds(r1, PEER_HALF), tcol].astype(BF16)
            q2 = qt_ref[pl.ds(r1 + PEER_HALF, PEER_HALF), tcol].astype(BF16)
            s1 = _dot(keys_ref[hh], q1)
            s2 = _dot(keys_ref[PEER_HEADS + hh], q2)
            for s, v_ref in ((s1, v1_ref), (s2, v2_ref)):
                top = _merge_sublanes(_compare_exchange(tiles(s), SORT16))
                for k in range(k_top):
                    v_ref[k:k + 1, tcol] = top[k][0:1]
            v1row = lambda a: v1_ref[a:a + 1, tcol]
            v2row = lambda b: v2_ref[b:b + 1, tcol]
            v1lo = v1_ref[0:SUBLANES, tcol]
            v2lo = v2_ref[0:SUBLANES, tcol]
            cand = [v1lo + v2row(b) for b in range(k_top)]
            tail = [v1row(a) + v2lo for a in range(SUBLANES, k_top)]
            for k in range(SUBLANES, k_top):
                cand[k] = jnp.maximum(cand[k], tail[k_top - 1 - k])
            best = _merge_sublanes(_compare_exchange(cand, BITONIC16))
            theta = best[k_top - 1][0:1]
            cmax = best[0][0:1]
            zsum = jnp.zeros((1, LANE), F32)
            for k in range(k_top):
                zsum = zsum + jnp.exp(best[k][0:1] - cmax)
            rz = 1.0 / zsum
            n_top = jnp.zeros((1, LANE), F32)
            for b in range(k_top):
                n_top = jnp.where(v1row(0) + v2row(b) >= theta, float(b + 1), n_top)
            n1 = jnp.zeros(s1.shape, F32)
            for b in range(SUBLANES):
                n1 = jnp.where(s1 + v2row(b) >= theta, float(b + 1), n1)
            n1 = jnp.where(s1 >= v1row(0), n_top, n1)
            r2 = jnp.zeros(s2.shape, F32)
            for b in range(k_top):
                r2 = jnp.where(v2row(b) > s2, float(b + 1), r2)
            e1 = jnp.where(s1 >= v1row(k_top - 1), jnp.exp(s1 - v1row(0)), 0.0) * rz
            e2 = jnp.where(s2 >= v2row(k_top - 1), jnp.exp(s2 - v2row(0)), 0.0)
            e1w_ref[tc, hh] = _dup_bf16_words(e1)
            n1w_ref[tc, hh] = _dup_bf16_words(n1)
            e2_ref[tc, hh] = pltpu.bitcast(e2.astype(BF16), jnp.int32)
            r2_ref[tc, hh] = pltpu.bitcast(r2.astype(BF16), jnp.int32)
        return carry

    lax.fori_loop(0, PEER_HEADS, head_body, 0)


def _stage_f(xs, mod, g2, wqt, keys, tb):
    b, l, d = xs.shape
    nblk = l // tb
    ntok = b * l
    nch = ntok // LANE
    kern = functools.partial(_stage_f_kernel, tb=tb)
    row_spec = pl.BlockSpec((tb // LANE, PEER_HEADS, PEER_NKEYS, LANE), lambda bi, i: (bi * nblk + i, 0, 0, 0))
    pair_spec = pl.BlockSpec((tb // LANE, PEER_HEADS, PEER_NKEYS // 2, LANE), lambda bi, i: (bi * nblk + i, 0, 0, 0))
    desc_shape = lambda rows: jax.ShapeDtypeStruct((nch, PEER_HEADS, rows, LANE), jnp.int32)
    return pl.pallas_call(
        kern,
        grid=(b, nblk),
        in_specs=[
            pl.BlockSpec((1, tb, d), lambda bi, i: (bi, i, 0)),
            pl.BlockSpec((1, 8, d), lambda bi, i: (bi, 0, 0)),
            _full_spec(g2.shape), _full_spec(wqt.shape), _full_spec(keys.shape),
        ],
        out_specs=[pl.BlockSpec((d, tb), lambda bi, i: (0, bi * nblk + i)),
                   row_spec, row_spec, pair_spec, pair_spec],
        out_shape=[jax.ShapeDtypeStruct((d, ntok), BF16), desc_shape(PEER_NKEYS), desc_shape(PEER_NKEYS),
                   desc_shape(PEER_NKEYS // 2), desc_shape(PEER_NKEYS // 2)],
        scratch_shapes=[
            pltpu.VMEM((PEER_HEADS * 2 * PEER_HALF, tb), F32),
            pltpu.VMEM((PEER_TOPK, tb), F32),
            pltpu.VMEM((PEER_TOPK, tb), F32),
        ],
        compiler_params=_cparams(("parallel", "parallel")),
    )(xs, mod, g2, wqt, keys)


PEER_TBG = 512
PEER_TE = 1024
PEER_I1_PER_TILE = PEER_TE // PEER_NKEYS
PEER_N_TILES = PEER_NKEYS * PEER_NKEYS // PEER_TE
PEER_MXU_COLS = 256


def _stage_g_kernel(ht_ref, e1w_ref, n1w_ref, e2_ref, r2_ref, u_ref, vt_ref, x_ref, mod_ref,
                    o_ref, acc_ref, pa_ref, pb_ref, *, tb):
    i = pl.program_id(1)
    n_tiles = pl.num_programs(1) - 1

    @pl.when(i == 0)
    def _():
        acc_ref[...] = jnp.zeros_like(acc_ref)
        pb_ref[...] = jnp.zeros_like(pb_ref)

    def step(cur_ref, prev_ref):
        per_grp = PEER_MXU_COLS // LANE
        for grp in range(tb // PEER_MXU_COLS):
            cols = slice(grp * PEER_MXU_COLS, (grp + 1) * PEER_MXU_COLS)
            for tcl in range(per_grp):
                tc = grp * per_grp + tcl
                tcol = slice(tc * LANE, (tc + 1) * LANE)
                for i1l in range(PEER_I1_PER_TILE):
                    rows = slice(i1l * PEER_NKEYS, (i1l + 1) * PEER_NKEYS)
                    gate = jnp.zeros((PEER_NKEYS, LANE), BF16)
                    for hh in range(PEER_HEADS):
                        e1row = e1w_ref[tc, hh, i1l:i1l + 1, :]
                        n1row = n1w_ref[tc, hh, i1l:i1l + 1, :]
                        e1 = pltpu.bitcast(jnp.broadcast_to(e1row, (PEER_NKEYS // 2, LANE)), BF16)
                        n1 = pltpu.bitcast(jnp.broadcast_to(n1row, (PEER_NKEYS // 2, LANE)), BF16)
                        r2 = pltpu.bitcast(r2_ref[tc, hh], BF16)
                        e2 = pltpu.bitcast(e2_ref[tc, hh], BF16)
                        gate = gate + e2 * jnp.where(r2 < n1, e1, 0.0)
                    cur_ref[rows, tcol] = gate
            acc_ref[:, cols] += _dot(vt_ref[...], prev_ref[:, cols])
            at = _dot(u_ref[...], ht_ref[:, cols])
            for tcl in range(per_grp):
                tcol = slice((grp * per_grp + tcl) * LANE, (grp * per_grp + tcl + 1) * LANE)
                for i1l in range(PEER_I1_PER_TILE):
                    rows = slice(i1l * PEER_NKEYS, (i1l + 1) * PEER_NKEYS)
                    a = at[rows, tcl * LANE:(tcl + 1) * LANE]
                    cur_ref[rows, tcol] = cur_ref[rows, tcol] * _gelu(a).astype(BF16)

    @pl.when(jnp.logical_and(i < n_tiles, i % 2 == 0))
    def _():
        step(pa_ref, pb_ref)

    @pl.when(jnp.logical_and(i < n_tiles, i % 2 == 1))
    def _():
        step(pb_ref, pa_ref)

    @pl.when(i == n_tiles)
    def _():
        last_ref = pb_ref if (PEER_N_TILES - 1) % 2 else pa_ref
        acc = acc_ref[...] + _dot(vt_ref[...], last_ref[...])
        o_ref[...] = x_ref[...] + mod_ref[0, 5:6, :] * acc.T


def _stage_g(ht, e1w, n1w, e2, r2, u_bf, vt_bf, xflat, mod, tokens_per_batch, tb):
    d, ntok = ht.shape
    assert u_bf.shape[0] == PEER_N_TILES * PEER_TE
    blocks_per_batch = tokens_per_batch // tb
    kern = functools.partial(_stage_g_kernel, tb=tb)
    last = PEER_N_TILES - 1
    desc_spec = pl.BlockSpec((tb // LANE, PEER_HEADS, PEER_NKEYS // 2, LANE), lambda j, i: (j, 0, 0, 0))
    row_spec = pl.BlockSpec((tb // LANE, PEER_HEADS, PEER_I1_PER_TILE, LANE),
                            lambda j, i: (j, 0, jnp.minimum(i, last), 0))
    return pl.pallas_call(
        kern,
        grid=(ntok // tb, PEER_N_TILES + 1),
        in_specs=[
            pl.BlockSpec((d, tb), lambda j, i: (0, j)),
            row_spec, row_spec, desc_spec, desc_spec,
            pl.BlockSpec((PEER_TE, d), lambda j, i: (jnp.minimum(i, last), 0)),
            pl.BlockSpec((d, PEER_TE), lambda j, i: (0, jnp.maximum(i - 1, 0))),
            pl.BlockSpec((tb, d), lambda j, i: (j, 0)),
            pl.BlockSpec((1, 8, d), lambda j, i: (j // blocks_per_batch, 0, 0)),
        ],
        out_specs=pl.BlockSpec((tb, d), lambda j, i: (j, 0)),
        out_shape=jax.ShapeDtypeStruct((ntok, d), F32),
        scratch_shapes=[
            pltpu.VMEM((d, tb), F32),
            pltpu.VMEM((PEER_TE, tb), BF16),
            pltpu.VMEM((PEER_TE, tb), BF16),
        ],
        compiler_params=_cparams(("parallel", "arbitrary")),
    )(ht, e1w, n1w, e2, r2, u_bf, vt_bf, xflat, mod)


def _final_norm_kernel(x_ref, g_ref, o_ref):
    x = x_ref[...]
    ms = jnp.mean(x * x, axis=-1, keepdims=True)
    o_ref[...] = x * lax.rsqrt(ms + EPS) * g_ref[...]


def _final_norm(xflat, g, tb=512):
    n, d = xflat.shape
    return pl.pallas_call(
        _final_norm_kernel,
        grid=(n // tb,),
        in_specs=[pl.BlockSpec((tb, d), lambda i: (i, 0)), _full_spec(g.shape)],
        out_specs=pl.BlockSpec((tb, d), lambda i: (i, 0)),
        out_shape=jax.ShapeDtypeStruct((n, d), F32),
        compiler_params=_cparams(("parallel",)),
    )(xflat, g)


def _block_ones(n, blk):
    idx = np.arange(n) // blk
    return jnp.asarray((idx[:, None] == idx[None, :]).astype(np.float32), dtype=BF16)


def _s5_discretise(lam_re, lam_im, b_re, b_im, log_step):
    lam_re = jnp.minimum(lam_re.astype(F32), -1e-4)
    lam_im = lam_im.astype(F32)
    dt = jnp.exp(log_step.astype(F32))[:, None]
    mag = jnp.exp(lam_re * dt)
    a_re = mag * jnp.cos(lam_im * dt)
    a_im = mag * jnp.sin(lam_im * dt)
    den = lam_re * lam_re + lam_im * lam_im
    f_re = ((a_re - 1.0) * lam_re + a_im * lam_im) / den
    f_im = (a_im * lam_re - (a_re - 1.0) * lam_im) / den
    b_re = b_re.astype(F32)
    b_im = b_im.astype(F32)
    bb_re = f_re[..., None] * b_re - f_im[..., None] * b_im
    bb_im = f_re[..., None] * b_im + f_im[..., None] * b_re
    return a_re, a_im, bb_re, bb_im


def _group_block_diag(t):
    g, r, c = t.shape
    eye = jnp.eye(g, dtype=t.dtype)
    return (t[:, :, None, :] * eye[:, None, :, None]).reshape(g * r, g * c)


def _s5_params(lam_re, lam_im, b_re, b_im, c_re, c_im, log_step, nseq):
    b_rows, c_cols, ars, ais = [], [], [], []
    for d in range(2):
        a_re, a_im, bb_re, bb_im = _s5_discretise(lam_re[d], lam_im[d], b_re[d], b_im[d], log_step[d])
        bm = jnp.concatenate([_group_block_diag(jnp.swapaxes(bb_re, 1, 2)),
                              _group_block_diag(jnp.swapaxes(bb_im, 1, 2))], axis=1)
        b_rows.append(bm)
        cm = jnp.concatenate([_group_block_diag(jnp.swapaxes(c_re[d].astype(F32), 1, 2)),
                              -_group_block_diag(jnp.swapaxes(c_im[d].astype(F32), 1, 2))], axis=0)
        c_cols.append(cm)
        ars.append(jnp.broadcast_to(a_re.reshape(1, S5_LANES), (nseq, S5_LANES)))
        ais.append(jnp.broadcast_to(a_im.reshape(1, S5_LANES), (nseq, S5_LANES)))
    b2 = jnp.concatenate(b_rows, axis=0).astype(BF16)
    c2 = jnp.concatenate(c_cols, axis=1).astype(BF16)
    return b2, jnp.concatenate(ars, axis=0), jnp.concatenate(ais, axis=0), c2


def _s5_sequences(s5x_c, s5x_l):
    fwd = jnp.concatenate([s5x_c, s5x_l], axis=1)
    bwd = jnp.concatenate([jnp.flip(s5x_c, axis=1), jnp.flip(s5x_l, axis=1)], axis=1)
    z = jnp.zeros_like(fwd)
    rows = jnp.concatenate([jnp.concatenate([fwd, z], axis=-1), jnp.concatenate([z, bwd], axis=-1)], axis=0)
    return jnp.swapaxes(rows, 0, 1)


def _s5_unpack(y, nb, c_len):
    yf = jnp.swapaxes(y[:, :nb], 0, 1)
    yb = jnp.swapaxes(y[:, nb:], 0, 1)
    return (yf[:, c_len:], jnp.flip(yb[:, c_len:], axis=1), yf[:, :c_len], jnp.flip(yb[:, :c_len], axis=1))


def kernel(x, c, ctx, c_ctx, w_mod, b_mod, norm1_g, norm2_g, w_in, w_out, sgu_w, sgu_b, s5_lambda_re, s5_lambda_im, s5_b_re, s5_b_im, s5_c_re, s5_c_im, s5_log_step, s5_d, s5_w_glu, gla_w_gate, gla_b_gate, gla_norm_g, peer_w_query, peer_sub_keys, peer_expert_u, peer_expert_v, final_norm_g):
    nb, seq, d = x.shape
    c_len = ctx.shape[1]
    depth = w_mod.shape[0]

    cc = jnp.concatenate([c, c_ctx[None, :], jnp.zeros((8 - nb - 1, d), F32)], axis=0)
    mods = _mod_call(cc, w_mod, b_mod)

    ones_sgu = _block_ones(SGU_DIM, SGU_HEAD_DIM)
    ones_gla = _block_ones(GLA_DIM, GLA_DV)
    tri_np = np.tril(np.ones((GLA_CHUNK, GLA_CHUNK), np.float32))
    trif = jnp.asarray(tri_np, dtype=BF16)
    trib = jnp.asarray(tri_np.T, dtype=BF16)
    s_zero = jnp.zeros((nb, 2, GLA_DIM, GLA_KEY_DIM), F32)

    xl, xc = x, ctx
    for l in range(depth):
        ctx_out = l < depth - 1
        m6 = mods[l].reshape(8, N_MOD, d)
        mod_l = jnp.pad(m6[:nb], ((0, 0), (0, 2), (0, 0)))
        mod_c = jnp.broadcast_to(jnp.pad(m6[nb], ((0, 2), (0, 0)))[None], (nb, 8, d))

        win = jnp.pad(w_in[l], ((0, 0), (0, IN_PAD - IN_WIDTH))).astype(BF16)
        sguw = sgu_w[l].astype(BF16)
        sgub = jnp.repeat(jnp.swapaxes(sgu_b[l], 0, 1), SGU_HEAD_DIM, axis=1)
        gw = jnp.zeros((LANE, 2 * GLA_KEY_DIM), F32)
        gw = gw.at[0:GLA_RANK, 0:GLA_KEY_DIM].set(gla_w_gate[l, 0])
        gw = gw.at[GLA_RANK:2 * GLA_RANK, GLA_KEY_DIM:].set(gla_w_gate[l, 1]).astype(BF16)
        gb = gla_b_gate[l].reshape(1, 2 * GLA_KEY_DIM)
        g1 = norm1_g[l].reshape(1, d)

        a_l = _stage_a(xl, mod_l, g1, win, sguw, sgub, ones_sgu, gw, gb, tb=256)
        a_c = _stage_a(xc, mod_c, g1, win, sguw, sgub, ones_sgu, gw, gb, tb=256)
        sgu_l, s5x_l, qk_l, v_l, g_l, la_l = a_l
        sgu_c, s5x_c, qk_c, v_c, g_c, la_c = a_c

        b2, ar, ai, c2 = _s5_params(s5_lambda_re[l], s5_lambda_im[l], s5_b_re[l], s5_b_im[l],
                                    s5_c_re[l], s5_c_im[l], s5_log_step[l], nb)
        yseq = _s5_call(_s5_sequences(s5x_c, s5x_l), b2, ar, ai, c2)
        yf_l, yb_l, yf_c, yb_c = _s5_unpack(yseq, nb, c_len)

        of_c, ob_c, s_ctx = _gla_call(qk_c, v_c, la_c, s_zero, trif, trib, grid_order=False)
        of_l, ob_l, _ = _gla_call(qk_l, v_l, la_l, s_ctx, trif, trib, grid_order=True)

        dskip = s5_d[l].reshape(1, S5_DIM)
        wglu = s5_w_glu[l].astype(BF16)
        normg = gla_norm_g[l].reshape(1, GLA_DIM)
        wout = w_out[l].astype(BF16)
        g2 = norm2_g[l].reshape(1, d)
        wqt = jnp.swapaxes(peer_w_query[l], 0, 1).astype(BF16)
        keys = peer_sub_keys[l].reshape(2 * PEER_HEADS, PEER_NKEYS, PEER_HALF).astype(BF16)
        u_bf = peer_expert_u[l].astype(BF16)
        vt_bf = jnp.swapaxes(peer_expert_v[l], 0, 1).astype(BF16)

        xl = _stage_e(xl, mod_l, sgu_l, yf_l, yb_l, s5x_l, of_l, ob_l, g_l,
                      dskip, wglu, normg, ones_gla, wout, tb=256)
        desc = _stage_f(xl, mod_l, g2, wqt, keys, tb=PEER_TBF)
        xl = _stage_g(*desc, u_bf, vt_bf, xl.reshape(nb * seq, d), mod_l, seq, PEER_TBG).reshape(nb, seq, d)

        if ctx_out:
            xc = _stage_e(xc, mod_c, sgu_c, yf_c, yb_c, s5x_c, of_c, ob_c, g_c,
                          dskip, wglu, normg, ones_gla, wout, tb=256)
            desc = _stage_f(xc, mod_c, g2, wqt, keys, tb=PEER_TBF)
            xc = _stage_g(*desc, u_bf, vt_bf, xc.reshape(nb * c_len, d), mod_c, c_len,
                          min(PEER_TBG, c_len)).reshape(nb, c_len, d)

    return _final_norm(xl.reshape(nb * seq, d), final_norm_g.reshape(1, d)).reshape(nb, seq, d)
```

```python
import functools
import math

import numpy as np
import jax
import jax.numpy as jnp
from jax import lax
from jax.experimental import pallas as pl
from jax.experimental.pallas import tpu as pltpu

F32 = jnp.float32
BF16 = jnp.bfloat16

EPS = 1e-6
N_MOD = 6
GRID_W = 64

SGU_DIM = 256
SGU_HEADS = 4
SGU_HEAD_DIM = 64
SGU_CHUNK = 128

S5_DIM = 256
S5_GROUP = 16
S5_GROUPS = 16
S5_STATE = 64
S5_LANES = S5_GROUPS * S5_STATE

GLA_DIM = 512
GLA_HEADS = 8
GLA_DV = 64
GLA_DK = 32
GLA_KEY_DIM = 256
GLA_RANK = 16
GLA_GATE_TEMP = 16.0
GLA_CHUNK = 64

PEER_HEADS = 8
PEER_NKEYS = 128
PEER_HALF = 128
PEER_TOPK = 16

IN_WIDTH = 2336
IN_PAD = 2432
LANE = 128

VMEM_LIMIT = 56 * 1024 * 1024
TB_LATENT = 512
TB_CTX = 256

NEG_INF = float("-inf")
POS_INF = float("inf")


def _cparams(sem):
    return pltpu.CompilerParams(dimension_semantics=sem, vmem_limit_bytes=VMEM_LIMIT)


def _gelu(x):
    c = math.sqrt(2.0 / math.pi)
    return 0.5 * x * (1.0 + jnp.tanh(c * (x + 0.044715 * (x * x * x))))


def _sigmoid(x):
    return 1.0 / (1.0 + jnp.exp(-x))


def _dot(a, b):
    return jnp.dot(a, b, preferred_element_type=F32)


def _dot_nt(a, b):
    return lax.dot_general(a, b, (((1,), (1,)), ((), ())), preferred_element_type=F32)


def _dot_tn(a, b):
    return lax.dot_general(a, b, (((0,), (0,)), ((), ())), preferred_element_type=F32)


def _split3(x):
    hi = x.astype(BF16)
    r = x - hi.astype(F32)
    mid = r.astype(BF16)
    lo = (r - mid.astype(F32)).astype(BF16)
    return hi, mid, lo


def _dot_x_exact(x, m):
    hi, mid, lo = _split3(x)
    return _dot(hi, m) + _dot(mid, m) + _dot(lo, m)


def _dot_m_exact(m, x):
    hi, mid, lo = _split3(x)
    return _dot(m, hi) + _dot(m, mid) + _dot(m, lo)


def _full_spec(shape):
    nd = len(shape)
    return pl.BlockSpec(shape, lambda *_: (0,) * nd)


MOD_TILE = 512


def _mod_kernel(c_ref, w_ref, b_ref, o_ref):
    c = c_ref[...]
    a = c * _sigmoid(c)
    o_ref[0] = jnp.dot(a, w_ref[0], preferred_element_type=F32,
                       precision=lax.Precision.HIGHEST) + b_ref[0]


def _mod_call(cc, w_mod, b_mod):
    depth, d, nd = w_mod.shape
    rows = cc.shape[0]
    return pl.pallas_call(
        _mod_kernel,
        grid=(depth, nd // MOD_TILE),
        in_specs=[
            pl.BlockSpec((rows, d), lambda l, j: (0, 0)),
            pl.BlockSpec((1, d, MOD_TILE), lambda l, j: (l, 0, j)),
            pl.BlockSpec((1, 1, MOD_TILE), lambda l, j: (l, 0, j)),
        ],
        out_specs=pl.BlockSpec((1, rows, MOD_TILE), lambda l, j: (l, 0, j)),
        out_shape=jax.ShapeDtypeStruct((depth, rows, nd), F32),
        compiler_params=_cparams(("parallel", "parallel")),
    )(cc, w_mod, b_mod.reshape(depth, 1, nd))


def _store_tokens(ref, val, grid_layout):
    if grid_layout:
        for r in range(val.shape[0] // GRID_W):
            ref[0, :, r, :] = val[r * GRID_W:(r + 1) * GRID_W]
    else:
        ref[0] = val


def _load_tokens(ref, grid_layout):
    if grid_layout:
        return jnp.concatenate([ref[0, :, r, :] for r in range(ref.shape[2])], axis=0)
    return ref[0]


def _stage_a_kernel(x_ref, mod_ref, g1_ref, win_ref, sguw_ref, sgub_ref, ones_ref, gw_ref, gb_ref, s5_in_ref,
                    sgu_ref, s5x_ref, qk_ref, v_ref, g_ref, la_ref, *, tb, grid_layout):
    del s5_in_ref
    x = x_ref[0]
    ms = jnp.mean(x * x, axis=-1, keepdims=True)
    xn = x * lax.rsqrt(ms + EPS) * g1_ref[...]
    h = xn * (1.0 + mod_ref[0, 1:2, :]) + mod_ref[0, 0:1, :]
    cols = _dot(h.astype(BF16), win_ref[...])

    u = _gelu(cols[:, 0:SGU_DIM])
    v = _gelu(cols[:, SGU_DIM:2 * SGU_DIM])
    msq = _dot_x_exact(v * v, ones_ref[...]) * (1.0 / SGU_HEAD_DIM)
    vn = (v * lax.rsqrt(msq + EPS)).astype(BF16)
    head_of_lane = lax.broadcasted_iota(jnp.int32, (SGU_CHUNK, SGU_DIM), 1) // SGU_HEAD_DIM
    for ci in range(tb // SGU_CHUNK):
        rows = slice(ci * SGU_CHUNK, (ci + 1) * SGU_CHUNK)
        vc = vn[rows]
        mixed = sgub_ref[...]
        for hh in range(SGU_HEADS):
            mixed = mixed + jnp.where(head_of_lane == hh, _dot(sguw_ref[hh], vc), 0.0)
        sgu_ref[0, rows, :] = u[rows] * mixed

    s5x_ref[0] = cols[:, 512:768]
    q = cols[:, 768:1024] * (GLA_DK ** -0.5)
    _store_tokens(qk_ref, jnp.concatenate([q, cols[:, 1024:1280]], axis=-1), grid_layout)
    _store_tokens(v_ref, cols[:, 1280:1792], grid_layout)
    g_ref[0] = cols[:, 1792:2304]

    z = cols[:, 2304:2432].astype(BF16)
    za = _dot(z, gw_ref[...]) + gb_ref[...]
    log_sig = jnp.minimum(za, 0.0) - jnp.log1p(jnp.exp(-jnp.abs(za)))
    _store_tokens(la_ref, log_sig * (1.0 / GLA_GATE_TEMP), grid_layout)


def _stage_a(xs, mod, g1, win, sguw, sgub, ones_sgu, gw, gb, s5_all, s5_row0, tb, grid_layout):
    b, l, d = xs.shape
    assert s5_row0 % tb == 0
    kern = functools.partial(_stage_a_kernel, tb=tb, grid_layout=grid_layout)
    tok = lambda w: pl.BlockSpec((1, tb, w), lambda bi, i: (bi, i, 0))
    if grid_layout:
        assert tb % GRID_W == 0 and l % GRID_W == 0
        gla = lambda w: pl.BlockSpec((1, GRID_W, tb // GRID_W, w), lambda bi, i: (bi, 0, i, 0))
        gla_shape = lambda w: jax.ShapeDtypeStruct((b, GRID_W, l // GRID_W, w), F32)
    else:
        gla = tok
        gla_shape = lambda w: jax.ShapeDtypeStruct((b, l, w), F32)
    nat_shape = lambda w: jax.ShapeDtypeStruct((b, l, w), F32)
    return pl.pallas_call(
        kern,
        grid=(b, l // tb),
        in_specs=[
            tok(d),
            pl.BlockSpec((1, 8, d), lambda bi, i: (bi, 0, 0)),
            _full_spec(g1.shape), _full_spec(win.shape), _full_spec(sguw.shape), _full_spec(sgub.shape),
            _full_spec(ones_sgu.shape), _full_spec(gw.shape), _full_spec(gb.shape),
            pl.BlockSpec(memory_space=pl.ANY),
        ],
        out_specs=[tok(SGU_DIM),
                   pl.BlockSpec((1, tb, S5_DIM), lambda bi, i: (bi, s5_row0 // tb + i, 0)),
                   gla(2 * GLA_KEY_DIM), gla(GLA_DIM), tok(GLA_DIM), gla(2 * GLA_KEY_DIM)],
        out_shape=[nat_shape(SGU_DIM), jax.ShapeDtypeStruct(s5_all.shape, F32),
                   gla_shape(2 * GLA_KEY_DIM), gla_shape(GLA_DIM), nat_shape(GLA_DIM), gla_shape(2 * GLA_KEY_DIM)],
        input_output_aliases={9: 1},
        compiler_params=_cparams(("parallel", "parallel")),
    )(xs, mod, g1, win, sguw, sgub, ones_sgu, gw, gb, s5_all)


S5_TC = 128


def _s5_kernel(uf_ref, ub_ref, perm_ref, permt_ref, b2_ref, ar_ref, ai_ref, c2_ref, yf_ref, yb_ref,
               h_ref, buf_ref, *, nseq):
    tc = S5_TC
    rows = tc * 2 * nseq

    @pl.when(pl.program_id(0) == 0)
    def _():
        h_ref[...] = jnp.zeros_like(h_ref)

    x = jnp.concatenate([uf_ref[b] for b in range(nseq)] + [ub_ref[b] for b in range(nseq)], axis=0)
    u_tm = _dot(perm_ref[...], x.astype(BF16))
    fwd_row = lax.broadcasted_iota(jnp.int32, (rows, S5_DIM), 0) % (2 * nseq) < nseq
    u = jnp.concatenate([jnp.where(fwd_row, u_tm, 0.0), jnp.where(fwd_row, 0.0, u_tm)], axis=-1).astype(BF16)
    buf_ref[...] = _dot(u, b2_ref[...]).reshape(tc, 2 * nseq, 2 * S5_LANES)
    ar = ar_ref[...]
    ai = ai_ref[...]

    def step(t, carry):
        hr, hi = carry
        bur = buf_ref[t, :, 0:S5_LANES]
        bui = buf_ref[t, :, S5_LANES:]
        nhr = ar * hr - ai * hi + bur
        nhi = ar * hi + ai * hr + bui
        buf_ref[t, :, 0:S5_LANES] = nhr
        buf_ref[t, :, S5_LANES:] = nhi
        return nhr, nhi

    hr, hi = lax.fori_loop(0, tc, step, (h_ref[:, 0:S5_LANES], h_ref[:, S5_LANES:]))
    h_ref[:, 0:S5_LANES] = hr
    h_ref[:, S5_LANES:] = hi

    hs = buf_ref[...].reshape(rows, 2 * S5_LANES).astype(BF16)
    y2 = _dot(hs, c2_ref[...])
    y = jnp.where(fwd_row, y2[:, 0:S5_DIM], y2[:, S5_DIM:])
    y_hi = y.astype(BF16)
    y_lo = (y - y_hi.astype(F32)).astype(BF16)
    y_nat = _dot(permt_ref[...], y_hi) + _dot(permt_ref[...], y_lo)
    for b in range(nseq):
        yf_ref[b] = y_nat[b * tc:(b + 1) * tc]
        yb_ref[b] = y_nat[(nseq + b) * tc:(nseq + b + 1) * tc]


def _s5_permutation(nseq):
    tc = S5_TC
    p = np.zeros((tc * 2 * nseq, tc * 2 * nseq), np.float32)
    for q in range(2 * nseq):
        for t in range(tc):
            p[t * 2 * nseq + q, q * tc + (t if q < nseq else tc - 1 - t)] = 1.0
    return p


def _s5_call(s5_all, n_lat, b2, ar, ai, c2):
    nseq, t, _ = s5_all.shape
    n = t // S5_TC
    n_l = n_lat // S5_TC
    n_c = n - n_l
    perm = _s5_permutation(nseq)
    kern = functools.partial(_s5_kernel, nseq=nseq)
    fwd_map = lambda s: (0, jnp.where(s < n_c, n_l + s, s - n_c), 0)
    bwd_map = lambda s: (0, n - 1 - s, 0)
    blk = (nseq, S5_TC, S5_DIM)
    rows = 2 * nseq
    return pl.pallas_call(
        kern,
        grid=(n,),
        in_specs=[
            pl.BlockSpec(blk, fwd_map), pl.BlockSpec(blk, bwd_map),
            _full_spec(perm.shape), _full_spec(perm.shape),
            _full_spec(b2.shape), _full_spec(ar.shape), _full_spec(ai.shape), _full_spec(c2.shape),
        ],
        out_specs=[pl.BlockSpec(blk, fwd_map), pl.BlockSpec(blk, bwd_map)],
        out_shape=[jax.ShapeDtypeStruct(s5_all.shape, F32), jax.ShapeDtypeStruct(s5_all.shape, F32)],
        scratch_shapes=[
            pltpu.VMEM((rows, 2 * S5_LANES), F32),
            pltpu.VMEM((S5_TC, rows, 2 * S5_LANES), F32),
        ],
        compiler_params=_cparams(("arbitrary",)),
    )(s5_all, s5_all, jnp.asarray(perm, dtype=BF16), jnp.asarray(perm.T, dtype=BF16), b2, ar, ai, c2)


def _gla_stream(q, k, v, la, s_t, tri, last_row, ref_row, tri_mask, hm_k, hm_v, hm_s):
    bcum = _dot_m_exact(tri, la)
    blast = bcum[last_row:last_row + 1]
    bref = bcum[ref_row:ref_row + 1]
    qe = q * jnp.exp(bcum)
    qd = q * jnp.exp(bcum - bref)
    kd = k * jnp.exp(bref - bcum)
    kdec = k * jnp.exp(blast - bcum)
    kst = jnp.where(hm_k, jnp.concatenate([kd] * GLA_HEADS, axis=0), 0.0).astype(BF16)
    sc = jnp.where(tri_mask, _dot_nt(qd.astype(BF16), kst), 0.0)
    vbd = jnp.where(hm_v, jnp.concatenate([v] * GLA_HEADS, axis=0), 0.0).astype(BF16)
    o = _dot(sc.astype(BF16), vbd) + _dot_nt(qe.astype(BF16), s_t.astype(BF16))
    kv_t = jnp.where(hm_s, _dot_tn(v.astype(BF16), kdec.astype(BF16)), 0.0)
    s_new = s_t * jnp.exp(blast) + kv_t
    return o, s_new


def _gla_kernel(qkf_ref, vf_ref, laf_ref, qkb_ref, vb_ref, lab_ref, s0_ref, trif_ref, trib_ref,
                of_ref, ob_ref, sout_ref, s_ref, *, nb):
    c = pl.program_id(0)

    @pl.when(c == 0)
    def _():
        s_ref[...] = s0_ref[...]

    ch = GLA_CHUNK
    r_k = lax.broadcasted_iota(jnp.int32, (GLA_HEADS * ch, GLA_KEY_DIM), 0) // ch
    c_k = lax.broadcasted_iota(jnp.int32, (GLA_HEADS * ch, GLA_KEY_DIM), 1) // GLA_DK
    hm_k = r_k == c_k
    r_v = lax.broadcasted_iota(jnp.int32, (GLA_HEADS * ch, GLA_DIM), 0) // ch
    c_v = lax.broadcasted_iota(jnp.int32, (GLA_HEADS * ch, GLA_DIM), 1) // GLA_DV
    hm_v = r_v == c_v
    r_s = lax.broadcasted_iota(jnp.int32, (GLA_DIM, GLA_KEY_DIM), 0) // GLA_DV
    c_s = lax.broadcasted_iota(jnp.int32, (GLA_DIM, GLA_KEY_DIM), 1) // GLA_DK
    hm_s = r_s == c_s
    t_i = lax.broadcasted_iota(jnp.int32, (ch, GLA_HEADS * ch), 0)
    s_i = lax.broadcasted_iota(jnp.int32, (ch, GLA_HEADS * ch), 1) % ch
    mask_f = t_i >= s_i
    mask_b = t_i <= s_i
    trif = trif_ref[...]
    trib = trib_ref[...]

    def body(b, carry):
        qk = qkf_ref[b]
        o, s_new = _gla_stream(qk[:, 0:GLA_KEY_DIM], qk[:, GLA_KEY_DIM:], vf_ref[b], laf_ref[b], s_ref[b, 0],
                               trif, ch - 1, ch // 2, mask_f, hm_k, hm_v, hm_s)
        of_ref[b] = o
        s_ref[b, 0] = s_new
        qk = qkb_ref[b]
        o, s_new = _gla_stream(qk[:, 0:GLA_KEY_DIM], qk[:, GLA_KEY_DIM:], vb_ref[b], lab_ref[b], s_ref[b, 1],
                               trib, 0, ch - 1 - ch // 2, mask_b, hm_k, hm_v, hm_s)
        ob_ref[b] = o
        s_ref[b, 1] = s_new
        return carry

    lax.fori_loop(0, nb, body, 0)

    @pl.when(c == pl.num_programs(0) - 1)
    def _():
        sout_ref[...] = s_ref[...]


def _gla_call(qk, v, la, s0, trif, trib):
    b = qk.shape[0]
    ch = GLA_CHUNK
    if qk.ndim == 4:
        n = qk.shape[1]
        assert qk.shape[2] == ch
        spec = lambda w, off, rev: pl.BlockSpec(
            (b, None, ch, w), (lambda c: (0, n - 1 - c, 0, off)) if rev else (lambda c: (0, c, 0, off)))
    else:
        n = qk.shape[1] // ch
        spec = lambda w, off, rev: pl.BlockSpec(
            (b, ch, w), (lambda c: (0, n - 1 - c, off)) if rev else (lambda c: (0, c, off)))
    kern = functools.partial(_gla_kernel, nb=b)
    o_shape = jax.ShapeDtypeStruct(v.shape, F32)
    return pl.pallas_call(
        kern,
        grid=(n,),
        in_specs=[
            spec(2 * GLA_KEY_DIM, 0, False), spec(GLA_DIM, 0, False), spec(GLA_KEY_DIM, 0, False),
            spec(2 * GLA_KEY_DIM, 0, True), spec(GLA_DIM, 0, True), spec(GLA_KEY_DIM, 1, True),
            _full_spec(s0.shape), _full_spec(trif.shape), _full_spec(trib.shape),
        ],
        out_specs=[spec(GLA_DIM, 0, False), spec(GLA_DIM, 0, True), _full_spec(s0.shape)],
        out_shape=[o_shape, o_shape, jax.ShapeDtypeStruct(s0.shape, F32)],
        scratch_shapes=[pltpu.VMEM(s0.shape, F32)],
        compiler_params=_cparams(("arbitrary",)),
    )(qk, v, la, qk, v, la, s0, trif, trib)


def _stage_e_kernel(x_ref, mod_ref, sgu_ref, yf_ref, yb_ref, s5x_ref, of_ref, ob_ref, g_ref,
                    dskip_ref, wglu_ref, normg_ref, ones_ref, wout_ref, o_ref, *, grid_layout):
    ys = yf_ref[0] + yb_ref[0] + dskip_ref[...] * s5x_ref[0]
    z = _dot(_gelu(ys).astype(BF16), wglu_ref[...])
    s5o = z[:, 0:S5_DIM] * _sigmoid(z[:, S5_DIM:])
    o = _load_tokens(of_ref, grid_layout) + _load_tokens(ob_ref, grid_layout)
    ms = _dot_x_exact(o * o, ones_ref[...]) * (1.0 / GLA_DV)
    g = g_ref[0]
    gl = o * lax.rsqrt(ms + EPS) * normg_ref[...] * (g * _sigmoid(g))
    y = (_dot(sgu_ref[0].astype(BF16), wout_ref[0:SGU_DIM, :])
         + _dot(s5o.astype(BF16), wout_ref[SGU_DIM:SGU_DIM + S5_DIM, :])
         + _dot(gl.astype(BF16), wout_ref[SGU_DIM + S5_DIM:, :]))
    o_ref[0] = x_ref[0] + mod_ref[0, 2:3, :] * y


def _stage_e(xs, mod, sgu, yf_all, yb_all, s5_all, s5_row0, of, ob, g, dskip, wglu, normg, ones_gla, wout, tb):
    b, l, d = xs.shape
    assert s5_row0 % tb == 0
    grid_layout = of.ndim == 4
    tok = lambda w: pl.BlockSpec((1, tb, w), lambda bi, i: (bi, i, 0))
    s5 = pl.BlockSpec((1, tb, S5_DIM), lambda bi, i: (bi, s5_row0 // tb + i, 0))
    if grid_layout:
        assert tb % GRID_W == 0
        gla = pl.BlockSpec((1, GRID_W, tb // GRID_W, GLA_DIM), lambda bi, i: (bi, 0, i, 0))
    else:
        gla = tok(GLA_DIM)
    return pl.pallas_call(
        functools.partial(_stage_e_kernel, grid_layout=grid_layout),
        grid=(b, l // tb),
        in_specs=[
            tok(d), pl.BlockSpec((1, 8, d), lambda bi, i: (bi, 0, 0)),
            tok(SGU_DIM), s5, s5, s5, gla, gla, tok(GLA_DIM),
            _full_spec(dskip.shape), _full_spec(wglu.shape), _full_spec(normg.shape),
            _full_spec(ones_gla.shape), _full_spec(wout.shape),
        ],
        out_specs=tok(d),
        out_shape=jax.ShapeDtypeStruct((b, l, d), F32),
        compiler_params=_cparams(("parallel", "parallel")),
    )(xs, mod, sgu, yf_all, yb_all, s5_all, of, ob, g, dskip, wglu, normg, ones_gla, wout)


PEER_TBF = 256
SUBLANES = 8


def _sort_network_16():
    def merge(lo, hi, r):
        step = r * 2
        if step < hi - lo:
            yield from merge(lo, hi, step)
            yield from merge(lo + r, hi, step)
            for i in range(lo + r, hi - r, step):
                yield (i, i + r)
        else:
            yield (lo, lo + r)

    def sort(lo, hi):
        if hi - lo >= 1:
            mid = lo + (hi - lo) // 2
            yield from sort(lo, mid)
            yield from sort(mid + 1, hi)
            yield from merge(lo, hi, 1)

    return tuple(sort(0, PEER_TOPK - 1))


SORT16 = _sort_network_16()
BITONIC16 = tuple((k, k + s) for s in (8, 4, 2, 1) for k in range(PEER_TOPK) if not k & s)


def _compare_exchange(xs, pairs):
    xs = list(xs)
    for i, j in pairs:
        hi = jnp.maximum(xs[i], xs[j])
        lo = jnp.minimum(xs[i], xs[j])
        xs[i], xs[j] = hi, lo
    return xs


def _merge_sublanes(xs):
    for shift in (4, 6, 7):
        rolled = [pltpu.roll(x, shift, 0) for x in xs]
        xs = [jnp.maximum(xs[k], rolled[PEER_TOPK - 1 - k]) for k in range(PEER_TOPK)]
        xs = _compare_exchange(xs, BITONIC16)
    return xs


def _dup_bf16_words(x):
    bits = pltpu.bitcast(x.astype(BF16).astype(F32), jnp.int32)
    return bits | lax.shift_right_logical(bits, 16)


def _stage_f_kernel(x_ref, mod_ref, g2_ref, wqt_ref, keys_ref,
                    ht_ref, e1w_ref, n1w_ref, e2_ref, r2_ref, qt_ref, v1_ref, v2_ref, *, tb):
    x = x_ref[0]
    ms = jnp.mean(x * x, axis=-1, keepdims=True)
    xn = x * lax.rsqrt(ms + EPS) * g2_ref[...]
    h = xn * (1.0 + mod_ref[0, 4:5, :]) + mod_ref[0, 3:4, :]
    ht = h.T.astype(BF16)
    ht_ref[...] = ht
    qt_ref[...] = _dot(wqt_ref[...], ht)
    k_top = PEER_TOPK

    def tiles(s):
        return [s[SUBLANES * k:SUBLANES * (k + 1)] for k in range(PEER_NKEYS // SUBLANES)]

    def head_body(hh, carry):
        for tc in range(tb // LANE):
            tcol = slice(tc * LANE, (tc + 1) * LANE)
            r1 = pl.multiple_of(hh * (2 * PEER_HALF), 2 * PEER_HALF)
            q1 = qt_ref[pl.ds(r1, PEER_HALF), tcol].astype(BF16)
            q2 = qt_ref[pl.ds(r1 + PEER_HALF, PEER_HALF), tcol].astype(BF16)
            s1 = _dot(keys_ref[hh], q1)
            s2 = _dot(keys_ref[PEER_HEADS + hh], q2)
            for s, v_ref in ((s1, v1_ref), (s2, v2_ref)):
                top = _merge_sublanes(_compare_exchange(tiles(s), SORT16))
                for k in range(k_top):
                    v_ref[k:k + 1, tcol] = top[k][0:1]
            v1row = lambda a: v1_ref[a:a + 1, tcol]
            v2row = lambda b: v2_ref[b:b + 1, tcol]
            v1lo = v1_ref[0:SUBLANES, tcol]
            v2lo = v2_ref[0:SUBLANES, tcol]
            cand = [v1lo + v2row(b) for b in range(k_top)]
            tail = [v1row(a) + v2lo for a in range(SUBLANES, k_top)]
            for k in range(SUBLANES, k_top):
                cand[k] = jnp.maximum(cand[k], tail[k_top - 1 - k])
            best = _merge_sublanes(_compare_exchange(cand, BITONIC16))
            theta = best[k_top - 1][0:1]
            cmax = best[0][0:1]
            zsum = jnp.zeros((1, LANE), F32)
            for k in range(k_top):
                zsum = zsum + jnp.exp(best[k][0:1] - cmax)
            rz = 1.0 / zsum
            n_top = jnp.zeros((1, LANE), F32)
            for b in range(k_top):
                n_top = jnp.where(v1row(0) + v2row(b) >= theta, float(b + 1), n_top)
            n1 = jnp.zeros(s1.shape, F32)
            for b in range(SUBLANES):
                n1 = jnp.where(s1 + v2row(b) >= theta, float(b + 1), n1)
            n1 = jnp.where(s1 >= v1row(0), n_top, n1)
            r2 = jnp.zeros(s2.shape, F32)
            for b in range(k_top):
                r2 = jnp.where(v2row(b) > s2, float(b + 1), r2)
            e1 = jnp.where(s1 >= v1row(k_top - 1), jnp.exp(s1 - v1row(0)), 0.0) * rz
            e2 = jnp.where(s2 >= v2row(k_top - 1), jnp.exp(s2 - v2row(0)), 0.0)
            e1w_ref[tc, hh] = _dup_bf16_words(e1)
            n1w_ref[tc, hh] = _dup_bf16_words(n1)
            e2_ref[tc, hh] = pltpu.bitcast(e2.astype(BF16), jnp.int32)
            r2_ref[tc, hh] = pltpu.bitcast(r2.astype(BF16), jnp.int32)
        return carry

    lax.fori_loop(0, PEER_HEADS, head_body, 0)


def _stage_f(xs, mod, g2, wqt, keys, tb):
    b, l, d = xs.shape
    nblk = l // tb
    ntok = b * l
    nch = ntok // LANE
    kern = functools.partial(_stage_f_kernel, tb=tb)
    row_spec = pl.BlockSpec((tb // LANE, PEER_HEADS, PEER_NKEYS, LANE), lambda bi, i: (bi * nblk + i, 0, 0, 0))
    pair_spec = pl.BlockSpec((tb // LANE, PEER_HEADS, PEER_NKEYS // 2, LANE), lambda bi, i: (bi * nblk + i, 0, 0, 0))
    desc_shape = lambda rows: jax.ShapeDtypeStruct((nch, PEER_HEADS, rows, LANE), jnp.int32)
    return pl.pallas_call(
        kern,
        grid=(b, nblk),
        in_specs=[
            pl.BlockSpec((1, tb, d), lambda bi, i: (bi, i, 0)),
            pl.BlockSpec((1, 8, d), lambda bi, i: (bi, 0, 0)),
            _full_spec(g2.shape), _full_spec(wqt.shape), _full_spec(keys.shape),
        ],
        out_specs=[pl.BlockSpec((d, tb), lambda bi, i: (0, bi * nblk + i)),
                   row_spec, row_spec, pair_spec, pair_spec],
        out_shape=[jax.ShapeDtypeStruct((d, ntok), BF16), desc_shape(PEER_NKEYS), desc_shape(PEER_NKEYS),
                   desc_shape(PEER_NKEYS // 2), desc_shape(PEER_NKEYS // 2)],
        scratch_shapes=[
            pltpu.VMEM((PEER_HEADS * 2 * PEER_HALF, tb), F32),
            pltpu.VMEM((PEER_TOPK, tb), F32),
            pltpu.VMEM((PEER_TOPK, tb), F32),
        ],
        compiler_params=_cparams(("parallel", "parallel")),
    )(xs, mod, g2, wqt, keys)


PEER_TBG = 512
PEER_TE = 1024
PEER_I1_PER_TILE = PEER_TE // PEER_NKEYS
PEER_N_TILES = PEER_NKEYS * PEER_NKEYS // PEER_TE
PEER_CHUNK = 512
PEER_I1_PER_CHUNK = PEER_CHUNK // PEER_NKEYS


def _stage_g_kernel(ht_ref, e1w_ref, n1w_ref, e2_ref, r2_ref, u_ref, vt_ref, x_ref, mod_ref,
                    o_ref, acc_ref, ata_ref, atb_ref, p_ref, *, tb):
    s = pl.program_id(1)

    @pl.when(s == 0)
    def _():
        acc_ref[...] = jnp.zeros_like(acc_ref)
        atb_ref[...] = jnp.zeros_like(atb_ref)

    def step(at_cur_ref, at_next_ref):
        at_next_ref[...] = _dot(u_ref[...], ht_ref[...]).astype(BF16)
        half = PEER_NKEYS // 2
        for c in range(PEER_TE // PEER_CHUNK):
            crow = slice(c * PEER_CHUNK, (c + 1) * PEER_CHUNK)
            i1s = range(c * PEER_I1_PER_CHUNK, (c + 1) * PEER_I1_PER_CHUNK)
            for tc in range(tb // LANE):
                tcol = slice(tc * LANE, (tc + 1) * LANE)
                for hf in range(2):
                    wrows = slice(hf * half // 2, (hf + 1) * half // 2)
                    gates = [jnp.zeros((half, LANE), BF16) for _ in i1s]
                    for hh in range(PEER_HEADS):
                        r2 = pltpu.bitcast(r2_ref[tc, hh, wrows, :], BF16)
                        e2 = pltpu.bitcast(e2_ref[tc, hh, wrows, :], BF16)
                        for k, i1l in enumerate(i1s):
                            e1row = e1w_ref[tc, hh, i1l:i1l + 1, :]
                            n1row = n1w_ref[tc, hh, i1l:i1l + 1, :]
                            e1 = pltpu.bitcast(jnp.broadcast_to(e1row, (half // 2, LANE)), BF16)
                            n1 = pltpu.bitcast(jnp.broadcast_to(n1row, (half // 2, LANE)), BF16)
                            gates[k] = gates[k] + e2 * jnp.where(r2 < n1, e1, 0.0)
                    for k, i1l in enumerate(i1s):
                        rows = slice(i1l * PEER_NKEYS + hf * half, i1l * PEER_NKEYS + (hf + 1) * half)
                        p_ref[rows, tcol] = gates[k] * _gelu(at_cur_ref[rows, tcol])
            acc_ref[...] += _dot(vt_ref[:, crow], p_ref[crow, :])

    @pl.when(s % 2 == 0)
    def _():
        step(atb_ref, ata_ref)

    @pl.when(s % 2 == 1)
    def _():
        step(ata_ref, atb_ref)

    @pl.when(s == pl.num_programs(1) - 1)
    def _():
        o_ref[...] = x_ref[...] + mod_ref[0, 5:6, :] * acc_ref[...].T


def _stage_g(ht, e1w, n1w, e2, r2, u_bf, vt_bf, xflat, mod, tokens_per_batch, tb):
    d, ntok = ht.shape
    assert u_bf.shape[0] == PEER_N_TILES * PEER_TE
    blocks_per_batch = tokens_per_batch // tb
    kern = functools.partial(_stage_g_kernel, tb=tb)
    last = PEER_N_TILES - 1
    desc_spec = pl.BlockSpec((tb // LANE, PEER_HEADS, PEER_NKEYS // 2, LANE), lambda j, i: (j, 0, 0, 0))
    row_spec = pl.BlockSpec((tb // LANE, PEER_HEADS, PEER_I1_PER_TILE, LANE),
                            lambda j, i: (j, 0, jnp.maximum(i - 1, 0), 0))
    return pl.pallas_call(
        kern,
        grid=(ntok // tb, PEER_N_TILES + 1),
        in_specs=[
            pl.BlockSpec((d, tb), lambda j, i: (0, j)),
            row_spec, row_spec, desc_spec, desc_spec,
            pl.BlockSpec((PEER_TE, d), lambda j, i: (jnp.minimum(i, last), 0)),
            pl.BlockSpec((d, PEER_TE), lambda j, i: (0, jnp.maximum(i - 1, 0))),
            pl.BlockSpec((tb, d), lambda j, i: (j, 0)),
            pl.BlockSpec((1, 8, d), lambda j, i: (j // blocks_per_batch, 0, 0)),
        ],
        out_specs=pl.BlockSpec((tb, d), lambda j, i: (j, 0)),
        out_shape=jax.ShapeDtypeStruct((ntok, d), F32),
        scratch_shapes=[
            pltpu.VMEM((d, tb), F32),
            pltpu.VMEM((PEER_TE, tb), BF16),
            pltpu.VMEM((PEER_TE, tb), BF16),
            pltpu.VMEM((PEER_TE, tb), BF16),
        ],
        compiler_params=_cparams(("parallel", "arbitrary")),
    )(ht, e1w, n1w, e2, r2, u_bf, vt_bf, xflat, mod)


def _final_norm_kernel(x_ref, g_ref, o_ref):
    x = x_ref[...]
    ms = jnp.mean(x * x, axis=-1, keepdims=True)
    o_ref[...] = x * lax.rsqrt(ms + EPS) * g_ref[...]


def _final_norm(xflat, g, tb=512):
    n, d = xflat.shape
    return pl.pallas_call(
        _final_norm_kernel,
        grid=(n // tb,),
        in_specs=[pl.BlockSpec((tb, d), lambda i: (i, 0)), _full_spec(g.shape)],
        out_specs=pl.BlockSpec((tb, d), lambda i: (i, 0)),
        out_shape=jax.ShapeDtypeStruct((n, d), F32),
        compiler_params=_cparams(("parallel",)),
    )(xflat, g)


def _block_ones(n, blk):
    idx = np.arange(n) // blk
    return jnp.asarray((idx[:, None] == idx[None, :]).astype(np.float32), dtype=BF16)


def _s5_discretise(lam_re, lam_im, b_re, b_im, log_step):
    lam_re = jnp.minimum(lam_re.astype(F32), -1e-4)
    lam_im = lam_im.astype(F32)
    dt = jnp.exp(log_step.astype(F32))[:, None]
    mag = jnp.exp(lam_re * dt)
    a_re = mag * jnp.cos(lam_im * dt)
    a_im = mag * jnp.sin(lam_im * dt)
    den = lam_re * lam_re + lam_im * lam_im
    f_re = ((a_re - 1.0) * lam_re + a_im * lam_im) / den
    f_im = (a_im * lam_re - (a_re - 1.0) * lam_im) / den
    b_re = b_re.astype(F32)
    b_im = b_im.astype(F32)
    bb_re = f_re[..., None] * b_re - f_im[..., None] * b_im
    bb_im = f_re[..., None] * b_im + f_im[..., None] * b_re
    return a_re, a_im, bb_re, bb_im


def _group_block_diag(t):
    g, r, c = t.shape
    eye = jnp.eye(g, dtype=t.dtype)
    return (t[:, :, None, :] * eye[:, None, :, None]).reshape(g * r, g * c)


def _s5_params(lam_re, lam_im, b_re, b_im, c_re, c_im, log_step, nseq):
    b_rows, c_cols, ars, ais = [], [], [], []
    for d in range(2):
        a_re, a_im, bb_re, bb_im = _s5_discretise(lam_re[d], lam_im[d], b_re[d], b_im[d], log_step[d])
        bm = jnp.concatenate([_group_block_diag(jnp.swapaxes(bb_re, 1, 2)),
                              _group_block_diag(jnp.swapaxes(bb_im, 1, 2))], axis=1)
        b_rows.append(bm)
        cm = jnp.concatenate([_group_block_diag(jnp.swapaxes(c_re[d].astype(F32), 1, 2)),
                              -_group_block_diag(jnp.swapaxes(c_im[d].astype(F32), 1, 2))], axis=0)
        c_cols.append(cm)
        ars.append(jnp.broadcast_to(a_re.reshape(1, S5_LANES), (nseq, S5_LANES)))
        ais.append(jnp.broadcast_to(a_im.reshape(1, S5_LANES), (nseq, S5_LANES)))
    b2 = jnp.concatenate(b_rows, axis=0).astype(BF16)
    c2 = jnp.concatenate(c_cols, axis=1).astype(BF16)
    return b2, jnp.concatenate(ars, axis=0), jnp.concatenate(ais, axis=0), c2


def kernel(x, c, ctx, c_ctx, w_mod, b_mod, norm1_g, norm2_g, w_in, w_out, sgu_w, sgu_b, s5_lambda_re, s5_lambda_im, s5_b_re, s5_b_im, s5_c_re, s5_c_im, s5_log_step, s5_d, s5_w_glu, gla_w_gate, gla_b_gate, gla_norm_g, peer_w_query, peer_sub_keys, peer_expert_u, peer_expert_v, final_norm_g):
    nb, seq, d = x.shape
    c_len = ctx.shape[1]
    depth = w_mod.shape[0]

    cc = jnp.concatenate([c, c_ctx[None, :], jnp.zeros((8 - nb - 1, d), F32)], axis=0)
    mods = _mod_call(cc, w_mod, b_mod)

    ones_sgu = _block_ones(SGU_DIM, SGU_HEAD_DIM)
    ones_gla = _block_ones(GLA_DIM, GLA_DV)
    tri_np = np.tril(np.ones((GLA_CHUNK, GLA_CHUNK), np.float32))
    trif = jnp.asarray(tri_np, dtype=BF16)
    trib = jnp.asarray(tri_np.T, dtype=BF16)
    s_zero = jnp.zeros((nb, 2, GLA_DIM, GLA_KEY_DIM), F32)

    xl, xc = x, ctx
    for l in range(depth):
        ctx_out = l < depth - 1
        m6 = mods[l].reshape(8, N_MOD, d)
        mod_l = jnp.pad(m6[:nb], ((0, 0), (0, 2), (0, 0)))
        mod_c = jnp.broadcast_to(jnp.pad(m6[nb], ((0, 2), (0, 0)))[None], (nb, 8, d))

        win = jnp.pad(w_in[l], ((0, 0), (0, IN_PAD - IN_WIDTH))).astype(BF16)
        sguw = sgu_w[l].astype(BF16)
        sgub = jnp.repeat(jnp.swapaxes(sgu_b[l], 0, 1), SGU_HEAD_DIM, axis=1)
        gw = jnp.zeros((LANE, 2 * GLA_KEY_DIM), F32)
        gw = gw.at[0:GLA_RANK, 0:GLA_KEY_DIM].set(gla_w_gate[l, 0])
        gw = gw.at[GLA_RANK:2 * GLA_RANK, GLA_KEY_DIM:].set(gla_w_gate[l, 1]).astype(BF16)
        gb = gla_b_gate[l].reshape(1, 2 * GLA_KEY_DIM)
        g1 = norm1_g[l].reshape(1, d)

        s5_all = jnp.zeros((nb, seq + c_len, S5_DIM), F32)
        sgu_l, s5_all, qk_l, v_l, g_l, la_l = _stage_a(xl, mod_l, g1, win, sguw, sgub, ones_sgu, gw, gb,
                                                       s5_all, 0, tb=TB_LATENT, grid_layout=True)
        sgu_c, s5_all, qk_c, v_c, g_c, la_c = _stage_a(xc, mod_c, g1, win, sguw, sgub, ones_sgu, gw, gb,
                                                       s5_all, seq, tb=TB_CTX, grid_layout=False)

        b2, ar, ai, c2 = _s5_params(s5_lambda_re[l], s5_lambda_im[l], s5_b_re[l], s5_b_im[l],
                                    s5_c_re[l], s5_c_im[l], s5_log_step[l], nb)
        yf_all, yb_all = _s5_call(s5_all, seq, b2, ar, ai, c2)

        of_c, ob_c, s_ctx = _gla_call(qk_c, v_c, la_c, s_zero, trif, trib)
        of_l, ob_l, _ = _gla_call(qk_l, v_l, la_l, s_ctx, trif, trib)

        dskip = s5_d[l].reshape(1, S5_DIM)
        wglu = s5_w_glu[l].astype(BF16)
        normg = gla_norm_g[l].reshape(1, GLA_DIM)
        wout = w_out[l].astype(BF16)
        g2 = norm2_g[l].reshape(1, d)
        wqt = jnp.swapaxes(peer_w_query[l], 0, 1).astype(BF16)
        keys = peer_sub_keys[l].reshape(2 * PEER_HEADS, PEER_NKEYS, PEER_HALF).astype(BF16)
        u_bf = peer_expert_u[l].astype(BF16)
        vt_bf = jnp.swapaxes(peer_expert_v[l], 0, 1).astype(BF16)

        xl = _stage_e(xl, mod_l, sgu_l, yf_all, yb_all, s5_all, 0, of_l, ob_l, g_l,
                      dskip, wglu, normg, ones_gla, wout, tb=TB_LATENT)
        desc = _stage_f(xl, mod_l, g2, wqt, keys, tb=PEER_TBF)
        xl = _stage_g(*desc, u_bf, vt_bf, xl.reshape(nb * seq, d), mod_l, seq, PEER_TBG).reshape(nb, seq, d)

        if ctx_out:
            xc = _stage_e(xc, mod_c, sgu_c, yf_all, yb_all, s5_all, seq, of_c, ob_c, g_c,
                          dskip, wglu, normg, ones_gla, wout, tb=TB_CTX)
            desc = _stage_f(xc, mod_c, g2, wqt, keys, tb=PEER_TBF)
            xc = _stage_g(*desc, u_bf, vt_bf, xc.reshape(nb * c_len, d), mod_c, c_len,
                          min(PEER_TBG, c_len)).reshape(nb, c_len, d)

    return _final_norm(xl.reshape(nb * seq, d), final_norm_g.reshape(1, d)).reshape(nb, seq, d)
```

```python
import functools
import math

import numpy as np
import jax
import jax.numpy as jnp
from jax import lax
from jax.experimental import pallas as pl
from jax.experimental.pallas import tpu as pltpu

F32 = jnp.float32
BF16 = jnp.bfloat16

EPS = 1e-6
N_MOD = 6
GRID_W = 64

SGU_DIM = 256
SGU_HEADS = 4
SGU_HEAD_DIM = 64
SGU_CHUNK = 128

S5_DIM = 256
S5_GROUP = 16
S5_GROUPS = 16
S5_STATE = 64
S5_LANES = S5_GROUPS * S5_STATE

GLA_DIM = 512
GLA_HEADS = 8
GLA_DV = 64
GLA_DK = 32
GLA_KEY_DIM = 256
GLA_RANK = 16
GLA_GATE_TEMP = 16.0
GLA_CHUNK = 64

PEER_HEADS = 8
PEER_NKEYS = 128
PEER_HALF = 128
PEER_TOPK = 16

IN_WIDTH = 2336
IN_PAD = 2432
LANE = 128

VMEM_LIMIT = 56 * 1024 * 1024
TB_LATENT = 512
TB_CTX = 256

NEG_INF = float("-inf")
POS_INF = float("inf")


def _cparams(sem):
    return pltpu.CompilerParams(dimension_semantics=sem, vmem_limit_bytes=VMEM_LIMIT)


def _gelu(x):
    c = math.sqrt(2.0 / math.pi)
    return 0.5 * x * (1.0 + jnp.tanh(c * (x + 0.044715 * (x * x * x))))


def _sigmoid(x):
    return 1.0 / (1.0 + jnp.exp(-x))


def _dot(a, b):
    return jnp.dot(a, b, preferred_element_type=F32)


def _dot_nt(a, b):
    return lax.dot_general(a, b, (((1,), (1,)), ((), ())), preferred_element_type=F32)


def _dot_tn(a, b):
    return lax.dot_general(a, b, (((0,), (0,)), ((), ())), preferred_element_type=F32)


def _split3(x):
    hi = x.astype(BF16)
    r = x - hi.astype(F32)
    mid = r.astype(BF16)
    lo = (r - mid.astype(F32)).astype(BF16)
    return hi, mid, lo


def _dot_x_exact(x, m):
    hi, mid, lo = _split3(x)
    return _dot(hi, m) + _dot(mid, m) + _dot(lo, m)


def _dot_m_exact(m, x):
    hi, mid, lo = _split3(x)
    return _dot(m, hi) + _dot(m, mid) + _dot(m, lo)


def _full_spec(shape):
    nd = len(shape)
    return pl.BlockSpec(shape, lambda *_: (0,) * nd)


MOD_TILE = 512


def _mod_kernel(c_ref, w_ref, b_ref, o_ref):
    c = c_ref[...]
    a = c * _sigmoid(c)
    o_ref[0] = jnp.dot(a, w_ref[0], preferred_element_type=F32,
                       precision=lax.Precision.HIGHEST) + b_ref[0]


def _mod_call(cc, w_mod, b_mod):
    depth, d, nd = w_mod.shape
    rows = cc.shape[0]
    return pl.pallas_call(
        _mod_kernel,
        grid=(depth, nd // MOD_TILE),
        in_specs=[
            pl.BlockSpec((rows, d), lambda l, j: (0, 0)),
            pl.BlockSpec((1, d, MOD_TILE), lambda l, j: (l, 0, j)),
            pl.BlockSpec((1, 1, MOD_TILE), lambda l, j: (l, 0, j)),
        ],
        out_specs=pl.BlockSpec((1, rows, MOD_TILE), lambda l, j: (l, 0, j)),
        out_shape=jax.ShapeDtypeStruct((depth, rows, nd), F32),
        compiler_params=_cparams(("parallel", "parallel")),
    )(cc, w_mod, b_mod.reshape(depth, 1, nd))


def _store_tokens(ref, val, grid_layout):
    if grid_layout:
        for r in range(val.shape[0] // GRID_W):
            ref[:, r, :] = val[r * GRID_W:(r + 1) * GRID_W]
    else:
        ref[0] = val


def _load_tokens(ref, grid_layout):
    if grid_layout:
        return jnp.concatenate([ref[:, r, :] for r in range(ref.shape[1])], axis=0)
    return ref[0]


def _stage_a_kernel(x_ref, mod_ref, g1_ref, win_ref, sguw_ref, sgub_ref, ones_ref, gw_ref, gb_ref, s5_in_ref,
                    sgu_ref, s5x_ref, qk_ref, v_ref, g_ref, la_ref, *, tb, grid_layout):
    del s5_in_ref
    x = x_ref[0]
    ms = jnp.mean(x * x, axis=-1, keepdims=True)
    xn = x * lax.rsqrt(ms + EPS) * g1_ref[...]
    h = xn * (1.0 + mod_ref[0, 1:2, :]) + mod_ref[0, 0:1, :]
    cols = _dot(h.astype(BF16), win_ref[...])

    u = _gelu(cols[:, 0:SGU_DIM])
    v = _gelu(cols[:, SGU_DIM:2 * SGU_DIM])
    msq = _dot_x_exact(v * v, ones_ref[...]) * (1.0 / SGU_HEAD_DIM)
    vn = (v * lax.rsqrt(msq + EPS)).astype(BF16)
    head_of_lane = lax.broadcasted_iota(jnp.int32, (SGU_CHUNK, SGU_DIM), 1) // SGU_HEAD_DIM
    for ci in range(tb // SGU_CHUNK):
        rows = slice(ci * SGU_CHUNK, (ci + 1) * SGU_CHUNK)
        vc = vn[rows]
        mixed = sgub_ref[...]
        for hh in range(SGU_HEADS):
            mixed = mixed + jnp.where(head_of_lane == hh, _dot(sguw_ref[hh], vc), 0.0)
        sgu_ref[0, rows, :] = u[rows] * mixed

    s5x_ref[0] = cols[:, 512:768]
    q = cols[:, 768:1024] * (GLA_DK ** -0.5)
    _store_tokens(qk_ref, jnp.concatenate([q, cols[:, 1024:1280]], axis=-1), grid_layout)
    _store_tokens(v_ref, cols[:, 1280:1792], grid_layout)
    g_ref[0] = cols[:, 1792:2304]

    z = cols[:, 2304:2432].astype(BF16)
    za = _dot(z, gw_ref[...]) + gb_ref[...]
    log_sig = jnp.minimum(za, 0.0) - jnp.log1p(jnp.exp(-jnp.abs(za)))
    _store_tokens(la_ref, log_sig * (1.0 / GLA_GATE_TEMP), grid_layout)


def _stage_a(xs, mod, g1, win, sguw, sgub, ones_sgu, gw, gb, s5_all, s5_row0, tb, grid_layout):
    b, l, d = xs.shape
    assert s5_row0 % tb == 0
    kern = functools.partial(_stage_a_kernel, tb=tb, grid_layout=grid_layout)
    tok = lambda w: pl.BlockSpec((1, tb, w), lambda bi, i: (bi, i, 0))
    if grid_layout:
        assert tb % GRID_W == 0 and l % GRID_W == 0
        gla = lambda w: pl.BlockSpec((GRID_W, None, tb // GRID_W, w), lambda bi, i: (0, bi, i, 0))
        gla_shape = lambda w: jax.ShapeDtypeStruct((GRID_W, b, l // GRID_W, w), F32)
    else:
        gla = tok
        gla_shape = lambda w: jax.ShapeDtypeStruct((b, l, w), F32)
    nat_shape = lambda w: jax.ShapeDtypeStruct((b, l, w), F32)
    return pl.pallas_call(
        kern,
        grid=(b, l // tb),
        in_specs=[
            tok(d),
            pl.BlockSpec((1, 8, d), lambda bi, i: (bi, 0, 0)),
            _full_spec(g1.shape), _full_spec(win.shape), _full_spec(sguw.shape), _full_spec(sgub.shape),
            _full_spec(ones_sgu.shape), _full_spec(gw.shape), _full_spec(gb.shape),
            pl.BlockSpec(memory_space=pl.ANY),
        ],
        out_specs=[tok(SGU_DIM),
                   pl.BlockSpec((1, tb, S5_DIM), lambda bi, i: (bi, s5_row0 // tb + i, 0)),
                   gla(2 * GLA_KEY_DIM), gla(GLA_DIM), tok(GLA_DIM), gla(2 * GLA_KEY_DIM)],
        out_shape=[nat_shape(SGU_DIM), jax.ShapeDtypeStruct(s5_all.shape, F32),
                   gla_shape(2 * GLA_KEY_DIM), gla_shape(GLA_DIM), nat_shape(GLA_DIM), gla_shape(2 * GLA_KEY_DIM)],
        input_output_aliases={9: 1},
        compiler_params=_cparams(("parallel", "parallel")),
    )(xs, mod, g1, win, sguw, sgub, ones_sgu, gw, gb, s5_all)


S5_TC = 128


def _s5_kernel(uf_ref, ub_ref, perm_ref, permt_ref, b2_ref, ar_ref, ai_ref, c2_ref, yf_ref, yb_ref,
               h_ref, buf_ref, *, nseq):
    tc = S5_TC
    rows = tc * 2 * nseq

    @pl.when(pl.program_id(0) == 0)
    def _():
        h_ref[...] = jnp.zeros_like(h_ref)

    x = jnp.concatenate([uf_ref[b] for b in range(nseq)] + [ub_ref[b] for b in range(nseq)], axis=0)
    u_tm = _dot(perm_ref[...], x.astype(BF16))
    fwd_row = lax.broadcasted_iota(jnp.int32, (rows, S5_DIM), 0) % (2 * nseq) < nseq
    u = jnp.concatenate([jnp.where(fwd_row, u_tm, 0.0), jnp.where(fwd_row, 0.0, u_tm)], axis=-1).astype(BF16)
    buf_ref[...] = _dot(u, b2_ref[...]).reshape(tc, 2 * nseq, 2 * S5_LANES)
    ar = ar_ref[...]
    ai = ai_ref[...]

    def step(t, carry):
        hr, hi = carry
        bur = buf_ref[t, :, 0:S5_LANES]
        bui = buf_ref[t, :, S5_LANES:]
        nhr = ar * hr - ai * hi + bur
        nhi = ar * hi + ai * hr + bui
        buf_ref[t, :, 0:S5_LANES] = nhr
        buf_ref[t, :, S5_LANES:] = nhi
        return nhr, nhi

    hr, hi = lax.fori_loop(0, tc, step, (h_ref[:, 0:S5_LANES], h_ref[:, S5_LANES:]))
    h_ref[:, 0:S5_LANES] = hr
    h_ref[:, S5_LANES:] = hi

    hs = buf_ref[...].reshape(rows, 2 * S5_LANES).astype(BF16)
    y2 = _dot(hs, c2_ref[...])
    y = jnp.where(fwd_row, y2[:, 0:S5_DIM], y2[:, S5_DIM:])
    y_hi = y.astype(BF16)
    y_lo = (y - y_hi.astype(F32)).astype(BF16)
    y_nat = _dot(permt_ref[...], y_hi) + _dot(permt_ref[...], y_lo)
    for b in range(nseq):
        yf_ref[b] = y_nat[b * tc:(b + 1) * tc]
        yb_ref[b] = y_nat[(nseq + b) * tc:(nseq + b + 1) * tc]


def _s5_permutation(nseq):
    tc = S5_TC
    p = np.zeros((tc * 2 * nseq, tc * 2 * nseq), np.float32)
    for q in range(2 * nseq):
        for t in range(tc):
            p[t * 2 * nseq + q, q * tc + (t if q < nseq else tc - 1 - t)] = 1.0
    return p


def _s5_call(s5_all, n_lat, b2, ar, ai, c2):
    nseq, t, _ = s5_all.shape
    n = t // S5_TC
    n_l = n_lat // S5_TC
    n_c = n - n_l
    perm = _s5_permutation(nseq)
    kern = functools.partial(_s5_kernel, nseq=nseq)
    fwd_map = lambda s: (0, jnp.where(s < n_c, n_l + s, s - n_c), 0)
    bwd_map = lambda s: (0, n - 1 - s, 0)
    blk = (nseq, S5_TC, S5_DIM)
    rows = 2 * nseq
    return pl.pallas_call(
        kern,
        grid=(n,),
        in_specs=[
            pl.BlockSpec(blk, fwd_map), pl.BlockSpec(blk, bwd_map),
            _full_spec(perm.shape), _full_spec(perm.shape),
            _full_spec(b2.shape), _full_spec(ar.shape), _full_spec(ai.shape), _full_spec(c2.shape),
        ],
        out_specs=[pl.BlockSpec(blk, fwd_map), pl.BlockSpec(blk, bwd_map)],
        out_shape=[jax.ShapeDtypeStruct(s5_all.shape, F32), jax.ShapeDtypeStruct(s5_all.shape, F32)],
        scratch_shapes=[
            pltpu.VMEM((rows, 2 * S5_LANES), F32),
            pltpu.VMEM((S5_TC, rows, 2 * S5_LANES), F32),
        ],
        compiler_params=_cparams(("arbitrary",)),
    )(s5_all, s5_all, jnp.asarray(perm, dtype=BF16), jnp.asarray(perm.T, dtype=BF16), b2, ar, ai, c2)


def _gla_stream(q, k, v, la, s_t, tri, last_row, ref_row, tri_mask, hm_k, hm_v, hm_s):
    bcum = _dot_m_exact(tri, la)
    blast = bcum[last_row:last_row + 1]
    bref = bcum[ref_row:ref_row + 1]
    qe = q * jnp.exp(bcum)
    qd = q * jnp.exp(bcum - bref)
    kd = k * jnp.exp(bref - bcum)
    kdec = k * jnp.exp(blast - bcum)
    kst = jnp.where(hm_k, jnp.concatenate([kd] * GLA_HEADS, axis=0), 0.0).astype(BF16)
    sc = jnp.where(tri_mask, _dot_nt(qd.astype(BF16), kst), 0.0)
    vbd = jnp.where(hm_v, jnp.concatenate([v] * GLA_HEADS, axis=0), 0.0).astype(BF16)
    o = _dot(sc.astype(BF16), vbd) + _dot_nt(qe.astype(BF16), s_t.astype(BF16))
    kv_t = jnp.where(hm_s, _dot_tn(v.astype(BF16), kdec.astype(BF16)), 0.0)
    s_new = s_t * jnp.exp(blast) + kv_t
    return o, s_new


def _gla_kernel(qkf_ref, vf_ref, laf_ref, qkb_ref, vb_ref, lab_ref, s0_ref, trif_ref, trib_ref,
                of_ref, ob_ref, sout_ref, s_ref, *, nb):
    c = pl.program_id(0)

    @pl.when(c == 0)
    def _():
        s_ref[...] = s0_ref[...]

    ch = GLA_CHUNK
    r_k = lax.broadcasted_iota(jnp.int32, (GLA_HEADS * ch, GLA_KEY_DIM), 0) // ch
    c_k = lax.broadcasted_iota(jnp.int32, (GLA_HEADS * ch, GLA_KEY_DIM), 1) // GLA_DK
    hm_k = r_k == c_k
    r_v = lax.broadcasted_iota(jnp.int32, (GLA_HEADS * ch, GLA_DIM), 0) // ch
    c_v = lax.broadcasted_iota(jnp.int32, (GLA_HEADS * ch, GLA_DIM), 1) // GLA_DV
    hm_v = r_v == c_v
    r_s = lax.broadcasted_iota(jnp.int32, (GLA_DIM, GLA_KEY_DIM), 0) // GLA_DV
    c_s = lax.broadcasted_iota(jnp.int32, (GLA_DIM, GLA_KEY_DIM), 1) // GLA_DK
    hm_s = r_s == c_s
    t_i = lax.broadcasted_iota(jnp.int32, (ch, GLA_HEADS * ch), 0)
    s_i = lax.broadcasted_iota(jnp.int32, (ch, GLA_HEADS * ch), 1) % ch
    mask_f = t_i >= s_i
    mask_b = t_i <= s_i
    trif = trif_ref[...]
    trib = trib_ref[...]

    def body(b, carry):
        qk = qkf_ref[b]
        o, s_new = _gla_stream(qk[:, 0:GLA_KEY_DIM], qk[:, GLA_KEY_DIM:], vf_ref[b], laf_ref[b], s_ref[b, 0],
                               trif, ch - 1, ch // 2, mask_f, hm_k, hm_v, hm_s)
        of_ref[b] = o
        s_ref[b, 0] = s_new
        qk = qkb_ref[b]
        o, s_new = _gla_stream(qk[:, 0:GLA_KEY_DIM], qk[:, GLA_KEY_DIM:], vb_ref[b], lab_ref[b], s_ref[b, 1],
                               trib, 0, ch - 1 - ch // 2, mask_b, hm_k, hm_v, hm_s)
        ob_ref[b] = o
        s_ref[b, 1] = s_new
        return carry

    lax.fori_loop(0, nb, body, 0)

    @pl.when(c == pl.num_programs(0) - 1)
    def _():
        sout_ref[...] = s_ref[...]


def _gla_call(qk, v, la, s0, trif, trib):
    ch = GLA_CHUNK
    if qk.ndim == 4:
        n, b = qk.shape[0], qk.shape[1]
        assert qk.shape[2] == ch
        spec = lambda w, off, rev: pl.BlockSpec(
            (None, b, ch, w), (lambda c: (n - 1 - c, 0, 0, off)) if rev else (lambda c: (c, 0, 0, off)))
    else:
        b = qk.shape[0]
        n = qk.shape[1] // ch
        spec = lambda w, off, rev: pl.BlockSpec(
            (b, ch, w), (lambda c: (0, n - 1 - c, off)) if rev else (lambda c: (0, c, off)))
    kern = functools.partial(_gla_kernel, nb=b)
    o_shape = jax.ShapeDtypeStruct(v.shape, F32)
    return pl.pallas_call(
        kern,
        grid=(n,),
        in_specs=[
            spec(2 * GLA_KEY_DIM, 0, False), spec(GLA_DIM, 0, False), spec(GLA_KEY_DIM, 0, False),
            spec(2 * GLA_KEY_DIM, 0, True), spec(GLA_DIM, 0, True), spec(GLA_KEY_DIM, 1, True),
            _full_spec(s0.shape), _full_spec(trif.shape), _full_spec(trib.shape),
        ],
        out_specs=[spec(GLA_DIM, 0, False), spec(GLA_DIM, 0, True), _full_spec(s0.shape)],
        out_shape=[o_shape, o_shape, jax.ShapeDtypeStruct(s0.shape, F32)],
        scratch_shapes=[pltpu.VMEM(s0.shape, F32)],
        compiler_params=_cparams(("arbitrary",)),
    )(qk, v, la, qk, v, la, s0, trif, trib)


def _stage_e_kernel(x_ref, mod_ref, sgu_ref, yf_ref, yb_ref, s5x_ref, of_ref, ob_ref, g_ref,
                    dskip_ref, wglu_ref, normg_ref, ones_ref, wout_ref, o_ref, *, grid_layout):
    ys = yf_ref[0] + yb_ref[0] + dskip_ref[...] * s5x_ref[0]
    z = _dot(_gelu(ys).astype(BF16), wglu_ref[...])
    s5o = z[:, 0:S5_DIM] * _sigmoid(z[:, S5_DIM:])
    o = _load_tokens(of_ref, grid_layout) + _load_tokens(ob_ref, grid_layout)
    ms = _dot_x_exact(o * o, ones_ref[...]) * (1.0 / GLA_DV)
    g = g_ref[0]
    gl = o * lax.rsqrt(ms + EPS) * normg_ref[...] * (g * _sigmoid(g))
    y = (_dot(sgu_ref[0].astype(BF16), wout_ref[0:SGU_DIM, :])
         + _dot(s5o.astype(BF16), wout_ref[SGU_DIM:SGU_DIM + S5_DIM, :])
         + _dot(gl.astype(BF16), wout_ref[SGU_DIM + S5_DIM:, :]))
    o_ref[0] = x_ref[0] + mod_ref[0, 2:3, :] * y


def _stage_e(xs, mod, sgu, yf_all, yb_all, s5_all, s5_row0, of, ob, g, dskip, wglu, normg, ones_gla, wout, tb):
    b, l, d = xs.shape
    assert s5_row0 % tb == 0
    grid_layout = of.ndim == 4
    tok = lambda w: pl.BlockSpec((1, tb, w), lambda bi, i: (bi, i, 0))
    s5 = pl.BlockSpec((1, tb, S5_DIM), lambda bi, i: (bi, s5_row0 // tb + i, 0))
    if grid_layout:
        assert tb % GRID_W == 0
        gla = pl.BlockSpec((GRID_W, None, tb // GRID_W, GLA_DIM), lambda bi, i: (0, bi, i, 0))
    else:
        gla = tok(GLA_DIM)
    return pl.pallas_call(
        functools.partial(_stage_e_kernel, grid_layout=grid_layout),
        grid=(b, l // tb),
        in_specs=[
            tok(d), pl.BlockSpec((1, 8, d), lambda bi, i: (bi, 0, 0)),
            tok(SGU_DIM), s5, s5, s5, gla, gla, tok(GLA_DIM),
            _full_spec(dskip.shape), _full_spec(wglu.shape), _full_spec(normg.shape),
            _full_spec(ones_gla.shape), _full_spec(wout.shape),
        ],
        out_specs=tok(d),
        out_shape=jax.ShapeDtypeStruct((b, l, d), F32),
        compiler_params=_cparams(("parallel", "parallel")),
    )(xs, mod, sgu, yf_all, yb_all, s5_all, of, ob, g, dskip, wglu, normg, ones_gla, wout)


PEER_TBF = 256
SUBLANES = 8


def _sort_network_16():
    def merge(lo, hi, r):
        step = r * 2
        if step < hi - lo:
            yield from merge(lo, hi, step)
            yield from merge(lo + r, hi, step)
            for i in range(lo + r, hi - r, step):
                yield (i, i + r)
        else:
            yield (lo, lo + r)

    def sort(lo, hi):
        if hi - lo >= 1:
            mid = lo + (hi - lo) // 2
            yield from sort(lo, mid)
            yield from sort(mid + 1, hi)
            yield from merge(lo, hi, 1)

    return tuple(sort(0, PEER_TOPK - 1))


SORT16 = _sort_network_16()
BITONIC16 = tuple((k, k + s) for s in (8, 4, 2, 1) for k in range(PEER_TOPK) if not k & s)


def _compare_exchange(xs, pairs):
    xs = list(xs)
    for i, j in pairs:
        hi = jnp.maximum(xs[i], xs[j])
        lo = jnp.minimum(xs[i], xs[j])
        xs[i], xs[j] = hi, lo
    return xs


def _merge_sublanes(xs):
    for shift in (4, 6, 7):
        rolled = [pltpu.roll(x, shift, 0) for x in xs]
        xs = [jnp.maximum(xs[k], rolled[PEER_TOPK - 1 - k]) for k in range(PEER_TOPK)]
        xs = _compare_exchange(xs, BITONIC16)
    return xs


def _dup_bf16_words(x):
    bits = pltpu.bitcast(x.astype(BF16).astype(F32), jnp.int32)
    return bits | lax.shift_right_logical(bits, 16)


def _stage_f_kernel(x_ref, mod_ref, g2_ref, wqt_ref, keys_ref,
                    ht_ref, e1w_ref, n1w_ref, e2_ref, r2_ref, qt_ref, v1_ref, v2_ref, *, tb):
    x = x_ref[0]
    ms = jnp.mean(x * x, axis=-1, keepdims=True)
    xn = x * lax.rsqrt(ms + EPS) * g2_ref[...]
    h = xn * (1.0 + mod_ref[0, 4:5, :]) + mod_ref[0, 3:4, :]
    ht = h.T.astype(BF16)
    ht_ref[...] = ht
    qt_ref[...] = _dot(wqt_ref[...], ht)
    k_top = PEER_TOPK

    def tiles(s):
        return [s[SUBLANES * k:SUBLANES * (k + 1)] for k in range(PEER_NKEYS // SUBLANES)]

    def head_body(hh, carry):
        for tc in range(tb // LANE):
            tcol = slice(tc * LANE, (tc + 1) * LANE)
            r1 = pl.multiple_of(hh * (2 * PEER_HALF), 2 * PEER_HALF)
            q1 = qt_ref[pl.ds(r1, PEER_HALF), tcol].astype(BF16)
            q2 = qt_ref[pl.ds(r1 + PEER_HALF, PEER_HALF), tcol].astype(BF16)
            s1 = _dot(keys_ref[hh], q1)
            s2 = _dot(keys_ref[PEER_HEADS + hh], q2)
            for s, v_ref in ((s1, v1_ref), (s2, v2_ref)):
                top = _merge_sublanes(_compare_exchange(tiles(s), SORT16))
                for k in range(k_top):
                    v_ref[k:k + 1, tcol] = top[k][0:1]
            v1row = lambda a: v1_ref[a:a + 1, tcol]
            v2row = lambda b: v2_ref[b:b + 1, tcol]
            v1lo = v1_ref[0:SUBLANES, tcol]
            v2lo = v2_ref[0:SUBLANES, tcol]
            cand = [v1lo + v2row(b) for b in range(k_top)]
            tail = [v1row(a) + v2lo for a in range(SUBLANES, k_top)]
            for k in range(SUBLANES, k_top):
                cand[k] = jnp.maximum(cand[k], tail[k_top - 1 - k])
            best = _merge_sublanes(_compare_exchange(cand, BITONIC16))
            theta = best[k_top - 1][0:1]
            cmax = best[0][0:1]
            zsum = jnp.zeros((1, LANE), F32)
            for k in range(k_top):
                zsum = zsum + jnp.exp(best[k][0:1] - cmax)
            rz = 1.0 / zsum
            n_top = jnp.zeros((1, LANE), F32)
            for b in range(k_top):
                n_top = jnp.where(v1row(0) + v2row(b) >= theta, float(b + 1), n_top)
            n1 = jnp.zeros(s1.shape, F32)
            for b in range(SUBLANES):
                n1 = jnp.where(s1 + v2row(b) >= theta, float(b + 1), n1)
            n1 = jnp.where(s1 >= v1row(0), n_top, n1)
            r2 = jnp.zeros(s2.shape, F32)
            for b in range(k_top):
                r2 = jnp.where(v2row(b) > s2, float(b + 1), r2)
            e1 = jnp.where(s1 >= v1row(k_top - 1), jnp.exp(s1 - v1row(0)), 0.0) * rz
            e2 = jnp.where(s2 >= v2row(k_top - 1), jnp.exp(s2 - v2row(0)), 0.0)
            e1w_ref[tc, hh] = _dup_bf16_words(e1)
            n1w_ref[tc, hh] = _dup_bf16_words(n1)
            e2_ref[tc, hh] = pltpu.bitcast(e2.astype(BF16), jnp.int32)
            r2_ref[tc, hh] = pltpu.bitcast(r2.astype(BF16), jnp.int32)
        return carry

    lax.fori_loop(0, PEER_HEADS, head_body, 0)


def _stage_f(xs, mod, g2, wqt, keys, tb):
    b, l, d = xs.shape
    nblk = l // tb
    ntok = b * l
    nch = ntok // LANE
    kern = functools.partial(_stage_f_kernel, tb=tb)
    row_spec = pl.BlockSpec((tb // LANE, PEER_HEADS, PEER_NKEYS, LANE), lambda bi, i: (bi * nblk + i, 0, 0, 0))
    pair_spec = pl.BlockSpec((tb // LANE, PEER_HEADS, PEER_NKEYS // 2, LANE), lambda bi, i: (bi * nblk + i, 0, 0, 0))
    desc_shape = lambda rows: jax.ShapeDtypeStruct((nch, PEER_HEADS, rows, LANE), jnp.int32)
    return pl.pallas_call(
        kern,
        grid=(b, nblk),
        in_specs=[
            pl.BlockSpec((1, tb, d), lambda bi, i: (bi, i, 0)),
            pl.BlockSpec((1, 8, d), lambda bi, i: (bi, 0, 0)),
            _full_spec(g2.shape), _full_spec(wqt.shape), _full_spec(keys.shape),
        ],
        out_specs=[pl.BlockSpec((d, tb), lambda bi, i: (0, bi * nblk + i)),
                   row_spec, row_spec, pair_spec, pair_spec],
        out_shape=[jax.ShapeDtypeStruct((d, ntok), BF16), desc_shape(PEER_NKEYS), desc_shape(PEER_NKEYS),
                   desc_shape(PEER_NKEYS // 2), desc_shape(PEER_NKEYS // 2)],
        scratch_shapes=[
            pltpu.VMEM((PEER_HEADS * 2 * PEER_HALF, tb), F32),
            pltpu.VMEM((PEER_TOPK, tb), F32),
            pltpu.VMEM((PEER_TOPK, tb), F32),
        ],
        compiler_params=_cparams(("parallel", "parallel")),
    )(xs, mod, g2, wqt, keys)


PEER_TBG = 512
PEER_TE = 1024
PEER_I1_PER_TILE = PEER_TE // PEER_NKEYS
PEER_N_TILES = PEER_NKEYS * PEER_NKEYS // PEER_TE
PEER_MXU_COLS = 256


def _stage_g_kernel(ht_ref, e1w_ref, n1w_ref, e2_ref, r2_ref, u_ref, vt_ref, x_ref, mod_ref,
                    o_ref, acc_ref, pa_ref, pb_ref, *, tb):
    i = pl.program_id(1)
    n_tiles = pl.num_programs(1) - 1

    @pl.when(i == 0)
    def _():
        acc_ref[...] = jnp.zeros_like(acc_ref)
        pb_ref[...] = jnp.zeros_like(pb_ref)

    def step(cur_ref, prev_ref):
        per_grp = PEER_MXU_COLS // LANE
        for grp in range(tb // PEER_MXU_COLS):
            cols = slice(grp * PEER_MXU_COLS, (grp + 1) * PEER_MXU_COLS)
            for tcl in range(per_grp):
                tc = grp * per_grp + tcl
                tcol = slice(tc * LANE, (tc + 1) * LANE)
                for i1l in range(PEER_I1_PER_TILE):
                    rows = slice(i1l * PEER_NKEYS, (i1l + 1) * PEER_NKEYS)
                    gate = jnp.zeros((PEER_NKEYS, LANE), BF16)
                    for hh in range(PEER_HEADS):
                        e1row = e1w_ref[tc, hh, i1l:i1l + 1, :]
                        n1row = n1w_ref[tc, hh, i1l:i1l + 1, :]
                        e1 = pltpu.bitcast(jnp.broadcast_to(e1row, (PEER_NKEYS // 2, LANE)), BF16)
                        n1 = pltpu.bitcast(jnp.broadcast_to(n1row, (PEER_NKEYS // 2, LANE)), BF16)
                        r2 = pltpu.bitcast(r2_ref[tc, hh], BF16)
                        e2 = pltpu.bitcast(e2_ref[tc, hh], BF16)
                        gate = gate + e2 * jnp.where(r2 < n1, e1, 0.0)
                    cur_ref[rows, tcol] = gate
            acc_ref[:, cols] += _dot(vt_ref[...], prev_ref[:, cols])
            at = _dot(u_ref[...], ht_ref[:, cols])
            for tcl in range(per_grp):
                tcol = slice((grp * per_grp + tcl) * LANE, (grp * per_grp + tcl + 1) * LANE)
                for i1l in range(PEER_I1_PER_TILE):
                    rows = slice(i1l * PEER_NKEYS, (i1l + 1) * PEER_NKEYS)
                    a = at[rows, tcl * LANE:(tcl + 1) * LANE]
                    cur_ref[rows, tcol] = cur_ref[rows, tcol] * _gelu(a).astype(BF16)

    @pl.when(jnp.logical_and(i < n_tiles, i % 2 == 0))
    def _():
        step(pa_ref, pb_ref)

    @pl.when(jnp.logical_and(i < n_tiles, i % 2 == 1))
    def _():
        step(pb_ref, pa_ref)

    @pl.when(i == n_tiles)
    def _():
        last_ref = pb_ref if (PEER_N_TILES - 1) % 2 else pa_ref
        acc = acc_ref[...] + _dot(vt_ref[...], last_ref[...])
        o_ref[...] = x_ref[...] + mod_ref[0, 5:6, :] * acc.T


def _stage_g(ht, e1w, n1w, e2, r2, u_bf, vt_bf, xflat, mod, tokens_per_batch, tb):
    d, ntok = ht.shape
    assert u_bf.shape[0] == PEER_N_TILES * PEER_TE
    blocks_per_batch = tokens_per_batch // tb
    kern = functools.partial(_stage_g_kernel, tb=tb)
    last = PEER_N_TILES - 1
    desc_spec = pl.BlockSpec((tb // LANE, PEER_HEADS, PEER_NKEYS // 2, LANE), lambda j, i: (j, 0, 0, 0))
    row_spec = pl.BlockSpec((tb // LANE, PEER_HEADS, PEER_I1_PER_TILE, LANE),
                            lambda j, i: (j, 0, jnp.minimum(i, last), 0))
    return pl.pallas_call(
        kern,
        grid=(ntok // tb, PEER_N_TILES + 1),
        in_specs=[
            pl.BlockSpec((d, tb), lambda j, i: (0, j)),
            row_spec, row_spec, desc_spec, desc_spec,
            pl.BlockSpec((PEER_TE, d), lambda j, i: (jnp.minimum(i, last), 0)),
            pl.BlockSpec((d, PEER_TE), lambda j, i: (0, jnp.maximum(i - 1, 0))),
            pl.BlockSpec((tb, d), lambda j, i: (j, 0)),
            pl.BlockSpec((1, 8, d), lambda j, i: (j // blocks_per_batch, 0, 0)),
        ],
        out_specs=pl.BlockSpec((tb, d), lambda j, i: (j, 0)),
        out_shape=jax.ShapeDtypeStruct((ntok, d), F32),
        scratch_shapes=[
            pltpu.VMEM((d, tb), F32),
            pltpu.VMEM((PEER_TE, tb), BF16),
            pltpu.VMEM((PEER_TE, tb), BF16),
        ],
        compiler_params=_cparams(("parallel", "arbitrary")),
    )(ht, e1w, n1w, e2, r2, u_bf, vt_bf, xflat, mod)


def _final_norm_kernel(x_ref, g_ref, o_ref):
    x = x_ref[...]
    ms = jnp.mean(x * x, axis=-1, keepdims=True)
    o_ref[...] = x * lax.rsqrt(ms + EPS) * g_ref[...]


def _final_norm(xflat, g, tb=512):
    n, d = xflat.shape
    return pl.pallas_call(
        _final_norm_kernel,
        grid=(n // tb,),
        in_specs=[pl.BlockSpec((tb, d), lambda i: (i, 0)), _full_spec(g.shape)],
        out_specs=pl.BlockSpec((tb, d), lambda i: (i, 0)),
        out_shape=jax.ShapeDtypeStruct((n, d), F32),
        compiler_params=_cparams(("parallel",)),
    )(xflat, g)


def _block_ones(n, blk):
    idx = np.arange(n) // blk
    return jnp.asarray((idx[:, None] == idx[None, :]).astype(np.float32), dtype=BF16)


def _s5_discretise(lam_re, lam_im, b_re, b_im, log_step):
    lam_re = jnp.minimum(lam_re.astype(F32), -1e-4)
    lam_im = lam_im.astype(F32)
    dt = jnp.exp(log_step.astype(F32))[:, None]
    mag = jnp.exp(lam_re * dt)
    a_re = mag * jnp.cos(lam_im * dt)
    a_im = mag * jnp.sin(lam_im * dt)
    den = lam_re * lam_re + lam_im * lam_im
    f_re = ((a_re - 1.0) * lam_re + a_im * lam_im) / den
    f_im = (a_im * lam_re - (a_re - 1.0) * lam_im) / den
    b_re = b_re.astype(F32)
    b_im = b_im.astype(F32)
    bb_re = f_re[..., None] * b_re - f_im[..., None] * b_im
    bb_im = f_re[..., None] * b_im + f_im[..., None] * b_re
    return a_re, a_im, bb_re, bb_im


def _group_block_diag(t):
    g, r, c = t.shape
    eye = jnp.eye(g, dtype=t.dtype)
    return (t[:, :, None, :] * eye[:, None, :, None]).reshape(g * r, g * c)


def _s5_params(lam_re, lam_im, b_re, b_im, c_re, c_im, log_step, nseq):
    b_rows, c_cols, ars, ais = [], [], [], []
    for d in range(2):
        a_re, a_im, bb_re, bb_im = _s5_discretise(lam_re[d], lam_im[d], b_re[d], b_im[d], log_step[d])
        bm = jnp.concatenate([_group_block_diag(jnp.swapaxes(bb_re, 1, 2)),
                              _group_block_diag(jnp.swapaxes(bb_im, 1, 2))], axis=1)
        b_rows.append(bm)
        cm = jnp.concatenate([_group_block_diag(jnp.swapaxes(c_re[d].astype(F32), 1, 2)),
                              -_group_block_diag(jnp.swapaxes(c_im[d].astype(F32), 1, 2))], axis=0)
        c_cols.append(cm)
        ars.append(jnp.broadcast_to(a_re.reshape(1, S5_LANES), (nseq, S5_LANES)))
        ais.append(jnp.broadcast_to(a_im.reshape(1, S5_LANES), (nseq, S5_LANES)))
    b2 = jnp.concatenate(b_rows, axis=0).astype(BF16)
    c2 = jnp.concatenate(c_cols, axis=1).astype(BF16)
    return b2, jnp.concatenate(ars, axis=0), jnp.concatenate(ais, axis=0), c2


def kernel(x, c, ctx, c_ctx, w_mod, b_mod, norm1_g, norm2_g, w_in, w_out, sgu_w, sgu_b, s5_lambda_re, s5_lambda_im, s5_b_re, s5_b_im, s5_c_re, s5_c_im, s5_log_step, s5_d, s5_w_glu, gla_w_gate, gla_b_gate, gla_norm_g, peer_w_query, peer_sub_keys, peer_expert_u, peer_expert_v, final_norm_g):
    nb, seq, d = x.shape
    c_len = ctx.shape[1]
    depth = w_mod.shape[0]

    cc = jnp.concatenate([c, c_ctx[None, :], jnp.zeros((8 - nb - 1, d), F32)], axis=0)
    mods = _mod_call(cc, w_mod, b_mod)

    ones_sgu = _block_ones(SGU_DIM, SGU_HEAD_DIM)
    ones_gla = _block_ones(GLA_DIM, GLA_DV)
    tri_np = np.tril(np.ones((GLA_CHUNK, GLA_CHUNK), np.float32))
    trif = jnp.asarray(tri_np, dtype=BF16)
    trib = jnp.asarray(tri_np.T, dtype=BF16)
    s_zero = jnp.zeros((nb, 2, GLA_DIM, GLA_KEY_DIM), F32)

    xl, xc = x, ctx
    for l in range(depth):
        ctx_out = l < depth - 1
        m6 = mods[l].reshape(8, N_MOD, d)
        mod_l = jnp.pad(m6[:nb], ((0, 0), (0, 2), (0, 0)))
        mod_c = jnp.broadcast_to(jnp.pad(m6[nb], ((0, 2), (0, 0)))[None], (nb, 8, d))

        win = jnp.pad(w_in[l], ((0, 0), (0, IN_PAD - IN_WIDTH))).astype(BF16)
        sguw = sgu_w[l].astype(BF16)
        sgub = jnp.repeat(jnp.swapaxes(sgu_b[l], 0, 1), SGU_HEAD_DIM, axis=1)
        gw = jnp.zeros((LANE, 2 * GLA_KEY_DIM), F32)
        gw = gw.at[0:GLA_RANK, 0:GLA_KEY_DIM].set(gla_w_gate[l, 0])
        gw = gw.at[GLA_RANK:2 * GLA_RANK, GLA_KEY_DIM:].set(gla_w_gate[l, 1]).astype(BF16)
        gb = gla_b_gate[l].reshape(1, 2 * GLA_KEY_DIM)
        g1 = norm1_g[l].reshape(1, d)

        s5_all = jnp.zeros((nb, seq + c_len, S5_DIM), F32)
        sgu_l, s5_all, qk_l, v_l, g_l, la_l = _stage_a(xl, mod_l, g1, win, sguw, sgub, ones_sgu, gw, gb,
                                                       s5_all, 0, tb=TB_LATENT, grid_layout=True)
        sgu_c, s5_all, qk_c, v_c, g_c, la_c = _stage_a(xc, mod_c, g1, win, sguw, sgub, ones_sgu, gw, gb,
                                                       s5_all, seq, tb=TB_CTX, grid_layout=False)

        b2, ar, ai, c2 = _s5_params(s5_lambda_re[l], s5_lambda_im[l], s5_b_re[l], s5_b_im[l],
                                    s5_c_re[l], s5_c_im[l], s5_log_step[l], nb)
        yf_all, yb_all = _s5_call(s5_all, seq, b2, ar, ai, c2)

        of_c, ob_c, s_ctx = _gla_call(qk_c, v_c, la_c, s_zero, trif, trib)
        of_l, ob_l, _ = _gla_call(qk_l, v_l, la_l, s_ctx, trif, trib)

        dskip = s5_d[l].reshape(1, S5_DIM)
        wglu = s5_w_glu[l].astype(BF16)
        normg = gla_norm_g[l].reshape(1, GLA_DIM)
        wout = w_out[l].astype(BF16)
        g2 = norm2_g[l].reshape(1, d)
        wqt = jnp.swapaxes(peer_w_query[l], 0, 1).astype(BF16)
        keys = peer_sub_keys[l].reshape(2 * PEER_HEADS, PEER_NKEYS, PEER_HALF).astype(BF16)
        u_bf = peer_expert_u[l].astype(BF16)
        vt_bf = jnp.swapaxes(peer_expert_v[l], 0, 1).astype(BF16)

        xl = _stage_e(xl, mod_l, sgu_l, yf_all, yb_all, s5_all, 0, of_l, ob_l, g_l,
                      dskip, wglu, normg, ones_gla, wout, tb=TB_LATENT)
        desc = _stage_f(xl, mod_l, g2, wqt, keys, tb=PEER_TBF)
        xl = _stage_g(*desc, u_bf, vt_bf, xl.reshape(nb * seq, d), mod_l, seq, PEER_TBG).reshape(nb, seq, d)

        if ctx_out:
            xc = _stage_e(xc, mod_c, sgu_c, yf_all, yb_all, s5_all, seq, of_c, ob_c, g_c,
                          dskip, wglu, normg, ones_gla, wout, tb=TB_CTX)
            desc = _stage_f(xc, mod_c, g2, wqt, keys, tb=PEER_TBF)
            xc = _stage_g(*desc, u_bf, vt_bf, xc.reshape(nb * c_len, d), mod_c, c_len,
                          min(PEER_TBG, c_len)).reshape(nb, c_len, d)

    return _final_norm(xl.reshape(nb * seq, d), final_norm_g.reshape(1, d)).reshape(nb, seq, d)
```

```python
import functools
import math

import numpy as np
import jax
import jax.numpy as jnp
from jax import lax
from jax.experimental import pallas as pl
from jax.experimental.pallas import tpu as pltpu

F32 = jnp.float32
BF16 = jnp.bfloat16

EPS = 1e-6
N_MOD = 6
GRID_W = 64

SGU_DIM = 256
SGU_HEADS = 4
SGU_HEAD_DIM = 64
SGU_CHUNK = 128

S5_DIM = 256
S5_GROUP = 16
S5_GROUPS = 16
S5_STATE = 64
S5_LANES = S5_GROUPS * S5_STATE

GLA_DIM = 512
GLA_HEADS = 8
GLA_DV = 64
GLA_DK = 32
GLA_KEY_DIM = 256
GLA_RANK = 16
GLA_GATE_TEMP = 16.0
GLA_CHUNK = 64

PEER_HEADS = 8
PEER_NKEYS = 128
PEER_HALF = 128
PEER_TOPK = 16

IN_WIDTH = 2336
IN_PAD = 2432
LANE = 128

VMEM_LIMIT = 56 * 1024 * 1024
TB_LATENT = 512
TB_CTX = 256

NEG_INF = float("-inf")
POS_INF = float("inf")


def _cparams(sem, flags=None):
    return pltpu.CompilerParams(dimension_semantics=sem, vmem_limit_bytes=VMEM_LIMIT, flags=flags)


def _gelu(x):
    c = math.sqrt(2.0 / math.pi)
    return 0.5 * x * (1.0 + jnp.tanh(c * (x + 0.044715 * (x * x * x))))


def _sigmoid(x):
    return 1.0 / (1.0 + jnp.exp(-x))


def _dot(a, b):
    return jnp.dot(a, b, preferred_element_type=F32)


def _dot_nt(a, b):
    return lax.dot_general(a, b, (((1,), (1,)), ((), ())), preferred_element_type=F32)


def _dot_tn(a, b):
    return lax.dot_general(a, b, (((0,), (0,)), ((), ())), preferred_element_type=F32)


def _split3(x):
    hi = x.astype(BF16)
    r = x - hi.astype(F32)
    mid = r.astype(BF16)
    lo = (r - mid.astype(F32)).astype(BF16)
    return hi, mid, lo


def _dot_x_exact(x, m):
    hi, mid, lo = _split3(x)
    return _dot(hi, m) + _dot(mid, m) + _dot(lo, m)


def _dot_m_exact(m, x):
    hi, mid, lo = _split3(x)
    return _dot(m, hi) + _dot(m, mid) + _dot(m, lo)


def _full_spec(shape):
    nd = len(shape)
    return pl.BlockSpec(shape, lambda *_: (0,) * nd)


MOD_TILE = 512


def _mod_kernel(c_ref, w_ref, b_ref, o_ref):
    c = c_ref[...]
    a = c * _sigmoid(c)
    o_ref[0] = jnp.dot(a, w_ref[0], preferred_element_type=F32,
                       precision=lax.Precision.HIGHEST) + b_ref[0]


def _mod_call(cc, w_mod, b_mod):
    depth, d, nd = w_mod.shape
    rows = cc.shape[0]
    return pl.pallas_call(
        _mod_kernel,
        grid=(depth, nd // MOD_TILE),
        in_specs=[
            pl.BlockSpec((rows, d), lambda l, j: (0, 0)),
            pl.BlockSpec((1, d, MOD_TILE), lambda l, j: (l, 0, j)),
            pl.BlockSpec((1, 1, MOD_TILE), lambda l, j: (l, 0, j)),
        ],
        out_specs=pl.BlockSpec((1, rows, MOD_TILE), lambda l, j: (l, 0, j)),
        out_shape=jax.ShapeDtypeStruct((depth, rows, nd), F32),
        compiler_params=_cparams(("parallel", "parallel")),
    )(cc, w_mod, b_mod.reshape(depth, 1, nd))


def _store_tokens(ref, val, grid_layout):
    if grid_layout:
        for r in range(val.shape[0] // GRID_W):
            ref[:, r, :] = val[r * GRID_W:(r + 1) * GRID_W]
    else:
        ref[0] = val


def _load_tokens(ref, grid_layout):
    if grid_layout:
        return jnp.concatenate([ref[:, r, :] for r in range(ref.shape[1])], axis=0)
    return ref[0]


def _stage_a_kernel(x_ref, mod_ref, g1_ref, win_ref, sguw_ref, sgub_ref, ones_ref, gw_ref, gb_ref, s5_in_ref,
                    sgu_ref, s5x_ref, qk_ref, v_ref, g_ref, la_ref, *, tb, grid_layout):
    del s5_in_ref
    x = x_ref[0]
    ms = jnp.mean(x * x, axis=-1, keepdims=True)
    xn = x * lax.rsqrt(ms + EPS) * g1_ref[...]
    h = xn * (1.0 + mod_ref[0, 1:2, :]) + mod_ref[0, 0:1, :]
    cols = _dot(h.astype(BF16), win_ref[...])

    u = _gelu(cols[:, 0:SGU_DIM])
    v = _gelu(cols[:, SGU_DIM:2 * SGU_DIM])
    msq = _dot_x_exact(v * v, ones_ref[...]) * (1.0 / SGU_HEAD_DIM)
    vn = (v * lax.rsqrt(msq + EPS)).astype(BF16)
    head_of_lane = lax.broadcasted_iota(jnp.int32, (SGU_CHUNK, SGU_DIM), 1) // SGU_HEAD_DIM
    for ci in range(tb // SGU_CHUNK):
        rows = slice(ci * SGU_CHUNK, (ci + 1) * SGU_CHUNK)
        vc = vn[rows]
        mixed = sgub_ref[...]
        for hh in range(SGU_HEADS):
            mixed = mixed + jnp.where(head_of_lane == hh, _dot(sguw_ref[hh], vc), 0.0)
        sgu_ref[0, rows, :] = u[rows] * mixed

    s5x_ref[0] = cols[:, 512:768]
    q = cols[:, 768:1024] * (GLA_DK ** -0.5)
    _store_tokens(qk_ref, jnp.concatenate([q, cols[:, 1024:1280]], axis=-1), grid_layout)
    _store_tokens(v_ref, cols[:, 1280:1792], grid_layout)
    g_ref[0] = cols[:, 1792:2304]

    z = cols[:, 2304:2432].astype(BF16)
    za = _dot(z, gw_ref[...]) + gb_ref[...]
    log_sig = jnp.minimum(za, 0.0) - jnp.log1p(jnp.exp(-jnp.abs(za)))
    _store_tokens(la_ref, log_sig * (1.0 / GLA_GATE_TEMP), grid_layout)


def _stage_a(xs, mod, g1, win, sguw, sgub, ones_sgu, gw, gb, s5_all, s5_row0, tb, grid_layout):
    b, l, d = xs.shape
    assert s5_row0 % tb == 0
    kern = functools.partial(_stage_a_kernel, tb=tb, grid_layout=grid_layout)
    tok = lambda w: pl.BlockSpec((1, tb, w), lambda bi, i: (bi, i, 0))
    if grid_layout:
        assert tb % GRID_W == 0 and l % GRID_W == 0
        gla = lambda w: pl.BlockSpec((GRID_W, None, tb // GRID_W, w), lambda bi, i: (0, bi, i, 0))
        gla_shape = lambda w: jax.ShapeDtypeStruct((GRID_W, b, l // GRID_W, w), F32)
    else:
        gla = tok
        gla_shape = lambda w: jax.ShapeDtypeStruct((b, l, w), F32)
    nat_shape = lambda w: jax.ShapeDtypeStruct((b, l, w), F32)
    return pl.pallas_call(
        kern,
        grid=(b, l // tb),
        in_specs=[
            tok(d),
            pl.BlockSpec((1, 8, d), lambda bi, i: (bi, 0, 0)),
            _full_spec(g1.shape), _full_spec(win.shape), _full_spec(sguw.shape), _full_spec(sgub.shape),
            _full_spec(ones_sgu.shape), _full_spec(gw.shape), _full_spec(gb.shape),
            pl.BlockSpec(memory_space=pl.ANY),
        ],
        out_specs=[tok(SGU_DIM),
                   pl.BlockSpec((1, tb, S5_DIM), lambda bi, i: (bi, s5_row0 // tb + i, 0)),
                   gla(2 * GLA_KEY_DIM), gla(GLA_DIM), tok(GLA_DIM), gla(2 * GLA_KEY_DIM)],
        out_shape=[nat_shape(SGU_DIM), jax.ShapeDtypeStruct(s5_all.shape, F32),
                   gla_shape(2 * GLA_KEY_DIM), gla_shape(GLA_DIM), nat_shape(GLA_DIM), gla_shape(2 * GLA_KEY_DIM)],
        input_output_aliases={9: 1},
        compiler_params=_cparams(("parallel", "parallel")),
    )(xs, mod, g1, win, sguw, sgub, ones_sgu, gw, gb, s5_all)


S5_TC = 128


def _s5_kernel(uf_ref, ub_ref, perm_ref, permt_ref, b2_ref, ar_ref, ai_ref, c2_ref, yf_ref, yb_ref,
               h_ref, buf_ref, *, nseq):
    tc = S5_TC
    rows = tc * 2 * nseq

    @pl.when(pl.program_id(0) == 0)
    def _():
        h_ref[...] = jnp.zeros_like(h_ref)

    x = jnp.concatenate([uf_ref[b] for b in range(nseq)] + [ub_ref[b] for b in range(nseq)], axis=0)
    u_tm = _dot(perm_ref[...], x.astype(BF16))
    fwd_row = lax.broadcasted_iota(jnp.int32, (rows, S5_DIM), 0) % (2 * nseq) < nseq
    u = jnp.concatenate([jnp.where(fwd_row, u_tm, 0.0), jnp.where(fwd_row, 0.0, u_tm)], axis=-1).astype(BF16)
    buf_ref[...] = _dot(u, b2_ref[...]).reshape(tc, 2 * nseq, 2 * S5_LANES)
    ar = ar_ref[...]
    ai = ai_ref[...]

    def step(t, carry):
        hr, hi = carry
        bur = buf_ref[t, :, 0:S5_LANES]
        bui = buf_ref[t, :, S5_LANES:]
        nhr = ar * hr - ai * hi + bur
        nhi = ar * hi + ai * hr + bui
        buf_ref[t, :, 0:S5_LANES] = nhr
        buf_ref[t, :, S5_LANES:] = nhi
        return nhr, nhi

    hr, hi = lax.fori_loop(0, tc, step, (h_ref[:, 0:S5_LANES], h_ref[:, S5_LANES:]))
    h_ref[:, 0:S5_LANES] = hr
    h_ref[:, S5_LANES:] = hi

    hs = buf_ref[...].reshape(rows, 2 * S5_LANES).astype(BF16)
    y2 = _dot(hs, c2_ref[...])
    y = jnp.where(fwd_row, y2[:, 0:S5_DIM], y2[:, S5_DIM:])
    y_hi = y.astype(BF16)
    y_lo = (y - y_hi.astype(F32)).astype(BF16)
    y_nat = _dot(permt_ref[...], y_hi) + _dot(permt_ref[...], y_lo)
    for b in range(nseq):
        yf_ref[b] = y_nat[b * tc:(b + 1) * tc]
        yb_ref[b] = y_nat[(nseq + b) * tc:(nseq + b + 1) * tc]


def _s5_permutation(nseq):
    tc = S5_TC
    p = np.zeros((tc * 2 * nseq, tc * 2 * nseq), np.float32)
    for q in range(2 * nseq):
        for t in range(tc):
            p[t * 2 * nseq + q, q * tc + (t if q < nseq else tc - 1 - t)] = 1.0
    return p


def _s5_call(s5_all, n_lat, b2, ar, ai, c2):
    nseq, t, _ = s5_all.shape
    n = t // S5_TC
    n_l = n_lat // S5_TC
    n_c = n - n_l
    perm = _s5_permutation(nseq)
    kern = functools.partial(_s5_kernel, nseq=nseq)
    fwd_map = lambda s: (0, jnp.where(s < n_c, n_l + s, s - n_c), 0)
    bwd_map = lambda s: (0, n - 1 - s, 0)
    blk = (nseq, S5_TC, S5_DIM)
    rows = 2 * nseq
    return pl.pallas_call(
        kern,
        grid=(n,),
        in_specs=[
            pl.BlockSpec(blk, fwd_map), pl.BlockSpec(blk, bwd_map),
            _full_spec(perm.shape), _full_spec(perm.shape),
            _full_spec(b2.shape), _full_spec(ar.shape), _full_spec(ai.shape), _full_spec(c2.shape),
        ],
        out_specs=[pl.BlockSpec(blk, fwd_map), pl.BlockSpec(blk, bwd_map)],
        out_shape=[jax.ShapeDtypeStruct(s5_all.shape, F32), jax.ShapeDtypeStruct(s5_all.shape, F32)],
        scratch_shapes=[
            pltpu.VMEM((rows, 2 * S5_LANES), F32),
            pltpu.VMEM((S5_TC, rows, 2 * S5_LANES), F32),
        ],
        compiler_params=_cparams(("arbitrary",)),
    )(s5_all, s5_all, jnp.asarray(perm, dtype=BF16), jnp.asarray(perm.T, dtype=BF16), b2, ar, ai, c2)


def _gla_stream(q, k, v, la, s_t, tri, last_row, ref_row, tri_mask, hm_k, hm_v, hm_s):
    bcum = _dot_m_exact(tri, la)
    blast = bcum[last_row:last_row + 1]
    bref = bcum[ref_row:ref_row + 1]
    qe = q * jnp.exp(bcum)
    qd = q * jnp.exp(bcum - bref)
    kd = k * jnp.exp(bref - bcum)
    kdec = k * jnp.exp(blast - bcum)
    kst = jnp.where(hm_k, jnp.concatenate([kd] * GLA_HEADS, axis=0), 0.0).astype(BF16)
    sc = jnp.where(tri_mask, _dot_nt(qd.astype(BF16), kst), 0.0)
    vbd = jnp.where(hm_v, jnp.concatenate([v] * GLA_HEADS, axis=0), 0.0).astype(BF16)
    o = _dot(sc.astype(BF16), vbd) + _dot_nt(qe.astype(BF16), s_t.astype(BF16))
    kv_t = jnp.where(hm_s, _dot_tn(v.astype(BF16), kdec.astype(BF16)), 0.0)
    s_new = s_t * jnp.exp(blast) + kv_t
    return o, s_new


def _gla_kernel(qkf_ref, vf_ref, laf_ref, qkb_ref, vb_ref, lab_ref, s0_ref, trif_ref, trib_ref,
                of_ref, ob_ref, sout_ref, s_ref, *, nb):
    c = pl.program_id(0)

    @pl.when(c == 0)
    def _():
        s_ref[...] = s0_ref[...]

    ch = GLA_CHUNK
    r_k = lax.broadcasted_iota(jnp.int32, (GLA_HEADS * ch, GLA_KEY_DIM), 0) // ch
    c_k = lax.broadcasted_iota(jnp.int32, (GLA_HEADS * ch, GLA_KEY_DIM), 1) // GLA_DK
    hm_k = r_k == c_k
    r_v = lax.broadcasted_iota(jnp.int32, (GLA_HEADS * ch, GLA_DIM), 0) // ch
    c_v = lax.broadcasted_iota(jnp.int32, (GLA_HEADS * ch, GLA_DIM), 1) // GLA_DV
    hm_v = r_v == c_v
    r_s = lax.broadcasted_iota(jnp.int32, (GLA_DIM, GLA_KEY_DIM), 0) // GLA_DV
    c_s = lax.broadcasted_iota(jnp.int32, (GLA_DIM, GLA_KEY_DIM), 1) // GLA_DK
    hm_s = r_s == c_s
    t_i = lax.broadcasted_iota(jnp.int32, (ch, GLA_HEADS * ch), 0)
    s_i = lax.broadcasted_iota(jnp.int32, (ch, GLA_HEADS * ch), 1) % ch
    mask_f = t_i >= s_i
    mask_b = t_i <= s_i
    trif = trif_ref[...]
    trib = trib_ref[...]

    def body(b, carry):
        qk = qkf_ref[b]
        o, s_new = _gla_stream(qk[:, 0:GLA_KEY_DIM], qk[:, GLA_KEY_DIM:], vf_ref[b], laf_ref[b], s_ref[b, 0],
                               trif, ch - 1, ch // 2, mask_f, hm_k, hm_v, hm_s)
        of_ref[b] = o
        s_ref[b, 0] = s_new
        qk = qkb_ref[b]
        o, s_new = _gla_stream(qk[:, 0:GLA_KEY_DIM], qk[:, GLA_KEY_DIM:], vb_ref[b], lab_ref[b], s_ref[b, 1],
                               trib, 0, ch - 1 - ch // 2, mask_b, hm_k, hm_v, hm_s)
        ob_ref[b] = o
        s_ref[b, 1] = s_new
        return carry

    lax.fori_loop(0, nb, body, 0)

    @pl.when(c == pl.num_programs(0) - 1)
    def _():
        sout_ref[...] = s_ref[...]


def _gla_call(qk, v, la, s0, trif, trib):
    ch = GLA_CHUNK
    if qk.ndim == 4:
        n, b = qk.shape[0], qk.shape[1]
        assert qk.shape[2] == ch
        spec = lambda w, off, rev: pl.BlockSpec(
            (None, b, ch, w), (lambda c: (n - 1 - c, 0, 0, off)) if rev else (lambda c: (c, 0, 0, off)))
    else:
        b = qk.shape[0]
        n = qk.shape[1] // ch
        spec = lambda w, off, rev: pl.BlockSpec(
            (b, ch, w), (lambda c: (0, n - 1 - c, off)) if rev else (lambda c: (0, c, off)))
    kern = functools.partial(_gla_kernel, nb=b)
    o_shape = jax.ShapeDtypeStruct(v.shape, F32)
    return pl.pallas_call(
        kern,
        grid=(n,),
        in_specs=[
            spec(2 * GLA_KEY_DIM, 0, False), spec(GLA_DIM, 0, False), spec(GLA_KEY_DIM, 0, False),
            spec(2 * GLA_KEY_DIM, 0, True), spec(GLA_DIM, 0, True), spec(GLA_KEY_DIM, 1, True),
            _full_spec(s0.shape), _full_spec(trif.shape), _full_spec(trib.shape),
        ],
        out_specs=[spec(GLA_DIM, 0, False), spec(GLA_DIM, 0, True), _full_spec(s0.shape)],
        out_shape=[o_shape, o_shape, jax.ShapeDtypeStruct(s0.shape, F32)],
        scratch_shapes=[pltpu.VMEM(s0.shape, F32)],
        compiler_params=_cparams(("arbitrary",)),
    )(qk, v, la, qk, v, la, s0, trif, trib)


def _stage_e_kernel(x_ref, mod_ref, sgu_ref, yf_ref, yb_ref, s5x_ref, of_ref, ob_ref, g_ref,
                    dskip_ref, wglu_ref, normg_ref, ones_ref, wout_ref, o_ref, *, grid_layout):
    ys = yf_ref[0] + yb_ref[0] + dskip_ref[...] * s5x_ref[0]
    z = _dot(_gelu(ys).astype(BF16), wglu_ref[...])
    s5o = z[:, 0:S5_DIM] * _sigmoid(z[:, S5_DIM:])
    o = _load_tokens(of_ref, grid_layout) + _load_tokens(ob_ref, grid_layout)
    ms = _dot_x_exact(o * o, ones_ref[...]) * (1.0 / GLA_DV)
    g = g_ref[0]
    gl = o * lax.rsqrt(ms + EPS) * normg_ref[...] * (g * _sigmoid(g))
    y = (_dot(sgu_ref[0].astype(BF16), wout_ref[0:SGU_DIM, :])
         + _dot(s5o.astype(BF16), wout_ref[SGU_DIM:SGU_DIM + S5_DIM, :])
         + _dot(gl.astype(BF16), wout_ref[SGU_DIM + S5_DIM:, :]))
    o_ref[0] = x_ref[0] + mod_ref[0, 2:3, :] * y


def _stage_e(xs, mod, sgu, yf_all, yb_all, s5_all, s5_row0, of, ob, g, dskip, wglu, normg, ones_gla, wout, tb):
    b, l, d = xs.shape
    assert s5_row0 % tb == 0
    grid_layout = of.ndim == 4
    tok = lambda w: pl.BlockSpec((1, tb, w), lambda bi, i: (bi, i, 0))
    s5 = pl.BlockSpec((1, tb, S5_DIM), lambda bi, i: (bi, s5_row0 // tb + i, 0))
    if grid_layout:
        assert tb % GRID_W == 0
        gla = pl.BlockSpec((GRID_W, None, tb // GRID_W, GLA_DIM), lambda bi, i: (0, bi, i, 0))
    else:
        gla = tok(GLA_DIM)
    return pl.pallas_call(
        functools.partial(_stage_e_kernel, grid_layout=grid_layout),
        grid=(b, l // tb),
        in_specs=[
            tok(d), pl.BlockSpec((1, 8, d), lambda bi, i: (bi, 0, 0)),
            tok(SGU_DIM), s5, s5, s5, gla, gla, tok(GLA_DIM),
            _full_spec(dskip.shape), _full_spec(wglu.shape), _full_spec(normg.shape),
            _full_spec(ones_gla.shape), _full_spec(wout.shape),
        ],
        out_specs=tok(d),
        out_shape=jax.ShapeDtypeStruct((b, l, d), F32),
        compiler_params=_cparams(("parallel", "parallel")),
    )(xs, mod, sgu, yf_all, yb_all, s5_all, of, ob, g, dskip, wglu, normg, ones_gla, wout)


PEER_TBF = 256
SUBLANES = 8


def _sort_network_16():
    def merge(lo, hi, r):
        step = r * 2
        if step < hi - lo:
            yield from merge(lo, hi, step)
            yield from merge(lo + r, hi, step)
            for i in range(lo + r, hi - r, step):
                yield (i, i + r)
        else:
            yield (lo, lo + r)

    def sort(lo, hi):
        if hi - lo >= 1:
            mid = lo + (hi - lo) // 2
            yield from sort(lo, mid)
            yield from sort(mid + 1, hi)
            yield from merge(lo, hi, 1)

    return tuple(sort(0, PEER_TOPK - 1))


SORT16 = _sort_network_16()
BITONIC16 = tuple((k, k + s) for s in (8, 4, 2, 1) for k in range(PEER_TOPK) if not k & s)


def _compare_exchange(xs, pairs):
    xs = list(xs)
    for i, j in pairs:
        hi = jnp.maximum(xs[i], xs[j])
        lo = jnp.minimum(xs[i], xs[j])
        xs[i], xs[j] = hi, lo
    return xs


def _merge_sublanes(xs):
    for shift in (4, 6, 7):
        rolled = [pltpu.roll(x, shift, 0) for x in xs]
        xs = [jnp.maximum(xs[k], rolled[PEER_TOPK - 1 - k]) for k in range(PEER_TOPK)]
        xs = _compare_exchange(xs, BITONIC16)
    return xs


def _dup_bf16_words(x):
    bits = pltpu.bitcast(x.astype(BF16).astype(F32), jnp.int32)
    return bits | lax.shift_right_logical(bits, 16)


def _stage_f_kernel(x_ref, mod_ref, g2_ref, wqt_ref, keys_ref,
                    ht_ref, e1w_ref, n1w_ref, e2_ref, r2_ref, qt_ref, v1_ref, v2_ref, *, tb):
    x = x_ref[0]
    ms = jnp.mean(x * x, axis=-1, keepdims=True)
    xn = x * lax.rsqrt(ms + EPS) * g2_ref[...]
    h = xn * (1.0 + mod_ref[0, 4:5, :]) + mod_ref[0, 3:4, :]
    ht = h.T.astype(BF16)
    ht_ref[...] = ht
    qt_ref[...] = _dot(wqt_ref[...], ht)
    k_top = PEER_TOPK

    def tiles(s):
        return [s[SUBLANES * k:SUBLANES * (k + 1)] for k in range(PEER_NKEYS // SUBLANES)]

    def head_body(hh, carry):
        for tc in range(tb // LANE):
            tcol = slice(tc * LANE, (tc + 1) * LANE)
            r1 = pl.multiple_of(hh * (2 * PEER_HALF), 2 * PEER_HALF)
            q1 = qt_ref[pl.ds(r1, PEER_HALF), tcol].astype(BF16)
            q2 = qt_ref[pl.ds(r1 + PEER_HALF, PEER_HALF), tcol].astype(BF16)
            s1 = _dot(keys_ref[hh], q1)
            s2 = _dot(keys_ref[PEER_HEADS + hh], q2)
            for s, v_ref in ((s1, v1_ref), (s2, v2_ref)):
                top = _merge_sublanes(_compare_exchange(tiles(s), SORT16))
                for k in range(k_top):
                    v_ref[k:k + 1, tcol] = top[k][0:1]
            v1row = lambda a: v1_ref[a:a + 1, tcol]
            v2row = lambda b: v2_ref[b:b + 1, tcol]
            v1lo = v1_ref[0:SUBLANES, tcol]
            v2lo = v2_ref[0:SUBLANES, tcol]
            cand = [v1lo + v2row(b) for b in range(k_top)]
            tail = [v1row(a) + v2lo for a in range(SUBLANES, k_top)]
            for k in range(SUBLANES, k_top):
                cand[k] = jnp.maximum(cand[k], tail[k_top - 1 - k])
            best = _merge_sublanes(_compare_exchange(cand, BITONIC16))
            theta = best[k_top - 1][0:1]
            cmax = best[0][0:1]
            zsum = jnp.zeros((1, LANE), F32)
            for k in range(k_top):
                zsum = zsum + jnp.exp(best[k][0:1] - cmax)
            rz = 1.0 / zsum
            n_top = jnp.zeros((1, LANE), F32)
            for b in range(k_top):
                n_top = jnp.where(v1row(0) + v2row(b) >= theta, float(b + 1), n_top)
            n1 = jnp.zeros(s1.shape, F32)
            for b in range(SUBLANES):
                n1 = jnp.where(s1 + v2row(b) >= theta, float(b + 1), n1)
            n1 = jnp.where(s1 >= v1row(0), n_top, n1)
            r2 = jnp.zeros(s2.shape, F32)
            for b in range(k_top):
                r2 = jnp.where(v2row(b) > s2, float(b + 1), r2)
            e1 = jnp.where(s1 >= v1row(k_top - 1), jnp.exp(s1 - v1row(0)), 0.0) * rz
            e2 = jnp.where(s2 >= v2row(k_top - 1), jnp.exp(s2 - v2row(0)), 0.0)
            e1w_ref[tc, hh] = _dup_bf16_words(e1)
            n1w_ref[tc, hh] = _dup_bf16_words(n1)
            e2_ref[tc, hh] = pltpu.bitcast(e2.astype(BF16), jnp.int32)
            r2_ref[tc, hh] = pltpu.bitcast(r2.astype(BF16), jnp.int32)
        return carry

    lax.fori_loop(0, PEER_HEADS, head_body, 0)


def _stage_f(xs, mod, g2, wqt, keys, tb):
    b, l, d = xs.shape
    nblk = l // tb
    ntok = b * l
    nch = ntok // LANE
    kern = functools.partial(_stage_f_kernel, tb=tb)
    row_spec = pl.BlockSpec((tb // LANE, PEER_HEADS, PEER_NKEYS, LANE), lambda bi, i: (bi * nblk + i, 0, 0, 0))
    pair_spec = pl.BlockSpec((tb // LANE, PEER_HEADS, PEER_NKEYS // 2, LANE), lambda bi, i: (bi * nblk + i, 0, 0, 0))
    desc_shape = lambda rows: jax.ShapeDtypeStruct((nch, PEER_HEADS, rows, LANE), jnp.int32)
    return pl.pallas_call(
        kern,
        grid=(b, nblk),
        in_specs=[
            pl.BlockSpec((1, tb, d), lambda bi, i: (bi, i, 0)),
            pl.BlockSpec((1, 8, d), lambda bi, i: (bi, 0, 0)),
            _full_spec(g2.shape), _full_spec(wqt.shape), _full_spec(keys.shape),
        ],
        out_specs=[pl.BlockSpec((d, tb), lambda bi, i: (0, bi * nblk + i)),
                   row_spec, row_spec, pair_spec, pair_spec],
        out_shape=[jax.ShapeDtypeStruct((d, ntok), BF16), desc_shape(PEER_NKEYS), desc_shape(PEER_NKEYS),
                   desc_shape(PEER_NKEYS // 2), desc_shape(PEER_NKEYS // 2)],
        scratch_shapes=[
            pltpu.VMEM((PEER_HEADS * 2 * PEER_HALF, tb), F32),
            pltpu.VMEM((PEER_TOPK, tb), F32),
            pltpu.VMEM((PEER_TOPK, tb), F32),
        ],
        compiler_params=_cparams(("parallel", "parallel")),
    )(xs, mod, g2, wqt, keys)


PEER_TBG = 1024
PEER_TE = 1024
PEER_I1_PER_TILE = PEER_TE // PEER_NKEYS
PEER_N_TILES = PEER_NKEYS * PEER_NKEYS // PEER_TE
PEER_MXU_COLS = 256
PEER_G_FLAGS = None


def _stage_g_kernel(ht_ref, e1w_ref, n1w_ref, e2_ref, r2_ref, u_ref, vt_ref, x_ref, mod_ref,
                    o_ref, acc_ref, pa_ref, pb_ref, *, tb):
    i = pl.program_id(1)
    n_tiles = pl.num_programs(1) - 1

    @pl.when(i == 0)
    def _():
        acc_ref[...] = jnp.zeros_like(acc_ref)
        pb_ref[...] = jnp.zeros_like(pb_ref)

    def step(cur_ref, prev_ref):
        per_grp = PEER_MXU_COLS // LANE
        for grp in range(tb // PEER_MXU_COLS):
            cols = slice(grp * PEER_MXU_COLS, (grp + 1) * PEER_MXU_COLS)
            for tcl in range(per_grp):
                tc = grp * per_grp + tcl
                tcol = slice(tc * LANE, (tc + 1) * LANE)
                for i1l in range(PEER_I1_PER_TILE):
                    rows = slice(i1l * PEER_NKEYS, (i1l + 1) * PEER_NKEYS)
                    gate = jnp.zeros((PEER_NKEYS, LANE), BF16)
                    for hh in range(PEER_HEADS):
                        e1row = e1w_ref[tc, hh, i1l:i1l + 1, :]
                        n1row = n1w_ref[tc, hh, i1l:i1l + 1, :]
                        e1 = pltpu.bitcast(jnp.broadcast_to(e1row, (PEER_NKEYS // 2, LANE)), BF16)
                        n1 = pltpu.bitcast(jnp.broadcast_to(n1row, (PEER_NKEYS // 2, LANE)), BF16)
                        r2 = pltpu.bitcast(r2_ref[tc, hh], BF16)
                        e2 = pltpu.bitcast(e2_ref[tc, hh], BF16)
                        gate = gate + e2 * jnp.where(r2 < n1, e1, 0.0)
                    cur_ref[rows, tcol] = gate
            acc_ref[:, cols] += _dot(vt_ref[...], prev_ref[:, cols])
            at = _dot(u_ref[...], ht_ref[:, cols])
            for tcl in range(per_grp):
                tcol = slice((grp * per_grp + tcl) * LANE, (grp * per_grp + tcl + 1) * LANE)
                for i1l in range(PEER_I1_PER_TILE):
                    rows = slice(i1l * PEER_NKEYS, (i1l + 1) * PEER_NKEYS)
                    a = at[rows, tcl * LANE:(tcl + 1) * LANE]
                    cur_ref[rows, tcol] = cur_ref[rows, tcol] * _gelu(a).astype(BF16)

    @pl.when(jnp.logical_and(i < n_tiles, i % 2 == 0))
    def _():
        step(pa_ref, pb_ref)

    @pl.when(jnp.logical_and(i < n_tiles, i % 2 == 1))
    def _():
        step(pb_ref, pa_ref)

    @pl.when(i == n_tiles)
    def _():
        last_ref = pb_ref if (PEER_N_TILES - 1) % 2 else pa_ref
        acc = acc_ref[...] + _dot(vt_ref[...], last_ref[...])
        o_ref[...] = x_ref[...] + mod_ref[0, 5:6, :] * acc.T


def _stage_g(ht, e1w, n1w, e2, r2, u_bf, vt_bf, xflat, mod, tokens_per_batch, tb):
    d, ntok = ht.shape
    assert u_bf.shape[0] == PEER_N_TILES * PEER_TE
    blocks_per_batch = tokens_per_batch // tb
    kern = functools.partial(_stage_g_kernel, tb=tb)
    last = PEER_N_TILES - 1
    desc_spec = pl.BlockSpec((tb // LANE, PEER_HEADS, PEER_NKEYS // 2, LANE), lambda j, i: (j, 0, 0, 0))
    row_spec = pl.BlockSpec((tb // LANE, PEER_HEADS, PEER_I1_PER_TILE, LANE),
                            lambda j, i: (j, 0, jnp.minimum(i, last), 0))
    return pl.pallas_call(
        kern,
        grid=(ntok // tb, PEER_N_TILES + 1),
        in_specs=[
            pl.BlockSpec((d, tb), lambda j, i: (0, j)),
            row_spec, row_spec, desc_spec, desc_spec,
            pl.BlockSpec((PEER_TE, d), lambda j, i: (jnp.minimum(i, last), 0)),
            pl.BlockSpec((d, PEER_TE), lambda j, i: (0, jnp.maximum(i - 1, 0))),
            pl.BlockSpec((tb, d), lambda j, i: (j, 0)),
            pl.BlockSpec((1, 8, d), lambda j, i: (j // blocks_per_batch, 0, 0)),
        ],
        out_specs=pl.BlockSpec((tb, d), lambda j, i: (j, 0)),
        out_shape=jax.ShapeDtypeStruct((ntok, d), F32),
        scratch_shapes=[
            pltpu.VMEM((d, tb), F32),
            pltpu.VMEM((PEER_TE, tb), BF16),
            pltpu.VMEM((PEER_TE, tb), BF16),
        ],
        compiler_params=_cparams(("parallel", "arbitrary"), PEER_G_FLAGS),
    )(ht, e1w, n1w, e2, r2, u_bf, vt_bf, xflat, mod)


def _final_norm_kernel(x_ref, g_ref, o_ref):
    x = x_ref[...]
    ms = jnp.mean(x * x, axis=-1, keepdims=True)
    o_ref[...] = x * lax.rsqrt(ms + EPS) * g_ref[...]


def _final_norm(xflat, g, tb=512):
    n, d = xflat.shape
    return pl.pallas_call(
        _final_norm_kernel,
        grid=(n // tb,),
        in_specs=[pl.BlockSpec((tb, d), lambda i: (i, 0)), _full_spec(g.shape)],
        out_specs=pl.BlockSpec((tb, d), lambda i: (i, 0)),
        out_shape=jax.ShapeDtypeStruct((n, d), F32),
        compiler_params=_cparams(("parallel",)),
    )(xflat, g)


def _block_ones(n, blk):
    idx = np.arange(n) // blk
    return jnp.asarray((idx[:, None] == idx[None, :]).astype(np.float32), dtype=BF16)


def _s5_discretise(lam_re, lam_im, b_re, b_im, log_step):
    lam_re = jnp.minimum(lam_re.astype(F32), -1e-4)
    lam_im = lam_im.astype(F32)
    dt = jnp.exp(log_step.astype(F32))[:, None]
    mag = jnp.exp(lam_re * dt)
    a_re = mag * jnp.cos(lam_im * dt)
    a_im = mag * jnp.sin(lam_im * dt)
    den = lam_re * lam_re + lam_im * lam_im
    f_re = ((a_re - 1.0) * lam_re + a_im * lam_im) / den
    f_im = (a_im * lam_re - (a_re - 1.0) * lam_im) / den
    b_re = b_re.astype(F32)
    b_im = b_im.astype(F32)
    bb_re = f_re[..., None] * b_re - f_im[..., None] * b_im
    bb_im = f_re[..., None] * b_im + f_im[..., None] * b_re
    return a_re, a_im, bb_re, bb_im


def _group_block_diag(t):
    g, r, c = t.shape
    eye = jnp.eye(g, dtype=t.dtype)
    return (t[:, :, None, :] * eye[:, None, :, None]).reshape(g * r, g * c)


def _s5_params(lam_re, lam_im, b_re, b_im, c_re, c_im, log_step, nseq):
    b_rows, c_cols, ars, ais = [], [], [], []
    for d in range(2):
        a_re, a_im, bb_re, bb_im = _s5_discretise(lam_re[d], lam_im[d], b_re[d], b_im[d], log_step[d])
        bm = jnp.concatenate([_group_block_diag(jnp.swapaxes(bb_re, 1, 2)),
                              _group_block_diag(jnp.swapaxes(bb_im, 1, 2))], axis=1)
        b_rows.append(bm)
        cm = jnp.concatenate([_group_block_diag(jnp.swapaxes(c_re[d].astype(F32), 1, 2)),
                              -_group_block_diag(jnp.swapaxes(c_im[d].astype(F32), 1, 2))], axis=0)
        c_cols.append(cm)
        ars.append(jnp.broadcast_to(a_re.reshape(1, S5_LANES), (nseq, S5_LANES)))
        ais.append(jnp.broadcast_to(a_im.reshape(1, S5_LANES), (nseq, S5_LANES)))
    b2 = jnp.concatenate(b_rows, axis=0).astype(BF16)
    c2 = jnp.concatenate(c_cols, axis=1).astype(BF16)
    return b2, jnp.concatenate(ars, axis=0), jnp.concatenate(ais, axis=0), c2


def kernel(x, c, ctx, c_ctx, w_mod, b_mod, norm1_g, norm2_g, w_in, w_out, sgu_w, sgu_b, s5_lambda_re, s5_lambda_im, s5_b_re, s5_b_im, s5_c_re, s5_c_im, s5_log_step, s5_d, s5_w_glu, gla_w_gate, gla_b_gate, gla_norm_g, peer_w_query, peer_sub_keys, peer_expert_u, peer_expert_v, final_norm_g):
    nb, seq, d = x.shape
    c_len = ctx.shape[1]
    depth = w_mod.shape[0]

    cc = jnp.concatenate([c, c_ctx[None, :], jnp.zeros((8 - nb - 1, d), F32)], axis=0)
    mods = _mod_call(cc, w_mod, b_mod)

    ones_sgu = _block_ones(SGU_DIM, SGU_HEAD_DIM)
    ones_gla = _block_ones(GLA_DIM, GLA_DV)
    tri_np = np.tril(np.ones((GLA_CHUNK, GLA_CHUNK), np.float32))
    trif = jnp.asarray(tri_np, dtype=BF16)
    trib = jnp.asarray(tri_np.T, dtype=BF16)
    s_zero = jnp.zeros((nb, 2, GLA_DIM, GLA_KEY_DIM), F32)

    xl, xc = x, ctx
    for l in range(depth):
        ctx_out = l < depth - 1
        m6 = mods[l].reshape(8, N_MOD, d)
        mod_l = jnp.pad(m6[:nb], ((0, 0), (0, 2), (0, 0)))
        mod_c = jnp.broadcast_to(jnp.pad(m6[nb], ((0, 2), (0, 0)))[None], (nb, 8, d))

        win = jnp.pad(w_in[l], ((0, 0), (0, IN_PAD - IN_WIDTH))).astype(BF16)
        sguw = sgu_w[l].astype(BF16)
        sgub = jnp.repeat(jnp.swapaxes(sgu_b[l], 0, 1), SGU_HEAD_DIM, axis=1)
        gw = jnp.zeros((LANE, 2 * GLA_KEY_DIM), F32)
        gw = gw.at[0:GLA_RANK, 0:GLA_KEY_DIM].set(gla_w_gate[l, 0])
        gw = gw.at[GLA_RANK:2 * GLA_RANK, GLA_KEY_DIM:].set(gla_w_gate[l, 1]).astype(BF16)
        gb = gla_b_gate[l].reshape(1, 2 * GLA_KEY_DIM)
        g1 = norm1_g[l].reshape(1, d)

        s5_all = jnp.zeros((nb, seq + c_len, S5_DIM), F32)
        sgu_l, s5_all, qk_l, v_l, g_l, la_l = _stage_a(xl, mod_l, g1, win, sguw, sgub, ones_sgu, gw, gb,
                                                       s5_all, 0, tb=TB_LATENT, grid_layout=True)
        sgu_c, s5_all, qk_c, v_c, g_c, la_c = _stage_a(xc, mod_c, g1, win, sguw, sgub, ones_sgu, gw, gb,
                                                       s5_all, seq, tb=TB_CTX, grid_layout=False)

        b2, ar, ai, c2 = _s5_params(s5_lambda_re[l], s5_lambda_im[l], s5_b_re[l], s5_b_im[l],
                                    s5_c_re[l], s5_c_im[l], s5_log_step[l], nb)
        yf_all, yb_all = _s5_call(s5_all, seq, b2, ar, ai, c2)

        of_c, ob_c, s_ctx = _gla_call(qk_c, v_c, la_c, s_zero, trif, trib)
        of_l, ob_l, _ = _gla_call(qk_l, v_l, la_l, s_ctx, trif, trib)

        dskip = s5_d[l].reshape(1, S5_DIM)
        wglu = s5_w_glu[l].astype(BF16)
        normg = gla_norm_g[l].reshape(1, GLA_DIM)
        wout = w_out[l].astype(BF16)
        g2 = norm2_g[l].reshape(1, d)
        wqt = jnp.swapaxes(peer_w_query[l], 0, 1).astype(BF16)
        keys = peer_sub_keys[l].reshape(2 * PEER_HEADS, PEER_NKEYS, PEER_HALF).astype(BF16)
        u_bf = peer_expert_u[l].astype(BF16)
        vt_bf = jnp.swapaxes(peer_expert_v[l], 0, 1).astype(BF16)

        xl = _stage_e(xl, mod_l, sgu_l, yf_all, yb_all, s5_all, 0, of_l, ob_l, g_l,
                      dskip, wglu, normg, ones_gla, wout, tb=TB_LATENT)
        desc = _stage_f(xl, mod_l, g2, wqt, keys, tb=PEER_TBF)
        xl = _stage_g(*desc, u_bf, vt_bf, xl.reshape(nb * seq, d), mod_l, seq, PEER_TBG).reshape(nb, seq, d)

        if ctx_out:
            xc = _stage_e(xc, mod_c, sgu_c, yf_all, yb_all, s5_all, seq, of_c, ob_c, g_c,
                          dskip, wglu, normg, ones_gla, wout, tb=TB_CTX)
            desc = _stage_f(xc, mod_c, g2, wqt, keys, tb=PEER_TBF)
            xc = _stage_g(*desc, u_bf, vt_bf, xc.reshape(nb * c_len, d), mod_c, c_len,
                          min(PEER_TBG, c_len)).reshape(nb, c_len, d)

    return _final_norm(xl.reshape(nb * seq, d), final_norm_g.reshape(1, d)).reshape(nb, seq, d)
```

```python
import functools
import math

import numpy as np
import jax
import jax.numpy as jnp
from jax import lax
from jax.experimental import pallas as pl
from jax.experimental.pallas import tpu as pltpu

F32 = jnp.float32
BF16 = jnp.bfloat16

EPS = 1e-6
N_MOD = 6
GRID_W = 64

SGU_DIM = 256
SGU_HEADS = 4
SGU_HEAD_DIM = 64
SGU_CHUNK = 128

S5_DIM = 256
S5_GROUP = 16
S5_GROUPS = 16
S5_STATE = 64
S5_LANES = S5_GROUPS * S5_STATE

GLA_DIM = 512
GLA_HEADS = 8
GLA_DV = 64
GLA_DK = 32
GLA_KEY_DIM = 256
GLA_RANK = 16
GLA_GATE_TEMP = 16.0
GLA_CHUNK = 64

PEER_HEADS = 8
PEER_NKEYS = 128
PEER_HALF = 128
PEER_TOPK = 16

IN_WIDTH = 2336
IN_PAD = 2432
LANE = 128

VMEM_LIMIT = 56 * 1024 * 1024
TB_LATENT = 512
TB_CTX = 256

NEG_INF = float("-inf")
POS_INF = float("inf")


def _cparams(sem, flags=None):
    return pltpu.CompilerParams(dimension_semantics=sem, vmem_limit_bytes=VMEM_LIMIT, flags=flags)


def _gelu(x):
    c = math.sqrt(2.0 / math.pi)
    return 0.5 * x * (1.0 + jnp.tanh(c * (x + 0.044715 * (x * x * x))))


def _sigmoid(x):
    return 1.0 / (1.0 + jnp.exp(-x))


def _dot(a, b):
    return jnp.dot(a, b, preferred_element_type=F32)


def _dot_nt(a, b):
    return lax.dot_general(a, b, (((1,), (1,)), ((), ())), preferred_element_type=F32)


def _dot_tn(a, b):
    return lax.dot_general(a, b, (((0,), (0,)), ((), ())), preferred_element_type=F32)


def _split3(x):
    hi = x.astype(BF16)
    r = x - hi.astype(F32)
    mid = r.astype(BF16)
    lo = (r - mid.astype(F32)).astype(BF16)
    return hi, mid, lo


def _dot_x_exact(x, m):
    hi, mid, lo = _split3(x)
    return _dot(hi, m) + _dot(mid, m) + _dot(lo, m)


def _dot_m_exact(m, x):
    hi, mid, lo = _split3(x)
    return _dot(m, hi) + _dot(m, mid) + _dot(m, lo)


def _full_spec(shape):
    nd = len(shape)
    return pl.BlockSpec(shape, lambda *_: (0,) * nd)


MOD_TILE = 512


def _mod_kernel(c_ref, w_ref, b_ref, o_ref):
    c = c_ref[...]
    a = c * _sigmoid(c)
    o_ref[0] = jnp.dot(a, w_ref[0], preferred_element_type=F32,
                       precision=lax.Precision.HIGHEST) + b_ref[0]


def _mod_call(cc, w_mod, b_mod):
    depth, d, nd = w_mod.shape
    rows = cc.shape[0]
    return pl.pallas_call(
        _mod_kernel,
        grid=(depth, nd // MOD_TILE),
        in_specs=[
            pl.BlockSpec((rows, d), lambda l, j: (0, 0)),
            pl.BlockSpec((1, d, MOD_TILE), lambda l, j: (l, 0, j)),
            pl.BlockSpec((1, 1, MOD_TILE), lambda l, j: (l, 0, j)),
        ],
        out_specs=pl.BlockSpec((1, rows, MOD_TILE), lambda l, j: (l, 0, j)),
        out_shape=jax.ShapeDtypeStruct((depth, rows, nd), F32),
        compiler_params=_cparams(("parallel", "parallel")),
    )(cc, w_mod, b_mod.reshape(depth, 1, nd))


def _store_tokens(ref, val, grid_layout):
    if grid_layout:
        for r in range(val.shape[0] // GRID_W):
            ref[:, r, :] = val[r * GRID_W:(r + 1) * GRID_W]
    else:
        ref[0] = val


def _load_tokens(ref, grid_layout):
    if grid_layout:
        return jnp.concatenate([ref[:, r, :] for r in range(ref.shape[1])], axis=0)
    return ref[0]


def _stage_a_kernel(x_ref, mod_ref, g1_ref, win_ref, sguw_ref, sgub_ref, ones_ref, gw_ref, gb_ref, s5_in_ref,
                    sgu_ref, s5x_ref, qk_ref, v_ref, g_ref, la_ref, *, tb, grid_layout):
    del s5_in_ref
    x = x_ref[0]
    ms = jnp.mean(x * x, axis=-1, keepdims=True)
    xn = x * lax.rsqrt(ms + EPS) * g1_ref[...]
    h = xn * (1.0 + mod_ref[0, 1:2, :]) + mod_ref[0, 0:1, :]
    cols = _dot(h.astype(BF16), win_ref[...])

    u = _gelu(cols[:, 0:SGU_DIM])
    v = _gelu(cols[:, SGU_DIM:2 * SGU_DIM])
    msq = _dot_x_exact(v * v, ones_ref[...]) * (1.0 / SGU_HEAD_DIM)
    vn = (v * lax.rsqrt(msq + EPS)).astype(BF16)
    head_of_lane = lax.broadcasted_iota(jnp.int32, (SGU_CHUNK, SGU_DIM), 1) // SGU_HEAD_DIM
    for ci in range(tb // SGU_CHUNK):
        rows = slice(ci * SGU_CHUNK, (ci + 1) * SGU_CHUNK)
        vc = vn[rows]
        mixed = sgub_ref[...]
        for hh in range(SGU_HEADS):
            mixed = mixed + jnp.where(head_of_lane == hh, _dot(sguw_ref[hh], vc), 0.0)
        sgu_ref[0, rows, :] = u[rows] * mixed

    s5x_ref[0] = cols[:, 512:768]
    q = cols[:, 768:1024] * (GLA_DK ** -0.5)
    _store_tokens(qk_ref, jnp.concatenate([q, cols[:, 1024:1280]], axis=-1), grid_layout)
    _store_tokens(v_ref, cols[:, 1280:1792], grid_layout)
    g_ref[0] = cols[:, 1792:2304]

    z = cols[:, 2304:2432].astype(BF16)
    za = _dot(z, gw_ref[...]) + gb_ref[...]
    log_sig = jnp.minimum(za, 0.0) - jnp.log1p(jnp.exp(-jnp.abs(za)))
    _store_tokens(la_ref, log_sig * (1.0 / GLA_GATE_TEMP), grid_layout)


def _stage_a(xs, mod, g1, win, sguw, sgub, ones_sgu, gw, gb, s5_all, s5_row0, tb, grid_layout):
    b, l, d = xs.shape
    assert s5_row0 % tb == 0
    kern = functools.partial(_stage_a_kernel, tb=tb, grid_layout=grid_layout)
    tok = lambda w: pl.BlockSpec((1, tb, w), lambda bi, i: (bi, i, 0))
    if grid_layout:
        assert tb % GRID_W == 0 and l % GRID_W == 0
        gla = lambda w: pl.BlockSpec((GRID_W, None, tb // GRID_W, w), lambda bi, i: (0, bi, i, 0))
        gla_shape = lambda w: jax.ShapeDtypeStruct((GRID_W, b, l // GRID_W, w), F32)
    else:
        gla = tok
        gla_shape = lambda w: jax.ShapeDtypeStruct((b, l, w), F32)
    nat_shape = lambda w: jax.ShapeDtypeStruct((b, l, w), F32)
    return pl.pallas_call(
        kern,
        grid=(b, l // tb),
        in_specs=[
            tok(d),
            pl.BlockSpec((1, 8, d), lambda bi, i: (bi, 0, 0)),
            _full_spec(g1.shape), _full_spec(win.shape), _full_spec(sguw.shape), _full_spec(sgub.shape),
            _full_spec(ones_sgu.shape), _full_spec(gw.shape), _full_spec(gb.shape),
            pl.BlockSpec(memory_space=pl.ANY),
        ],
        out_specs=[tok(SGU_DIM),
                   pl.BlockSpec((1, tb, S5_DIM), lambda bi, i: (bi, s5_row0 // tb + i, 0)),
                   gla(2 * GLA_KEY_DIM), gla(GLA_DIM), tok(GLA_DIM), gla(2 * GLA_KEY_DIM)],
        out_shape=[nat_shape(SGU_DIM), jax.ShapeDtypeStruct(s5_all.shape, F32),
                   gla_shape(2 * GLA_KEY_DIM), gla_shape(GLA_DIM), nat_shape(GLA_DIM), gla_shape(2 * GLA_KEY_DIM)],
        input_output_aliases={9: 1},
        compiler_params=_cparams(("parallel", "parallel")),
    )(xs, mod, g1, win, sguw, sgub, ones_sgu, gw, gb, s5_all)


S5_TC = 128


def _s5_kernel(uf_ref, ub_ref, perm_ref, permt_ref, b2_ref, ar_ref, ai_ref, c2_ref, yf_ref, yb_ref,
               h_ref, buf_ref, *, nseq):
    tc = S5_TC
    rows = tc * 2 * nseq

    @pl.when(pl.program_id(0) == 0)
    def _():
        h_ref[...] = jnp.zeros_like(h_ref)

    x = jnp.concatenate([uf_ref[b] for b in range(nseq)] + [ub_ref[b] for b in range(nseq)], axis=0)
    u_tm = _dot(perm_ref[...], x.astype(BF16))
    fwd_row = lax.broadcasted_iota(jnp.int32, (rows, S5_DIM), 0) % (2 * nseq) < nseq
    u = jnp.concatenate([jnp.where(fwd_row, u_tm, 0.0), jnp.where(fwd_row, 0.0, u_tm)], axis=-1).astype(BF16)
    buf_ref[...] = _dot(u, b2_ref[...]).reshape(tc, 2 * nseq, 2 * S5_LANES)
    ar = ar_ref[...]
    ai = ai_ref[...]

    def step(t, carry):
        hr, hi = carry
        bur = buf_ref[t, :, 0:S5_LANES]
        bui = buf_ref[t, :, S5_LANES:]
        nhr = ar * hr - ai * hi + bur
        nhi = ar * hi + ai * hr + bui
        buf_ref[t, :, 0:S5_LANES] = nhr
        buf_ref[t, :, S5_LANES:] = nhi
        return nhr, nhi

    hr, hi = lax.fori_loop(0, tc, step, (h_ref[:, 0:S5_LANES], h_ref[:, S5_LANES:]))
    h_ref[:, 0:S5_LANES] = hr
    h_ref[:, S5_LANES:] = hi

    hs = buf_ref[...].reshape(rows, 2 * S5_LANES).astype(BF16)
    y2 = _dot(hs, c2_ref[...])
    y = jnp.where(fwd_row, y2[:, 0:S5_DIM], y2[:, S5_DIM:])
    y_hi = y.astype(BF16)
    y_lo = (y - y_hi.astype(F32)).astype(BF16)
    y_nat = _dot(permt_ref[...], y_hi) + _dot(permt_ref[...], y_lo)
    for b in range(nseq):
        yf_ref[b] = y_nat[b * tc:(b + 1) * tc]
        yb_ref[b] = y_nat[(nseq + b) * tc:(nseq + b + 1) * tc]


def _s5_permutation(nseq):
    tc = S5_TC
    p = np.zeros((tc * 2 * nseq, tc * 2 * nseq), np.float32)
    for q in range(2 * nseq):
        for t in range(tc):
            p[t * 2 * nseq + q, q * tc + (t if q < nseq else tc - 1 - t)] = 1.0
    return p


def _s5_call(s5_all, n_lat, b2, ar, ai, c2):
    nseq, t, _ = s5_all.shape
    n = t // S5_TC
    n_l = n_lat // S5_TC
    n_c = n - n_l
    perm = _s5_permutation(nseq)
    kern = functools.partial(_s5_kernel, nseq=nseq)
    fwd_map = lambda s: (0, jnp.where(s < n_c, n_l + s, s - n_c), 0)
    bwd_map = lambda s: (0, n - 1 - s, 0)
    blk = (nseq, S5_TC, S5_DIM)
    rows = 2 * nseq
    return pl.pallas_call(
        kern,
        grid=(n,),
        in_specs=[
            pl.BlockSpec(blk, fwd_map), pl.BlockSpec(blk, bwd_map),
            _full_spec(perm.shape), _full_spec(perm.shape),
            _full_spec(b2.shape), _full_spec(ar.shape), _full_spec(ai.shape), _full_spec(c2.shape),
        ],
        out_specs=[pl.BlockSpec(blk, fwd_map), pl.BlockSpec(blk, bwd_map)],
        out_shape=[jax.ShapeDtypeStruct(s5_all.shape, F32), jax.ShapeDtypeStruct(s5_all.shape, F32)],
        scratch_shapes=[
            pltpu.VMEM((rows, 2 * S5_LANES), F32),
            pltpu.VMEM((S5_TC, rows, 2 * S5_LANES), F32),
        ],
        compiler_params=_cparams(("arbitrary",)),
    )(s5_all, s5_all, jnp.asarray(perm, dtype=BF16), jnp.asarray(perm.T, dtype=BF16), b2, ar, ai, c2)


def _gla_stream(q, k, v, la, s_t, tri, last_row, ref_row, tri_mask, hm_k, hm_v, hm_s):
    bcum = _dot_m_exact(tri, la)
    blast = bcum[last_row:last_row + 1]
    bref = bcum[ref_row:ref_row + 1]
    qe = q * jnp.exp(bcum)
    qd = q * jnp.exp(bcum - bref)
    kd = k * jnp.exp(bref - bcum)
    kdec = k * jnp.exp(blast - bcum)
    kst = jnp.where(hm_k, jnp.concatenate([kd] * GLA_HEADS, axis=0), 0.0).astype(BF16)
    sc = jnp.where(tri_mask, _dot_nt(qd.astype(BF16), kst), 0.0)
    vbd = jnp.where(hm_v, jnp.concatenate([v] * GLA_HEADS, axis=0), 0.0).astype(BF16)
    o = _dot(sc.astype(BF16), vbd) + _dot_nt(qe.astype(BF16), s_t.astype(BF16))
    kv_t = jnp.where(hm_s, _dot_tn(v.astype(BF16), kdec.astype(BF16)), 0.0)
    s_new = s_t * jnp.exp(blast) + kv_t
    return o, s_new


def _gla_kernel(qkf_ref, vf_ref, laf_ref, qkb_ref, vb_ref, lab_ref, s0_ref, trif_ref, trib_ref,
                of_ref, ob_ref, sout_ref, s_ref, *, nb):
    c = pl.program_id(0)

    @pl.when(c == 0)
    def _():
        s_ref[...] = s0_ref[...]

    ch = GLA_CHUNK
    r_k = lax.broadcasted_iota(jnp.int32, (GLA_HEADS * ch, GLA_KEY_DIM), 0) // ch
    c_k = lax.broadcasted_iota(jnp.int32, (GLA_HEADS * ch, GLA_KEY_DIM), 1) // GLA_DK
    hm_k = r_k == c_k
    r_v = lax.broadcasted_iota(jnp.int32, (GLA_HEADS * ch, GLA_DIM), 0) // ch
    c_v = lax.broadcasted_iota(jnp.int32, (GLA_HEADS * ch, GLA_DIM), 1) // GLA_DV
    hm_v = r_v == c_v
    r_s = lax.broadcasted_iota(jnp.int32, (GLA_DIM, GLA_KEY_DIM), 0) // GLA_DV
    c_s = lax.broadcasted_iota(jnp.int32, (GLA_DIM, GLA_KEY_DIM), 1) // GLA_DK
    hm_s = r_s == c_s
    t_i = lax.broadcasted_iota(jnp.int32, (ch, GLA_HEADS * ch), 0)
    s_i = lax.broadcasted_iota(jnp.int32, (ch, GLA_HEADS * ch), 1) % ch
    mask_f = t_i >= s_i
    mask_b = t_i <= s_i
    trif = trif_ref[...]
    trib = trib_ref[...]

    def body(b, carry):
        qk = qkf_ref[b]
        o, s_new = _gla_stream(qk[:, 0:GLA_KEY_DIM], qk[:, GLA_KEY_DIM:], vf_ref[b], laf_ref[b], s_ref[b, 0],
                               trif, ch - 1, ch // 2, mask_f, hm_k, hm_v, hm_s)
        of_ref[b] = o
        s_ref[b, 0] = s_new
        qk = qkb_ref[b]
        o, s_new = _gla_stream(qk[:, 0:GLA_KEY_DIM], qk[:, GLA_KEY_DIM:], vb_ref[b], lab_ref[b], s_ref[b, 1],
                               trib, 0, ch - 1 - ch // 2, mask_b, hm_k, hm_v, hm_s)
        ob_ref[b] = o
        s_ref[b, 1] = s_new
        return carry

    lax.fori_loop(0, nb, body, 0)

    @pl.when(c == pl.num_programs(0) - 1)
    def _():
        sout_ref[...] = s_ref[...]


def _gla_call(qk, v, la, s0, trif, trib):
    ch = GLA_CHUNK
    if qk.ndim == 4:
        n, b = qk.shape[0], qk.shape[1]
        assert qk.shape[2] == ch
        spec = lambda w, off, rev: pl.BlockSpec(
            (None, b, ch, w), (lambda c: (n - 1 - c, 0, 0, off)) if rev else (lambda c: (c, 0, 0, off)))
    else:
        b = qk.shape[0]
        n = qk.shape[1] // ch
        spec = lambda w, off, rev: pl.BlockSpec(
            (b, ch, w), (lambda c: (0, n - 1 - c, off)) if rev else (lambda c: (0, c, off)))
    kern = functools.partial(_gla_kernel, nb=b)
    o_shape = jax.ShapeDtypeStruct(v.shape, F32)
    return pl.pallas_call(
        kern,
        grid=(n,),
        in_specs=[
            spec(2 * GLA_KEY_DIM, 0, False), spec(GLA_DIM, 0, False), spec(GLA_KEY_DIM, 0, False),
            spec(2 * GLA_KEY_DIM, 0, True), spec(GLA_DIM, 0, True), spec(GLA_KEY_DIM, 1, True),
            _full_spec(s0.shape), _full_spec(trif.shape), _full_spec(trib.shape),
        ],
        out_specs=[spec(GLA_DIM, 0, False), spec(GLA_DIM, 0, True), _full_spec(s0.shape)],
        out_shape=[o_shape, o_shape, jax.ShapeDtypeStruct(s0.shape, F32)],
        scratch_shapes=[pltpu.VMEM(s0.shape, F32)],
        compiler_params=_cparams(("arbitrary",)),
    )(qk, v, la, qk, v, la, s0, trif, trib)


def _stage_e_kernel(x_ref, mod_ref, sgu_ref, yf_ref, yb_ref, s5x_ref, of_ref, ob_ref, g_ref,
                    dskip_ref, wglu_ref, normg_ref, ones_ref, wout_ref, o_ref, *, grid_layout):
    ys = yf_ref[0] + yb_ref[0] + dskip_ref[...] * s5x_ref[0]
    z = _dot(_gelu(ys).astype(BF16), wglu_ref[...])
    s5o = z[:, 0:S5_DIM] * _sigmoid(z[:, S5_DIM:])
    o = _load_tokens(of_ref, grid_layout) + _load_tokens(ob_ref, grid_layout)
    ms = _dot_x_exact(o * o, ones_ref[...]) * (1.0 / GLA_DV)
    g = g_ref[0]
    gl = o * lax.rsqrt(ms + EPS) * normg_ref[...] * (g * _sigmoid(g))
    y = (_dot(sgu_ref[0].astype(BF16), wout_ref[0:SGU_DIM, :])
         + _dot(s5o.astype(BF16), wout_ref[SGU_DIM:SGU_DIM + S5_DIM, :])
         + _dot(gl.astype(BF16), wout_ref[SGU_DIM + S5_DIM:, :]))
    o_ref[0] = x_ref[0] + mod_ref[0, 2:3, :] * y


def _stage_e(xs, mod, sgu, yf_all, yb_all, s5_all, s5_row0, of, ob, g, dskip, wglu, normg, ones_gla, wout, tb):
    b, l, d = xs.shape
    assert s5_row0 % tb == 0
    grid_layout = of.ndim == 4
    tok = lambda w: pl.BlockSpec((1, tb, w), lambda bi, i: (bi, i, 0))
    s5 = pl.BlockSpec((1, tb, S5_DIM), lambda bi, i: (bi, s5_row0 // tb + i, 0))
    if grid_layout:
        assert tb % GRID_W == 0
        gla = pl.BlockSpec((GRID_W, None, tb // GRID_W, GLA_DIM), lambda bi, i: (0, bi, i, 0))
    else:
        gla = tok(GLA_DIM)
    return pl.pallas_call(
        functools.partial(_stage_e_kernel, grid_layout=grid_layout),
        grid=(b, l // tb),
        in_specs=[
            tok(d), pl.BlockSpec((1, 8, d), lambda bi, i: (bi, 0, 0)),
            tok(SGU_DIM), s5, s5, s5, gla, gla, tok(GLA_DIM),
            _full_spec(dskip.shape), _full_spec(wglu.shape), _full_spec(normg.shape),
            _full_spec(ones_gla.shape), _full_spec(wout.shape),
        ],
        out_specs=tok(d),
        out_shape=jax.ShapeDtypeStruct((b, l, d), F32),
        compiler_params=_cparams(("parallel", "parallel")),
    )(xs, mod, sgu, yf_all, yb_all, s5_all, of, ob, g, dskip, wglu, normg, ones_gla, wout)


PEER_TBF = 256
SUBLANES = 8


def _sort_network_16():
    def merge(lo, hi, r):
        step = r * 2
        if step < hi - lo:
            yield from merge(lo, hi, step)
            yield from merge(lo + r, hi, step)
            for i in range(lo + r, hi - r, step):
                yield (i, i + r)
        else:
            yield (lo, lo + r)

    def sort(lo, hi):
        if hi - lo >= 1:
            mid = lo + (hi - lo) // 2
            yield from sort(lo, mid)
            yield from sort(mid + 1, hi)
            yield from merge(lo, hi, 1)

    return tuple(sort(0, PEER_TOPK - 1))


SORT16 = _sort_network_16()
BITONIC16 = tuple((k, k + s) for s in (8, 4, 2, 1) for k in range(PEER_TOPK) if not k & s)


def _compare_exchange(xs, pairs):
    xs = list(xs)
    for i, j in pairs:
        hi = jnp.maximum(xs[i], xs[j])
        lo = jnp.minimum(xs[i], xs[j])
        xs[i], xs[j] = hi, lo
    return xs


def _merge_sublanes(xs):
    for shift in (4, 6, 7):
        rolled = [pltpu.roll(x, shift, 0) for x in xs]
        xs = [jnp.maximum(xs[k], rolled[PEER_TOPK - 1 - k]) for k in range(PEER_TOPK)]
        xs = _compare_exchange(xs, BITONIC16)
    return xs


def _dup_bf16_words(x):
    bits = pltpu.bitcast(x.astype(BF16).astype(F32), jnp.int32)
    return bits | lax.shift_right_logical(bits, 16)


def _stage_f_kernel(x_ref, mod_ref, g2_ref, wqt_ref, keys_ref,
                    ht_ref, e1w_ref, n1w_ref, e2_ref, r2_ref, qt_ref, v1_ref, v2_ref, *, tb):
    x = x_ref[0]
    ms = jnp.mean(x * x, axis=-1, keepdims=True)
    xn = x * lax.rsqrt(ms + EPS) * g2_ref[...]
    h = xn * (1.0 + mod_ref[0, 4:5, :]) + mod_ref[0, 3:4, :]
    ht = h.T.astype(BF16)
    ht_ref[...] = pltpu.bitcast(ht, jnp.int32)
    qt_ref[...] = _dot(wqt_ref[...], ht)
    k_top = PEER_TOPK

    def tiles(s):
        return [s[SUBLANES * k:SUBLANES * (k + 1)] for k in range(PEER_NKEYS // SUBLANES)]

    def head_body(hh, carry):
        for tc in range(tb // LANE):
            tcol = slice(tc * LANE, (tc + 1) * LANE)
            r1 = pl.multiple_of(hh * (2 * PEER_HALF), 2 * PEER_HALF)
            q1 = qt_ref[pl.ds(r1, PEER_HALF), tcol].astype(BF16)
            q2 = qt_ref[pl.ds(r1 + PEER_HALF, PEER_HALF), tcol].astype(BF16)
            s1 = _dot(keys_ref[hh], q1)
            s2 = _dot(keys_ref[PEER_HEADS + hh], q2)
            for s, v_ref in ((s1, v1_ref), (s2, v2_ref)):
                top = _merge_sublanes(_compare_exchange(tiles(s), SORT16))
                for k in range(k_top):
                    v_ref[k:k + 1, tcol] = top[k][0:1]
            v1row = lambda a: v1_ref[a:a + 1, tcol]
            v2row = lambda b: v2_ref[b:b + 1, tcol]
            v1lo = v1_ref[0:SUBLANES, tcol]
            v2lo = v2_ref[0:SUBLANES, tcol]
            cand = [v1lo + v2row(b) for b in range(k_top)]
            tail = [v1row(a) + v2lo for a in range(SUBLANES, k_top)]
            for k in range(SUBLANES, k_top):
                cand[k] = jnp.maximum(cand[k], tail[k_top - 1 - k])
            best = _merge_sublanes(_compare_exchange(cand, BITONIC16))
            theta = best[k_top - 1][0:1]
            cmax = best[0][0:1]
            zsum = jnp.zeros((1, LANE), F32)
            for k in range(k_top):
                zsum = zsum + jnp.exp(best[k][0:1] - cmax)
            rz = 1.0 / zsum
            n_top = jnp.zeros((1, LANE), F32)
            for b in range(k_top):
                n_top = jnp.where(v1row(0) + v2row(b) >= theta, float(b + 1), n_top)
            n1 = jnp.zeros(s1.shape, F32)
            for b in range(SUBLANES):
                n1 = jnp.where(s1 + v2row(b) >= theta, float(b + 1), n1)
            n1 = jnp.where(s1 >= v1row(0), n_top, n1)
            r2 = jnp.zeros(s2.shape, F32)
            for b in range(k_top):
                r2 = jnp.where(v2row(b) > s2, float(b + 1), r2)
            e1 = jnp.where(s1 >= v1row(k_top - 1), jnp.exp(s1 - v1row(0)), 0.0) * rz
            e2 = jnp.where(s2 >= v2row(k_top - 1), jnp.exp(s2 - v2row(0)), 0.0)
            e1w_ref[tc, hh] = _dup_bf16_words(e1)
            n1w_ref[tc, hh] = _dup_bf16_words(n1)
            e2_ref[tc, hh] = pltpu.bitcast(e2.astype(BF16), jnp.int32)
            r2_ref[tc, hh] = pltpu.bitcast(r2.astype(BF16), jnp.int32)
        return carry

    lax.fori_loop(0, PEER_HEADS, head_body, 0)


def _stage_f(xs, mod, g2, wqt, keys, tb):
    b, l, d = xs.shape
    nblk = l // tb
    ntok = b * l
    nch = ntok // LANE
    kern = functools.partial(_stage_f_kernel, tb=tb)
    row_spec = pl.BlockSpec((tb // LANE, PEER_HEADS, PEER_NKEYS, LANE), lambda bi, i: (bi * nblk + i, 0, 0, 0))
    pair_spec = pl.BlockSpec((tb // LANE, PEER_HEADS, PEER_NKEYS // 2, LANE), lambda bi, i: (bi * nblk + i, 0, 0, 0))
    desc_shape = lambda rows: jax.ShapeDtypeStruct((nch, PEER_HEADS, rows, LANE), jnp.int32)
    return pl.pallas_call(
        kern,
        grid=(b, nblk),
        in_specs=[
            pl.BlockSpec((1, tb, d), lambda bi, i: (bi, i, 0)),
            pl.BlockSpec((1, 8, d), lambda bi, i: (bi, 0, 0)),
            _full_spec(g2.shape), _full_spec(wqt.shape), _full_spec(keys.shape),
        ],
        out_specs=[pl.BlockSpec((d // 2, tb), lambda bi, i: (0, bi * nblk + i)),
                   row_spec, row_spec, pair_spec, pair_spec],
        out_shape=[jax.ShapeDtypeStruct((d // 2, ntok), jnp.int32), desc_shape(PEER_NKEYS), desc_shape(PEER_NKEYS),
                   desc_shape(PEER_NKEYS // 2), desc_shape(PEER_NKEYS // 2)],
        scratch_shapes=[
            pltpu.VMEM((PEER_HEADS * 2 * PEER_HALF, tb), F32),
            pltpu.VMEM((PEER_TOPK, tb), F32),
            pltpu.VMEM((PEER_TOPK, tb), F32),
        ],
        compiler_params=_cparams(("parallel", "parallel")),
    )(xs, mod, g2, wqt, keys)


PEER_TBG = 1024
PEER_TE = 1024
PEER_I1_PER_TILE = PEER_TE // PEER_NKEYS
PEER_N_TILES = PEER_NKEYS * PEER_NKEYS // PEER_TE
PEER_MXU_COLS = 256
PEER_G_FLAGS = None


def _stage_g_kernel(htw_ref, e1w_ref, n1w_ref, e2_ref, r2_ref, uw_ref, vtw_ref, x_ref, mod_ref,
                    o_ref, acc_ref, pa_ref, pb_ref, *, tb):
    i = pl.program_id(1)
    n_tiles = pl.num_programs(1) - 1

    @pl.when(i == 0)
    def _():
        acc_ref[...] = jnp.zeros_like(acc_ref)
        pb_ref[...] = jnp.zeros_like(pb_ref)

    def step(cur_ref, prev_ref):
        per_grp = PEER_MXU_COLS // LANE
        for grp in range(tb // PEER_MXU_COLS):
            cols = slice(grp * PEER_MXU_COLS, (grp + 1) * PEER_MXU_COLS)
            for tcl in range(per_grp):
                tc = grp * per_grp + tcl
                tcol = slice(tc * LANE, (tc + 1) * LANE)
                for i1l in range(PEER_I1_PER_TILE):
                    rows = slice(i1l * PEER_NKEYS, (i1l + 1) * PEER_NKEYS)
                    gate = jnp.zeros((PEER_NKEYS, LANE), BF16)
                    for hh in range(PEER_HEADS):
                        e1row = e1w_ref[tc, hh, i1l:i1l + 1, :]
                        n1row = n1w_ref[tc, hh, i1l:i1l + 1, :]
                        e1 = pltpu.bitcast(jnp.broadcast_to(e1row, (PEER_NKEYS // 2, LANE)), BF16)
                        n1 = pltpu.bitcast(jnp.broadcast_to(n1row, (PEER_NKEYS // 2, LANE)), BF16)
                        r2 = pltpu.bitcast(r2_ref[tc, hh], BF16)
                        e2 = pltpu.bitcast(e2_ref[tc, hh], BF16)
                        gate = gate + e2 * jnp.where(r2 < n1, e1, 0.0)
                    cur_ref[rows, tcol] = gate
            vt = pltpu.bitcast(vtw_ref[...], BF16)
            u = pltpu.bitcast(uw_ref[...], BF16)
            ht = pltpu.bitcast(htw_ref[:, cols], BF16)
            acc_ref[:, cols] += _dot(vt, prev_ref[:, cols])
            at = _dot(u, ht).astype(BF16)
            for tcl in range(per_grp):
                tcol = slice((grp * per_grp + tcl) * LANE, (grp * per_grp + tcl + 1) * LANE)
                for i1l in range(PEER_I1_PER_TILE):
                    rows = slice(i1l * PEER_NKEYS, (i1l + 1) * PEER_NKEYS)
                    a = at[rows, tcl * LANE:(tcl + 1) * LANE]
                    cur_ref[rows, tcol] = cur_ref[rows, tcol] * _gelu(a)

    @pl.when(jnp.logical_and(i < n_tiles, i % 2 == 0))
    def _():
        step(pa_ref, pb_ref)

    @pl.when(jnp.logical_and(i < n_tiles, i % 2 == 1))
    def _():
        step(pb_ref, pa_ref)

    @pl.when(i == n_tiles)
    def _():
        last_ref = pb_ref if (PEER_N_TILES - 1) % 2 else pa_ref
        acc = acc_ref[...] + _dot(pltpu.bitcast(vtw_ref[...], BF16), last_ref[...])
        o_ref[...] = x_ref[...] + mod_ref[0, 5:6, :] * acc.T


def _stage_g(htw, e1w, n1w, e2, r2, uw, vtw, xflat, mod, tokens_per_batch, tb):
    ntok = htw.shape[1]
    d = 2 * htw.shape[0]
    assert 2 * uw.shape[0] == PEER_N_TILES * PEER_TE
    blocks_per_batch = tokens_per_batch // tb
    kern = functools.partial(_stage_g_kernel, tb=tb)
    last = PEER_N_TILES - 1
    desc_spec = pl.BlockSpec((tb // LANE, PEER_HEADS, PEER_NKEYS // 2, LANE), lambda j, i: (j, 0, 0, 0))
    row_spec = pl.BlockSpec((tb // LANE, PEER_HEADS, PEER_I1_PER_TILE, LANE),
                            lambda j, i: (j, 0, jnp.minimum(i, last), 0))
    return pl.pallas_call(
        kern,
        grid=(ntok // tb, PEER_N_TILES + 1),
        in_specs=[
            pl.BlockSpec((d // 2, tb), lambda j, i: (0, j)),
            row_spec, row_spec, desc_spec, desc_spec,
            pl.BlockSpec((PEER_TE // 2, d), lambda j, i: (jnp.minimum(i, last), 0)),
            pl.BlockSpec((d // 2, PEER_TE), lambda j, i: (0, jnp.maximum(i - 1, 0))),
            pl.BlockSpec((tb, d), lambda j, i: (j, 0)),
            pl.BlockSpec((1, 8, d), lambda j, i: (j // blocks_per_batch, 0, 0)),
        ],
        out_specs=pl.BlockSpec((tb, d), lambda j, i: (j, 0)),
        out_shape=jax.ShapeDtypeStruct((ntok, d), F32),
        scratch_shapes=[
            pltpu.VMEM((d, tb), F32),
            pltpu.VMEM((PEER_TE, tb), BF16),
            pltpu.VMEM((PEER_TE, tb), BF16),
        ],
        compiler_params=_cparams(("parallel", "arbitrary"), PEER_G_FLAGS),
    )(htw, e1w, n1w, e2, r2, uw, vtw, xflat, mod)


def _final_norm_kernel(x_ref, g_ref, o_ref):
    x = x_ref[...]
    ms = jnp.mean(x * x, axis=-1, keepdims=True)
    o_ref[...] = x * lax.rsqrt(ms + EPS) * g_ref[...]


def _final_norm(xflat, g, tb=512):
    n, d = xflat.shape
    return pl.pallas_call(
        _final_norm_kernel,
        grid=(n // tb,),
        in_specs=[pl.BlockSpec((tb, d), lambda i: (i, 0)), _full_spec(g.shape)],
        out_specs=pl.BlockSpec((tb, d), lambda i: (i, 0)),
        out_shape=jax.ShapeDtypeStruct((n, d), F32),
        compiler_params=_cparams(("parallel",)),
    )(xflat, g)


def _pack_row_pairs(x):
    r2, c = x.shape
    return lax.bitcast_convert_type(jnp.swapaxes(x.reshape(r2 // 2, 2, c), 1, 2), jnp.int32)


def _block_ones(n, blk):
    idx = np.arange(n) // blk
    return jnp.asarray((idx[:, None] == idx[None, :]).astype(np.float32), dtype=BF16)


def _s5_discretise(lam_re, lam_im, b_re, b_im, log_step):
    lam_re = jnp.minimum(lam_re.astype(F32), -1e-4)
    lam_im = lam_im.astype(F32)
    dt = jnp.exp(log_step.astype(F32))[:, None]
    mag = jnp.exp(lam_re * dt)
    a_re = mag * jnp.cos(lam_im * dt)
    a_im = mag * jnp.sin(lam_im * dt)
    den = lam_re * lam_re + lam_im * lam_im
    f_re = ((a_re - 1.0) * lam_re + a_im * lam_im) / den
    f_im = (a_im * lam_re - (a_re - 1.0) * lam_im) / den
    b_re = b_re.astype(F32)
    b_im = b_im.astype(F32)
    bb_re = f_re[..., None] * b_re - f_im[..., None] * b_im
    bb_im = f_re[..., None] * b_im + f_im[..., None] * b_re
    return a_re, a_im, bb_re, bb_im


def _group_block_diag(t):
    g, r, c = t.shape
    eye = jnp.eye(g, dtype=t.dtype)
    return (t[:, :, None, :] * eye[:, None, :, None]).reshape(g * r, g * c)


def _s5_params(lam_re, lam_im, b_re, b_im, c_re, c_im, log_step, nseq):
    b_rows, c_cols, ars, ais = [], [], [], []
    for d in range(2):
        a_re, a_im, bb_re, bb_im = _s5_discretise(lam_re[d], lam_im[d], b_re[d], b_im[d], log_step[d])
        bm = jnp.concatenate([_group_block_diag(jnp.swapaxes(bb_re, 1, 2)),
                              _group_block_diag(jnp.swapaxes(bb_im, 1, 2))], axis=1)
        b_rows.append(bm)
        cm = jnp.concatenate([_group_block_diag(jnp.swapaxes(c_re[d].astype(F32), 1, 2)),
                              -_group_block_diag(jnp.swapaxes(c_im[d].astype(F32), 1, 2))], axis=0)
        c_cols.append(cm)
        ars.append(jnp.broadcast_to(a_re.reshape(1, S5_LANES), (nseq, S5_LANES)))
        ais.append(jnp.broadcast_to(a_im.reshape(1, S5_LANES), (nseq, S5_LANES)))
    b2 = jnp.concatenate(b_rows, axis=0).astype(BF16)
    c2 = jnp.concatenate(c_cols, axis=1).astype(BF16)
    return b2, jnp.concatenate(ars, axis=0), jnp.concatenate(ais, axis=0), c2


def kernel(x, c, ctx, c_ctx, w_mod, b_mod, norm1_g, norm2_g, w_in, w_out, sgu_w, sgu_b, s5_lambda_re, s5_lambda_im, s5_b_re, s5_b_im, s5_c_re, s5_c_im, s5_log_step, s5_d, s5_w_glu, gla_w_gate, gla_b_gate, gla_norm_g, peer_w_query, peer_sub_keys, peer_expert_u, peer_expert_v, final_norm_g):
    nb, seq, d = x.shape
    c_len = ctx.shape[1]
    depth = w_mod.shape[0]

    cc = jnp.concatenate([c, c_ctx[None, :], jnp.zeros((8 - nb - 1, d), F32)], axis=0)
    mods = _mod_call(cc, w_mod, b_mod)

    ones_sgu = _block_ones(SGU_DIM, SGU_HEAD_DIM)
    ones_gla = _block_ones(GLA_DIM, GLA_DV)
    tri_np = np.tril(np.ones((GLA_CHUNK, GLA_CHUNK), np.float32))
    trif = jnp.asarray(tri_np, dtype=BF16)
    trib = jnp.asarray(tri_np.T, dtype=BF16)
    s_zero = jnp.zeros((nb, 2, GLA_DIM, GLA_KEY_DIM), F32)

    xl, xc = x, ctx
    for l in range(depth):
        ctx_out = l < depth - 1
        m6 = mods[l].reshape(8, N_MOD, d)
        mod_l = jnp.pad(m6[:nb], ((0, 0), (0, 2), (0, 0)))
        mod_c = jnp.broadcast_to(jnp.pad(m6[nb], ((0, 2), (0, 0)))[None], (nb, 8, d))

        win = jnp.pad(w_in[l], ((0, 0), (0, IN_PAD - IN_WIDTH))).astype(BF16)
        sguw = sgu_w[l].astype(BF16)
        sgub = jnp.repeat(jnp.swapaxes(sgu_b[l], 0, 1), SGU_HEAD_DIM, axis=1)
        gw = jnp.zeros((LANE, 2 * GLA_KEY_DIM), F32)
        gw = gw.at[0:GLA_RANK, 0:GLA_KEY_DIM].set(gla_w_gate[l, 0])
        gw = gw.at[GLA_RANK:2 * GLA_RANK, GLA_KEY_DIM:].set(gla_w_gate[l, 1]).astype(BF16)
        gb = gla_b_gate[l].reshape(1, 2 * GLA_KEY_DIM)
        g1 = norm1_g[l].reshape(1, d)

        s5_all = jnp.zeros((nb, seq + c_len, S5_DIM), F32)
        sgu_l, s5_all, qk_l, v_l, g_l, la_l = _stage_a(xl, mod_l, g1, win, sguw, sgub, ones_sgu, gw, gb,
                                                       s5_all, 0, tb=TB_LATENT, grid_layout=True)
        sgu_c, s5_all, qk_c, v_c, g_c, la_c = _stage_a(xc, mod_c, g1, win, sguw, sgub, ones_sgu, gw, gb,
                                                       s5_all, seq, tb=TB_CTX, grid_layout=False)

        b2, ar, ai, c2 = _s5_params(s5_lambda_re[l], s5_lambda_im[l], s5_b_re[l], s5_b_im[l],
                                    s5_c_re[l], s5_c_im[l], s5_log_step[l], nb)
        yf_all, yb_all = _s5_call(s5_all, seq, b2, ar, ai, c2)

        of_c, ob_c, s_ctx = _gla_call(qk_c, v_c, la_c, s_zero, trif, trib)
        of_l, ob_l, _ = _gla_call(qk_l, v_l, la_l, s_ctx, trif, trib)

        dskip = s5_d[l].reshape(1, S5_DIM)
        wglu = s5_w_glu[l].astype(BF16)
        normg = gla_norm_g[l].reshape(1, GLA_DIM)
        wout = w_out[l].astype(BF16)
        g2 = norm2_g[l].reshape(1, d)
        wqt = jnp.swapaxes(peer_w_query[l], 0, 1).astype(BF16)
        keys = peer_sub_keys[l].reshape(2 * PEER_HEADS, PEER_NKEYS, PEER_HALF).astype(BF16)
        u_bf = _pack_row_pairs(peer_expert_u[l].astype(BF16))
        vt_bf = _pack_row_pairs(jnp.swapaxes(peer_expert_v[l], 0, 1).astype(BF16))

        xl = _stage_e(xl, mod_l, sgu_l, yf_all, yb_all, s5_all, 0, of_l, ob_l, g_l,
                      dskip, wglu, normg, ones_gla, wout, tb=TB_LATENT)
        desc = _stage_f(xl, mod_l, g2, wqt, keys, tb=PEER_TBF)
        xl = _stage_g(*desc, u_bf, vt_bf, xl.reshape(nb * seq, d), mod_l, seq, PEER_TBG).reshape(nb, seq, d)

        if ctx_out:
            xc = _stage_e(xc, mod_c, sgu_c, yf_all, yb_all, s5_all, seq, of_c, ob_c, g_c,
                          dskip, wglu, normg, ones_gla, wout, tb=TB_CTX)
            desc = _stage_f(xc, mod_c, g2, wqt, keys, tb=PEER_TBF)
            xc = _stage_g(*desc, u_bf, vt_bf, xc.reshape(nb * c_len, d), mod_c, c_len,
                          min(PEER_TBG, c_len)).reshape(nb, c_len, d)

    return _final_norm(xl.reshape(nb * seq, d), final_norm_g.reshape(1, d)).reshape(nb, seq, d)
```

```python
import functools
import math

import numpy as np
import jax
import jax.numpy as jnp
from jax import lax
from jax.experimental import pallas as pl
from jax.experimental.pallas import tpu as pltpu

F32 = jnp.float32
BF16 = jnp.bfloat16

EPS = 1e-6
N_MOD = 6
GRID_W = 64

SGU_DIM = 256
SGU_HEADS = 4
SGU_HEAD_DIM = 64
SGU_CHUNK = 128

S5_DIM = 256
S5_GROUP = 16
S5_GROUPS = 16
S5_STATE = 64
S5_LANES = S5_GROUPS * S5_STATE

GLA_DIM = 512
GLA_HEADS = 8
GLA_DV = 64
GLA_DK = 32
GLA_KEY_DIM = 256
GLA_RANK = 16
GLA_GATE_TEMP = 16.0
GLA_CHUNK = 64

PEER_HEADS = 8
PEER_NKEYS = 128
PEER_HALF = 128
PEER_TOPK = 16

IN_WIDTH = 2336
IN_PAD = 2432
LANE = 128

VMEM_LIMIT = 56 * 1024 * 1024
TB_LATENT = 512
TB_CTX = 256

NEG_INF = float("-inf")
POS_INF = float("inf")


def _cparams(sem, flags=None):
    return pltpu.CompilerParams(dimension_semantics=sem, vmem_limit_bytes=VMEM_LIMIT, flags=flags)


def _gelu(x):
    c = math.sqrt(2.0 / math.pi)
    return 0.5 * x * (1.0 + jnp.tanh(c * (x + 0.044715 * (x * x * x))))


def _sigmoid(x):
    return 1.0 / (1.0 + jnp.exp(-x))


def _dot(a, b):
    return jnp.dot(a, b, preferred_element_type=F32)


def _dot_nt(a, b):
    return lax.dot_general(a, b, (((1,), (1,)), ((), ())), preferred_element_type=F32)


def _dot_tn(a, b):
    return lax.dot_general(a, b, (((0,), (0,)), ((), ())), preferred_element_type=F32)


def _split3(x):
    hi = x.astype(BF16)
    r = x - hi.astype(F32)
    mid = r.astype(BF16)
    lo = (r - mid.astype(F32)).astype(BF16)
    return hi, mid, lo


def _dot_x_exact(x, m):
    hi, mid, lo = _split3(x)
    return _dot(hi, m) + _dot(mid, m) + _dot(lo, m)


def _dot_m_exact(m, x):
    hi, mid, lo = _split3(x)
    return _dot(m, hi) + _dot(m, mid) + _dot(m, lo)


def _full_spec(shape):
    nd = len(shape)
    return pl.BlockSpec(shape, lambda *_: (0,) * nd)


MOD_TILE = 512


def _mod_kernel(c_ref, w_ref, b_ref, o_ref):
    c = c_ref[...]
    a = c * _sigmoid(c)
    o_ref[0] = jnp.dot(a, w_ref[0], preferred_element_type=F32,
                       precision=lax.Precision.HIGHEST) + b_ref[0]


def _mod_call(cc, w_mod, b_mod):
    depth, d, nd = w_mod.shape
    rows = cc.shape[0]
    return pl.pallas_call(
        _mod_kernel,
        grid=(depth, nd // MOD_TILE),
        in_specs=[
            pl.BlockSpec((rows, d), lambda l, j: (0, 0)),
            pl.BlockSpec((1, d, MOD_TILE), lambda l, j: (l, 0, j)),
            pl.BlockSpec((1, 1, MOD_TILE), lambda l, j: (l, 0, j)),
        ],
        out_specs=pl.BlockSpec((1, rows, MOD_TILE), lambda l, j: (l, 0, j)),
        out_shape=jax.ShapeDtypeStruct((depth, rows, nd), F32),
        compiler_params=_cparams(("parallel", "parallel")),
    )(cc, w_mod, b_mod.reshape(depth, 1, nd))


def _store_tokens(ref, val, grid_layout):
    if grid_layout:
        for r in range(val.shape[0] // GRID_W):
            ref[:, r, :] = val[r * GRID_W:(r + 1) * GRID_W]
    else:
        ref[0] = val


def _load_tokens(ref, grid_layout):
    if grid_layout:
        return jnp.concatenate([ref[:, r, :] for r in range(ref.shape[1])], axis=0)
    return ref[0]


def _stage_a_kernel(x_ref, mod_ref, g1_ref, win_ref, sguw_ref, sgub_ref, ones_ref, gw_ref, gb_ref, s5_in_ref,
                    sgu_ref, s5x_ref, qk_ref, v_ref, g_ref, la_ref, *, tb, grid_layout):
    del s5_in_ref
    x = x_ref[0]
    ms = jnp.mean(x * x, axis=-1, keepdims=True)
    xn = x * lax.rsqrt(ms + EPS) * g1_ref[...]
    h = xn * (1.0 + mod_ref[0, 1:2, :]) + mod_ref[0, 0:1, :]
    cols = _dot(h.astype(BF16), win_ref[...])

    u = _gelu(cols[:, 0:SGU_DIM])
    v = _gelu(cols[:, SGU_DIM:2 * SGU_DIM])
    msq = _dot_x_exact(v * v, ones_ref[...]) * (1.0 / SGU_HEAD_DIM)
    vn = (v * lax.rsqrt(msq + EPS)).astype(BF16)
    head_of_lane = lax.broadcasted_iota(jnp.int32, (SGU_CHUNK, SGU_DIM), 1) // SGU_HEAD_DIM
    for ci in range(tb // SGU_CHUNK):
        rows = slice(ci * SGU_CHUNK, (ci + 1) * SGU_CHUNK)
        vc = vn[rows]
        mixed = sgub_ref[...]
        for hh in range(SGU_HEADS):
            mixed = mixed + jnp.where(head_of_lane == hh, _dot(sguw_ref[hh], vc), 0.0)
        sgu_ref[0, rows, :] = u[rows] * mixed

    s5x_ref[0] = cols[:, 512:768]
    q = cols[:, 768:1024] * (GLA_DK ** -0.5)
    _store_tokens(qk_ref, jnp.concatenate([q, cols[:, 1024:1280]], axis=-1), grid_layout)
    _store_tokens(v_ref, cols[:, 1280:1792], grid_layout)
    g_ref[0] = cols[:, 1792:2304]

    z = cols[:, 2304:2432].astype(BF16)
    za = _dot(z, gw_ref[...]) + gb_ref[...]
    log_sig = jnp.minimum(za, 0.0) - jnp.log1p(jnp.exp(-jnp.abs(za)))
    _store_tokens(la_ref, log_sig * (1.0 / GLA_GATE_TEMP), grid_layout)


def _stage_a(xs, mod, g1, win, sguw, sgub, ones_sgu, gw, gb, s5_all, s5_row0, tb, grid_layout):
    b, l, d = xs.shape
    assert s5_row0 % tb == 0
    kern = functools.partial(_stage_a_kernel, tb=tb, grid_layout=grid_layout)
    tok = lambda w: pl.BlockSpec((1, tb, w), lambda bi, i: (bi, i, 0))
    if grid_layout:
        assert tb % GRID_W == 0 and l % GRID_W == 0
        gla = lambda w: pl.BlockSpec((GRID_W, None, tb // GRID_W, w), lambda bi, i: (0, bi, i, 0))
        gla_shape = lambda w: jax.ShapeDtypeStruct((GRID_W, b, l // GRID_W, w), F32)
    else:
        gla = tok
        gla_shape = lambda w: jax.ShapeDtypeStruct((b, l, w), F32)
    nat_shape = lambda w: jax.ShapeDtypeStruct((b, l, w), F32)
    return pl.pallas_call(
        kern,
        grid=(b, l // tb),
        in_specs=[
            tok(d),
            pl.BlockSpec((1, 8, d), lambda bi, i: (bi, 0, 0)),
            _full_spec(g1.shape), _full_spec(win.shape), _full_spec(sguw.shape), _full_spec(sgub.shape),
            _full_spec(ones_sgu.shape), _full_spec(gw.shape), _full_spec(gb.shape),
            pl.BlockSpec(memory_space=pl.ANY),
        ],
        out_specs=[tok(SGU_DIM),
                   pl.BlockSpec((1, tb, S5_DIM), lambda bi, i: (bi, s5_row0 // tb + i, 0)),
                   gla(2 * GLA_KEY_DIM), gla(GLA_DIM), tok(GLA_DIM), gla(2 * GLA_KEY_DIM)],
        out_shape=[nat_shape(SGU_DIM), jax.ShapeDtypeStruct(s5_all.shape, F32),
                   gla_shape(2 * GLA_KEY_DIM), gla_shape(GLA_DIM), nat_shape(GLA_DIM), gla_shape(2 * GLA_KEY_DIM)],
        input_output_aliases={9: 1},
        compiler_params=_cparams(("parallel", "parallel")),
    )(xs, mod, g1, win, sguw, sgub, ones_sgu, gw, gb, s5_all)


S5_TC = 128


def _s5_kernel(uf_ref, ub_ref, perm_ref, permt_ref, b2_ref, ar_ref, ai_ref, c2_ref, yf_ref, yb_ref,
               h_ref, buf_ref, *, nseq):
    tc = S5_TC
    rows = tc * 2 * nseq

    @pl.when(pl.program_id(0) == 0)
    def _():
        h_ref[...] = jnp.zeros_like(h_ref)

    x = jnp.concatenate([uf_ref[b] for b in range(nseq)] + [ub_ref[b] for b in range(nseq)], axis=0)
    u_tm = _dot(perm_ref[...], x.astype(BF16))
    fwd_row = lax.broadcasted_iota(jnp.int32, (rows, S5_DIM), 0) % (2 * nseq) < nseq
    u = jnp.concatenate([jnp.where(fwd_row, u_tm, 0.0), jnp.where(fwd_row, 0.0, u_tm)], axis=-1).astype(BF16)
    buf_ref[...] = _dot(u, b2_ref[...]).reshape(tc, 2 * nseq, 2 * S5_LANES)
    ar = ar_ref[...]
    ai = ai_ref[...]

    def step(t, carry):
        hr, hi = carry
        bur = buf_ref[t, :, 0:S5_LANES]
        bui = buf_ref[t, :, S5_LANES:]
        nhr = ar * hr - ai * hi + bur
        nhi = ar * hi + ai * hr + bui
        buf_ref[t, :, 0:S5_LANES] = nhr
        buf_ref[t, :, S5_LANES:] = nhi
        return nhr, nhi

    hr, hi = lax.fori_loop(0, tc, step, (h_ref[:, 0:S5_LANES], h_ref[:, S5_LANES:]))
    h_ref[:, 0:S5_LANES] = hr
    h_ref[:, S5_LANES:] = hi

    hs = buf_ref[...].reshape(rows, 2 * S5_LANES).astype(BF16)
    y2 = _dot(hs, c2_ref[...])
    y = jnp.where(fwd_row, y2[:, 0:S5_DIM], y2[:, S5_DIM:])
    y_hi = y.astype(BF16)
    y_lo = (y - y_hi.astype(F32)).astype(BF16)
    y_nat = _dot(permt_ref[...], y_hi) + _dot(permt_ref[...], y_lo)
    for b in range(nseq):
        yf_ref[b] = y_nat[b * tc:(b + 1) * tc]
        yb_ref[b] = y_nat[(nseq + b) * tc:(nseq + b + 1) * tc]


def _s5_permutation(nseq):
    tc = S5_TC
    p = np.zeros((tc * 2 * nseq, tc * 2 * nseq), np.float32)
    for q in range(2 * nseq):
        for t in range(tc):
            p[t * 2 * nseq + q, q * tc + (t if q < nseq else tc - 1 - t)] = 1.0
    return p


def _s5_call(s5_all, n_lat, b2, ar, ai, c2):
    nseq, t, _ = s5_all.shape
    n = t // S5_TC
    n_l = n_lat // S5_TC
    n_c = n - n_l
    perm = _s5_permutation(nseq)
    kern = functools.partial(_s5_kernel, nseq=nseq)
    fwd_map = lambda s: (0, jnp.where(s < n_c, n_l + s, s - n_c), 0)
    bwd_map = lambda s: (0, n - 1 - s, 0)
    blk = (nseq, S5_TC, S5_DIM)
    rows = 2 * nseq
    return pl.pallas_call(
        kern,
        grid=(n,),
        in_specs=[
            pl.BlockSpec(blk, fwd_map), pl.BlockSpec(blk, bwd_map),
            _full_spec(perm.shape), _full_spec(perm.shape),
            _full_spec(b2.shape), _full_spec(ar.shape), _full_spec(ai.shape), _full_spec(c2.shape),
        ],
        out_specs=[pl.BlockSpec(blk, fwd_map), pl.BlockSpec(blk, bwd_map)],
        out_shape=[jax.ShapeDtypeStruct(s5_all.shape, F32), jax.ShapeDtypeStruct(s5_all.shape, F32)],
        scratch_shapes=[
            pltpu.VMEM((rows, 2 * S5_LANES), F32),
            pltpu.VMEM((S5_TC, rows, 2 * S5_LANES), F32),
        ],
        compiler_params=_cparams(("arbitrary",)),
    )(s5_all, s5_all, jnp.asarray(perm, dtype=BF16), jnp.asarray(perm.T, dtype=BF16), b2, ar, ai, c2)


def _gla_stream(q, k, v, la, s_t, tri, last_row, ref_row, tri_mask, hm_k, hm_v, hm_s):
    bcum = _dot_m_exact(tri, la)
    blast = bcum[last_row:last_row + 1]
    bref = bcum[ref_row:ref_row + 1]
    qe = q * jnp.exp(bcum)
    qd = q * jnp.exp(bcum - bref)
    kd = k * jnp.exp(bref - bcum)
    kdec = k * jnp.exp(blast - bcum)
    kst = jnp.where(hm_k, jnp.concatenate([kd] * GLA_HEADS, axis=0), 0.0).astype(BF16)
    sc = jnp.where(tri_mask, _dot_nt(qd.astype(BF16), kst), 0.0)
    vbd = jnp.where(hm_v, jnp.concatenate([v] * GLA_HEADS, axis=0), 0.0).astype(BF16)
    o = _dot(sc.astype(BF16), vbd) + _dot_nt(qe.astype(BF16), s_t.astype(BF16))
    kv_t = jnp.where(hm_s, _dot_tn(v.astype(BF16), kdec.astype(BF16)), 0.0)
    s_new = s_t * jnp.exp(blast) + kv_t
    return o, s_new


def _gla_kernel(qkf_ref, vf_ref, laf_ref, qkb_ref, vb_ref, lab_ref, s0_ref, trif_ref, trib_ref,
                of_ref, ob_ref, sout_ref, s_ref, *, nb):
    c = pl.program_id(0)

    @pl.when(c == 0)
    def _():
        s_ref[...] = s0_ref[...]

    ch = GLA_CHUNK
    r_k = lax.broadcasted_iota(jnp.int32, (GLA_HEADS * ch, GLA_KEY_DIM), 0) // ch
    c_k = lax.broadcasted_iota(jnp.int32, (GLA_HEADS * ch, GLA_KEY_DIM), 1) // GLA_DK
    hm_k = r_k == c_k
    r_v = lax.broadcasted_iota(jnp.int32, (GLA_HEADS * ch, GLA_DIM), 0) // ch
    c_v = lax.broadcasted_iota(jnp.int32, (GLA_HEADS * ch, GLA_DIM), 1) // GLA_DV
    hm_v = r_v == c_v
    r_s = lax.broadcasted_iota(jnp.int32, (GLA_DIM, GLA_KEY_DIM), 0) // GLA_DV
    c_s = lax.broadcasted_iota(jnp.int32, (GLA_DIM, GLA_KEY_DIM), 1) // GLA_DK
    hm_s = r_s == c_s
    t_i = lax.broadcasted_iota(jnp.int32, (ch, GLA_HEADS * ch), 0)
    s_i = lax.broadcasted_iota(jnp.int32, (ch, GLA_HEADS * ch), 1) % ch
    mask_f = t_i >= s_i
    mask_b = t_i <= s_i
    trif = trif_ref[...]
    trib = trib_ref[...]

    def body(b, carry):
        qk = qkf_ref[b]
        o, s_new = _gla_stream(qk[:, 0:GLA_KEY_DIM], qk[:, GLA_KEY_DIM:], vf_ref[b], laf_ref[b], s_ref[b, 0],
                               trif, ch - 1, ch // 2, mask_f, hm_k, hm_v, hm_s)
        of_ref[b] = o
        s_ref[b, 0] = s_new
        qk = qkb_ref[b]
        o, s_new = _gla_stream(qk[:, 0:GLA_KEY_DIM], qk[:, GLA_KEY_DIM:], vb_ref[b], lab_ref[b], s_ref[b, 1],
                               trib, 0, ch - 1 - ch // 2, mask_b, hm_k, hm_v, hm_s)
        ob_ref[b] = o
        s_ref[b, 1] = s_new
        return carry

    lax.fori_loop(0, nb, body, 0)

    @pl.when(c == pl.num_programs(0) - 1)
    def _():
        sout_ref[...] = s_ref[...]


def _gla_call(qk, v, la, s0, trif, trib):
    ch = GLA_CHUNK
    if qk.ndim == 4:
        n, b = qk.shape[0], qk.shape[1]
        assert qk.shape[2] == ch
        spec = lambda w, off, rev: pl.BlockSpec(
            (None, b, ch, w), (lambda c: (n - 1 - c, 0, 0, off)) if rev else (lambda c: (c, 0, 0, off)))
    else:
        b = qk.shape[0]
        n = qk.shape[1] // ch
        spec = lambda w, off, rev: pl.BlockSpec(
            (b, ch, w), (lambda c: (0, n - 1 - c, off)) if rev else (lambda c: (0, c, off)))
    kern = functools.partial(_gla_kernel, nb=b)
    o_shape = jax.ShapeDtypeStruct(v.shape, F32)
    return pl.pallas_call(
        kern,
        grid=(n,),
        in_specs=[
            spec(2 * GLA_KEY_DIM, 0, False), spec(GLA_DIM, 0, False), spec(GLA_KEY_DIM, 0, False),
            spec(2 * GLA_KEY_DIM, 0, True), spec(GLA_DIM, 0, True), spec(GLA_KEY_DIM, 1, True),
            _full_spec(s0.shape), _full_spec(trif.shape), _full_spec(trib.shape),
        ],
        out_specs=[spec(GLA_DIM, 0, False), spec(GLA_DIM, 0, True), _full_spec(s0.shape)],
        out_shape=[o_shape, o_shape, jax.ShapeDtypeStruct(s0.shape, F32)],
        scratch_shapes=[pltpu.VMEM(s0.shape, F32)],
        compiler_params=_cparams(("arbitrary",)),
    )(qk, v, la, qk, v, la, s0, trif, trib)


def _stage_e_kernel(x_ref, mod_ref, sgu_ref, yf_ref, yb_ref, s5x_ref, of_ref, ob_ref, g_ref,
                    dskip_ref, wglu_ref, normg_ref, ones_ref, wout_ref, o_ref, *, grid_layout):
    ys = yf_ref[0] + yb_ref[0] + dskip_ref[...] * s5x_ref[0]
    z = _dot(_gelu(ys).astype(BF16), wglu_ref[...])
    s5o = z[:, 0:S5_DIM] * _sigmoid(z[:, S5_DIM:])
    o = _load_tokens(of_ref, grid_layout) + _load_tokens(ob_ref, grid_layout)
    ms = _dot_x_exact(o * o, ones_ref[...]) * (1.0 / GLA_DV)
    g = g_ref[0]
    gl = o * lax.rsqrt(ms + EPS) * normg_ref[...] * (g * _sigmoid(g))
    y = (_dot(sgu_ref[0].astype(BF16), wout_ref[0:SGU_DIM, :])
         + _dot(s5o.astype(BF16), wout_ref[SGU_DIM:SGU_DIM + S5_DIM, :])
         + _dot(gl.astype(BF16), wout_ref[SGU_DIM + S5_DIM:, :]))
    o_ref[0] = x_ref[0] + mod_ref[0, 2:3, :] * y


def _stage_e(xs, mod, sgu, yf_all, yb_all, s5_all, s5_row0, of, ob, g, dskip, wglu, normg, ones_gla, wout, tb):
    b, l, d = xs.shape
    assert s5_row0 % tb == 0
    grid_layout = of.ndim == 4
    tok = lambda w: pl.BlockSpec((1, tb, w), lambda bi, i: (bi, i, 0))
    s5 = pl.BlockSpec((1, tb, S5_DIM), lambda bi, i: (bi, s5_row0 // tb + i, 0))
    if grid_layout:
        assert tb % GRID_W == 0
        gla = pl.BlockSpec((GRID_W, None, tb // GRID_W, GLA_DIM), lambda bi, i: (0, bi, i, 0))
    else:
        gla = tok(GLA_DIM)
    return pl.pallas_call(
        functools.partial(_stage_e_kernel, grid_layout=grid_layout),
        grid=(b, l // tb),
        in_specs=[
            tok(d), pl.BlockSpec((1, 8, d), lambda bi, i: (bi, 0, 0)),
            tok(SGU_DIM), s5, s5, s5, gla, gla, tok(GLA_DIM),
            _full_spec(dskip.shape), _full_spec(wglu.shape), _full_spec(normg.shape),
            _full_spec(ones_gla.shape), _full_spec(wout.shape),
        ],
        out_specs=tok(d),
        out_shape=jax.ShapeDtypeStruct((b, l, d), F32),
        compiler_params=_cparams(("parallel", "parallel")),
    )(xs, mod, sgu, yf_all, yb_all, s5_all, of, ob, g, dskip, wglu, normg, ones_gla, wout)


PEER_TBF = 256
SUBLANES = 8


def _sort_network_16():
    def merge(lo, hi, r):
        step = r * 2
        if step < hi - lo:
            yield from merge(lo, hi, step)
            yield from merge(lo + r, hi, step)
            for i in range(lo + r, hi - r, step):
                yield (i, i + r)
        else:
            yield (lo, lo + r)

    def sort(lo, hi):
        if hi - lo >= 1:
            mid = lo + (hi - lo) // 2
            yield from sort(lo, mid)
            yield from sort(mid + 1, hi)
            yield from merge(lo, hi, 1)

    return tuple(sort(0, PEER_TOPK - 1))


SORT16 = _sort_network_16()
BITONIC16 = tuple((k, k + s) for s in (8, 4, 2, 1) for k in range(PEER_TOPK) if not k & s)


def _compare_exchange(xs, pairs):
    xs = list(xs)
    for i, j in pairs:
        hi = jnp.maximum(xs[i], xs[j])
        lo = jnp.minimum(xs[i], xs[j])
        xs[i], xs[j] = hi, lo
    return xs


def _merge_sublanes(xs):
    for shift in (4, 6, 7):
        rolled = [pltpu.roll(x, shift, 0) for x in xs]
        xs = [jnp.maximum(xs[k], rolled[PEER_TOPK - 1 - k]) for k in range(PEER_TOPK)]
        xs = _compare_exchange(xs, BITONIC16)
    return xs


def _dup_bf16_words(x):
    bits = pltpu.bitcast(x.astype(BF16).astype(F32), jnp.int32)
    return bits | lax.shift_right_logical(bits, 16)


def _stage_f_kernel(x_ref, mod_ref, g2_ref, wqt_ref, keys_ref,
                    ht_ref, e1w_ref, n1w_ref, e2_ref, r2_ref, qt_ref, v1_ref, v2_ref, *, tb):
    x = x_ref[0]
    ms = jnp.mean(x * x, axis=-1, keepdims=True)
    xn = x * lax.rsqrt(ms + EPS) * g2_ref[...]
    h = xn * (1.0 + mod_ref[0, 4:5, :]) + mod_ref[0, 3:4, :]
    ht = h.T.astype(BF16)
    ht_ref[...] = pltpu.bitcast(ht, jnp.int32)
    qt_ref[...] = _dot(wqt_ref[...], ht)
    k_top = PEER_TOPK

    def tiles(s):
        return [s[SUBLANES * k:SUBLANES * (k + 1)] for k in range(PEER_NKEYS // SUBLANES)]

    def head_body(hh, carry):
        for tc in range(tb // LANE):
            tcol = slice(tc * LANE, (tc + 1) * LANE)
            r1 = pl.multiple_of(hh * (2 * PEER_HALF), 2 * PEER_HALF)
            q1 = qt_ref[pl.ds(r1, PEER_HALF), tcol].astype(BF16)
            q2 = qt_ref[pl.ds(r1 + PEER_HALF, PEER_HALF), tcol].astype(BF16)
            s1 = _dot(keys_ref[hh], q1)
            s2 = _dot(keys_ref[PEER_HEADS + hh], q2)
            for s, v_ref in ((s1, v1_ref), (s2, v2_ref)):
                top = _merge_sublanes(_compare_exchange(tiles(s), SORT16))
                for k in range(k_top):
                    v_ref[k:k + 1, tcol] = top[k][0:1]
            v1row = lambda a: v1_ref[a:a + 1, tcol]
            v2row = lambda b: v2_ref[b:b + 1, tcol]
            v1lo = v1_ref[0:SUBLANES, tcol]
            v2lo = v2_ref[0:SUBLANES, tcol]
            cand = [v1lo + v2row(b) for b in range(k_top)]
            tail = [v1row(a) + v2lo for a in range(SUBLANES, k_top)]
            for k in range(SUBLANES, k_top):
                cand[k] = jnp.maximum(cand[k], tail[k_top - 1 - k])
            best = _merge_sublanes(_compare_exchange(cand, BITONIC16))
            theta = best[k_top - 1][0:1]
            cmax = best[0][0:1]
            zsum = jnp.zeros((1, LANE), F32)
            for k in range(k_top):
                zsum = zsum + jnp.exp(best[k][0:1] - cmax)
            rz = 1.0 / zsum
            n_top = jnp.zeros((1, LANE), F32)
            for b in range(k_top):
                n_top = jnp.where(v1row(0) + v2row(b) >= theta, float(b + 1), n_top)
            n1 = jnp.zeros(s1.shape, F32)
            for b in range(SUBLANES):
                n1 = jnp.where(s1 + v2row(b) >= theta, float(b + 1), n1)
            n1 = jnp.where(s1 >= v1row(0), n_top, n1)
            r2 = jnp.zeros(s2.shape, F32)
            for b in range(k_top):
                r2 = jnp.where(v2row(b) > s2, float(b + 1), r2)
            e1 = jnp.where(s1 >= v1row(k_top - 1), jnp.exp(s1 - v1row(0)), 0.0) * rz
            e2 = jnp.where(s2 >= v2row(k_top - 1), jnp.exp(s2 - v2row(0)), 0.0)
            e1w_ref[tc, hh] = _dup_bf16_words(e1)
            n1w_ref[tc, hh] = _dup_bf16_words(n1)
            e2_ref[tc, hh] = pltpu.bitcast(e2.astype(BF16), jnp.int32)
            r2_ref[tc, hh] = pltpu.bitcast(r2.astype(BF16), jnp.int32)
        return carry

    lax.fori_loop(0, PEER_HEADS, head_body, 0)


def _stage_f(xs, mod, g2, wqt, keys, tb):
    b, l, d = xs.shape
    nblk = l // tb
    ntok = b * l
    nch = ntok // LANE
    kern = functools.partial(_stage_f_kernel, tb=tb)
    row_spec = pl.BlockSpec((tb // LANE, PEER_HEADS, PEER_NKEYS, LANE), lambda bi, i: (bi * nblk + i, 0, 0, 0))
    pair_spec = pl.BlockSpec((tb // LANE, PEER_HEADS, PEER_NKEYS // 2, LANE), lambda bi, i: (bi * nblk + i, 0, 0, 0))
    desc_shape = lambda rows: jax.ShapeDtypeStruct((nch, PEER_HEADS, rows, LANE), jnp.int32)
    return pl.pallas_call(
        kern,
        grid=(b, nblk),
        in_specs=[
            pl.BlockSpec((1, tb, d), lambda bi, i: (bi, i, 0)),
            pl.BlockSpec((1, 8, d), lambda bi, i: (bi, 0, 0)),
            _full_spec(g2.shape), _full_spec(wqt.shape), _full_spec(keys.shape),
        ],
        out_specs=[pl.BlockSpec((d // 2, tb), lambda bi, i: (0, bi * nblk + i)),
                   row_spec, row_spec, pair_spec, pair_spec],
        out_shape=[jax.ShapeDtypeStruct((d // 2, ntok), jnp.int32), desc_shape(PEER_NKEYS), desc_shape(PEER_NKEYS),
                   desc_shape(PEER_NKEYS // 2), desc_shape(PEER_NKEYS // 2)],
        scratch_shapes=[
            pltpu.VMEM((PEER_HEADS * 2 * PEER_HALF, tb), F32),
            pltpu.VMEM((PEER_TOPK, tb), F32),
            pltpu.VMEM((PEER_TOPK, tb), F32),
        ],
        compiler_params=_cparams(("parallel", "parallel")),
    )(xs, mod, g2, wqt, keys)


PEER_TBG = 1024
PEER_TE = 1024
PEER_I1_PER_TILE = PEER_TE // PEER_NKEYS
PEER_N_TILES = PEER_NKEYS * PEER_NKEYS // PEER_TE
PEER_MXU_COLS = 256
PEER_G_FLAGS = None


def _stage_g_kernel(htw_ref, e1w_ref, n1w_ref, e2_ref, r2_ref, uw_ref, vtw_ref, x_ref, mod_ref,
                    o_ref, acc_ref, pa_ref, pb_ref, *, tb):
    i = pl.program_id(1)
    n_tiles = pl.num_programs(1) - 1

    @pl.when(i == 0)
    def _():
        acc_ref[...] = jnp.zeros_like(acc_ref)
        pb_ref[...] = jnp.zeros_like(pb_ref)

    def step(cur_ref, prev_ref):
        per_grp = PEER_MXU_COLS // LANE
        for grp in range(tb // PEER_MXU_COLS):
            cols = slice(grp * PEER_MXU_COLS, (grp + 1) * PEER_MXU_COLS)
            for tcl in range(per_grp):
                tc = grp * per_grp + tcl
                tcol = slice(tc * LANE, (tc + 1) * LANE)
                for i1l in range(PEER_I1_PER_TILE):
                    rows = slice(i1l * PEER_NKEYS, (i1l + 1) * PEER_NKEYS)
                    gate = jnp.zeros((PEER_NKEYS, LANE), BF16)
                    for hh in range(PEER_HEADS):
                        e1row = e1w_ref[tc, hh, i1l:i1l + 1, :]
                        n1row = n1w_ref[tc, hh, i1l:i1l + 1, :]
                        e1 = pltpu.bitcast(jnp.broadcast_to(e1row, (PEER_NKEYS // 2, LANE)), BF16)
                        n1 = pltpu.bitcast(jnp.broadcast_to(n1row, (PEER_NKEYS // 2, LANE)), BF16)
                        r2 = pltpu.bitcast(r2_ref[tc, hh], BF16)
                        e2 = pltpu.bitcast(e2_ref[tc, hh], BF16)
                        gate = gate + e2 * jnp.where(r2 < n1, e1, 0.0)
                    cur_ref[rows, tcol] = gate
            vt = pltpu.bitcast(vtw_ref[...], BF16)
            u = pltpu.bitcast(uw_ref[...], BF16)
            ht = pltpu.bitcast(htw_ref[:, cols], BF16)
            acc_ref[:, cols] += _dot(vt, prev_ref[:, cols])
            at = _dot(u, ht).astype(BF16)
            for tcl in range(per_grp):
                tcol = slice((grp * per_grp + tcl) * LANE, (grp * per_grp + tcl + 1) * LANE)
                for i1l in range(PEER_I1_PER_TILE):
                    rows = slice(i1l * PEER_NKEYS, (i1l + 1) * PEER_NKEYS)
                    a = at[rows, tcl * LANE:(tcl + 1) * LANE]
                    cur_ref[rows, tcol] = cur_ref[rows, tcol] * _gelu(a)

    @pl.when(jnp.logical_and(i < n_tiles, i % 2 == 0))
    def _():
        step(pa_ref, pb_ref)

    @pl.when(jnp.logical_and(i < n_tiles, i % 2 == 1))
    def _():
        step(pb_ref, pa_ref)

    @pl.when(i == n_tiles)
    def _():
        last_ref = pb_ref if (PEER_N_TILES - 1) % 2 else pa_ref
        acc = acc_ref[...] + _dot(pltpu.bitcast(vtw_ref[...], BF16), last_ref[...])
        o_ref[...] = x_ref[...] + mod_ref[0, 5:6, :] * acc.T


def _stage_g(htw, e1w, n1w, e2, r2, uw, vtw, xflat, mod, tokens_per_batch, tb):
    ntok = htw.shape[1]
    d = 2 * htw.shape[0]
    assert 2 * uw.shape[0] == PEER_N_TILES * PEER_TE
    blocks_per_batch = tokens_per_batch // tb
    kern = functools.partial(_stage_g_kernel, tb=tb)
    last = PEER_N_TILES - 1
    desc_spec = pl.BlockSpec((tb // LANE, PEER_HEADS, PEER_NKEYS // 2, LANE), lambda j, i: (j, 0, 0, 0))
    row_spec = pl.BlockSpec((tb // LANE, PEER_HEADS, PEER_I1_PER_TILE, LANE),
                            lambda j, i: (j, 0, jnp.minimum(i, last), 0))
    return pl.pallas_call(
        kern,
        grid=(ntok // tb, PEER_N_TILES + 1),
        in_specs=[
            pl.BlockSpec((d // 2, tb), lambda j, i: (0, j)),
            row_spec, row_spec, desc_spec, desc_spec,
            pl.BlockSpec((PEER_TE // 2, d), lambda j, i: (jnp.minimum(i, last), 0)),
            pl.BlockSpec((d // 2, PEER_TE), lambda j, i: (0, jnp.maximum(i - 1, 0))),
            pl.BlockSpec((tb, d), lambda j, i: (j, 0)),
            pl.BlockSpec((1, 8, d), lambda j, i: (j // blocks_per_batch, 0, 0)),
        ],
        out_specs=pl.BlockSpec((tb, d), lambda j, i: (j, 0)),
        out_shape=jax.ShapeDtypeStruct((ntok, d), F32),
        scratch_shapes=[
            pltpu.VMEM((d, tb), F32),
            pltpu.VMEM((PEER_TE, tb), BF16),
            pltpu.VMEM((PEER_TE, tb), BF16),
        ],
        compiler_params=_cparams(("parallel", "arbitrary"), PEER_G_FLAGS),
    )(htw, e1w, n1w, e2, r2, uw, vtw, xflat, mod)


def _final_norm_kernel(x_ref, g_ref, o_ref):
    x = x_ref[...]
    ms = jnp.mean(x * x, axis=-1, keepdims=True)
    o_ref[...] = x * lax.rsqrt(ms + EPS) * g_ref[...]


def _final_norm(xflat, g, tb=512):
    n, d = xflat.shape
    return pl.pallas_call(
        _final_norm_kernel,
        grid=(n // tb,),
        in_specs=[pl.BlockSpec((tb, d), lambda i: (i, 0)), _full_spec(g.shape)],
        out_specs=pl.BlockSpec((tb, d), lambda i: (i, 0)),
        out_shape=jax.ShapeDtypeStruct((n, d), F32),
        compiler_params=_cparams(("parallel",)),
    )(xflat, g)


def _pack_kernel(x_ref, o_ref, *, transpose):
    x = x_ref[...]
    if transpose:
        x = x.T
    o_ref[...] = pltpu.bitcast(x.astype(BF16), jnp.int32)


def _pack_row_pairs(x, layer, transpose=False, tile=1024):
    _, r, c = x.shape
    if transpose:
        out_shape, out_spec = (c // 2, r), pl.BlockSpec((c // 2, tile), lambda i: (0, i))
    else:
        out_shape, out_spec = (r // 2, c), pl.BlockSpec((tile // 2, c), lambda i: (i, 0))
    return pl.pallas_call(
        functools.partial(_pack_kernel, transpose=transpose),
        grid=(r // tile,),
        in_specs=[pl.BlockSpec((None, tile, c), lambda i: (layer, i, 0))],
        out_specs=out_spec,
        out_shape=jax.ShapeDtypeStruct(out_shape, jnp.int32),
        compiler_params=_cparams(("parallel",)),
    )(x)


def _block_ones(n, blk):
    idx = np.arange(n) // blk
    return jnp.asarray((idx[:, None] == idx[None, :]).astype(np.float32), dtype=BF16)


def _s5_discretise(lam_re, lam_im, b_re, b_im, log_step):
    lam_re = jnp.minimum(lam_re.astype(F32), -1e-4)
    lam_im = lam_im.astype(F32)
    dt = jnp.exp(log_step.astype(F32))[:, None]
    mag = jnp.exp(lam_re * dt)
    a_re = mag * jnp.cos(lam_im * dt)
    a_im = mag * jnp.sin(lam_im * dt)
    den = lam_re * lam_re + lam_im * lam_im
    f_re = ((a_re - 1.0) * lam_re + a_im * lam_im) / den
    f_im = (a_im * lam_re - (a_re - 1.0) * lam_im) / den
    b_re = b_re.astype(F32)
    b_im = b_im.astype(F32)
    bb_re = f_re[..., None] * b_re - f_im[..., None] * b_im
    bb_im = f_re[..., None] * b_im + f_im[..., None] * b_re
    return a_re, a_im, bb_re, bb_im


def _group_block_diag(t):
    g, r, c = t.shape
    eye = jnp.eye(g, dtype=t.dtype)
    return (t[:, :, None, :] * eye[:, None, :, None]).reshape(g * r, g * c)


def _s5_params(lam_re, lam_im, b_re, b_im, c_re, c_im, log_step, nseq):
    b_rows, c_cols, ars, ais = [], [], [], []
    for d in range(2):
        a_re, a_im, bb_re, bb_im = _s5_discretise(lam_re[d], lam_im[d], b_re[d], b_im[d], log_step[d])
        bm = jnp.concatenate([_group_block_diag(jnp.swapaxes(bb_re, 1, 2)),
                              _group_block_diag(jnp.swapaxes(bb_im, 1, 2))], axis=1)
        b_rows.append(bm)
        cm = jnp.concatenate([_group_block_diag(jnp.swapaxes(c_re[d].astype(F32), 1, 2)),
                              -_group_block_diag(jnp.swapaxes(c_im[d].astype(F32), 1, 2))], axis=0)
        c_cols.append(cm)
        ars.append(jnp.broadcast_to(a_re.reshape(1, S5_LANES), (nseq, S5_LANES)))
        ais.append(jnp.broadcast_to(a_im.reshape(1, S5_LANES), (nseq, S5_LANES)))
    b2 = jnp.concatenate(b_rows, axis=0).astype(BF16)
    c2 = jnp.concatenate(c_cols, axis=1).astype(BF16)
    return b2, jnp.concatenate(ars, axis=0), jnp.concatenate(ais, axis=0), c2


def kernel(x, c, ctx, c_ctx, w_mod, b_mod, norm1_g, norm2_g, w_in, w_out, sgu_w, sgu_b, s5_lambda_re, s5_lambda_im, s5_b_re, s5_b_im, s5_c_re, s5_c_im, s5_log_step, s5_d, s5_w_glu, gla_w_gate, gla_b_gate, gla_norm_g, peer_w_query, peer_sub_keys, peer_expert_u, peer_expert_v, final_norm_g):
    nb, seq, d = x.shape
    c_len = ctx.shape[1]
    depth = w_mod.shape[0]

    cc = jnp.concatenate([c, c_ctx[None, :], jnp.zeros((8 - nb - 1, d), F32)], axis=0)
    mods = _mod_call(cc, w_mod, b_mod)

    ones_sgu = _block_ones(SGU_DIM, SGU_HEAD_DIM)
    ones_gla = _block_ones(GLA_DIM, GLA_DV)
    tri_np = np.tril(np.ones((GLA_CHUNK, GLA_CHUNK), np.float32))
    trif = jnp.asarray(tri_np, dtype=BF16)
    trib = jnp.asarray(tri_np.T, dtype=BF16)
    s_zero = jnp.zeros((nb, 2, GLA_DIM, GLA_KEY_DIM), F32)

    xl, xc = x, ctx
    for l in range(depth):
        ctx_out = l < depth - 1
        m6 = mods[l].reshape(8, N_MOD, d)
        mod_l = jnp.pad(m6[:nb], ((0, 0), (0, 2), (0, 0)))
        mod_c = jnp.broadcast_to(jnp.pad(m6[nb], ((0, 2), (0, 0)))[None], (nb, 8, d))

        win = jnp.pad(w_in[l], ((0, 0), (0, IN_PAD - IN_WIDTH))).astype(BF16)
        sguw = sgu_w[l].astype(BF16)
        sgub = jnp.repeat(jnp.swapaxes(sgu_b[l], 0, 1), SGU_HEAD_DIM, axis=1)
        gw = jnp.zeros((LANE, 2 * GLA_KEY_DIM), F32)
        gw = gw.at[0:GLA_RANK, 0:GLA_KEY_DIM].set(gla_w_gate[l, 0])
        gw = gw.at[GLA_RANK:2 * GLA_RANK, GLA_KEY_DIM:].set(gla_w_gate[l, 1]).astype(BF16)
        gb = gla_b_gate[l].reshape(1, 2 * GLA_KEY_DIM)
        g1 = norm1_g[l].reshape(1, d)

        s5_all = jnp.zeros((nb, seq + c_len, S5_DIM), F32)
        sgu_l, s5_all, qk_l, v_l, g_l, la_l = _stage_a(xl, mod_l, g1, win, sguw, sgub, ones_sgu, gw, gb,
                                                       s5_all, 0, tb=TB_LATENT, grid_layout=True)
        sgu_c, s5_all, qk_c, v_c, g_c, la_c = _stage_a(xc, mod_c, g1, win, sguw, sgub, ones_sgu, gw, gb,
                                                       s5_all, seq, tb=TB_CTX, grid_layout=False)

        b2, ar, ai, c2 = _s5_params(s5_lambda_re[l], s5_lambda_im[l], s5_b_re[l], s5_b_im[l],
                                    s5_c_re[l], s5_c_im[l], s5_log_step[l], nb)
        yf_all, yb_all = _s5_call(s5_all, seq, b2, ar, ai, c2)

        of_c, ob_c, s_ctx = _gla_call(qk_c, v_c, la_c, s_zero, trif, trib)
        of_l, ob_l, _ = _gla_call(qk_l, v_l, la_l, s_ctx, trif, trib)

        dskip = s5_d[l].reshape(1, S5_DIM)
        wglu = s5_w_glu[l].astype(BF16)
        normg = gla_norm_g[l].reshape(1, GLA_DIM)
        wout = w_out[l].astype(BF16)
        g2 = norm2_g[l].reshape(1, d)
        wqt = jnp.swapaxes(peer_w_query[l], 0, 1).astype(BF16)
        keys = peer_sub_keys[l].reshape(2 * PEER_HEADS, PEER_NKEYS, PEER_HALF).astype(BF16)
        u_bf = _pack_row_pairs(peer_expert_u, l)
        vt_bf = _pack_row_pairs(peer_expert_v, l, transpose=True)

        xl = _stage_e(xl, mod_l, sgu_l, yf_all, yb_all, s5_all, 0, of_l, ob_l, g_l,
                      dskip, wglu, normg, ones_gla, wout, tb=TB_LATENT)
        desc = _stage_f(xl, mod_l, g2, wqt, keys, tb=PEER_TBF)
        xl = _stage_g(*desc, u_bf, vt_bf, xl.reshape(nb * seq, d), mod_l, seq, PEER_TBG).reshape(nb, seq, d)

        if ctx_out:
            xc = _stage_e(xc, mod_c, sgu_c, yf_all, yb_all, s5_all, seq, of_c, ob_c, g_c,
                          dskip, wglu, normg, ones_gla, wout, tb=TB_CTX)
            desc = _stage_f(xc, mod_c, g2, wqt, keys, tb=PEER_TBF)
            xc = _stage_g(*desc, u_bf, vt_bf, xc.reshape(nb * c_len, d), mod_c, c_len,
                          min(PEER_TBG, c_len)).reshape(nb, c_len, d)

    return _final_norm(xl.reshape(nb * seq, d), final_norm_g.reshape(1, d)).reshape(nb, seq, d)
```

```python
import functools
import math

import numpy as np
import jax
import jax.numpy as jnp
from jax import lax
from jax.experimental import pallas as pl
from jax.experimental.pallas import tpu as pltpu

F32 = jnp.float32
BF16 = jnp.bfloat16

EPS = 1e-6
N_MOD = 6
GRID_W = 64

SGU_DIM = 256
SGU_HEADS = 4
SGU_HEAD_DIM = 64
SGU_CHUNK = 128

S5_DIM = 256
S5_GROUP = 16
S5_GROUPS = 16
S5_STATE = 64
S5_LANES = S5_GROUPS * S5_STATE

GLA_DIM = 512
GLA_HEADS = 8
GLA_DV = 64
GLA_DK = 32
GLA_KEY_DIM = 256
GLA_RANK = 16
GLA_GATE_TEMP = 16.0
GLA_CHUNK = 64

PEER_HEADS = 8
PEER_NKEYS = 128
PEER_HALF = 128
PEER_TOPK = 16

IN_WIDTH = 2336
IN_PAD = 2432
LANE = 128

VMEM_LIMIT = 56 * 1024 * 1024
TB_LATENT = 512
TB_CTX = 256

NEG_INF = float("-inf")
POS_INF = float("inf")


def _cparams(sem, flags=None):
    return pltpu.CompilerParams(dimension_semantics=sem, vmem_limit_bytes=VMEM_LIMIT, flags=flags)


def _gelu(x):
    c = math.sqrt(2.0 / math.pi)
    return 0.5 * x * (1.0 + jnp.tanh(c * (x + 0.044715 * (x * x * x))))


def _sigmoid(x):
    return 1.0 / (1.0 + jnp.exp(-x))


def _dot(a, b):
    return jnp.dot(a, b, preferred_element_type=F32)


def _dot_nt(a, b):
    return lax.dot_general(a, b, (((1,), (1,)), ((), ())), preferred_element_type=F32)


def _dot_tn(a, b):
    return lax.dot_general(a, b, (((0,), (0,)), ((), ())), preferred_element_type=F32)


def _split3(x):
    hi = x.astype(BF16)
    r = x - hi.astype(F32)
    mid = r.astype(BF16)
    lo = (r - mid.astype(F32)).astype(BF16)
    return hi, mid, lo


def _dot_x_exact(x, m):
    hi, mid, lo = _split3(x)
    return _dot(hi, m) + _dot(mid, m) + _dot(lo, m)


def _dot_m_exact(m, x):
    hi, mid, lo = _split3(x)
    return _dot(m, hi) + _dot(m, mid) + _dot(m, lo)


def _full_spec(shape):
    nd = len(shape)
    return pl.BlockSpec(shape, lambda *_: (0,) * nd)


MOD_TILE = 512


def _mod_kernel(c_ref, w_ref, b_ref, o_ref):
    c = c_ref[...]
    a = c * _sigmoid(c)
    o_ref[0] = jnp.dot(a, w_ref[0], preferred_element_type=F32,
                       precision=lax.Precision.HIGHEST) + b_ref[0]


def _mod_call(cc, w_mod, b_mod):
    depth, d, nd = w_mod.shape
    rows = cc.shape[0]
    return pl.pallas_call(
        _mod_kernel,
        grid=(depth, nd // MOD_TILE),
        in_specs=[
            pl.BlockSpec((rows, d), lambda l, j: (0, 0)),
            pl.BlockSpec((1, d, MOD_TILE), lambda l, j: (l, 0, j)),
            pl.BlockSpec((1, 1, MOD_TILE), lambda l, j: (l, 0, j)),
        ],
        out_specs=pl.BlockSpec((1, rows, MOD_TILE), lambda l, j: (l, 0, j)),
        out_shape=jax.ShapeDtypeStruct((depth, rows, nd), F32),
        compiler_params=_cparams(("parallel", "parallel")),
    )(cc, w_mod, b_mod.reshape(depth, 1, nd))


def _store_tokens(ref, val, grid_layout):
    if grid_layout:
        for r in range(val.shape[0] // GRID_W):
            ref[:, r, :] = val[r * GRID_W:(r + 1) * GRID_W]
    else:
        ref[0] = val


def _load_tokens(ref, grid_layout):
    if grid_layout:
        return jnp.concatenate([ref[:, r, :] for r in range(ref.shape[1])], axis=0)
    return ref[0]


def _stage_a_kernel(x_ref, mod_ref, g1_ref, win_ref, sguw_ref, sgub_ref, ones_ref, gw_ref, gb_ref, s5_in_ref,
                    sgu_ref, s5x_ref, qk_ref, v_ref, g_ref, la_ref, *, tb, grid_layout):
    del s5_in_ref
    x = x_ref[0]
    ms = jnp.mean(x * x, axis=-1, keepdims=True)
    xn = x * lax.rsqrt(ms + EPS) * g1_ref[...]
    h = xn * (1.0 + mod_ref[0, 1:2, :]) + mod_ref[0, 0:1, :]
    cols = _dot(h.astype(BF16), win_ref[...])

    u = _gelu(cols[:, 0:SGU_DIM])
    v = _gelu(cols[:, SGU_DIM:2 * SGU_DIM])
    msq = _dot_x_exact(v * v, ones_ref[...]) * (1.0 / SGU_HEAD_DIM)
    vn = (v * lax.rsqrt(msq + EPS)).astype(BF16)
    head_of_lane = lax.broadcasted_iota(jnp.int32, (SGU_CHUNK, SGU_DIM), 1) // SGU_HEAD_DIM
    for ci in range(tb // SGU_CHUNK):
        rows = slice(ci * SGU_CHUNK, (ci + 1) * SGU_CHUNK)
        vc = vn[rows]
        mixed = sgub_ref[...]
        for hh in range(SGU_HEADS):
            mixed = mixed + jnp.where(head_of_lane == hh, _dot(sguw_ref[hh], vc), 0.0)
        sgu_ref[0, rows, :] = u[rows] * mixed

    s5x_ref[0] = cols[:, 512:768]
    q = cols[:, 768:1024] * (GLA_DK ** -0.5)
    _store_tokens(qk_ref, jnp.concatenate([q, cols[:, 1024:1280]], axis=-1), grid_layout)
    _store_tokens(v_ref, cols[:, 1280:1792], grid_layout)
    g_ref[0] = cols[:, 1792:2304]

    z = cols[:, 2304:2432].astype(BF16)
    za = _dot(z, gw_ref[...]) + gb_ref[...]
    log_sig = jnp.minimum(za, 0.0) - jnp.log1p(jnp.exp(-jnp.abs(za)))
    _store_tokens(la_ref, log_sig * (1.0 / GLA_GATE_TEMP), grid_layout)


def _stage_a(xs, mod, g1, win, sguw, sgub, ones_sgu, gw, gb, s5_all, s5_row0, tb, grid_layout):
    b, l, d = xs.shape
    assert s5_row0 % tb == 0
    kern = functools.partial(_stage_a_kernel, tb=tb, grid_layout=grid_layout)
    tok = lambda w: pl.BlockSpec((1, tb, w), lambda bi, i: (bi, i, 0))
    if grid_layout:
        assert tb % GRID_W == 0 and l % GRID_W == 0
        gla = lambda w: pl.BlockSpec((GRID_W, None, tb // GRID_W, w), lambda bi, i: (0, bi, i, 0))
        gla_shape = lambda w: jax.ShapeDtypeStruct((GRID_W, b, l // GRID_W, w), F32)
    else:
        gla = tok
        gla_shape = lambda w: jax.ShapeDtypeStruct((b, l, w), F32)
    nat_shape = lambda w: jax.ShapeDtypeStruct((b, l, w), F32)
    return pl.pallas_call(
        kern,
        grid=(b, l // tb),
        in_specs=[
            tok(d),
            pl.BlockSpec((1, 8, d), lambda bi, i: (bi, 0, 0)),
            _full_spec(g1.shape), _full_spec(win.shape), _full_spec(sguw.shape), _full_spec(sgub.shape),
            _full_spec(ones_sgu.shape), _full_spec(gw.shape), _full_spec(gb.shape),
            pl.BlockSpec(memory_space=pl.ANY),
        ],
        out_specs=[tok(SGU_DIM),
                   pl.BlockSpec((1, tb, S5_DIM), lambda bi, i: (bi, s5_row0 // tb + i, 0)),
                   gla(2 * GLA_KEY_DIM), gla(GLA_DIM), tok(GLA_DIM), gla(2 * GLA_KEY_DIM)],
        out_shape=[nat_shape(SGU_DIM), jax.ShapeDtypeStruct(s5_all.shape, F32),
                   gla_shape(2 * GLA_KEY_DIM), gla_shape(GLA_DIM), nat_shape(GLA_DIM), gla_shape(2 * GLA_KEY_DIM)],
        input_output_aliases={9: 1},
        compiler_params=_cparams(("parallel", "parallel")),
    )(xs, mod, g1, win, sguw, sgub, ones_sgu, gw, gb, s5_all)


S5_TC = 128


def _s5_kernel(uf_ref, ub_ref, perm_ref, permt_ref, b2_ref, ar_ref, ai_ref, c2_ref, yf_ref, yb_ref,
               h_ref, buf_ref, *, nseq):
    tc = S5_TC
    rows = tc * 2 * nseq

    @pl.when(pl.program_id(0) == 0)
    def _():
        h_ref[...] = jnp.zeros_like(h_ref)

    x = jnp.concatenate([uf_ref[b] for b in range(nseq)] + [ub_ref[b] for b in range(nseq)], axis=0)
    u_tm = _dot(perm_ref[...], x.astype(BF16))
    fwd_row = lax.broadcasted_iota(jnp.int32, (rows, S5_DIM), 0) % (2 * nseq) < nseq
    u = jnp.concatenate([jnp.where(fwd_row, u_tm, 0.0), jnp.where(fwd_row, 0.0, u_tm)], axis=-1).astype(BF16)
    buf_ref[...] = _dot(u, b2_ref[...]).reshape(tc, 2 * nseq, 2 * S5_LANES)
    ar = ar_ref[...]
    ai = ai_ref[...]

    def step(t, carry):
        hr, hi = carry
        bur = buf_ref[t, :, 0:S5_LANES]
        bui = buf_ref[t, :, S5_LANES:]
        nhr = ar * hr - ai * hi + bur
        nhi = ar * hi + ai * hr + bui
        buf_ref[t, :, 0:S5_LANES] = nhr
        buf_ref[t, :, S5_LANES:] = nhi
        return nhr, nhi

    hr, hi = lax.fori_loop(0, tc, step, (h_ref[:, 0:S5_LANES], h_ref[:, S5_LANES:]))
    h_ref[:, 0:S5_LANES] = hr
    h_ref[:, S5_LANES:] = hi

    hs = buf_ref[...].reshape(rows, 2 * S5_LANES).astype(BF16)
    y2 = _dot(hs, c2_ref[...])
    y = jnp.where(fwd_row, y2[:, 0:S5_DIM], y2[:, S5_DIM:])
    y_hi = y.astype(BF16)
    y_lo = (y - y_hi.astype(F32)).astype(BF16)
    y_nat = _dot(permt_ref[...], y_hi) + _dot(permt_ref[...], y_lo)
    for b in range(nseq):
        yf_ref[b] = y_nat[b * tc:(b + 1) * tc]
        yb_ref[b] = y_nat[(nseq + b) * tc:(nseq + b + 1) * tc]


def _s5_permutation(nseq):
    tc = S5_TC
    p = np.zeros((tc * 2 * nseq, tc * 2 * nseq), np.float32)
    for q in range(2 * nseq):
        for t in range(tc):
            p[t * 2 * nseq + q, q * tc + (t if q < nseq else tc - 1 - t)] = 1.0
    return p


def _s5_call(s5_all, n_lat, b2, ar, ai, c2):
    nseq, t, _ = s5_all.shape
    n = t // S5_TC
    n_l = n_lat // S5_TC
    n_c = n - n_l
    perm = _s5_permutation(nseq)
    kern = functools.partial(_s5_kernel, nseq=nseq)
    fwd_map = lambda s: (0, jnp.where(s < n_c, n_l + s, s - n_c), 0)
    bwd_map = lambda s: (0, n - 1 - s, 0)
    blk = (nseq, S5_TC, S5_DIM)
    rows = 2 * nseq
    return pl.pallas_call(
        kern,
        grid=(n,),
        in_specs=[
            pl.BlockSpec(blk, fwd_map), pl.BlockSpec(blk, bwd_map),
            _full_spec(perm.shape), _full_spec(perm.shape),
            _full_spec(b2.shape), _full_spec(ar.shape), _full_spec(ai.shape), _full_spec(c2.shape),
        ],
        out_specs=[pl.BlockSpec(blk, fwd_map), pl.BlockSpec(blk, bwd_map)],
        out_shape=[jax.ShapeDtypeStruct(s5_all.shape, F32), jax.ShapeDtypeStruct(s5_all.shape, F32)],
        scratch_shapes=[
            pltpu.VMEM((rows, 2 * S5_LANES), F32),
            pltpu.VMEM((S5_TC, rows, 2 * S5_LANES), F32),
        ],
        compiler_params=_cparams(("arbitrary",)),
    )(s5_all, s5_all, jnp.asarray(perm, dtype=BF16), jnp.asarray(perm.T, dtype=BF16), b2, ar, ai, c2)


def _gla_kernel(qkf_ref, vf_ref, laf_ref, qkb_ref, vb_ref, lab_ref, s0_ref, trif_ref, trib_ref,
                of_ref, ob_ref, sout_ref, s_ref, *, nb):
    c = pl.program_id(0)

    @pl.when(c == 0)
    def _():
        s_ref[...] = s0_ref[...]

    ch = GLA_CHUNK
    r_k = lax.broadcasted_iota(jnp.int32, (GLA_HEADS * ch, GLA_KEY_DIM), 0) // ch
    c_k = lax.broadcasted_iota(jnp.int32, (GLA_HEADS * ch, GLA_KEY_DIM), 1) // GLA_DK
    hm_k = r_k == c_k
    r_v = lax.broadcasted_iota(jnp.int32, (GLA_HEADS * ch, GLA_DIM), 0) // ch
    c_v = lax.broadcasted_iota(jnp.int32, (GLA_HEADS * ch, GLA_DIM), 1) // GLA_DV
    hm_v = r_v == c_v
    r_s = lax.broadcasted_iota(jnp.int32, (GLA_DIM, GLA_KEY_DIM), 0) // GLA_DV
    c_s = lax.broadcasted_iota(jnp.int32, (GLA_DIM, GLA_KEY_DIM), 1) // GLA_DK
    hm_s = r_s == c_s
    t_i = lax.broadcasted_iota(jnp.int32, (ch, GLA_HEADS * ch), 0)
    s_i = lax.broadcasted_iota(jnp.int32, (ch, GLA_HEADS * ch), 1) % ch
    mask_f = t_i >= s_i
    mask_b = t_i <= s_i
    trif = trif_ref[...]
    trib = trib_ref[...]

    fwd = dict(qk=qkf_ref, v=vf_ref, la=laf_ref, o=of_ref, tri=trif, last=ch - 1, ref=ch // 2, mask=mask_f, d=0)
    bwd = dict(qk=qkb_ref, v=vb_ref, la=lab_ref, o=ob_ref, tri=trib, last=0, ref=ch - 1 - ch // 2, mask=mask_b, d=1)
    streams = [(b, p) for b in range(nb) for p in (fwd, bwd)]

    bcums = [_dot_m_exact(p["tri"], p["la"][b]) for b, p in streams]
    scaled = []
    for (b, p), bcum in zip(streams, bcums):
        qk = p["qk"][b]
        q, k = qk[:, 0:GLA_KEY_DIM], qk[:, GLA_KEY_DIM:]
        blast = bcum[p["last"]:p["last"] + 1]
        bref = bcum[p["ref"]:p["ref"] + 1]
        qe = (q * jnp.exp(bcum)).astype(BF16)
        qd = (q * jnp.exp(bcum - bref)).astype(BF16)
        kd = k * jnp.exp(bref - bcum)
        kdec = (k * jnp.exp(blast - bcum)).astype(BF16)
        kst = jnp.where(hm_k, jnp.concatenate([kd] * GLA_HEADS, axis=0), 0.0).astype(BF16)
        scaled.append((qe, qd, kdec, kst, jnp.exp(blast)))
    prods = []
    for (b, p), (qe, qd, kdec, kst, _) in zip(streams, scaled):
        v = p["v"][b]
        sc = _dot_nt(qd, kst)
        o_inter = _dot_nt(qe, s_ref[b, p["d"]].astype(BF16))
        kv_t = _dot_tn(v.astype(BF16), kdec)
        prods.append((sc, o_inter, kv_t))
    for (b, p), (_, _, _, _, decay), (sc, o_inter, kv_t) in zip(streams, scaled, prods):
        vbd = jnp.where(hm_v, jnp.concatenate([p["v"][b]] * GLA_HEADS, axis=0), 0.0).astype(BF16)
        p["o"][b] = _dot(jnp.where(p["mask"], sc, 0.0).astype(BF16), vbd) + o_inter
        s_ref[b, p["d"]] = s_ref[b, p["d"]] * decay + jnp.where(hm_s, kv_t, 0.0)

    @pl.when(c == pl.num_programs(0) - 1)
    def _():
        sout_ref[...] = s_ref[...]


def _gla_call(qk, v, la, s0, trif, trib):
    ch = GLA_CHUNK
    if qk.ndim == 4:
        n, b = qk.shape[0], qk.shape[1]
        assert qk.shape[2] == ch
        spec = lambda w, off, rev: pl.BlockSpec(
            (None, b, ch, w), (lambda c: (n - 1 - c, 0, 0, off)) if rev else (lambda c: (c, 0, 0, off)))
    else:
        b = qk.shape[0]
        n = qk.shape[1] // ch
        spec = lambda w, off, rev: pl.BlockSpec(
            (b, ch, w), (lambda c: (0, n - 1 - c, off)) if rev else (lambda c: (0, c, off)))
    kern = functools.partial(_gla_kernel, nb=b)
    o_shape = jax.ShapeDtypeStruct(v.shape, F32)
    return pl.pallas_call(
        kern,
        grid=(n,),
        in_specs=[
            spec(2 * GLA_KEY_DIM, 0, False), spec(GLA_DIM, 0, False), spec(GLA_KEY_DIM, 0, False),
            spec(2 * GLA_KEY_DIM, 0, True), spec(GLA_DIM, 0, True), spec(GLA_KEY_DIM, 1, True),
            _full_spec(s0.shape), _full_spec(trif.shape), _full_spec(trib.shape),
        ],
        out_specs=[spec(GLA_DIM, 0, False), spec(GLA_DIM, 0, True), _full_spec(s0.shape)],
        out_shape=[o_shape, o_shape, jax.ShapeDtypeStruct(s0.shape, F32)],
        scratch_shapes=[pltpu.VMEM(s0.shape, F32)],
        compiler_params=_cparams(("arbitrary",)),
    )(qk, v, la, qk, v, la, s0, trif, trib)


def _stage_e_kernel(x_ref, mod_ref, sgu_ref, yf_ref, yb_ref, s5x_ref, of_ref, ob_ref, g_ref,
                    dskip_ref, wglu_ref, normg_ref, ones_ref, wout_ref, o_ref, *, grid_layout):
    ys = yf_ref[0] + yb_ref[0] + dskip_ref[...] * s5x_ref[0]
    z = _dot(_gelu(ys).astype(BF16), wglu_ref[...])
    s5o = z[:, 0:S5_DIM] * _sigmoid(z[:, S5_DIM:])
    o = _load_tokens(of_ref, grid_layout) + _load_tokens(ob_ref, grid_layout)
    ms = _dot_x_exact(o * o, ones_ref[...]) * (1.0 / GLA_DV)
    g = g_ref[0]
    gl = o * lax.rsqrt(ms + EPS) * normg_ref[...] * (g * _sigmoid(g))
    y = (_dot(sgu_ref[0].astype(BF16), wout_ref[0:SGU_DIM, :])
         + _dot(s5o.astype(BF16), wout_ref[SGU_DIM:SGU_DIM + S5_DIM, :])
         + _dot(gl.astype(BF16), wout_ref[SGU_DIM + S5_DIM:, :]))
    o_ref[0] = x_ref[0] + mod_ref[0, 2:3, :] * y


def _stage_e(xs, mod, sgu, yf_all, yb_all, s5_all, s5_row0, of, ob, g, dskip, wglu, normg, ones_gla, wout, tb):
    b, l, d = xs.shape
    assert s5_row0 % tb == 0
    grid_layout = of.ndim == 4
    tok = lambda w: pl.BlockSpec((1, tb, w), lambda bi, i: (bi, i, 0))
    s5 = pl.BlockSpec((1, tb, S5_DIM), lambda bi, i: (bi, s5_row0 // tb + i, 0))
    if grid_layout:
        assert tb % GRID_W == 0
        gla = pl.BlockSpec((GRID_W, None, tb // GRID_W, GLA_DIM), lambda bi, i: (0, bi, i, 0))
    else:
        gla = tok(GLA_DIM)
    return pl.pallas_call(
        functools.partial(_stage_e_kernel, grid_layout=grid_layout),
        grid=(b, l // tb),
        in_specs=[
            tok(d), pl.BlockSpec((1, 8, d), lambda bi, i: (bi, 0, 0)),
            tok(SGU_DIM), s5, s5, s5, gla, gla, tok(GLA_DIM),
            _full_spec(dskip.shape), _full_spec(wglu.shape), _full_spec(normg.shape),
            _full_spec(ones_gla.shape), _full_spec(wout.shape),
        ],
        out_specs=tok(d),
        out_shape=jax.ShapeDtypeStruct((b, l, d), F32),
        compiler_params=_cparams(("parallel", "parallel")),
    )(xs, mod, sgu, yf_all, yb_all, s5_all, of, ob, g, dskip, wglu, normg, ones_gla, wout)


PEER_TBF = 256
SUBLANES = 8


def _sort_network_16():
    def merge(lo, hi, r):
        step = r * 2
        if step < hi - lo:
            yield from merge(lo, hi, step)
            yield from merge(lo + r, hi, step)
            for i in range(lo + r, hi - r, step):
                yield (i, i + r)
        else:
            yield (lo, lo + r)

    def sort(lo, hi):
        if hi - lo >= 1:
            mid = lo + (hi - lo) // 2
            yield from sort(lo, mid)
            yield from sort(mid + 1, hi)
            yield from merge(lo, hi, 1)

    return tuple(sort(0, PEER_TOPK - 1))


SORT16 = _sort_network_16()
BITONIC16 = tuple((k, k + s) for s in (8, 4, 2, 1) for k in range(PEER_TOPK) if not k & s)


def _compare_exchange(xs, pairs):
    xs = list(xs)
    for i, j in pairs:
        hi = jnp.maximum(xs[i], xs[j])
        lo = jnp.minimum(xs[i], xs[j])
        xs[i], xs[j] = hi, lo
    return xs


def _merge_sublanes(xs):
    for shift in (4, 6, 7):
        rolled = [pltpu.roll(x, shift, 0) for x in xs]
        xs = [jnp.maximum(xs[k], rolled[PEER_TOPK - 1 - k]) for k in range(PEER_TOPK)]
        xs = _compare_exchange(xs, BITONIC16)
    return xs


def _dup_bf16_words(x):
    bits = pltpu.bitcast(x.astype(BF16).astype(F32), jnp.int32)
    return bits | lax.shift_right_logical(bits, 16)


def _stage_f_kernel(x_ref, mod_ref, g2_ref, wqt_ref, keys_ref,
                    ht_ref, e1w_ref, n1w_ref, e2_ref, r2_ref, qt_ref, v1_ref, v2_ref, *, tb):
    x = x_ref[0]
    ms = jnp.mean(x * x, axis=-1, keepdims=True)
    xn = x * lax.rsqrt(ms + EPS) * g2_ref[...]
    h = xn * (1.0 + mod_ref[0, 4:5, :]) + mod_ref[0, 3:4, :]
    ht = h.T.astype(BF16)
    ht_ref[...] = pltpu.bitcast(ht, jnp.int32)
    qt_ref[...] = _dot(wqt_ref[...], ht)
    k_top = PEER_TOPK

    def tiles(s):
        return [s[SUBLANES * k:SUBLANES * (k + 1)] for k in range(PEER_NKEYS // SUBLANES)]

    def head_body(hh, carry):
        for tc in range(tb // LANE):
            tcol = slice(tc * LANE, (tc + 1) * LANE)
            r1 = pl.multiple_of(hh * (2 * PEER_HALF), 2 * PEER_HALF)
            q1 = qt_ref[pl.ds(r1, PEER_HALF), tcol].astype(BF16)
            q2 = qt_ref[pl.ds(r1 + PEER_HALF, PEER_HALF), tcol].astype(BF16)
            s1 = _dot(keys_ref[hh], q1)
            s2 = _dot(keys_ref[PEER_HEADS + hh], q2)
            for s, v_ref in ((s1, v1_ref), (s2, v2_ref)):
                top = _merge_sublanes(_compare_exchange(tiles(s), SORT16))
                for k in range(k_top):
                    v_ref[k:k + 1, tcol] = top[k][0:1]
            v1row = lambda a: v1_ref[a:a + 1, tcol]
            v2row = lambda b: v2_ref[b:b + 1, tcol]
            v1lo = v1_ref[0:SUBLANES, tcol]
            v2lo = v2_ref[0:SUBLANES, tcol]
            cand = [v1lo + v2row(b) for b in range(k_top)]
            tail = [v1row(a) + v2lo for a in range(SUBLANES, k_top)]
            for k in range(SUBLANES, k_top):
                cand[k] = jnp.maximum(cand[k], tail[k_top - 1 - k])
            best = _merge_sublanes(_compare_exchange(cand, BITONIC16))
            theta = best[k_top - 1][0:1]
            cmax = best[0][0:1]
            zsum = jnp.zeros((1, LANE), F32)
            for k in range(k_top):
                zsum = zsum + jnp.exp(best[k][0:1] - cmax)
            rz = 1.0 / zsum
            n_top = jnp.zeros((1, LANE), F32)
            for b in range(k_top):
                n_top = jnp.where(v1row(0) + v2row(b) >= theta, float(b + 1), n_top)
            n1 = jnp.zeros(s1.shape, F32)
            for b in range(SUBLANES):
                n1 = jnp.where(s1 + v2row(b) >= theta, float(b + 1), n1)
            n1 = jnp.where(s1 >= v1row(0), n_top, n1)
            r2 = jnp.zeros(s2.shape, F32)
            for b in range(k_top):
                r2 = jnp.where(v2row(b) > s2, float(b + 1), r2)
            e1 = jnp.where(s1 >= v1row(k_top - 1), jnp.exp(s1 - v1row(0)), 0.0) * rz
            e2 = jnp.where(s2 >= v2row(k_top - 1), jnp.exp(s2 - v2row(0)), 0.0)
            e1w_ref[tc, hh] = _dup_bf16_words(e1)
            n1w_ref[tc, hh] = _dup_bf16_words(n1)
            e2_ref[tc, hh] = pltpu.bitcast(e2.astype(BF16), jnp.int32)
            r2_ref[tc, hh] = pltpu.bitcast(r2.astype(BF16), jnp.int32)
        return carry

    lax.fori_loop(0, PEER_HEADS, head_body, 0)


def _stage_f(xs, mod, g2, wqt, keys, tb):
    b, l, d = xs.shape
    nblk = l // tb
    ntok = b * l
    nch = ntok // LANE
    kern = functools.partial(_stage_f_kernel, tb=tb)
    row_spec = pl.BlockSpec((tb // LANE, PEER_HEADS, PEER_NKEYS, LANE), lambda bi, i: (bi * nblk + i, 0, 0, 0))
    pair_spec = pl.BlockSpec((tb // LANE, PEER_HEADS, PEER_NKEYS // 2, LANE), lambda bi, i: (bi * nblk + i, 0, 0, 0))
    desc_shape = lambda rows: jax.ShapeDtypeStruct((nch, PEER_HEADS, rows, LANE), jnp.int32)
    return pl.pallas_call(
        kern,
        grid=(b, nblk),
        in_specs=[
            pl.BlockSpec((1, tb, d), lambda bi, i: (bi, i, 0)),
            pl.BlockSpec((1, 8, d), lambda bi, i: (bi, 0, 0)),
            _full_spec(g2.shape), _full_spec(wqt.shape), _full_spec(keys.shape),
        ],
        out_specs=[pl.BlockSpec((d // 2, tb), lambda bi, i: (0, bi * nblk + i)),
                   row_spec, row_spec, pair_spec, pair_spec],
        out_shape=[jax.ShapeDtypeStruct((d // 2, ntok), jnp.int32), desc_shape(PEER_NKEYS), desc_shape(PEER_NKEYS),
                   desc_shape(PEER_NKEYS // 2), desc_shape(PEER_NKEYS // 2)],
        scratch_shapes=[
            pltpu.VMEM((PEER_HEADS * 2 * PEER_HALF, tb), F32),
            pltpu.VMEM((PEER_TOPK, tb), F32),
            pltpu.VMEM((PEER_TOPK, tb), F32),
        ],
        compiler_params=_cparams(("parallel", "parallel")),
    )(xs, mod, g2, wqt, keys)


PEER_TBG = 1024
PEER_TE = 1024
PEER_I1_PER_TILE = PEER_TE // PEER_NKEYS
PEER_N_TILES = PEER_NKEYS * PEER_NKEYS // PEER_TE
PEER_MXU_COLS = 256
PEER_G_FLAGS = None


def _stage_g_kernel(htw_ref, e1w_ref, n1w_ref, e2_ref, r2_ref, uw_ref, vtw_ref, x_ref, mod_ref,
                    o_ref, acc_ref, pa_ref, pb_ref, *, tb):
    i = pl.program_id(1)
    n_tiles = pl.num_programs(1) - 1

    @pl.when(i == 0)
    def _():
        acc_ref[...] = jnp.zeros_like(acc_ref)
        pb_ref[...] = jnp.zeros_like(pb_ref)

    def step(cur_ref, prev_ref):
        per_grp = PEER_MXU_COLS // LANE
        for grp in range(tb // PEER_MXU_COLS):
            cols = slice(grp * PEER_MXU_COLS, (grp + 1) * PEER_MXU_COLS)
            for tcl in range(per_grp):
                tc = grp * per_grp + tcl
                tcol = slice(tc * LANE, (tc + 1) * LANE)
                for i1l in range(PEER_I1_PER_TILE):
                    rows = slice(i1l * PEER_NKEYS, (i1l + 1) * PEER_NKEYS)
                    gate = jnp.zeros((PEER_NKEYS, LANE), BF16)
                    for hh in range(PEER_HEADS):
                        e1row = e1w_ref[tc, hh, i1l:i1l + 1, :]
                        n1row = n1w_ref[tc, hh, i1l:i1l + 1, :]
                        e1 = pltpu.bitcast(jnp.broadcast_to(e1row, (PEER_NKEYS // 2, LANE)), BF16)
                        n1 = pltpu.bitcast(jnp.broadcast_to(n1row, (PEER_NKEYS // 2, LANE)), BF16)
                        r2 = pltpu.bitcast(r2_ref[tc, hh], BF16)
                        e2 = pltpu.bitcast(e2_ref[tc, hh], BF16)
                        gate = gate + e2 * jnp.where(r2 < n1, e1, 0.0)
                    cur_ref[rows, tcol] = gate
            vt = pltpu.bitcast(vtw_ref[...], BF16)
            u = pltpu.bitcast(uw_ref[...], BF16)
            ht = pltpu.bitcast(htw_ref[:, cols], BF16)
            acc_ref[:, cols] += _dot(vt, prev_ref[:, cols])
            at = _dot(u, ht).astype(BF16)
            for tcl in range(per_grp):
                tcol = slice((grp * per_grp + tcl) * LANE, (grp * per_grp + tcl + 1) * LANE)
                for i1l in range(PEER_I1_PER_TILE):
                    rows = slice(i1l * PEER_NKEYS, (i1l + 1) * PEER_NKEYS)
                    a = at[rows, tcl * LANE:(tcl + 1) * LANE]
                    cur_ref[rows, tcol] = cur_ref[rows, tcol] * _gelu(a)

    @pl.when(jnp.logical_and(i < n_tiles, i % 2 == 0))
    def _():
        step(pa_ref, pb_ref)

    @pl.when(jnp.logical_and(i < n_tiles, i % 2 == 1))
    def _():
        step(pb_ref, pa_ref)

    @pl.when(i == n_tiles)
    def _():
        last_ref = pb_ref if (PEER_N_TILES - 1) % 2 else pa_ref
        acc = acc_ref[...] + _dot(pltpu.bitcast(vtw_ref[...], BF16), last_ref[...])
        o_ref[...] = x_ref[...] + mod_ref[0, 5:6, :] * acc.T


def _stage_g(htw, e1w, n1w, e2, r2, uw, vtw, xflat, mod, tokens_per_batch, tb):
    ntok = htw.shape[1]
    d = 2 * htw.shape[0]
    assert 2 * uw.shape[0] == PEER_N_TILES * PEER_TE
    blocks_per_batch = tokens_per_batch // tb
    kern = functools.partial(_stage_g_kernel, tb=tb)
    last = PEER_N_TILES - 1
    desc_spec = pl.BlockSpec((tb // LANE, PEER_HEADS, PEER_NKEYS // 2, LANE), lambda j, i: (j, 0, 0, 0))
    row_spec = pl.BlockSpec((tb // LANE, PEER_HEADS, PEER_I1_PER_TILE, LANE),
                            lambda j, i: (j, 0, jnp.minimum(i, last), 0))
    return pl.pallas_call(
        kern,
        grid=(ntok // tb, PEER_N_TILES + 1),
        in_specs=[
            pl.BlockSpec((d // 2, tb), lambda j, i: (0, j)),
            row_spec, row_spec, desc_spec, desc_spec,
            pl.BlockSpec((PEER_TE // 2, d), lambda j, i: (jnp.minimum(i, last), 0)),
            pl.BlockSpec((d // 2, PEER_TE), lambda j, i: (0, jnp.maximum(i - 1, 0))),
            pl.BlockSpec((tb, d), lambda j, i: (j, 0)),
            pl.BlockSpec((1, 8, d), lambda j, i: (j // blocks_per_batch, 0, 0)),
        ],
        out_specs=pl.BlockSpec((tb, d), lambda j, i: (j, 0)),
        out_shape=jax.ShapeDtypeStruct((ntok, d), F32),
        scratch_shapes=[
            pltpu.VMEM((d, tb), F32),
            pltpu.VMEM((PEER_TE, tb), BF16),
            pltpu.VMEM((PEER_TE, tb), BF16),
        ],
        compiler_params=_cparams(("parallel", "arbitrary"), PEER_G_FLAGS),
    )(htw, e1w, n1w, e2, r2, uw, vtw, xflat, mod)


def _final_norm_kernel(x_ref, g_ref, o_ref):
    x = x_ref[...]
    ms = jnp.mean(x * x, axis=-1, keepdims=True)
    o_ref[...] = x * lax.rsqrt(ms + EPS) * g_ref[...]


def _final_norm(xflat, g, tb=512):
    n, d = xflat.shape
    return pl.pallas_call(
        _final_norm_kernel,
        grid=(n // tb,),
        in_specs=[pl.BlockSpec((tb, d), lambda i: (i, 0)), _full_spec(g.shape)],
        out_specs=pl.BlockSpec((tb, d), lambda i: (i, 0)),
        out_shape=jax.ShapeDtypeStruct((n, d), F32),
        compiler_params=_cparams(("parallel",)),
    )(xflat, g)


def _pack_kernel(x_ref, o_ref, *, transpose):
    x = x_ref[...]
    if transpose:
        x = x.T
    o_ref[...] = pltpu.bitcast(x.astype(BF16), jnp.int32)


def _pack_row_pairs(x, layer, transpose=False, tile=1024):
    _, r, c = x.shape
    if transpose:
        out_shape, out_spec = (c // 2, r), pl.BlockSpec((c // 2, tile), lambda i: (0, i))
    else:
        out_shape, out_spec = (r // 2, c), pl.BlockSpec((tile // 2, c), lambda i: (i, 0))
    return pl.pallas_call(
        functools.partial(_pack_kernel, transpose=transpose),
        grid=(r // tile,),
        in_specs=[pl.BlockSpec((None, tile, c), lambda i: (layer, i, 0))],
        out_specs=out_spec,
        out_shape=jax.ShapeDtypeStruct(out_shape, jnp.int32),
        compiler_params=_cparams(("parallel",)),
    )(x)


def _block_ones(n, blk):
    idx = np.arange(n) // blk
    return jnp.asarray((idx[:, None] == idx[None, :]).astype(np.float32), dtype=BF16)


def _s5_discretise(lam_re, lam_im, b_re, b_im, log_step):
    lam_re = jnp.minimum(lam_re.astype(F32), -1e-4)
    lam_im = lam_im.astype(F32)
    dt = jnp.exp(log_step.astype(F32))[:, None]
    mag = jnp.exp(lam_re * dt)
    a_re = mag * jnp.cos(lam_im * dt)
    a_im = mag * jnp.sin(lam_im * dt)
    den = lam_re * lam_re + lam_im * lam_im
    f_re = ((a_re - 1.0) * lam_re + a_im * lam_im) / den
    f_im = (a_im * lam_re - (a_re - 1.0) * lam_im) / den
    b_re = b_re.astype(F32)
    b_im = b_im.astype(F32)
    bb_re = f_re[..., None] * b_re - f_im[..., None] * b_im
    bb_im = f_re[..., None] * b_im + f_im[..., None] * b_re
    return a_re, a_im, bb_re, bb_im


def _group_block_diag(t):
    g, r, c = t.shape
    eye = jnp.eye(g, dtype=t.dtype)
    return (t[:, :, None, :] * eye[:, None, :, None]).reshape(g * r, g * c)


def _s5_params(lam_re, lam_im, b_re, b_im, c_re, c_im, log_step, nseq):
    b_rows, c_cols, ars, ais = [], [], [], []
    for d in range(2):
        a_re, a_im, bb_re, bb_im = _s5_discretise(lam_re[d], lam_im[d], b_re[d], b_im[d], log_step[d])
        bm = jnp.concatenate([_group_block_diag(jnp.swapaxes(bb_re, 1, 2)),
                              _group_block_diag(jnp.swapaxes(bb_im, 1, 2))], axis=1)
        b_rows.append(bm)
        cm = jnp.concatenate([_group_block_diag(jnp.swapaxes(c_re[d].astype(F32), 1, 2)),
                              -_group_block_diag(jnp.swapaxes(c_im[d].astype(F32), 1, 2))], axis=0)
        c_cols.append(cm)
        ars.append(jnp.broadcast_to(a_re.reshape(1, S5_LANES), (nseq, S5_LANES)))
        ais.append(jnp.broadcast_to(a_im.reshape(1, S5_LANES), (nseq, S5_LANES)))
    b2 = jnp.concatenate(b_rows, axis=0).astype(BF16)
    c2 = jnp.concatenate(c_cols, axis=1).astype(BF16)
    return b2, jnp.concatenate(ars, axis=0), jnp.concatenate(ais, axis=0), c2


def kernel(x, c, ctx, c_ctx, w_mod, b_mod, norm1_g, norm2_g, w_in, w_out, sgu_w, sgu_b, s5_lambda_re, s5_lambda_im, s5_b_re, s5_b_im, s5_c_re, s5_c_im, s5_log_step, s5_d, s5_w_glu, gla_w_gate, gla_b_gate, gla_norm_g, peer_w_query, peer_sub_keys, peer_expert_u, peer_expert_v, final_norm_g):
    nb, seq, d = x.shape
    c_len = ctx.shape[1]
    depth = w_mod.shape[0]

    cc = jnp.concatenate([c, c_ctx[None, :], jnp.zeros((8 - nb - 1, d), F32)], axis=0)
    mods = _mod_call(cc, w_mod, b_mod)

    ones_sgu = _block_ones(SGU_DIM, SGU_HEAD_DIM)
    ones_gla = _block_ones(GLA_DIM, GLA_DV)
    tri_np = np.tril(np.ones((GLA_CHUNK, GLA_CHUNK), np.float32))
    trif = jnp.asarray(tri_np, dtype=BF16)
    trib = jnp.asarray(tri_np.T, dtype=BF16)
    s_zero = jnp.zeros((nb, 2, GLA_DIM, GLA_KEY_DIM), F32)

    xl, xc = x, ctx
    for l in range(depth):
        ctx_out = l < depth - 1
        m6 = mods[l].reshape(8, N_MOD, d)
        mod_l = jnp.pad(m6[:nb], ((0, 0), (0, 2), (0, 0)))
        mod_c = jnp.broadcast_to(jnp.pad(m6[nb], ((0, 2), (0, 0)))[None], (nb, 8, d))

        win = jnp.pad(w_in[l], ((0, 0), (0, IN_PAD - IN_WIDTH))).astype(BF16)
        sguw = sgu_w[l].astype(BF16)
        sgub = jnp.repeat(jnp.swapaxes(sgu_b[l], 0, 1), SGU_HEAD_DIM, axis=1)
        gw = jnp.zeros((LANE, 2 * GLA_KEY_DIM), F32)
        gw = gw.at[0:GLA_RANK, 0:GLA_KEY_DIM].set(gla_w_gate[l, 0])
        gw = gw.at[GLA_RANK:2 * GLA_RANK, GLA_KEY_DIM:].set(gla_w_gate[l, 1]).astype(BF16)
        gb = gla_b_gate[l].reshape(1, 2 * GLA_KEY_DIM)
        g1 = norm1_g[l].reshape(1, d)

        s5_all = jnp.zeros((nb, seq + c_len, S5_DIM), F32)
        sgu_l, s5_all, qk_l, v_l, g_l, la_l = _stage_a(xl, mod_l, g1, win, sguw, sgub, ones_sgu, gw, gb,
                                                       s5_all, 0, tb=TB_LATENT, grid_layout=True)
        sgu_c, s5_all, qk_c, v_c, g_c, la_c = _stage_a(xc, mod_c, g1, win, sguw, sgub, ones_sgu, gw, gb,
                                                       s5_all, seq, tb=TB_CTX, grid_layout=False)

        b2, ar, ai, c2 = _s5_params(s5_lambda_re[l], s5_lambda_im[l], s5_b_re[l], s5_b_im[l],
                                    s5_c_re[l], s5_c_im[l], s5_log_step[l], nb)
        yf_all, yb_all = _s5_call(s5_all, seq, b2, ar, ai, c2)

        of_c, ob_c, s_ctx = _gla_call(qk_c, v_c, la_c, s_zero, trif, trib)
        of_l, ob_l, _ = _gla_call(qk_l, v_l, la_l, s_ctx, trif, trib)

        dskip = s5_d[l].reshape(1, S5_DIM)
        wglu = s5_w_glu[l].astype(BF16)
        normg = gla_norm_g[l].reshape(1, GLA_DIM)
        wout = w_out[l].astype(BF16)
        g2 = norm2_g[l].reshape(1, d)
        wqt = jnp.swapaxes(peer_w_query[l], 0, 1).astype(BF16)
        keys = peer_sub_keys[l].reshape(2 * PEER_HEADS, PEER_NKEYS, PEER_HALF).astype(BF16)
        u_bf = _pack_row_pairs(peer_expert_u, l)
        vt_bf = _pack_row_pairs(peer_expert_v, l, transpose=True)

        xl = _stage_e(xl, mod_l, sgu_l, yf_all, yb_all, s5_all, 0, of_l, ob_l, g_l,
                      dskip, wglu, normg, ones_gla, wout, tb=TB_LATENT)
        desc = _stage_f(xl, mod_l, g2, wqt, keys, tb=PEER_TBF)
        xl = _stage_g(*desc, u_bf, vt_bf, xl.reshape(nb * seq, d), mod_l, seq, PEER_TBG).reshape(nb, seq, d)

        if ctx_out:
            xc = _stage_e(xc, mod_c, sgu_c, yf_all, yb_all, s5_all, seq, of_c, ob_c, g_c,
                          dskip, wglu, normg, ones_gla, wout, tb=TB_CTX)
            desc = _stage_f(xc, mod_c, g2, wqt, keys, tb=PEER_TBF)
            xc = _stage_g(*desc, u_bf, vt_bf, xc.reshape(nb * c_len, d), mod_c, c_len,
                          min(PEER_TBG, c_len)).reshape(nb, c_len, d)

    return _final_norm(xl.reshape(nb * seq, d), final_norm_g.reshape(1, d)).reshape(nb, seq, d)
```

```python
import functools
import math

import numpy as np
import jax
import jax.numpy as jnp
from jax import lax
from jax.experimental import pallas as pl
from jax.experimental.pallas import tpu as pltpu

F32 = jnp.float32
BF16 = jnp.bfloat16

EPS = 1e-6
N_MOD = 6
GRID_W = 64

SGU_DIM = 256
SGU_HEADS = 4
SGU_HEAD_DIM = 64
SGU_CHUNK = 128

S5_DIM = 256
S5_GROUP = 16
S5_GROUPS = 16
S5_STATE = 64
S5_LANES = S5_GROUPS * S5_STATE

GLA_DIM = 512
GLA_HEADS = 8
GLA_DV = 64
GLA_DK = 32
GLA_KEY_DIM = 256
GLA_RANK = 16
GLA_GATE_TEMP = 16.0
GLA_CHUNK = 64

PEER_HEADS = 8
PEER_NKEYS = 128
PEER_HALF = 128
PEER_TOPK = 16

IN_WIDTH = 2336
IN_PAD = 2432
LANE = 128

VMEM_LIMIT = 56 * 1024 * 1024
TB_LATENT = 512
TB_CTX = 256

NEG_INF = float("-inf")
POS_INF = float("inf")


def _cparams(sem, flags=None):
    return pltpu.CompilerParams(dimension_semantics=sem, vmem_limit_bytes=VMEM_LIMIT, flags=flags)


def _gelu(x):
    c = math.sqrt(2.0 / math.pi)
    return 0.5 * x * (1.0 + jnp.tanh(c * (x + 0.044715 * (x * x * x))))


def _sigmoid(x):
    return 1.0 / (1.0 + jnp.exp(-x))


def _dot(a, b):
    return jnp.dot(a, b, preferred_element_type=F32)


def _dot_nt(a, b):
    return lax.dot_general(a, b, (((1,), (1,)), ((), ())), preferred_element_type=F32)


def _dot_tn(a, b):
    return lax.dot_general(a, b, (((0,), (0,)), ((), ())), preferred_element_type=F32)


def _split3(x):
    hi = x.astype(BF16)
    r = x - hi.astype(F32)
    mid = r.astype(BF16)
    lo = (r - mid.astype(F32)).astype(BF16)
    return hi, mid, lo


def _dot_x_exact(x, m):
    hi, mid, lo = _split3(x)
    return _dot(hi, m) + _dot(mid, m) + _dot(lo, m)


def _dot_m_exact(m, x):
    hi, mid, lo = _split3(x)
    return _dot(m, hi) + _dot(m, mid) + _dot(m, lo)


def _full_spec(shape):
    nd = len(shape)
    return pl.BlockSpec(shape, lambda *_: (0,) * nd)


MOD_TILE = 512


def _mod_kernel(c_ref, w_ref, b_ref, o_ref):
    c = c_ref[...]
    a = c * _sigmoid(c)
    o_ref[0] = jnp.dot(a, w_ref[0], preferred_element_type=F32,
                       precision=lax.Precision.HIGHEST) + b_ref[0]


def _mod_call(cc, w_mod, b_mod):
    depth, d, nd = w_mod.shape
    rows = cc.shape[0]
    return pl.pallas_call(
        _mod_kernel,
        grid=(depth, nd // MOD_TILE),
        in_specs=[
            pl.BlockSpec((rows, d), lambda l, j: (0, 0)),
            pl.BlockSpec((1, d, MOD_TILE), lambda l, j: (l, 0, j)),
            pl.BlockSpec((1, 1, MOD_TILE), lambda l, j: (l, 0, j)),
        ],
        out_specs=pl.BlockSpec((1, rows, MOD_TILE), lambda l, j: (l, 0, j)),
        out_shape=jax.ShapeDtypeStruct((depth, rows, nd), F32),
        compiler_params=_cparams(("parallel", "parallel")),
    )(cc, w_mod, b_mod.reshape(depth, 1, nd))


def _store_tokens(ref, val, grid_layout):
    if grid_layout:
        for r in range(val.shape[0] // GRID_W):
            ref[:, r, :] = val[r * GRID_W:(r + 1) * GRID_W]
    else:
        ref[0] = val


def _load_tokens(ref, grid_layout):
    if grid_layout:
        return jnp.concatenate([ref[:, r, :] for r in range(ref.shape[1])], axis=0)
    return ref[0]


def _stage_a_kernel(x_ref, mod_ref, g1_ref, win_ref, sguw_ref, sgub_ref, ones_ref, gw_ref, gb_ref, s5_in_ref,
                    sgu_ref, s5x_ref, qk_ref, v_ref, g_ref, la_ref, *, tb, grid_layout):
    del s5_in_ref
    x = x_ref[0]
    ms = jnp.mean(x * x, axis=-1, keepdims=True)
    xn = x * lax.rsqrt(ms + EPS) * g1_ref[...]
    h = xn * (1.0 + mod_ref[0, 1:2, :]) + mod_ref[0, 0:1, :]
    cols = _dot(h.astype(BF16), win_ref[...])

    u = _gelu(cols[:, 0:SGU_DIM])
    v = _gelu(cols[:, SGU_DIM:2 * SGU_DIM])
    msq = _dot_x_exact(v * v, ones_ref[...]) * (1.0 / SGU_HEAD_DIM)
    vn = (v * lax.rsqrt(msq + EPS)).astype(BF16)
    head_of_lane = lax.broadcasted_iota(jnp.int32, (SGU_CHUNK, SGU_DIM), 1) // SGU_HEAD_DIM
    for ci in range(tb // SGU_CHUNK):
        rows = slice(ci * SGU_CHUNK, (ci + 1) * SGU_CHUNK)
        vc = vn[rows]
        mixed = sgub_ref[...]
        for hh in range(SGU_HEADS):
            mixed = mixed + jnp.where(head_of_lane == hh, _dot(sguw_ref[hh], vc), 0.0)
        sgu_ref[0, rows, :] = u[rows] * mixed

    s5x_ref[0] = cols[:, 512:768]
    q = cols[:, 768:1024] * (GLA_DK ** -0.5)
    _store_tokens(qk_ref, jnp.concatenate([q, cols[:, 1024:1280]], axis=-1), grid_layout)
    _store_tokens(v_ref, cols[:, 1280:1792], grid_layout)
    g_ref[0] = cols[:, 1792:2304]

    z = cols[:, 2304:2432].astype(BF16)
    za = _dot(z, gw_ref[...]) + gb_ref[...]
    log_sig = jnp.minimum(za, 0.0) - jnp.log1p(jnp.exp(-jnp.abs(za)))
    _store_tokens(la_ref, log_sig * (1.0 / GLA_GATE_TEMP), grid_layout)


def _stage_a(xs, mod, g1, win, sguw, sgub, ones_sgu, gw, gb, s5_all, s5_row0, tb, grid_layout):
    b, l, d = xs.shape
    assert s5_row0 % tb == 0
    kern = functools.partial(_stage_a_kernel, tb=tb, grid_layout=grid_layout)
    tok = lambda w: pl.BlockSpec((1, tb, w), lambda bi, i: (bi, i, 0))
    if grid_layout:
        assert tb % GRID_W == 0 and l % GRID_W == 0
        gla = lambda w: pl.BlockSpec((GRID_W, None, tb // GRID_W, w), lambda bi, i: (0, bi, i, 0))
        gla_shape = lambda w: jax.ShapeDtypeStruct((GRID_W, b, l // GRID_W, w), F32)
    else:
        gla = tok
        gla_shape = lambda w: jax.ShapeDtypeStruct((b, l, w), F32)
    nat_shape = lambda w: jax.ShapeDtypeStruct((b, l, w), F32)
    return pl.pallas_call(
        kern,
        grid=(b, l // tb),
        in_specs=[
            tok(d),
            pl.BlockSpec((1, 8, d), lambda bi, i: (bi, 0, 0)),
            _full_spec(g1.shape), _full_spec(win.shape), _full_spec(sguw.shape), _full_spec(sgub.shape),
            _full_spec(ones_sgu.shape), _full_spec(gw.shape), _full_spec(gb.shape),
            pl.BlockSpec(memory_space=pl.ANY),
        ],
        out_specs=[tok(SGU_DIM),
                   pl.BlockSpec((1, tb, S5_DIM), lambda bi, i: (bi, s5_row0 // tb + i, 0)),
                   gla(2 * GLA_KEY_DIM), gla(GLA_DIM), tok(GLA_DIM), gla(2 * GLA_KEY_DIM)],
        out_shape=[nat_shape(SGU_DIM), jax.ShapeDtypeStruct(s5_all.shape, F32),
                   gla_shape(2 * GLA_KEY_DIM), gla_shape(GLA_DIM), nat_shape(GLA_DIM), gla_shape(2 * GLA_KEY_DIM)],
        input_output_aliases={9: 1},
        compiler_params=_cparams(("parallel", "parallel")),
    )(xs, mod, g1, win, sguw, sgub, ones_sgu, gw, gb, s5_all)


S5_TC = 128


def _s5_kernel(uf_ref, ub_ref, perm_ref, permt_ref, b2_ref, ar_ref, ai_ref, c2_ref, yf_ref, yb_ref,
               h_ref, buf_ref, *, nseq):
    tc = S5_TC
    rows = tc * 2 * nseq

    @pl.when(pl.program_id(0) == 0)
    def _():
        h_ref[...] = jnp.zeros_like(h_ref)

    x = jnp.concatenate([uf_ref[b] for b in range(nseq)] + [ub_ref[b] for b in range(nseq)], axis=0)
    u_tm = _dot(perm_ref[...], x.astype(BF16))
    fwd_row = lax.broadcasted_iota(jnp.int32, (rows, S5_DIM), 0) % (2 * nseq) < nseq
    u = jnp.concatenate([jnp.where(fwd_row, u_tm, 0.0), jnp.where(fwd_row, 0.0, u_tm)], axis=-1).astype(BF16)
    buf_ref[...] = _dot(u, b2_ref[...]).reshape(tc, 2 * nseq, 2 * S5_LANES)
    ar = ar_ref[...]
    ai = ai_ref[...]

    def step(t, carry):
        hr, hi = carry
        bur = buf_ref[t, :, 0:S5_LANES]
        bui = buf_ref[t, :, S5_LANES:]
        nhr = ar * hr - ai * hi + bur
        nhi = ar * hi + ai * hr + bui
        buf_ref[t, :, 0:S5_LANES] = nhr
        buf_ref[t, :, S5_LANES:] = nhi
        return nhr, nhi

    hr, hi = lax.fori_loop(0, tc, step, (h_ref[:, 0:S5_LANES], h_ref[:, S5_LANES:]))
    h_ref[:, 0:S5_LANES] = hr
    h_ref[:, S5_LANES:] = hi

    hs = buf_ref[...].reshape(rows, 2 * S5_LANES).astype(BF16)
    y2 = _dot(hs, c2_ref[...])
    y = jnp.where(fwd_row, y2[:, 0:S5_DIM], y2[:, S5_DIM:])
    y_hi = y.astype(BF16)
    y_lo = (y - y_hi.astype(F32)).astype(BF16)
    y_nat = _dot(permt_ref[...], y_hi) + _dot(permt_ref[...], y_lo)
    for b in range(nseq):
        yf_ref[b] = y_nat[b * tc:(b + 1) * tc]
        yb_ref[b] = y_nat[(nseq + b) * tc:(nseq + b + 1) * tc]


def _s5_permutation(nseq):
    tc = S5_TC
    p = np.zeros((tc * 2 * nseq, tc * 2 * nseq), np.float32)
    for q in range(2 * nseq):
        for t in range(tc):
            p[t * 2 * nseq + q, q * tc + (t if q < nseq else tc - 1 - t)] = 1.0
    return p


def _s5_call(s5_all, n_lat, b2, ar, ai, c2):
    nseq, t, _ = s5_all.shape
    n = t // S5_TC
    n_l = n_lat // S5_TC
    n_c = n - n_l
    perm = _s5_permutation(nseq)
    kern = functools.partial(_s5_kernel, nseq=nseq)
    fwd_map = lambda s: (0, jnp.where(s < n_c, n_l + s, s - n_c), 0)
    bwd_map = lambda s: (0, n - 1 - s, 0)
    blk = (nseq, S5_TC, S5_DIM)
    rows = 2 * nseq
    return pl.pallas_call(
        kern,
        grid=(n,),
        in_specs=[
            pl.BlockSpec(blk, fwd_map), pl.BlockSpec(blk, bwd_map),
            _full_spec(perm.shape), _full_spec(perm.shape),
            _full_spec(b2.shape), _full_spec(ar.shape), _full_spec(ai.shape), _full_spec(c2.shape),
        ],
        out_specs=[pl.BlockSpec(blk, fwd_map), pl.BlockSpec(blk, bwd_map)],
        out_shape=[jax.ShapeDtypeStruct(s5_all.shape, F32), jax.ShapeDtypeStruct(s5_all.shape, F32)],
        scratch_shapes=[
            pltpu.VMEM((rows, 2 * S5_LANES), F32),
            pltpu.VMEM((S5_TC, rows, 2 * S5_LANES), F32),
        ],
        compiler_params=_cparams(("arbitrary",)),
    )(s5_all, s5_all, jnp.asarray(perm, dtype=BF16), jnp.asarray(perm.T, dtype=BF16), b2, ar, ai, c2)


def _gla_kernel(qkf_ref, vf_ref, laf_ref, qkb_ref, vb_ref, lab_ref, s0_ref, trif_ref, trib_ref,
                of_ref, ob_ref, sout_ref, s_ref, *, nb):
    c = pl.program_id(0)

    @pl.when(c == 0)
    def _():
        s_ref[...] = s0_ref[...]

    ch = GLA_CHUNK
    r_k = lax.broadcasted_iota(jnp.int32, (GLA_HEADS * ch, GLA_KEY_DIM), 0) // ch
    c_k = lax.broadcasted_iota(jnp.int32, (GLA_HEADS * ch, GLA_KEY_DIM), 1) // GLA_DK
    hm_k = r_k == c_k
    r_v = lax.broadcasted_iota(jnp.int32, (GLA_HEADS * ch, GLA_DIM), 0) // ch
    c_v = lax.broadcasted_iota(jnp.int32, (GLA_HEADS * ch, GLA_DIM), 1) // GLA_DV
    hm_v = r_v == c_v
    r_s = lax.broadcasted_iota(jnp.int32, (GLA_DIM, GLA_KEY_DIM), 0) // GLA_DV
    c_s = lax.broadcasted_iota(jnp.int32, (GLA_DIM, GLA_KEY_DIM), 1) // GLA_DK
    hm_s = r_s == c_s
    t_i = lax.broadcasted_iota(jnp.int32, (ch, GLA_HEADS * ch), 0)
    s_i = lax.broadcasted_iota(jnp.int32, (ch, GLA_HEADS * ch), 1) % ch
    mask_f = t_i >= s_i
    mask_b = t_i <= s_i
    trif = trif_ref[...]
    trib = trib_ref[...]

    fwd = dict(qk=qkf_ref, v=vf_ref, la=laf_ref, o=of_ref, tri=trif, last=ch - 1, ref=ch // 2, mask=mask_f, d=0)
    bwd = dict(qk=qkb_ref, v=vb_ref, la=lab_ref, o=ob_ref, tri=trib, last=0, ref=ch - 1 - ch // 2, mask=mask_b, d=1)
    streams = [(b, p) for b in range(nb) for p in (fwd, bwd)]

    bcums = [_dot_m_exact(p["tri"], p["la"][b]) for b, p in streams]
    scaled = []
    for (b, p), bcum in zip(streams, bcums):
        qk = p["qk"][b]
        q, k = qk[:, 0:GLA_KEY_DIM], qk[:, GLA_KEY_DIM:]
        blast = bcum[p["last"]:p["last"] + 1]
        bref = bcum[p["ref"]:p["ref"] + 1]
        qe = (q * jnp.exp(bcum)).astype(BF16)
        qd = (q * jnp.exp(bcum - bref)).astype(BF16)
        kd = k * jnp.exp(bref - bcum)
        kdec = (k * jnp.exp(blast - bcum)).astype(BF16)
        kst = jnp.where(hm_k, jnp.concatenate([kd] * GLA_HEADS, axis=0), 0.0).astype(BF16)
        scaled.append((qe, qd, kdec, kst, jnp.exp(blast)))
    prods = []
    for (b, p), (qe, qd, kdec, kst, _) in zip(streams, scaled):
        v = p["v"][b]
        sc = _dot_nt(qd, kst)
        o_inter = _dot_nt(qe, s_ref[b, p["d"]].astype(BF16))
        kv_t = _dot_tn(v.astype(BF16), kdec)
        prods.append((sc, o_inter, kv_t))
    for (b, p), (_, _, _, _, decay), (sc, o_inter, kv_t) in zip(streams, scaled, prods):
        vbd = jnp.where(hm_v, jnp.concatenate([p["v"][b]] * GLA_HEADS, axis=0), 0.0).astype(BF16)
        p["o"][b] = _dot(jnp.where(p["mask"], sc, 0.0).astype(BF16), vbd) + o_inter
        s_ref[b, p["d"]] = s_ref[b, p["d"]] * decay + jnp.where(hm_s, kv_t, 0.0)

    @pl.when(c == pl.num_programs(0) - 1)
    def _():
        sout_ref[...] = s_ref[...]


def _gla_call(qk, v, la, s0, trif, trib):
    ch = GLA_CHUNK
    if qk.ndim == 4:
        n, b = qk.shape[0], qk.shape[1]
        assert qk.shape[2] == ch
        spec = lambda w, off, rev: pl.BlockSpec(
            (None, b, ch, w), (lambda c: (n - 1 - c, 0, 0, off)) if rev else (lambda c: (c, 0, 0, off)))
    else:
        b = qk.shape[0]
        n = qk.shape[1] // ch
        spec = lambda w, off, rev: pl.BlockSpec(
            (b, ch, w), (lambda c: (0, n - 1 - c, off)) if rev else (lambda c: (0, c, off)))
    kern = functools.partial(_gla_kernel, nb=b)
    o_shape = jax.ShapeDtypeStruct(v.shape, F32)
    return pl.pallas_call(
        kern,
        grid=(n,),
        in_specs=[
            spec(2 * GLA_KEY_DIM, 0, False), spec(GLA_DIM, 0, False), spec(GLA_KEY_DIM, 0, False),
            spec(2 * GLA_KEY_DIM, 0, True), spec(GLA_DIM, 0, True), spec(GLA_KEY_DIM, 1, True),
            _full_spec(s0.shape), _full_spec(trif.shape), _full_spec(trib.shape),
        ],
        out_specs=[spec(GLA_DIM, 0, False), spec(GLA_DIM, 0, True), _full_spec(s0.shape)],
        out_shape=[o_shape, o_shape, jax.ShapeDtypeStruct(s0.shape, F32)],
        scratch_shapes=[pltpu.VMEM(s0.shape, F32)],
        compiler_params=_cparams(("arbitrary",)),
    )(qk, v, la, qk, v, la, s0, trif, trib)


def _stage_e_kernel(x_ref, mod_ref, sgu_ref, yf_ref, yb_ref, s5x_ref, of_ref, ob_ref, g_ref,
                    dskip_ref, wglu_ref, normg_ref, ones_ref, wout_ref, o_ref, *, grid_layout):
    ys = yf_ref[0] + yb_ref[0] + dskip_ref[...] * s5x_ref[0]
    z = _dot(_gelu(ys).astype(BF16), wglu_ref[...])
    s5o = z[:, 0:S5_DIM] * _sigmoid(z[:, S5_DIM:])
    o = _load_tokens(of_ref, grid_layout) + _load_tokens(ob_ref, grid_layout)
    ms = _dot_x_exact(o * o, ones_ref[...]) * (1.0 / GLA_DV)
    g = g_ref[0]
    gl = o * lax.rsqrt(ms + EPS) * normg_ref[...] * (g * _sigmoid(g))
    y = (_dot(sgu_ref[0].astype(BF16), wout_ref[0:SGU_DIM, :])
         + _dot(s5o.astype(BF16), wout_ref[SGU_DIM:SGU_DIM + S5_DIM, :])
         + _dot(gl.astype(BF16), wout_ref[SGU_DIM + S5_DIM:, :]))
    o_ref[0] = x_ref[0] + mod_ref[0, 2:3, :] * y


def _stage_e(xs, mod, sgu, yf_all, yb_all, s5_all, s5_row0, of, ob, g, dskip, wglu, normg, ones_gla, wout, tb):
    b, l, d = xs.shape
    assert s5_row0 % tb == 0
    grid_layout = of.ndim == 4
    tok = lambda w: pl.BlockSpec((1, tb, w), lambda bi, i: (bi, i, 0))
    s5 = pl.BlockSpec((1, tb, S5_DIM), lambda bi, i: (bi, s5_row0 // tb + i, 0))
    if grid_layout:
        assert tb % GRID_W == 0
        gla = pl.BlockSpec((GRID_W, None, tb // GRID_W, GLA_DIM), lambda bi, i: (0, bi, i, 0))
    else:
        gla = tok(GLA_DIM)
    return pl.pallas_call(
        functools.partial(_stage_e_kernel, grid_layout=grid_layout),
        grid=(b, l // tb),
        in_specs=[
            tok(d), pl.BlockSpec((1, 8, d), lambda bi, i: (bi, 0, 0)),
            tok(SGU_DIM), s5, s5, s5, gla, gla, tok(GLA_DIM),
            _full_spec(dskip.shape), _full_spec(wglu.shape), _full_spec(normg.shape),
            _full_spec(ones_gla.shape), _full_spec(wout.shape),
        ],
        out_specs=tok(d),
        out_shape=jax.ShapeDtypeStruct((b, l, d), F32),
        compiler_params=_cparams(("parallel", "parallel")),
    )(xs, mod, sgu, yf_all, yb_all, s5_all, of, ob, g, dskip, wglu, normg, ones_gla, wout)


PEER_TBF = 256
SUBLANES = 8


def _sort_network_16():
    def merge(lo, hi, r):
        step = r * 2
        if step < hi - lo:
            yield from merge(lo, hi, step)
            yield from merge(lo + r, hi, step)
            for i in range(lo + r, hi - r, step):
                yield (i, i + r)
        else:
            yield (lo, lo + r)

    def sort(lo, hi):
        if hi - lo >= 1:
            mid = lo + (hi - lo) // 2
            yield from sort(lo, mid)
            yield from sort(mid + 1, hi)
            yield from merge(lo, hi, 1)

    return tuple(sort(0, PEER_TOPK - 1))


SORT16 = _sort_network_16()
BITONIC16 = tuple((k, k + s) for s in (8, 4, 2, 1) for k in range(PEER_TOPK) if not k & s)


def _compare_exchange(xs, pairs):
    xs = list(xs)
    for i, j in pairs:
        hi = jnp.maximum(xs[i], xs[j])
        lo = jnp.minimum(xs[i], xs[j])
        xs[i], xs[j] = hi, lo
    return xs


def _merge_sublanes(xs):
    for shift in (4, 6, 7):
        rolled = [pltpu.roll(x, shift, 0) for x in xs]
        xs = [jnp.maximum(xs[k], rolled[PEER_TOPK - 1 - k]) for k in range(PEER_TOPK)]
        xs = _compare_exchange(xs, BITONIC16)
    return xs


def _dup_bf16_words(x):
    bits = pltpu.bitcast(x.astype(BF16).astype(F32), jnp.int32)
    return bits | lax.shift_right_logical(bits, 16)


def _stage_f_kernel(x_ref, mod_ref, g2_ref, wqt_ref, keys_ref,
                    ht_ref, e1w_ref, n1w_ref, e2_ref, r2_ref, qt_ref, v1_ref, v2_ref, *, tb):
    x = x_ref[0]
    ms = jnp.mean(x * x, axis=-1, keepdims=True)
    xn = x * lax.rsqrt(ms + EPS) * g2_ref[...]
    h = xn * (1.0 + mod_ref[0, 4:5, :]) + mod_ref[0, 3:4, :]
    ht = h.T.astype(BF16)
    ht_ref[...] = pltpu.bitcast(ht, jnp.int32)
    qt_ref[...] = _dot(wqt_ref[...], ht)
    k_top = PEER_TOPK

    def tiles(s):
        return [s[SUBLANES * k:SUBLANES * (k + 1)] for k in range(PEER_NKEYS // SUBLANES)]

    def head_body(hh, carry):
        for tc in range(tb // LANE):
            tcol = slice(tc * LANE, (tc + 1) * LANE)
            r1 = pl.multiple_of(hh * (2 * PEER_HALF), 2 * PEER_HALF)
            q1 = qt_ref[pl.ds(r1, PEER_HALF), tcol].astype(BF16)
            q2 = qt_ref[pl.ds(r1 + PEER_HALF, PEER_HALF), tcol].astype(BF16)
            s1 = _dot(keys_ref[hh], q1)
            s2 = _dot(keys_ref[PEER_HEADS + hh], q2)
            for s, v_ref in ((s1, v1_ref), (s2, v2_ref)):
                top = _merge_sublanes(_compare_exchange(tiles(s), SORT16))
                for k in range(k_top):
                    v_ref[k:k + 1, tcol] = top[k][0:1]
            v1row = lambda a: v1_ref[a:a + 1, tcol]
            v2row = lambda b: v2_ref[b:b + 1, tcol]
            v1lo = v1_ref[0:SUBLANES, tcol]
            v2lo = v2_ref[0:SUBLANES, tcol]
            cand = [v1lo + v2row(b) for b in range(k_top)]
            tail = [v1row(a) + v2lo for a in range(SUBLANES, k_top)]
            for k in range(SUBLANES, k_top):
                cand[k] = jnp.maximum(cand[k], tail[k_top - 1 - k])
            best = _merge_sublanes(_compare_exchange(cand, BITONIC16))
            theta = best[k_top - 1][0:1]
            cmax = best[0][0:1]
            zsum = jnp.zeros((1, LANE), F32)
            for k in range(k_top):
                zsum = zsum + jnp.exp(best[k][0:1] - cmax)
            rz = 1.0 / zsum
            n_top = jnp.zeros((1, LANE), F32)
            for b in range(k_top):
                n_top = jnp.where(v1row(0) + v2row(b) >= theta, float(b + 1), n_top)
            n1 = jnp.zeros(s1.shape, F32)
            for b in range(SUBLANES):
                n1 = jnp.where(s1 + v2row(b) >= theta, float(b + 1), n1)
            n1 = jnp.where(s1 >= v1row(0), n_top, n1)
            r2 = jnp.zeros(s2.shape, F32)
            for b in range(k_top):
                r2 = jnp.where(v2row(b) > s2, float(b + 1), r2)
            e1 = jnp.where(s1 >= v1row(k_top - 1), jnp.exp(s1 - v1row(0)), 0.0) * rz
            e2 = jnp.where(s2 >= v2row(k_top - 1), jnp.exp(s2 - v2row(0)), 0.0)
            e1w_ref[tc, hh] = _dup_bf16_words(e1)
            n1w_ref[tc, hh] = _dup_bf16_words(n1)
            e2_ref[tc, hh] = pltpu.bitcast(e2.astype(BF16), jnp.int32)
            r2_ref[tc, hh] = pltpu.bitcast(r2.astype(BF16), jnp.int32)
        return carry

    lax.fori_loop(0, PEER_HEADS, head_body, 0)


def _stage_f(xs, mod, g2, wqt, keys, tb):
    b, l, d = xs.shape
    nblk = l // tb
    ntok = b * l
    nch = ntok // LANE
    kern = functools.partial(_stage_f_kernel, tb=tb)
    row_spec = pl.BlockSpec((tb // LANE, PEER_HEADS, PEER_NKEYS, LANE), lambda bi, i: (bi * nblk + i, 0, 0, 0))
    pair_spec = pl.BlockSpec((tb // LANE, PEER_HEADS, PEER_NKEYS // 2, LANE), lambda bi, i: (bi * nblk + i, 0, 0, 0))
    desc_shape = lambda rows: jax.ShapeDtypeStruct((nch, PEER_HEADS, rows, LANE), jnp.int32)
    return pl.pallas_call(
        kern,
        grid=(b, nblk),
        in_specs=[
            pl.BlockSpec((1, tb, d), lambda bi, i: (bi, i, 0)),
            pl.BlockSpec((1, 8, d), lambda bi, i: (bi, 0, 0)),
            _full_spec(g2.shape), _full_spec(wqt.shape), _full_spec(keys.shape),
        ],
        out_specs=[pl.BlockSpec((d // 2, tb), lambda bi, i: (0, bi * nblk + i)),
                   row_spec, row_spec, pair_spec, pair_spec],
        out_shape=[jax.ShapeDtypeStruct((d // 2, ntok), jnp.int32), desc_shape(PEER_NKEYS), desc_shape(PEER_NKEYS),
                   desc_shape(PEER_NKEYS // 2), desc_shape(PEER_NKEYS // 2)],
        scratch_shapes=[
            pltpu.VMEM((PEER_HEADS * 2 * PEER_HALF, tb), F32),
            pltpu.VMEM((PEER_TOPK, tb), F32),
            pltpu.VMEM((PEER_TOPK, tb), F32),
        ],
        compiler_params=_cparams(("parallel", "parallel")),
    )(xs, mod, g2, wqt, keys)


PEER_TBG = 1024
PEER_TE = 1024
PEER_I1_PER_TILE = PEER_TE // PEER_NKEYS
PEER_N_TILES = PEER_NKEYS * PEER_NKEYS // PEER_TE
PEER_MXU_COLS = 256
PEER_G_FLAGS = None


def _stage_g_kernel(htw_ref, e1w_ref, n1w_ref, e2_ref, r2_ref, uw_ref, vtw_ref, x_ref, mod_ref,
                    o_ref, acc_ref, ata_ref, atb_ref, p_ref, *, tb):
    s = pl.program_id(1)

    @pl.when(s == 0)
    def _():
        acc_ref[...] = jnp.zeros_like(acc_ref)
        atb_ref[...] = jnp.zeros_like(atb_ref)

    def step(at_cur_ref, at_next_ref):
        per_grp = PEER_MXU_COLS // LANE
        for grp in range(tb // PEER_MXU_COLS):
            cols = slice(grp * PEER_MXU_COLS, (grp + 1) * PEER_MXU_COLS)
            for tcl in range(per_grp):
                tc = grp * per_grp + tcl
                tcol = slice(tc * LANE, (tc + 1) * LANE)
                for i1l in range(PEER_I1_PER_TILE):
                    rows = slice(i1l * PEER_NKEYS, (i1l + 1) * PEER_NKEYS)
                    gate = jnp.zeros((PEER_NKEYS, LANE), BF16)
                    for hh in range(PEER_HEADS):
                        e1row = e1w_ref[tc, hh, i1l:i1l + 1, :]
                        n1row = n1w_ref[tc, hh, i1l:i1l + 1, :]
                        e1 = pltpu.bitcast(jnp.broadcast_to(e1row, (PEER_NKEYS // 2, LANE)), BF16)
                        n1 = pltpu.bitcast(jnp.broadcast_to(n1row, (PEER_NKEYS // 2, LANE)), BF16)
                        r2 = pltpu.bitcast(r2_ref[tc, hh], BF16)
                        e2 = pltpu.bitcast(e2_ref[tc, hh], BF16)
                        gate = gate + e2 * jnp.where(r2 < n1, e1, 0.0)
                    p_ref[rows, tcol] = gate * _gelu(at_cur_ref[rows, tcol])
            vt = pltpu.bitcast(vtw_ref[...], BF16)
            u = pltpu.bitcast(uw_ref[...], BF16)
            ht = pltpu.bitcast(htw_ref[:, cols], BF16)
            acc_ref[:, cols] += _dot(vt, p_ref[:, cols])
            at_next_ref[:, cols] = _dot(u, ht).astype(BF16)

    @pl.when(s % 2 == 0)
    def _():
        step(atb_ref, ata_ref)

    @pl.when(s % 2 == 1)
    def _():
        step(ata_ref, atb_ref)

    @pl.when(s == pl.num_programs(1) - 1)
    def _():
        o_ref[...] = x_ref[...] + mod_ref[0, 5:6, :] * acc_ref[...].T


def _stage_g(htw, e1w, n1w, e2, r2, uw, vtw, xflat, mod, tokens_per_batch, tb):
    ntok = htw.shape[1]
    d = 2 * htw.shape[0]
    assert 2 * uw.shape[0] == PEER_N_TILES * PEER_TE
    blocks_per_batch = tokens_per_batch // tb
    kern = functools.partial(_stage_g_kernel, tb=tb)
    last = PEER_N_TILES - 1
    desc_spec = pl.BlockSpec((tb // LANE, PEER_HEADS, PEER_NKEYS // 2, LANE), lambda j, i: (j, 0, 0, 0))
    row_spec = pl.BlockSpec((tb // LANE, PEER_HEADS, PEER_I1_PER_TILE, LANE),
                            lambda j, i: (j, 0, jnp.maximum(i - 1, 0), 0))
    return pl.pallas_call(
        kern,
        grid=(ntok // tb, PEER_N_TILES + 1),
        in_specs=[
            pl.BlockSpec((d // 2, tb), lambda j, i: (0, j)),
            row_spec, row_spec, desc_spec, desc_spec,
            pl.BlockSpec((PEER_TE // 2, d), lambda j, i: (jnp.minimum(i, last), 0)),
            pl.BlockSpec((d // 2, PEER_TE), lambda j, i: (0, jnp.maximum(i - 1, 0))),
            pl.BlockSpec((tb, d), lambda j, i: (j, 0)),
            pl.BlockSpec((1, 8, d), lambda j, i: (j // blocks_per_batch, 0, 0)),
        ],
        out_specs=pl.BlockSpec((tb, d), lambda j, i: (j, 0)),
        out_shape=jax.ShapeDtypeStruct((ntok, d), F32),
        scratch_shapes=[
            pltpu.VMEM((d, tb), F32),
            pltpu.VMEM((PEER_TE, tb), BF16),
            pltpu.VMEM((PEER_TE, tb), BF16),
            pltpu.VMEM((PEER_TE, tb), BF16),
        ],
        compiler_params=_cparams(("parallel", "arbitrary"), PEER_G_FLAGS),
    )(htw, e1w, n1w, e2, r2, uw, vtw, xflat, mod)


def _final_norm_kernel(x_ref, g_ref, o_ref):
    x = x_ref[...]
    ms = jnp.mean(x * x, axis=-1, keepdims=True)
    o_ref[...] = x * lax.rsqrt(ms + EPS) * g_ref[...]


def _final_norm(xflat, g, tb=512):
    n, d = xflat.shape
    return pl.pallas_call(
        _final_norm_kernel,
        grid=(n // tb,),
        in_specs=[pl.BlockSpec((tb, d), lambda i: (i, 0)), _full_spec(g.shape)],
        out_specs=pl.BlockSpec((tb, d), lambda i: (i, 0)),
        out_shape=jax.ShapeDtypeStruct((n, d), F32),
        compiler_params=_cparams(("parallel",)),
    )(xflat, g)


def _pack_kernel(x_ref, o_ref, *, transpose):
    x = x_ref[...]
    if transpose:
        x = x.T
    o_ref[...] = pltpu.bitcast(x.astype(BF16), jnp.int32)


def _pack_row_pairs(x, layer, transpose=False, tile=1024):
    _, r, c = x.shape
    if transpose:
        out_shape, out_spec = (c // 2, r), pl.BlockSpec((c // 2, tile), lambda i: (0, i))
    else:
        out_shape, out_spec = (r // 2, c), pl.BlockSpec((tile // 2, c), lambda i: (i, 0))
    return pl.pallas_call(
        functools.partial(_pack_kernel, transpose=transpose),
        grid=(r // tile,),
        in_specs=[pl.BlockSpec((None, tile, c), lambda i: (layer, i, 0))],
        out_specs=out_spec,
        out_shape=jax.ShapeDtypeStruct(out_shape, jnp.int32),
        compiler_params=_cparams(("parallel",)),
    )(x)


def _block_ones(n, blk):
    idx = np.arange(n) // blk
    return jnp.asarray((idx[:, None] == idx[None, :]).astype(np.float32), dtype=BF16)


def _s5_discretise(lam_re, lam_im, b_re, b_im, log_step):
    lam_re = jnp.minimum(lam_re.astype(F32), -1e-4)
    lam_im = lam_im.astype(F32)
    dt = jnp.exp(log_step.astype(F32))[:, None]
    mag = jnp.exp(lam_re * dt)
    a_re = mag * jnp.cos(lam_im * dt)
    a_im = mag * jnp.sin(lam_im * dt)
    den = lam_re * lam_re + lam_im * lam_im
    f_re = ((a_re - 1.0) * lam_re + a_im * lam_im) / den
    f_im = (a_im * lam_re - (a_re - 1.0) * lam_im) / den
    b_re = b_re.astype(F32)
    b_im = b_im.astype(F32)
    bb_re = f_re[..., None] * b_re - f_im[..., None] * b_im
    bb_im = f_re[..., None] * b_im + f_im[..., None] * b_re
    return a_re, a_im, bb_re, bb_im


def _group_block_diag(t):
    g, r, c = t.shape
    eye = jnp.eye(g, dtype=t.dtype)
    return (t[:, :, None, :] * eye[:, None, :, None]).reshape(g * r, g * c)


def _s5_params(lam_re, lam_im, b_re, b_im, c_re, c_im, log_step, nseq):
    b_rows, c_cols, ars, ais = [], [], [], []
    for d in range(2):
        a_re, a_im, bb_re, bb_im = _s5_discretise(lam_re[d], lam_im[d], b_re[d], b_im[d], log_step[d])
        bm = jnp.concatenate([_group_block_diag(jnp.swapaxes(bb_re, 1, 2)),
                              _group_block_diag(jnp.swapaxes(bb_im, 1, 2))], axis=1)
        b_rows.append(bm)
        cm = jnp.concatenate([_group_block_diag(jnp.swapaxes(c_re[d].astype(F32), 1, 2)),
                              -_group_block_diag(jnp.swapaxes(c_im[d].astype(F32), 1, 2))], axis=0)
        c_cols.append(cm)
        ars.append(jnp.broadcast_to(a_re.reshape(1, S5_LANES), (nseq, S5_LANES)))
        ais.append(jnp.broadcast_to(a_im.reshape(1, S5_LANES), (nseq, S5_LANES)))
    b2 = jnp.concatenate(b_rows, axis=0).astype(BF16)
    c2 = jnp.concatenate(c_cols, axis=1).astype(BF16)
    return b2, jnp.concatenate(ars, axis=0), jnp.concatenate(ais, axis=0), c2


def kernel(x, c, ctx, c_ctx, w_mod, b_mod, norm1_g, norm2_g, w_in, w_out, sgu_w, sgu_b, s5_lambda_re, s5_lambda_im, s5_b_re, s5_b_im, s5_c_re, s5_c_im, s5_log_step, s5_d, s5_w_glu, gla_w_gate, gla_b_gate, gla_norm_g, peer_w_query, peer_sub_keys, peer_expert_u, peer_expert_v, final_norm_g):
    nb, seq, d = x.shape
    c_len = ctx.shape[1]
    depth = w_mod.shape[0]

    cc = jnp.concatenate([c, c_ctx[None, :], jnp.zeros((8 - nb - 1, d), F32)], axis=0)
    mods = _mod_call(cc, w_mod, b_mod)

    ones_sgu = _block_ones(SGU_DIM, SGU_HEAD_DIM)
    ones_gla = _block_ones(GLA_DIM, GLA_DV)
    tri_np = np.tril(np.ones((GLA_CHUNK, GLA_CHUNK), np.float32))
    trif = jnp.asarray(tri_np, dtype=BF16)
    trib = jnp.asarray(tri_np.T, dtype=BF16)
    s_zero = jnp.zeros((nb, 2, GLA_DIM, GLA_KEY_DIM), F32)

    xl, xc = x, ctx
    for l in range(depth):
        ctx_out = l < depth - 1
        m6 = mods[l].reshape(8, N_MOD, d)
        mod_l = jnp.pad(m6[:nb], ((0, 0), (0, 2), (0, 0)))
        mod_c = jnp.broadcast_to(jnp.pad(m6[nb], ((0, 2), (0, 0)))[None], (nb, 8, d))

        win = jnp.pad(w_in[l], ((0, 0), (0, IN_PAD - IN_WIDTH))).astype(BF16)
        sguw = sgu_w[l].astype(BF16)
        sgub = jnp.repeat(jnp.swapaxes(sgu_b[l], 0, 1), SGU_HEAD_DIM, axis=1)
        gw = jnp.zeros((LANE, 2 * GLA_KEY_DIM), F32)
        gw = gw.at[0:GLA_RANK, 0:GLA_KEY_DIM].set(gla_w_gate[l, 0])
        gw = gw.at[GLA_RANK:2 * GLA_RANK, GLA_KEY_DIM:].set(gla_w_gate[l, 1]).astype(BF16)
        gb = gla_b_gate[l].reshape(1, 2 * GLA_KEY_DIM)
        g1 = norm1_g[l].reshape(1, d)

        s5_all = jnp.zeros((nb, seq + c_len, S5_DIM), F32)
        sgu_l, s5_all, qk_l, v_l, g_l, la_l = _stage_a(xl, mod_l, g1, win, sguw, sgub, ones_sgu, gw, gb,
                                                       s5_all, 0, tb=TB_LATENT, grid_layout=True)
        sgu_c, s5_all, qk_c, v_c, g_c, la_c = _stage_a(xc, mod_c, g1, win, sguw, sgub, ones_sgu, gw, gb,
                                                       s5_all, seq, tb=TB_CTX, grid_layout=False)

        b2, ar, ai, c2 = _s5_params(s5_lambda_re[l], s5_lambda_im[l], s5_b_re[l], s5_b_im[l],
                                    s5_c_re[l], s5_c_im[l], s5_log_step[l], nb)
        yf_all, yb_all = _s5_call(s5_all, seq, b2, ar, ai, c2)

        of_c, ob_c, s_ctx = _gla_call(qk_c, v_c, la_c, s_zero, trif, trib)
        of_l, ob_l, _ = _gla_call(qk_l, v_l, la_l, s_ctx, trif, trib)

        dskip = s5_d[l].reshape(1, S5_DIM)
        wglu = s5_w_glu[l].astype(BF16)
        normg = gla_norm_g[l].reshape(1, GLA_DIM)
        wout = w_out[l].astype(BF16)
        g2 = norm2_g[l].reshape(1, d)
        wqt = jnp.swapaxes(peer_w_query[l], 0, 1).astype(BF16)
        keys = peer_sub_keys[l].reshape(2 * PEER_HEADS, PEER_NKEYS, PEER_HALF).astype(BF16)
        u_bf = _pack_row_pairs(peer_expert_u, l)
        vt_bf = _pack_row_pairs(peer_expert_v, l, transpose=True)

        xl = _stage_e(xl, mod_l, sgu_l, yf_all, yb_all, s5_all, 0, of_l, ob_l, g_l,
                      dskip, wglu, normg, ones_gla, wout, tb=TB_LATENT)
        desc = _stage_f(xl, mod_l, g2, wqt, keys, tb=PEER_TBF)
        xl = _stage_g(*desc, u_bf, vt_bf, xl.reshape(nb * seq, d), mod_l, seq, PEER_TBG).reshape(nb, seq, d)

        if ctx_out:
            xc = _stage_e(xc, mod_c, sgu_c, yf_all, yb_all, s5_all, seq, of_c, ob_c, g_c,
                          dskip, wglu, normg, ones_gla, wout, tb=TB_CTX)
            desc = _stage_f(xc, mod_c, g2, wqt, keys, tb=PEER_TBF)
            xc = _stage_g(*desc, u_bf, vt_bf, xc.reshape(nb * c_len, d), mod_c, c_len,
                          min(PEER_TBG, c_len)).reshape(nb, c_len, d)

    return _final_norm(xl.reshape(nb * seq, d), final_norm_g.reshape(1, d)).reshape(nb, seq, d)
```

```python
import functools
import math

import numpy as np
import jax
import jax.numpy as jnp
from jax import lax
from jax.experimental import pallas as pl
from jax.experimental.pallas import tpu as pltpu

F32 = jnp.float32
BF16 = jnp.bfloat16

EPS = 1e-6
N_MOD = 6
GRID_W = 64

SGU_DIM = 256
SGU_HEADS = 4
SGU_HEAD_DIM = 64
SGU_CHUNK = 128

S5_DIM = 256
S5_GROUP = 16
S5_GROUPS = 16
S5_STATE = 64
S5_LANES = S5_GROUPS * S5_STATE

GLA_DIM = 512
GLA_HEADS = 8
GLA_DV = 64
GLA_DK = 32
GLA_KEY_DIM = 256
GLA_RANK = 16
GLA_GATE_TEMP = 16.0
GLA_CHUNK = 64

PEER_HEADS = 8
PEER_NKEYS = 128
PEER_HALF = 128
PEER_TOPK = 16

IN_WIDTH = 2336
IN_PAD = 2432
LANE = 128

VMEM_LIMIT = 56 * 1024 * 1024
TB_LATENT = 512
TB_CTX = 256

NEG_INF = float("-inf")
POS_INF = float("inf")


def _cparams(sem, flags=None):
    return pltpu.CompilerParams(dimension_semantics=sem, vmem_limit_bytes=VMEM_LIMIT, flags=flags)


def _gelu(x):
    c = math.sqrt(2.0 / math.pi)
    return 0.5 * x * (1.0 + jnp.tanh(c * (x + 0.044715 * (x * x * x))))


def _sigmoid(x):
    return 1.0 / (1.0 + jnp.exp(-x))


def _dot(a, b):
    return jnp.dot(a, b, preferred_element_type=F32)


def _dot_nt(a, b):
    return lax.dot_general(a, b, (((1,), (1,)), ((), ())), preferred_element_type=F32)


def _dot_tn(a, b):
    return lax.dot_general(a, b, (((0,), (0,)), ((), ())), preferred_element_type=F32)


def _split3(x):
    hi = x.astype(BF16)
    r = x - hi.astype(F32)
    mid = r.astype(BF16)
    lo = (r - mid.astype(F32)).astype(BF16)
    return hi, mid, lo


def _dot_x_exact(x, m):
    hi, mid, lo = _split3(x)
    return _dot(hi, m) + _dot(mid, m) + _dot(lo, m)


def _dot_m_exact(m, x):
    hi, mid, lo = _split3(x)
    return _dot(m, hi) + _dot(m, mid) + _dot(m, lo)


def _full_spec(shape):
    nd = len(shape)
    return pl.BlockSpec(shape, lambda *_: (0,) * nd)


MOD_TILE = 512


def _mod_kernel(c_ref, w_ref, b_ref, o_ref):
    c = c_ref[...]
    a = c * _sigmoid(c)
    o_ref[0] = jnp.dot(a, w_ref[0], preferred_element_type=F32,
                       precision=lax.Precision.HIGHEST) + b_ref[0]


def _mod_call(cc, w_mod, b_mod):
    depth, d, nd = w_mod.shape
    rows = cc.shape[0]
    return pl.pallas_call(
        _mod_kernel,
        grid=(depth, nd // MOD_TILE),
        in_specs=[
            pl.BlockSpec((rows, d), lambda l, j: (0, 0)),
            pl.BlockSpec((1, d, MOD_TILE), lambda l, j: (l, 0, j)),
            pl.BlockSpec((1, 1, MOD_TILE), lambda l, j: (l, 0, j)),
        ],
        out_specs=pl.BlockSpec((1, rows, MOD_TILE), lambda l, j: (l, 0, j)),
        out_shape=jax.ShapeDtypeStruct((depth, rows, nd), F32),
        compiler_params=_cparams(("parallel", "parallel")),
    )(cc, w_mod, b_mod.reshape(depth, 1, nd))


def _store_tokens(ref, val, grid_layout):
    if grid_layout:
        for r in range(val.shape[0] // GRID_W):
            ref[:, r, :] = val[r * GRID_W:(r + 1) * GRID_W]
    else:
        ref[0] = val


def _load_tokens(ref, grid_layout):
    if grid_layout:
        return jnp.concatenate([ref[:, r, :] for r in range(ref.shape[1])], axis=0)
    return ref[0]


def _stage_a_kernel(x_ref, mod_ref, g1_ref, win_ref, sguw_ref, sgub_ref, ones_ref, gw_ref, gb_ref, s5_in_ref,
                    sgu_ref, s5x_ref, qk_ref, v_ref, g_ref, la_ref, *, tb, grid_layout):
    del s5_in_ref
    x = x_ref[0]
    ms = jnp.mean(x * x, axis=-1, keepdims=True)
    xn = x * lax.rsqrt(ms + EPS) * g1_ref[...]
    h = xn * (1.0 + mod_ref[0, 1:2, :]) + mod_ref[0, 0:1, :]
    cols = _dot(h.astype(BF16), win_ref[...])

    u = _gelu(cols[:, 0:SGU_DIM])
    v = _gelu(cols[:, SGU_DIM:2 * SGU_DIM])
    msq = _dot_x_exact(v * v, ones_ref[...]) * (1.0 / SGU_HEAD_DIM)
    vn = (v * lax.rsqrt(msq + EPS)).astype(BF16)
    head_of_lane = lax.broadcasted_iota(jnp.int32, (SGU_CHUNK, SGU_DIM), 1) // SGU_HEAD_DIM
    for ci in range(tb // SGU_CHUNK):
        rows = slice(ci * SGU_CHUNK, (ci + 1) * SGU_CHUNK)
        vc = vn[rows]
        mixed = sgub_ref[...]
        for hh in range(SGU_HEADS):
            mixed = mixed + jnp.where(head_of_lane == hh, _dot(sguw_ref[hh], vc), 0.0)
        sgu_ref[0, rows, :] = u[rows] * mixed

    s5x_ref[0] = cols[:, 512:768]
    q = cols[:, 768:1024] * (GLA_DK ** -0.5)
    _store_tokens(qk_ref, jnp.concatenate([q, cols[:, 1024:1280]], axis=-1), grid_layout)
    _store_tokens(v_ref, cols[:, 1280:1792], grid_layout)
    g_ref[0] = cols[:, 1792:2304]

    z = cols[:, 2304:2432].astype(BF16)
    za = _dot(z, gw_ref[...]) + gb_ref[...]
    log_sig = jnp.minimum(za, 0.0) - jnp.log1p(jnp.exp(-jnp.abs(za)))
    _store_tokens(la_ref, log_sig * (1.0 / GLA_GATE_TEMP), grid_layout)


def _stage_a(xs, mod, g1, win, sguw, sgub, ones_sgu, gw, gb, s5_all, s5_row0, tb, grid_layout):
    b, l, d = xs.shape
    assert s5_row0 % tb == 0
    kern = functools.partial(_stage_a_kernel, tb=tb, grid_layout=grid_layout)
    tok = lambda w: pl.BlockSpec((1, tb, w), lambda bi, i: (bi, i, 0))
    if grid_layout:
        assert tb % GRID_W == 0 and l % GRID_W == 0
        gla = lambda w: pl.BlockSpec((GRID_W, None, tb // GRID_W, w), lambda bi, i: (0, bi, i, 0))
        gla_shape = lambda w: jax.ShapeDtypeStruct((GRID_W, b, l // GRID_W, w), F32)
    else:
        gla = tok
        gla_shape = lambda w: jax.ShapeDtypeStruct((b, l, w), F32)
    nat_shape = lambda w: jax.ShapeDtypeStruct((b, l, w), F32)
    return pl.pallas_call(
        kern,
        grid=(b, l // tb),
        in_specs=[
            tok(d),
            pl.BlockSpec((1, 8, d), lambda bi, i: (bi, 0, 0)),
            _full_spec(g1.shape), _full_spec(win.shape), _full_spec(sguw.shape), _full_spec(sgub.shape),
            _full_spec(ones_sgu.shape), _full_spec(gw.shape), _full_spec(gb.shape),
            pl.BlockSpec(memory_space=pl.ANY),
        ],
        out_specs=[tok(SGU_DIM),
                   pl.BlockSpec((1, tb, S5_DIM), lambda bi, i: (bi, s5_row0 // tb + i, 0)),
                   gla(2 * GLA_KEY_DIM), gla(GLA_DIM), tok(GLA_DIM), gla(2 * GLA_KEY_DIM)],
        out_shape=[nat_shape(SGU_DIM), jax.ShapeDtypeStruct(s5_all.shape, F32),
                   gla_shape(2 * GLA_KEY_DIM), gla_shape(GLA_DIM), nat_shape(GLA_DIM), gla_shape(2 * GLA_KEY_DIM)],
        input_output_aliases={9: 1},
        compiler_params=_cparams(("parallel", "parallel")),
    )(xs, mod, g1, win, sguw, sgub, ones_sgu, gw, gb, s5_all)


S5_TC = 128


def _s5_kernel(uf_ref, ub_ref, perm_ref, permt_ref, b2_ref, ar_ref, ai_ref, c2_ref, yf_ref, yb_ref,
               h_ref, buf_ref, *, nseq):
    tc = S5_TC
    rows = tc * 2 * nseq

    @pl.when(pl.program_id(0) == 0)
    def _():
        h_ref[...] = jnp.zeros_like(h_ref)

    x = jnp.concatenate([uf_ref[b] for b in range(nseq)] + [ub_ref[b] for b in range(nseq)], axis=0)
    u_tm = _dot(perm_ref[...], x.astype(BF16))
    fwd_row = lax.broadcasted_iota(jnp.int32, (rows, S5_DIM), 0) % (2 * nseq) < nseq
    u = jnp.concatenate([jnp.where(fwd_row, u_tm, 0.0), jnp.where(fwd_row, 0.0, u_tm)], axis=-1).astype(BF16)
    buf_ref[...] = _dot(u, b2_ref[...]).reshape(tc, 2 * nseq, 2 * S5_LANES)
    ar = ar_ref[...]
    ai = ai_ref[...]

    def step(t, carry):
        hr, hi = carry
        bur = buf_ref[t, :, 0:S5_LANES]
        bui = buf_ref[t, :, S5_LANES:]
        nhr = ar * hr - ai * hi + bur
        nhi = ar * hi + ai * hr + bui
        buf_ref[t, :, 0:S5_LANES] = nhr
        buf_ref[t, :, S5_LANES:] = nhi
        return nhr, nhi

    hr, hi = lax.fori_loop(0, tc, step, (h_ref[:, 0:S5_LANES], h_ref[:, S5_LANES:]))
    h_ref[:, 0:S5_LANES] = hr
    h_ref[:, S5_LANES:] = hi

    hs = buf_ref[...].reshape(rows, 2 * S5_LANES).astype(BF16)
    y2 = _dot(hs, c2_ref[...])
    y = jnp.where(fwd_row, y2[:, 0:S5_DIM], y2[:, S5_DIM:])
    y_hi = y.astype(BF16)
    y_lo = (y - y_hi.astype(F32)).astype(BF16)
    y_nat = _dot(permt_ref[...], y_hi) + _dot(permt_ref[...], y_lo)
    for b in range(nseq):
        yf_ref[b] = y_nat[b * tc:(b + 1) * tc]
        yb_ref[b] = y_nat[(nseq + b) * tc:(nseq + b + 1) * tc]


def _s5_permutation(nseq):
    tc = S5_TC
    p = np.zeros((tc * 2 * nseq, tc * 2 * nseq), np.float32)
    for q in range(2 * nseq):
        for t in range(tc):
            p[t * 2 * nseq + q, q * tc + (t if q < nseq else tc - 1 - t)] = 1.0
    return p


def _s5_call(s5_all, n_lat, b2, ar, ai, c2):
    nseq, t, _ = s5_all.shape
    n = t // S5_TC
    n_l = n_lat // S5_TC
    n_c = n - n_l
    perm = _s5_permutation(nseq)
    kern = functools.partial(_s5_kernel, nseq=nseq)
    fwd_map = lambda s: (0, jnp.where(s < n_c, n_l + s, s - n_c), 0)
    bwd_map = lambda s: (0, n - 1 - s, 0)
    blk = (nseq, S5_TC, S5_DIM)
    rows = 2 * nseq
    return pl.pallas_call(
        kern,
        grid=(n,),
        in_specs=[
            pl.BlockSpec(blk, fwd_map), pl.BlockSpec(blk, bwd_map),
            _full_spec(perm.shape), _full_spec(perm.shape),
            _full_spec(b2.shape), _full_spec(ar.shape), _full_spec(ai.shape), _full_spec(c2.shape),
        ],
        out_specs=[pl.BlockSpec(blk, fwd_map), pl.BlockSpec(blk, bwd_map)],
        out_shape=[jax.ShapeDtypeStruct(s5_all.shape, F32), jax.ShapeDtypeStruct(s5_all.shape, F32)],
        scratch_shapes=[
            pltpu.VMEM((rows, 2 * S5_LANES), F32),
            pltpu.VMEM((S5_TC, rows, 2 * S5_LANES), F32),
        ],
        compiler_params=_cparams(("arbitrary",)),
    )(s5_all, s5_all, jnp.asarray(perm, dtype=BF16), jnp.asarray(perm.T, dtype=BF16), b2, ar, ai, c2)


def _gla_kernel(qkf_ref, vf_ref, laf_ref, qkb_ref, vb_ref, lab_ref, s0_ref, trif_ref, trib_ref,
                of_ref, ob_ref, sout_ref, s_ref, *, nb):
    c = pl.program_id(0)

    @pl.when(c == 0)
    def _():
        s_ref[...] = s0_ref[...]

    ch = GLA_CHUNK
    r_k = lax.broadcasted_iota(jnp.int32, (GLA_HEADS * ch, GLA_KEY_DIM), 0) // ch
    c_k = lax.broadcasted_iota(jnp.int32, (GLA_HEADS * ch, GLA_KEY_DIM), 1) // GLA_DK
    hm_k = r_k == c_k
    r_v = lax.broadcasted_iota(jnp.int32, (GLA_HEADS * ch, GLA_DIM), 0) // ch
    c_v = lax.broadcasted_iota(jnp.int32, (GLA_HEADS * ch, GLA_DIM), 1) // GLA_DV
    hm_v = r_v == c_v
    r_s = lax.broadcasted_iota(jnp.int32, (GLA_DIM, GLA_KEY_DIM), 0) // GLA_DV
    c_s = lax.broadcasted_iota(jnp.int32, (GLA_DIM, GLA_KEY_DIM), 1) // GLA_DK
    hm_s = r_s == c_s
    t_i = lax.broadcasted_iota(jnp.int32, (ch, GLA_HEADS * ch), 0)
    s_i = lax.broadcasted_iota(jnp.int32, (ch, GLA_HEADS * ch), 1) % ch
    mask_f = t_i >= s_i
    mask_b = t_i <= s_i
    trif = trif_ref[...]
    trib = trib_ref[...]

    fwd = dict(qk=qkf_ref, v=vf_ref, la=laf_ref, o=of_ref, tri=trif, last=ch - 1, ref=ch // 2, mask=mask_f, d=0)
    bwd = dict(qk=qkb_ref, v=vb_ref, la=lab_ref, o=ob_ref, tri=trib, last=0, ref=ch - 1 - ch // 2, mask=mask_b, d=1)
    streams = [(b, p) for b in range(nb) for p in (fwd, bwd)]

    bcums = [_dot_m_exact(p["tri"], p["la"][b]) for b, p in streams]
    scaled = []
    for (b, p), bcum in zip(streams, bcums):
        qk = p["qk"][b]
        q, k = qk[:, 0:GLA_KEY_DIM], qk[:, GLA_KEY_DIM:]
        blast = bcum[p["last"]:p["last"] + 1]
        bref = bcum[p["ref"]:p["ref"] + 1]
        qe = (q * jnp.exp(bcum)).astype(BF16)
        qd = (q * jnp.exp(bcum - bref)).astype(BF16)
        kd = k * jnp.exp(bref - bcum)
        kdec = (k * jnp.exp(blast - bcum)).astype(BF16)
        kst = jnp.where(hm_k, jnp.concatenate([kd] * GLA_HEADS, axis=0), 0.0).astype(BF16)
        scaled.append((qe, qd, kdec, kst, jnp.exp(blast)))
    prods = []
    for (b, p), (qe, qd, kdec, kst, _) in zip(streams, scaled):
        v = p["v"][b]
        sc = _dot_nt(qd, kst)
        o_inter = _dot_nt(qe, s_ref[b, p["d"]].astype(BF16))
        kv_t = _dot_tn(v.astype(BF16), kdec)
        prods.append((sc, o_inter, kv_t))
    for (b, p), (_, _, _, _, decay), (sc, o_inter, kv_t) in zip(streams, scaled, prods):
        vbd = jnp.where(hm_v, jnp.concatenate([p["v"][b]] * GLA_HEADS, axis=0), 0.0).astype(BF16)
        p["o"][b] = _dot(jnp.where(p["mask"], sc, 0.0).astype(BF16), vbd) + o_inter
        s_ref[b, p["d"]] = s_ref[b, p["d"]] * decay + jnp.where(hm_s, kv_t, 0.0)

    @pl.when(c == pl.num_programs(0) - 1)
    def _():
        sout_ref[...] = s_ref[...]


def _gla_call(qk, v, la, s0, trif, trib):
    ch = GLA_CHUNK
    if qk.ndim == 4:
        n, b = qk.shape[0], qk.shape[1]
        assert qk.shape[2] == ch
        spec = lambda w, off, rev: pl.BlockSpec(
            (None, b, ch, w), (lambda c: (n - 1 - c, 0, 0, off)) if rev else (lambda c: (c, 0, 0, off)))
    else:
        b = qk.shape[0]
        n = qk.shape[1] // ch
        spec = lambda w, off, rev: pl.BlockSpec(
            (b, ch, w), (lambda c: (0, n - 1 - c, off)) if rev else (lambda c: (0, c, off)))
    kern = functools.partial(_gla_kernel, nb=b)
    o_shape = jax.ShapeDtypeStruct(v.shape, F32)
    return pl.pallas_call(
        kern,
        grid=(n,),
        in_specs=[
            spec(2 * GLA_KEY_DIM, 0, False), spec(GLA_DIM, 0, False), spec(GLA_KEY_DIM, 0, False),
            spec(2 * GLA_KEY_DIM, 0, True), spec(GLA_DIM, 0, True), spec(GLA_KEY_DIM, 1, True),
            _full_spec(s0.shape), _full_spec(trif.shape), _full_spec(trib.shape),
        ],
        out_specs=[spec(GLA_DIM, 0, False), spec(GLA_DIM, 0, True), _full_spec(s0.shape)],
        out_shape=[o_shape, o_shape, jax.ShapeDtypeStruct(s0.shape, F32)],
        scratch_shapes=[pltpu.VMEM(s0.shape, F32)],
        compiler_params=_cparams(("arbitrary",)),
    )(qk, v, la, qk, v, la, s0, trif, trib)


def _stage_e_kernel(x_ref, mod_ref, sgu_ref, yf_ref, yb_ref, s5x_ref, of_ref, ob_ref, g_ref,
                    dskip_ref, wglu_ref, normg_ref, ones_ref, wout_ref, o_ref, *, grid_layout):
    ys = yf_ref[0] + yb_ref[0] + dskip_ref[...] * s5x_ref[0]
    z = _dot(_gelu(ys).astype(BF16), wglu_ref[...])
    s5o = z[:, 0:S5_DIM] * _sigmoid(z[:, S5_DIM:])
    o = _load_tokens(of_ref, grid_layout) + _load_tokens(ob_ref, grid_layout)
    ms = _dot_x_exact(o * o, ones_ref[...]) * (1.0 / GLA_DV)
    g = g_ref[0]
    gl = o * lax.rsqrt(ms + EPS) * normg_ref[...] * (g * _sigmoid(g))
    y = (_dot(sgu_ref[0].astype(BF16), wout_ref[0:SGU_DIM, :])
         + _dot(s5o.astype(BF16), wout_ref[SGU_DIM:SGU_DIM + S5_DIM, :])
         + _dot(gl.astype(BF16), wout_ref[SGU_DIM + S5_DIM:, :]))
    o_ref[0] = x_ref[0] + mod_ref[0, 2:3, :] * y


def _stage_e(xs, mod, sgu, yf_all, yb_all, s5_all, s5_row0, of, ob, g, dskip, wglu, normg, ones_gla, wout, tb):
    b, l, d = xs.shape
    assert s5_row0 % tb == 0
    grid_layout = of.ndim == 4
    tok = lambda w: pl.BlockSpec((1, tb, w), lambda bi, i: (bi, i, 0))
    s5 = pl.BlockSpec((1, tb, S5_DIM), lambda bi, i: (bi, s5_row0 // tb + i, 0))
    if grid_layout:
        assert tb % GRID_W == 0
        gla = pl.BlockSpec((GRID_W, None, tb // GRID_W, GLA_DIM), lambda bi, i: (0, bi, i, 0))
    else:
        gla = tok(GLA_DIM)
    return pl.pallas_call(
        functools.partial(_stage_e_kernel, grid_layout=grid_layout),
        grid=(b, l // tb),
        in_specs=[
            tok(d), pl.BlockSpec((1, 8, d), lambda bi, i: (bi, 0, 0)),
            tok(SGU_DIM), s5, s5, s5, gla, gla, tok(GLA_DIM),
            _full_spec(dskip.shape), _full_spec(wglu.shape), _full_spec(normg.shape),
            _full_spec(ones_gla.shape), _full_spec(wout.shape),
        ],
        out_specs=tok(d),
        out_shape=jax.ShapeDtypeStruct((b, l, d), F32),
        compiler_params=_cparams(("parallel", "parallel")),
    )(xs, mod, sgu, yf_all, yb_all, s5_all, of, ob, g, dskip, wglu, normg, ones_gla, wout)


PEER_TBF = 256
SUBLANES = 8


def _sort_network_16():
    def merge(lo, hi, r):
        step = r * 2
        if step < hi - lo:
            yield from merge(lo, hi, step)
            yield from merge(lo + r, hi, step)
            for i in range(lo + r, hi - r, step):
                yield (i, i + r)
        else:
            yield (lo, lo + r)

    def sort(lo, hi):
        if hi - lo >= 1:
            mid = lo + (hi - lo) // 2
            yield from sort(lo, mid)
            yield from sort(mid + 1, hi)
            yield from merge(lo, hi, 1)

    return tuple(sort(0, PEER_TOPK - 1))


SORT16 = _sort_network_16()
BITONIC16 = tuple((k, k + s) for s in (8, 4, 2, 1) for k in range(PEER_TOPK) if not k & s)


def _compare_exchange(xs, pairs):
    xs = list(xs)
    for i, j in pairs:
        hi = jnp.maximum(xs[i], xs[j])
        lo = jnp.minimum(xs[i], xs[j])
        xs[i], xs[j] = hi, lo
    return xs


def _merge_sublanes(xs):
    for shift in (4, 6, 7):
        rolled = [pltpu.roll(x, shift, 0) for x in xs]
        xs = [jnp.maximum(xs[k], rolled[PEER_TOPK - 1 - k]) for k in range(PEER_TOPK)]
        xs = _compare_exchange(xs, BITONIC16)
    return xs


def _dup_bf16_words(x):
    bits = pltpu.bitcast(x.astype(BF16).astype(F32), jnp.int32)
    return bits | lax.shift_right_logical(bits, 16)


def _stage_f_kernel(x_ref, mod_ref, g2_ref, wqt_ref, keys_ref,
                    ht_ref, e1w_ref, n1w_ref, e2_ref, r2_ref, qt_ref, v1_ref, v2_ref, *, tb):
    x = x_ref[0]
    ms = jnp.mean(x * x, axis=-1, keepdims=True)
    xn = x * lax.rsqrt(ms + EPS) * g2_ref[...]
    h = xn * (1.0 + mod_ref[0, 4:5, :]) + mod_ref[0, 3:4, :]
    ht = h.T.astype(BF16)
    ht_ref[...] = pltpu.bitcast(ht, jnp.int32)
    qt_ref[...] = _dot(wqt_ref[...], ht)
    k_top = PEER_TOPK

    def tiles(s):
        return [s[SUBLANES * k:SUBLANES * (k + 1)] for k in range(PEER_NKEYS // SUBLANES)]

    def head_body(hh, carry):
        for tc in range(tb // LANE):
            tcol = slice(tc * LANE, (tc + 1) * LANE)
            r1 = hh * (2 * PEER_HALF)
            q1 = qt_ref[r1:r1 + PEER_HALF, tcol].astype(BF16)
            q2 = qt_ref[r1 + PEER_HALF:r1 + 2 * PEER_HALF, tcol].astype(BF16)
            s1 = _dot(keys_ref[hh], q1)
            s2 = _dot(keys_ref[PEER_HEADS + hh], q2)
            for s, v_ref in ((s1, v1_ref), (s2, v2_ref)):
                top = _merge_sublanes(_compare_exchange(tiles(s), SORT16))
                for k in range(k_top):
                    v_ref[k:k + 1, tcol] = top[k][0:1]
            v1row = lambda a: v1_ref[a:a + 1, tcol]
            v2row = lambda b: v2_ref[b:b + 1, tcol]
            v1lo = v1_ref[0:SUBLANES, tcol]
            v2lo = v2_ref[0:SUBLANES, tcol]
            cand = [v1lo + v2row(b) for b in range(k_top)]
            tail = [v1row(a) + v2lo for a in range(SUBLANES, k_top)]
            for k in range(SUBLANES, k_top):
                cand[k] = jnp.maximum(cand[k], tail[k_top - 1 - k])
            best = _merge_sublanes(_compare_exchange(cand, BITONIC16))
            theta = best[k_top - 1][0:1]
            cmax = best[0][0:1]
            zsum = jnp.zeros((1, LANE), F32)
            for k in range(k_top):
                zsum = zsum + jnp.exp(best[k][0:1] - cmax)
            rz = 1.0 / zsum
            n_top = jnp.zeros((1, LANE), F32)
            for b in range(k_top):
                n_top = jnp.where(v1row(0) + v2row(b) >= theta, float(b + 1), n_top)
            n1 = jnp.zeros(s1.shape, F32)
            for b in range(SUBLANES):
                n1 = jnp.where(s1 + v2row(b) >= theta, float(b + 1), n1)
            n1 = jnp.where(s1 >= v1row(0), n_top, n1)
            r2 = jnp.zeros(s2.shape, F32)
            for b in range(k_top):
                r2 = jnp.where(v2row(b) > s2, float(b + 1), r2)
            e1 = jnp.where(s1 >= v1row(k_top - 1), jnp.exp(s1 - v1row(0)), 0.0) * rz
            e2 = jnp.where(s2 >= v2row(k_top - 1), jnp.exp(s2 - v2row(0)), 0.0)
            e1w_ref[tc, hh] = _dup_bf16_words(e1)
            n1w_ref[tc, hh] = _dup_bf16_words(n1)
            e2_ref[tc, hh] = pltpu.bitcast(e2.astype(BF16), jnp.int32)
            r2_ref[tc, hh] = pltpu.bitcast(r2.astype(BF16), jnp.int32)
        return carry

    for hh in range(PEER_HEADS):
        head_body(hh, 0)


def _stage_f(xs, mod, g2, wqt, keys, tb):
    b, l, d = xs.shape
    nblk = l // tb
    ntok = b * l
    nch = ntok // LANE
    kern = functools.partial(_stage_f_kernel, tb=tb)
    row_spec = pl.BlockSpec((tb // LANE, PEER_HEADS, PEER_NKEYS, LANE), lambda bi, i: (bi * nblk + i, 0, 0, 0))
    pair_spec = pl.BlockSpec((tb // LANE, PEER_HEADS, PEER_NKEYS // 2, LANE), lambda bi, i: (bi * nblk + i, 0, 0, 0))
    desc_shape = lambda rows: jax.ShapeDtypeStruct((nch, PEER_HEADS, rows, LANE), jnp.int32)
    return pl.pallas_call(
        kern,
        grid=(b, nblk),
        in_specs=[
            pl.BlockSpec((1, tb, d), lambda bi, i: (bi, i, 0)),
            pl.BlockSpec((1, 8, d), lambda bi, i: (bi, 0, 0)),
            _full_spec(g2.shape), _full_spec(wqt.shape), _full_spec(keys.shape),
        ],
        out_specs=[pl.BlockSpec((d // 2, tb), lambda bi, i: (0, bi * nblk + i)),
                   row_spec, row_spec, pair_spec, pair_spec],
        out_shape=[jax.ShapeDtypeStruct((d // 2, ntok), jnp.int32), desc_shape(PEER_NKEYS), desc_shape(PEER_NKEYS),
                   desc_shape(PEER_NKEYS // 2), desc_shape(PEER_NKEYS // 2)],
        scratch_shapes=[
            pltpu.VMEM((PEER_HEADS * 2 * PEER_HALF, tb), F32),
            pltpu.VMEM((PEER_TOPK, tb), F32),
            pltpu.VMEM((PEER_TOPK, tb), F32),
        ],
        compiler_params=_cparams(("parallel", "parallel")),
    )(xs, mod, g2, wqt, keys)


PEER_TBG = 1024
PEER_TE = 1024
PEER_I1_PER_TILE = PEER_TE // PEER_NKEYS
PEER_N_TILES = PEER_NKEYS * PEER_NKEYS // PEER_TE
PEER_MXU_COLS = 256
PEER_G_FLAGS = None


def _stage_g_kernel(htw_ref, e1w_ref, n1w_ref, e2_ref, r2_ref, uw_ref, vtw_ref, x_ref, mod_ref,
                    o_ref, acc_ref, ata_ref, atb_ref, p_ref, *, tb):
    s = pl.program_id(1)

    @pl.when(s == 0)
    def _():
        acc_ref[...] = jnp.zeros_like(acc_ref)

    def step(at_cur_ref, at_next_ref):
        per_grp = PEER_MXU_COLS // LANE
        for grp in range(tb // PEER_MXU_COLS):
            cols = slice(grp * PEER_MXU_COLS, (grp + 1) * PEER_MXU_COLS)
            for tcl in range(per_grp if at_cur_ref is not None else 0):
                tc = grp * per_grp + tcl
                tcol = slice(tc * LANE, (tc + 1) * LANE)
                for i1l in range(PEER_I1_PER_TILE):
                    rows = slice(i1l * PEER_NKEYS, (i1l + 1) * PEER_NKEYS)
                    gate = jnp.zeros((PEER_NKEYS, LANE), BF16)
                    for hh in range(PEER_HEADS):
                        e1row = e1w_ref[tc, hh, i1l:i1l + 1, :]
                        n1row = n1w_ref[tc, hh, i1l:i1l + 1, :]
                        e1 = pltpu.bitcast(jnp.broadcast_to(e1row, (PEER_NKEYS // 2, LANE)), BF16)
                        n1 = pltpu.bitcast(jnp.broadcast_to(n1row, (PEER_NKEYS // 2, LANE)), BF16)
                        r2 = pltpu.bitcast(r2_ref[tc, hh], BF16)
                        e2 = pltpu.bitcast(e2_ref[tc, hh], BF16)
                        gate = gate + e2 * jnp.where(r2 < n1, e1, 0.0)
                    p_ref[rows, tcol] = gate * _gelu(at_cur_ref[rows, tcol])
            if at_cur_ref is not None:
                vt = pltpu.bitcast(vtw_ref[...], BF16)
                acc_ref[:, cols] += _dot(vt, p_ref[:, cols])
            if at_next_ref is not None:
                u = pltpu.bitcast(uw_ref[...], BF16)
                ht = pltpu.bitcast(htw_ref[:, cols], BF16)
                at_next_ref[:, cols] = _dot(u, ht).astype(BF16)

    last = pl.num_programs(1) - 1

    @pl.when(s == 0)
    def _():
        step(None, ata_ref)

    @pl.when(jnp.logical_and(s % 2 == 0, jnp.logical_and(s > 0, s < last)))
    def _():
        step(atb_ref, ata_ref)

    @pl.when(s % 2 == 1)
    def _():
        step(ata_ref, atb_ref)

    @pl.when(s == last)
    def _():
        step(atb_ref, None)
        o_ref[...] = x_ref[...] + mod_ref[0, 5:6, :] * acc_ref[...].T


def _stage_g(htw, e1w, n1w, e2, r2, uw, vtw, xflat, mod, tokens_per_batch, tb):
    ntok = htw.shape[1]
    d = 2 * htw.shape[0]
    assert 2 * uw.shape[0] == PEER_N_TILES * PEER_TE and PEER_N_TILES % 2 == 0
    blocks_per_batch = tokens_per_batch // tb
    kern = functools.partial(_stage_g_kernel, tb=tb)
    last = PEER_N_TILES - 1
    desc_spec = pl.BlockSpec((tb // LANE, PEER_HEADS, PEER_NKEYS // 2, LANE), lambda j, i: (j, 0, 0, 0))
    row_spec = pl.BlockSpec((tb // LANE, PEER_HEADS, PEER_I1_PER_TILE, LANE),
                            lambda j, i: (j, 0, jnp.maximum(i - 1, 0), 0))
    return pl.pallas_call(
        kern,
        grid=(ntok // tb, PEER_N_TILES + 1),
        in_specs=[
            pl.BlockSpec((d // 2, tb), lambda j, i: (0, j)),
            row_spec, row_spec, desc_spec, desc_spec,
            pl.BlockSpec((PEER_TE // 2, d), lambda j, i: (jnp.minimum(i, last), 0)),
            pl.BlockSpec((d // 2, PEER_TE), lambda j, i: (0, jnp.maximum(i - 1, 0))),
            pl.BlockSpec((tb, d), lambda j, i: (j, 0)),
            pl.BlockSpec((1, 8, d), lambda j, i: (j // blocks_per_batch, 0, 0)),
        ],
        out_specs=pl.BlockSpec((tb, d), lambda j, i: (j, 0)),
        out_shape=jax.ShapeDtypeStruct((ntok, d), F32),
        scratch_shapes=[
            pltpu.VMEM((d, tb), F32),
            pltpu.VMEM((PEER_TE, tb), BF16),
            pltpu.VMEM((PEER_TE, tb), BF16),
            pltpu.VMEM((PEER_TE, tb), BF16),
        ],
        compiler_params=_cparams(("parallel", "arbitrary"), PEER_G_FLAGS),
    )(htw, e1w, n1w, e2, r2, uw, vtw, xflat, mod)


def _final_norm_kernel(x_ref, g_ref, o_ref):
    x = x_ref[...]
    ms = jnp.mean(x * x, axis=-1, keepdims=True)
    o_ref[...] = x * lax.rsqrt(ms + EPS) * g_ref[...]


def _final_norm(xflat, g, tb=512):
    n, d = xflat.shape
    return pl.pallas_call(
        _final_norm_kernel,
        grid=(n // tb,),
        in_specs=[pl.BlockSpec((tb, d), lambda i: (i, 0)), _full_spec(g.shape)],
        out_specs=pl.BlockSpec((tb, d), lambda i: (i, 0)),
        out_shape=jax.ShapeDtypeStruct((n, d), F32),
        compiler_params=_cparams(("parallel",)),
    )(xflat, g)


def _pack_kernel(x_ref, o_ref, *, transpose):
    x = x_ref[...]
    if transpose:
        x = x.T
    o_ref[...] = pltpu.bitcast(x.astype(BF16), jnp.int32)


def _pack_row_pairs(x, layer, transpose=False, tile=1024):
    _, r, c = x.shape
    if transpose:
        out_shape, out_spec = (c // 2, r), pl.BlockSpec((c // 2, tile), lambda i: (0, i))
    else:
        out_shape, out_spec = (r // 2, c), pl.BlockSpec((tile // 2, c), lambda i: (i, 0))
    return pl.pallas_call(
        functools.partial(_pack_kernel, transpose=transpose),
        grid=(r // tile,),
        in_specs=[pl.BlockSpec((None, tile, c), lambda i: (layer, i, 0))],
        out_specs=out_spec,
        out_shape=jax.ShapeDtypeStruct(out_shape, jnp.int32),
        compiler_params=_cparams(("parallel",)),
    )(x)


def _block_ones(n, blk):
    idx = np.arange(n) // blk
    return jnp.asarray((idx[:, None] == idx[None, :]).astype(np.float32), dtype=BF16)


def _s5_discretise(lam_re, lam_im, b_re, b_im, log_step):
    lam_re = jnp.minimum(lam_re.astype(F32), -1e-4)
    lam_im = lam_im.astype(F32)
    dt = jnp.exp(log_step.astype(F32))[:, None]
    mag = jnp.exp(lam_re * dt)
    a_re = mag * jnp.cos(lam_im * dt)
    a_im = mag * jnp.sin(lam_im * dt)
    den = lam_re * lam_re + lam_im * lam_im
    f_re = ((a_re - 1.0) * lam_re + a_im * lam_im) / den
    f_im = (a_im * lam_re - (a_re - 1.0) * lam_im) / den
    b_re = b_re.astype(F32)
    b_im = b_im.astype(F32)
    bb_re = f_re[..., None] * b_re - f_im[..., None] * b_im
    bb_im = f_re[..., None] * b_im + f_im[..., None] * b_re
    return a_re, a_im, bb_re, bb_im


def _group_block_diag(t):
    g, r, c = t.shape
    eye = jnp.eye(g, dtype=t.dtype)
    return (t[:, :, None, :] * eye[:, None, :, None]).reshape(g * r, g * c)


def _s5_params(lam_re, lam_im, b_re, b_im, c_re, c_im, log_step, nseq):
    b_rows, c_cols, ars, ais = [], [], [], []
    for d in range(2):
        a_re, a_im, bb_re, bb_im = _s5_discretise(lam_re[d], lam_im[d], b_re[d], b_im[d], log_step[d])
        bm = jnp.concatenate([_group_block_diag(jnp.swapaxes(bb_re, 1, 2)),
                              _group_block_diag(jnp.swapaxes(bb_im, 1, 2))], axis=1)
        b_rows.append(bm)
        cm = jnp.concatenate([_group_block_diag(jnp.swapaxes(c_re[d].astype(F32), 1, 2)),
                              -_group_block_diag(jnp.swapaxes(c_im[d].astype(F32), 1, 2))], axis=0)
        c_cols.append(cm)
        ars.append(jnp.broadcast_to(a_re.reshape(1, S5_LANES), (nseq, S5_LANES)))
        ais.append(jnp.broadcast_to(a_im.reshape(1, S5_LANES), (nseq, S5_LANES)))
    b2 = jnp.concatenate(b_rows, axis=0).astype(BF16)
    c2 = jnp.concatenate(c_cols, axis=1).astype(BF16)
    return b2, jnp.concatenate(ars, axis=0), jnp.concatenate(ais, axis=0), c2


def kernel(x, c, ctx, c_ctx, w_mod, b_mod, norm1_g, norm2_g, w_in, w_out, sgu_w, sgu_b, s5_lambda_re, s5_lambda_im, s5_b_re, s5_b_im, s5_c_re, s5_c_im, s5_log_step, s5_d, s5_w_glu, gla_w_gate, gla_b_gate, gla_norm_g, peer_w_query, peer_sub_keys, peer_expert_u, peer_expert_v, final_norm_g):
    nb, seq, d = x.shape
    c_len = ctx.shape[1]
    depth = w_mod.shape[0]

    cc = jnp.concatenate([c, c_ctx[None, :], jnp.zeros((8 - nb - 1, d), F32)], axis=0)
    mods = _mod_call(cc, w_mod, b_mod)

    ones_sgu = _block_ones(SGU_DIM, SGU_HEAD_DIM)
    ones_gla = _block_ones(GLA_DIM, GLA_DV)
    tri_np = np.tril(np.ones((GLA_CHUNK, GLA_CHUNK), np.float32))
    trif = jnp.asarray(tri_np, dtype=BF16)
    trib = jnp.asarray(tri_np.T, dtype=BF16)
    s_zero = jnp.zeros((nb, 2, GLA_DIM, GLA_KEY_DIM), F32)

    xl, xc = x, ctx
    for l in range(depth):
        ctx_out = l < depth - 1
        m6 = mods[l].reshape(8, N_MOD, d)
        mod_l = jnp.pad(m6[:nb], ((0, 0), (0, 2), (0, 0)))
        mod_c = jnp.broadcast_to(jnp.pad(m6[nb], ((0, 2), (0, 0)))[None], (nb, 8, d))

        win = jnp.pad(w_in[l], ((0, 0), (0, IN_PAD - IN_WIDTH))).astype(BF16)
        sguw = sgu_w[l].astype(BF16)
        sgub = jnp.repeat(jnp.swapaxes(sgu_b[l], 0, 1), SGU_HEAD_DIM, axis=1)
        gw = jnp.zeros((LANE, 2 * GLA_KEY_DIM), F32)
        gw = gw.at[0:GLA_RANK, 0:GLA_KEY_DIM].set(gla_w_gate[l, 0])
        gw = gw.at[GLA_RANK:2 * GLA_RANK, GLA_KEY_DIM:].set(gla_w_gate[l, 1]).astype(BF16)
        gb = gla_b_gate[l].reshape(1, 2 * GLA_KEY_DIM)
        g1 = norm1_g[l].reshape(1, d)

        s5_all = jnp.zeros((nb, seq + c_len, S5_DIM), F32)
        sgu_l, s5_all, qk_l, v_l, g_l, la_l = _stage_a(xl, mod_l, g1, win, sguw, sgub, ones_sgu, gw, gb,
                                                       s5_all, 0, tb=TB_LATENT, grid_layout=True)
        sgu_c, s5_all, qk_c, v_c, g_c, la_c = _stage_a(xc, mod_c, g1, win, sguw, sgub, ones_sgu, gw, gb,
                                                       s5_all, seq, tb=TB_CTX, grid_layout=False)

        b2, ar, ai, c2 = _s5_params(s5_lambda_re[l], s5_lambda_im[l], s5_b_re[l], s5_b_im[l],
                                    s5_c_re[l], s5_c_im[l], s5_log_step[l], nb)
        yf_all, yb_all = _s5_call(s5_all, seq, b2, ar, ai, c2)

        of_c, ob_c, s_ctx = _gla_call(qk_c, v_c, la_c, s_zero, trif, trib)
        of_l, ob_l, _ = _gla_call(qk_l, v_l, la_l, s_ctx, trif, trib)

        dskip = s5_d[l].reshape(1, S5_DIM)
        wglu = s5_w_glu[l].astype(BF16)
        normg = gla_norm_g[l].reshape(1, GLA_DIM)
        wout = w_out[l].astype(BF16)
        g2 = norm2_g[l].reshape(1, d)
        wqt = jnp.swapaxes(peer_w_query[l], 0, 1).astype(BF16)
        keys = peer_sub_keys[l].reshape(2 * PEER_HEADS, PEER_NKEYS, PEER_HALF).astype(BF16)
        u_bf = _pack_row_pairs(peer_expert_u, l)
        vt_bf = _pack_row_pairs(peer_expert_v, l, transpose=True)

        xl = _stage_e(xl, mod_l, sgu_l, yf_all, yb_all, s5_all, 0, of_l, ob_l, g_l,
                      dskip, wglu, normg, ones_gla, wout, tb=TB_LATENT)
        desc = _stage_f(xl, mod_l, g2, wqt, keys, tb=PEER_TBF)
        xl = _stage_g(*desc, u_bf, vt_bf, xl.reshape(nb * seq, d), mod_l, seq, PEER_TBG).reshape(nb, seq, d)

        if ctx_out:
            xc = _stage_e(xc, mod_c, sgu_c, yf_all, yb_all, s5_all, seq, of_c, ob_c, g_c,
                          dskip, wglu, normg, ones_gla, wout, tb=TB_CTX)
            desc = _stage_f(xc, mod_c, g2, wqt, keys, tb=PEER_TBF)
            xc = _stage_g(*desc, u_bf, vt_bf, xc.reshape(nb * c_len, d), mod_c, c_len,
                          min(PEER_TBG, c_len)).reshape(nb, c_len, d)

    return _final_norm(xl.reshape(nb * seq, d), final_norm_g.reshape(1, d)).reshape(nb, seq, d)
```

```python
import functools
import math

import numpy as np
import jax
import jax.numpy as jnp
from jax import lax
from jax.experimental import pallas as pl
from jax.experimental.pallas import tpu as pltpu

F32 = jnp.float32
BF16 = jnp.bfloat16
F8 = jnp.float8_e4m3fn
FP8_TARGET = 256.0
FP8_TINY = 1e-30
SUBLANES = 8

EPS = 1e-6
N_MOD = 6
GRID_W = 64

SGU_DIM = 256
SGU_HEADS = 4
SGU_HEAD_DIM = 64
SGU_CHUNK = 128

S5_DIM = 256
S5_GROUP = 16
S5_GROUPS = 16
S5_STATE = 64
S5_LANES = S5_GROUPS * S5_STATE

GLA_DIM = 512
GLA_HEADS = 8
GLA_DV = 64
GLA_DK = 32
GLA_KEY_DIM = 256
GLA_RANK = 16
GLA_GATE_TEMP = 16.0
GLA_CHUNK = 64

PEER_HEADS = 8
PEER_NKEYS = 128
PEER_HALF = 128
PEER_TOPK = 16

IN_WIDTH = 2336
IN_PAD = 2432
LANE = 128

VMEM_LIMIT = 56 * 1024 * 1024
TB_LATENT = 512
TB_CTX = 256

NEG_INF = float("-inf")
POS_INF = float("inf")


def _cparams(sem, flags=None):
    return pltpu.CompilerParams(dimension_semantics=sem, vmem_limit_bytes=VMEM_LIMIT, flags=flags)


def _gelu(x):
    c = math.sqrt(2.0 / math.pi)
    return 0.5 * x * (1.0 + jnp.tanh(c * (x + 0.044715 * (x * x * x))))


def _sigmoid(x):
    return 1.0 / (1.0 + jnp.exp(-x))


def _dot(a, b):
    return jnp.dot(a, b, preferred_element_type=F32)


def _dot_nt(a, b):
    return lax.dot_general(a, b, (((1,), (1,)), ((), ())), preferred_element_type=F32)


def _dot_tn(a, b):
    return lax.dot_general(a, b, (((0,), (0,)), ((), ())), preferred_element_type=F32)


def _split3(x):
    hi = x.astype(BF16)
    r = x - hi.astype(F32)
    mid = r.astype(BF16)
    lo = (r - mid.astype(F32)).astype(BF16)
    return hi, mid, lo


def _dot_x_exact(x, m):
    hi, mid, lo = _split3(x)
    return _dot(hi, m) + _dot(mid, m) + _dot(lo, m)


def _dot_m_exact(m, x):
    hi, mid, lo = _split3(x)
    return _dot(m, hi) + _dot(m, mid) + _dot(m, lo)


def _full_spec(shape):
    nd = len(shape)
    return pl.BlockSpec(shape, lambda *_: (0,) * nd)


MOD_TILE = 512


def _mod_kernel(c_ref, w_ref, b_ref, o_ref):
    c = c_ref[...]
    a = c * _sigmoid(c)
    o_ref[0] = jnp.dot(a, w_ref[0], preferred_element_type=F32,
                       precision=lax.Precision.HIGHEST) + b_ref[0]


def _mod_call(cc, w_mod, b_mod):
    depth, d, nd = w_mod.shape
    rows = cc.shape[0]
    return pl.pallas_call(
        _mod_kernel,
        grid=(depth, nd // MOD_TILE),
        in_specs=[
            pl.BlockSpec((rows, d), lambda l, j: (0, 0)),
            pl.BlockSpec((1, d, MOD_TILE), lambda l, j: (l, 0, j)),
            pl.BlockSpec((1, 1, MOD_TILE), lambda l, j: (l, 0, j)),
        ],
        out_specs=pl.BlockSpec((1, rows, MOD_TILE), lambda l, j: (l, 0, j)),
        out_shape=jax.ShapeDtypeStruct((depth, rows, nd), F32),
        compiler_params=_cparams(("parallel", "parallel")),
    )(cc, w_mod, b_mod.reshape(depth, 1, nd))


def _store_tokens(ref, val, grid_layout):
    if grid_layout:
        for r in range(val.shape[0] // GRID_W):
            ref[:, r, :] = val[r * GRID_W:(r + 1) * GRID_W]
    else:
        ref[0] = val


def _load_tokens(ref, grid_layout):
    if grid_layout:
        return jnp.concatenate([ref[:, r, :] for r in range(ref.shape[1])], axis=0)
    return ref[0]


def _stage_a_kernel(x_ref, mod_ref, g1_ref, win_ref, sguw_ref, sgub_ref, ones_ref, gw_ref, gb_ref, s5_in_ref,
                    sgu_ref, s5x_ref, qk_ref, v_ref, g_ref, la_ref, *, tb, grid_layout):
    del s5_in_ref
    x = x_ref[0]
    ms = jnp.mean(x * x, axis=-1, keepdims=True)
    xn = x * lax.rsqrt(ms + EPS) * g1_ref[...]
    h = xn * (1.0 + mod_ref[0, 1:2, :]) + mod_ref[0, 0:1, :]
    cols = _dot(h.astype(BF16), win_ref[...])

    u = _gelu(cols[:, 0:SGU_DIM])
    v = _gelu(cols[:, SGU_DIM:2 * SGU_DIM])
    msq = _dot_x_exact(v * v, ones_ref[...]) * (1.0 / SGU_HEAD_DIM)
    vn = (v * lax.rsqrt(msq + EPS)).astype(BF16)
    head_of_lane = lax.broadcasted_iota(jnp.int32, (SGU_CHUNK, SGU_DIM), 1) // SGU_HEAD_DIM
    for ci in range(tb // SGU_CHUNK):
        rows = slice(ci * SGU_CHUNK, (ci + 1) * SGU_CHUNK)
        vc = vn[rows]
        mixed = sgub_ref[...]
        for hh in range(SGU_HEADS):
            mixed = mixed + jnp.where(head_of_lane == hh, _dot(sguw_ref[hh], vc), 0.0)
        sgu_ref[0, rows, :] = u[rows] * mixed

    s5x_ref[0] = cols[:, 512:768]
    q = cols[:, 768:1024] * (GLA_DK ** -0.5)
    _store_tokens(qk_ref, jnp.concatenate([q, cols[:, 1024:1280]], axis=-1), grid_layout)
    _store_tokens(v_ref, cols[:, 1280:1792], grid_layout)
    g_ref[0] = cols[:, 1792:2304]

    z = cols[:, 2304:2432].astype(BF16)
    za = _dot(z, gw_ref[...]) + gb_ref[...]
    log_sig = jnp.minimum(za, 0.0) - jnp.log1p(jnp.exp(-jnp.abs(za)))
    _store_tokens(la_ref, log_sig * (1.0 / GLA_GATE_TEMP), grid_layout)


def _stage_a(xs, mod, g1, win, sguw, sgub, ones_sgu, gw, gb, s5_all, s5_row0, tb, grid_layout):
    b, l, d = xs.shape
    assert s5_row0 % tb == 0
    kern = functools.partial(_stage_a_kernel, tb=tb, grid_layout=grid_layout)
    tok = lambda w: pl.BlockSpec((1, tb, w), lambda bi, i: (bi, i, 0))
    if grid_layout:
        assert tb % GRID_W == 0 and l % GRID_W == 0
        gla = lambda w: pl.BlockSpec((GRID_W, None, tb // GRID_W, w), lambda bi, i: (0, bi, i, 0))
        gla_shape = lambda w: jax.ShapeDtypeStruct((GRID_W, b, l // GRID_W, w), F32)
    else:
        gla = tok
        gla_shape = lambda w: jax.ShapeDtypeStruct((b, l, w), F32)
    nat_shape = lambda w: jax.ShapeDtypeStruct((b, l, w), F32)
    return pl.pallas_call(
        kern,
        grid=(b, l // tb),
        in_specs=[
            tok(d),
            pl.BlockSpec((1, 8, d), lambda bi, i: (bi, 0, 0)),
            _full_spec(g1.shape), _full_spec(win.shape), _full_spec(sguw.shape), _full_spec(sgub.shape),
            _full_spec(ones_sgu.shape), _full_spec(gw.shape), _full_spec(gb.shape),
            pl.BlockSpec(memory_space=pl.ANY),
        ],
        out_specs=[tok(SGU_DIM),
                   pl.BlockSpec((1, tb, S5_DIM), lambda bi, i: (bi, s5_row0 // tb + i, 0)),
                   gla(2 * GLA_KEY_DIM), gla(GLA_DIM), tok(GLA_DIM), gla(2 * GLA_KEY_DIM)],
        out_shape=[nat_shape(SGU_DIM), jax.ShapeDtypeStruct(s5_all.shape, F32),
                   gla_shape(2 * GLA_KEY_DIM), gla_shape(GLA_DIM), nat_shape(GLA_DIM), gla_shape(2 * GLA_KEY_DIM)],
        input_output_aliases={9: 1},
        compiler_params=_cparams(("parallel", "parallel")),
    )(xs, mod, g1, win, sguw, sgub, ones_sgu, gw, gb, s5_all)


S5_TC = 128


def _s5_kernel(uf_ref, ub_ref, perm_ref, permt_ref, b2_ref, ar_ref, ai_ref, c2_ref, yf_ref, yb_ref,
               h_ref, buf_ref, *, nseq):
    tc = S5_TC
    rows = tc * 2 * nseq

    @pl.when(pl.program_id(0) == 0)
    def _():
        h_ref[...] = jnp.zeros_like(h_ref)

    x = jnp.concatenate([uf_ref[b] for b in range(nseq)] + [ub_ref[b] for b in range(nseq)], axis=0)
    u_tm = _dot(perm_ref[...], x.astype(BF16))
    fwd_row = lax.broadcasted_iota(jnp.int32, (rows, S5_DIM), 0) % (2 * nseq) < nseq
    u = jnp.concatenate([jnp.where(fwd_row, u_tm, 0.0), jnp.where(fwd_row, 0.0, u_tm)], axis=-1).astype(BF16)
    buf_ref[...] = _dot(u, b2_ref[...]).reshape(tc, 2 * nseq, 2 * S5_LANES)
    ar = ar_ref[...]
    ai = ai_ref[...]

    def step(t, carry):
        hr, hi = carry
        bur = buf_ref[t, :, 0:S5_LANES]
        bui = buf_ref[t, :, S5_LANES:]
        nhr = ar * hr - ai * hi + bur
        nhi = ar * hi + ai * hr + bui
        buf_ref[t, :, 0:S5_LANES] = nhr
        buf_ref[t, :, S5_LANES:] = nhi
        return nhr, nhi

    hr, hi = lax.fori_loop(0, tc, step, (h_ref[:, 0:S5_LANES], h_ref[:, S5_LANES:]))
    h_ref[:, 0:S5_LANES] = hr
    h_ref[:, S5_LANES:] = hi

    hs = buf_ref[...].reshape(rows, 2 * S5_LANES).astype(BF16)
    y2 = _dot(hs, c2_ref[...])
    y = jnp.where(fwd_row, y2[:, 0:S5_DIM], y2[:, S5_DIM:])
    y_hi = y.astype(BF16)
    y_lo = (y - y_hi.astype(F32)).astype(BF16)
    y_nat = _dot(permt_ref[...], y_hi) + _dot(permt_ref[...], y_lo)
    for b in range(nseq):
        yf_ref[b] = y_nat[b * tc:(b + 1) * tc]
        yb_ref[b] = y_nat[(nseq + b) * tc:(nseq + b + 1) * tc]


def _s5_permutation(nseq):
    tc = S5_TC
    p = np.zeros((tc * 2 * nseq, tc * 2 * nseq), np.float32)
    for q in range(2 * nseq):
        for t in range(tc):
            p[t * 2 * nseq + q, q * tc + (t if q < nseq else tc - 1 - t)] = 1.0
    return p


def _s5_call(s5_all, n_lat, b2, ar, ai, c2):
    nseq, t, _ = s5_all.shape
    n = t // S5_TC
    n_l = n_lat // S5_TC
    n_c = n - n_l
    perm = _s5_permutation(nseq)
    kern = functools.partial(_s5_kernel, nseq=nseq)
    fwd_map = lambda s: (0, jnp.where(s < n_c, n_l + s, s - n_c), 0)
    bwd_map = lambda s: (0, n - 1 - s, 0)
    blk = (nseq, S5_TC, S5_DIM)
    rows = 2 * nseq
    return pl.pallas_call(
        kern,
        grid=(n,),
        in_specs=[
            pl.BlockSpec(blk, fwd_map), pl.BlockSpec(blk, bwd_map),
            _full_spec(perm.shape), _full_spec(perm.shape),
            _full_spec(b2.shape), _full_spec(ar.shape), _full_spec(ai.shape), _full_spec(c2.shape),
        ],
        out_specs=[pl.BlockSpec(blk, fwd_map), pl.BlockSpec(blk, bwd_map)],
        out_shape=[jax.ShapeDtypeStruct(s5_all.shape, F32), jax.ShapeDtypeStruct(s5_all.shape, F32)],
        scratch_shapes=[
            pltpu.VMEM((rows, 2 * S5_LANES), F32),
            pltpu.VMEM((S5_TC, rows, 2 * S5_LANES), F32),
        ],
        compiler_params=_cparams(("arbitrary",)),
    )(s5_all, s5_all, jnp.asarray(perm, dtype=BF16), jnp.asarray(perm.T, dtype=BF16), b2, ar, ai, c2)


def _gla_kernel(qkf_ref, vf_ref, laf_ref, qkb_ref, vb_ref, lab_ref, s0_ref, trif_ref, trib_ref,
                of_ref, ob_ref, sout_ref, s_ref, *, nb):
    c = pl.program_id(0)

    @pl.when(c == 0)
    def _():
        s_ref[...] = s0_ref[...]

    ch = GLA_CHUNK
    r_k = lax.broadcasted_iota(jnp.int32, (GLA_HEADS * ch, GLA_KEY_DIM), 0) // ch
    c_k = lax.broadcasted_iota(jnp.int32, (GLA_HEADS * ch, GLA_KEY_DIM), 1) // GLA_DK
    hm_k = r_k == c_k
    r_v = lax.broadcasted_iota(jnp.int32, (GLA_HEADS * ch, GLA_DIM), 0) // ch
    c_v = lax.broadcasted_iota(jnp.int32, (GLA_HEADS * ch, GLA_DIM), 1) // GLA_DV
    hm_v = r_v == c_v
    r_s = lax.broadcasted_iota(jnp.int32, (GLA_DIM, GLA_KEY_DIM), 0) // GLA_DV
    c_s = lax.broadcasted_iota(jnp.int32, (GLA_DIM, GLA_KEY_DIM), 1) // GLA_DK
    hm_s = r_s == c_s
    t_i = lax.broadcasted_iota(jnp.int32, (ch, GLA_HEADS * ch), 0)
    s_i = lax.broadcasted_iota(jnp.int32, (ch, GLA_HEADS * ch), 1) % ch
    mask_f = t_i >= s_i
    mask_b = t_i <= s_i
    trif = trif_ref[...]
    trib = trib_ref[...]

    fwd = dict(qk=qkf_ref, v=vf_ref, la=laf_ref, o=of_ref, tri=trif, last=ch - 1, ref=ch // 2, mask=mask_f, d=0)
    bwd = dict(qk=qkb_ref, v=vb_ref, la=lab_ref, o=ob_ref, tri=trib, last=0, ref=ch - 1 - ch // 2, mask=mask_b, d=1)
    streams = [(b, p) for b in range(nb) for p in (fwd, bwd)]

    bcums = [_dot_m_exact(p["tri"], p["la"][b]) for b, p in streams]
    scaled = []
    for (b, p), bcum in zip(streams, bcums):
        qk = p["qk"][b]
        q, k = qk[:, 0:GLA_KEY_DIM], qk[:, GLA_KEY_DIM:]
        blast = bcum[p["last"]:p["last"] + 1]
        bref = bcum[p["ref"]:p["ref"] + 1]
        qe = (q * jnp.exp(bcum)).astype(BF16)
        qd = (q * jnp.exp(bcum - bref)).astype(BF16)
        kd = k * jnp.exp(bref - bcum)
        kdec = (k * jnp.exp(blast - bcum)).astype(BF16)
        kst = jnp.where(hm_k, jnp.concatenate([kd] * GLA_HEADS, axis=0), 0.0).astype(BF16)
        scaled.append((qe, qd, kdec, kst, jnp.exp(blast)))
    prods = []
    for (b, p), (qe, qd, kdec, kst, _) in zip(streams, scaled):
        v = p["v"][b]
        sc = _dot_nt(qd, kst)
        o_inter = _dot_nt(qe, s_ref[b, p["d"]].astype(BF16))
        kv_t = _dot_tn(v.astype(BF16), kdec)
        prods.append((sc, o_inter, kv_t))
    for (b, p), (_, _, _, _, decay), (sc, o_inter, kv_t) in zip(streams, scaled, prods):
        vbd = jnp.where(hm_v, jnp.concatenate([p["v"][b]] * GLA_HEADS, axis=0), 0.0).astype(BF16)
        p["o"][b] = _dot(jnp.where(p["mask"], sc, 0.0).astype(BF16), vbd) + o_inter
        s_ref[b, p["d"]] = s_ref[b, p["d"]] * decay + jnp.where(hm_s, kv_t, 0.0)

    @pl.when(c == pl.num_programs(0) - 1)
    def _():
        sout_ref[...] = s_ref[...]


def _gla_call(qk, v, la, s0, trif, trib):
    ch = GLA_CHUNK
    if qk.ndim == 4:
        n, b = qk.shape[0], qk.shape[1]
        assert qk.shape[2] == ch
        spec = lambda w, off, rev: pl.BlockSpec(
            (None, b, ch, w), (lambda c: (n - 1 - c, 0, 0, off)) if rev else (lambda c: (c, 0, 0, off)))
    else:
        b = qk.shape[0]
        n = qk.shape[1] // ch
        spec = lambda w, off, rev: pl.BlockSpec(
            (b, ch, w), (lambda c: (0, n - 1 - c, off)) if rev else (lambda c: (0, c, off)))
    kern = functools.partial(_gla_kernel, nb=b)
    o_shape = jax.ShapeDtypeStruct(v.shape, F32)
    return pl.pallas_call(
        kern,
        grid=(n,),
        in_specs=[
            spec(2 * GLA_KEY_DIM, 0, False), spec(GLA_DIM, 0, False), spec(GLA_KEY_DIM, 0, False),
            spec(2 * GLA_KEY_DIM, 0, True), spec(GLA_DIM, 0, True), spec(GLA_KEY_DIM, 1, True),
            _full_spec(s0.shape), _full_spec(trif.shape), _full_spec(trib.shape),
        ],
        out_specs=[spec(GLA_DIM, 0, False), spec(GLA_DIM, 0, True), _full_spec(s0.shape)],
        out_shape=[o_shape, o_shape, jax.ShapeDtypeStruct(s0.shape, F32)],
        scratch_shapes=[pltpu.VMEM(s0.shape, F32)],
        compiler_params=_cparams(("arbitrary",)),
    )(qk, v, la, qk, v, la, s0, trif, trib)


def _stage_e_kernel(x_ref, mod_ref, sgu_ref, yf_ref, yb_ref, s5x_ref, of_ref, ob_ref, g_ref,
                    dskip_ref, wglu_ref, normg_ref, ones_ref, wout_ref, o_ref, *, grid_layout):
    ys = yf_ref[0] + yb_ref[0] + dskip_ref[...] * s5x_ref[0]
    z = _dot(_gelu(ys).astype(BF16), wglu_ref[...])
    s5o = z[:, 0:S5_DIM] * _sigmoid(z[:, S5_DIM:])
    o = _load_tokens(of_ref, grid_layout) + _load_tokens(ob_ref, grid_layout)
    ms = _dot_x_exact(o * o, ones_ref[...]) * (1.0 / GLA_DV)
    g = g_ref[0]
    gl = o * lax.rsqrt(ms + EPS) * normg_ref[...] * (g * _sigmoid(g))
    y = (_dot(sgu_ref[0].astype(BF16), wout_ref[0:SGU_DIM, :])
         + _dot(s5o.astype(BF16), wout_ref[SGU_DIM:SGU_DIM + S5_DIM, :])
         + _dot(gl.astype(BF16), wout_ref[SGU_DIM + S5_DIM:, :]))
    o_ref[0] = x_ref[0] + mod_ref[0, 2:3, :] * y


def _stage_e(xs, mod, sgu, yf_all, yb_all, s5_all, s5_row0, of, ob, g, dskip, wglu, normg, ones_gla, wout, tb):
    b, l, d = xs.shape
    assert s5_row0 % tb == 0
    grid_layout = of.ndim == 4
    tok = lambda w: pl.BlockSpec((1, tb, w), lambda bi, i: (bi, i, 0))
    s5 = pl.BlockSpec((1, tb, S5_DIM), lambda bi, i: (bi, s5_row0 // tb + i, 0))
    if grid_layout:
        assert tb % GRID_W == 0
        gla = pl.BlockSpec((GRID_W, None, tb // GRID_W, GLA_DIM), lambda bi, i: (0, bi, i, 0))
    else:
        gla = tok(GLA_DIM)
    return pl.pallas_call(
        functools.partial(_stage_e_kernel, grid_layout=grid_layout),
        grid=(b, l // tb),
        in_specs=[
            tok(d), pl.BlockSpec((1, 8, d), lambda bi, i: (bi, 0, 0)),
            tok(SGU_DIM), s5, s5, s5, gla, gla, tok(GLA_DIM),
            _full_spec(dskip.shape), _full_spec(wglu.shape), _full_spec(normg.shape),
            _full_spec(ones_gla.shape), _full_spec(wout.shape),
        ],
        out_specs=tok(d),
        out_shape=jax.ShapeDtypeStruct((b, l, d), F32),
        compiler_params=_cparams(("parallel", "parallel")),
    )(xs, mod, sgu, yf_all, yb_all, s5_all, of, ob, g, dskip, wglu, normg, ones_gla, wout)


PEER_TBF = 256


def _sort_network_16():
    def merge(lo, hi, r):
        step = r * 2
        if step < hi - lo:
            yield from merge(lo, hi, step)
            yield from merge(lo + r, hi, step)
            for i in range(lo + r, hi - r, step):
                yield (i, i + r)
        else:
            yield (lo, lo + r)

    def sort(lo, hi):
        if hi - lo >= 1:
            mid = lo + (hi - lo) // 2
            yield from sort(lo, mid)
            yield from sort(mid + 1, hi)
            yield from merge(lo, hi, 1)

    return tuple(sort(0, PEER_TOPK - 1))


SORT16 = _sort_network_16()
BITONIC16 = tuple((k, k + s) for s in (8, 4, 2, 1) for k in range(PEER_TOPK) if not k & s)


def _compare_exchange(xs, pairs):
    xs = list(xs)
    for i, j in pairs:
        hi = jnp.maximum(xs[i], xs[j])
        lo = jnp.minimum(xs[i], xs[j])
        xs[i], xs[j] = hi, lo
    return xs


def _merge_sublanes(xs):
    for shift in (4, 6, 7):
        rolled = [pltpu.roll(x, shift, 0) for x in xs]
        xs = [jnp.maximum(xs[k], rolled[PEER_TOPK - 1 - k]) for k in range(PEER_TOPK)]
        xs = _compare_exchange(xs, BITONIC16)
    return xs


def _dup_bf16_words(x):
    bits = pltpu.bitcast(x.astype(BF16).astype(F32), jnp.int32)
    return bits | lax.shift_right_logical(bits, 16)


def _stage_f_kernel(x_ref, mod_ref, g2_ref, wqt_ref, keys_ref,
                    ht_ref, hinv_ref, e1w_ref, n1w_ref, e2_ref, r2_ref, qt_ref, v1_ref, v2_ref, *, tb):
    x = x_ref[0]
    ms = jnp.mean(x * x, axis=-1, keepdims=True)
    xn = x * lax.rsqrt(ms + EPS) * g2_ref[...]
    h = xn * (1.0 + mod_ref[0, 4:5, :]) + mod_ref[0, 3:4, :]
    ht = h.T
    ht_ref[...], hinv_ref[0] = _to_fp8_words(ht)
    qt_ref[...] = _dot(wqt_ref[...], ht.astype(BF16))
    k_top = PEER_TOPK

    def tiles(s):
        return [s[SUBLANES * k:SUBLANES * (k + 1)] for k in range(PEER_NKEYS // SUBLANES)]

    def head_body(hh, carry):
        for tc in range(tb // LANE):
            tcol = slice(tc * LANE, (tc + 1) * LANE)
            r1 = hh * (2 * PEER_HALF)
            q1 = qt_ref[r1:r1 + PEER_HALF, tcol].astype(BF16)
            q2 = qt_ref[r1 + PEER_HALF:r1 + 2 * PEER_HALF, tcol].astype(BF16)
            s1 = _dot(keys_ref[hh], q1)
            s2 = _dot(keys_ref[PEER_HEADS + hh], q2)
            for s, v_ref in ((s1, v1_ref), (s2, v2_ref)):
                top = _merge_sublanes(_compare_exchange(tiles(s), SORT16))
                for k in range(k_top):
                    v_ref[k:k + 1, tcol] = top[k][0:1]
            v1row = lambda a: v1_ref[a:a + 1, tcol]
            v2row = lambda b: v2_ref[b:b + 1, tcol]
            v1lo = v1_ref[0:SUBLANES, tcol]
            v2lo = v2_ref[0:SUBLANES, tcol]
            cand = [v1lo + v2row(b) for b in range(k_top)]
            tail = [v1row(a) + v2lo for a in range(SUBLANES, k_top)]
            for k in range(SUBLANES, k_top):
                cand[k] = jnp.maximum(cand[k], tail[k_top - 1 - k])
            best = _merge_sublanes(_compare_exchange(cand, BITONIC16))
            theta = best[k_top - 1][0:1]
            cmax = best[0][0:1]
            zsum = jnp.zeros((1, LANE), F32)
            for k in range(k_top):
                zsum = zsum + jnp.exp(best[k][0:1] - cmax)
            rz = 1.0 / zsum
            n_top = jnp.zeros((1, LANE), F32)
            for b in range(k_top):
                n_top = jnp.where(v1row(0) + v2row(b) >= theta, float(b + 1), n_top)
            n1 = jnp.zeros(s1.shape, F32)
            for b in range(SUBLANES):
                n1 = jnp.where(s1 + v2row(b) >= theta, float(b + 1), n1)
            n1 = jnp.where(s1 >= v1row(0), n_top, n1)
            r2 = jnp.zeros(s2.shape, F32)
            for b in range(k_top):
                r2 = jnp.where(v2row(b) > s2, float(b + 1), r2)
            e1 = jnp.where(s1 >= v1row(k_top - 1), jnp.exp(s1 - v1row(0)), 0.0) * rz
            e2 = jnp.where(s2 >= v2row(k_top - 1), jnp.exp(s2 - v2row(0)), 0.0)
            e1w_ref[tc, hh] = _dup_bf16_words(e1)
            n1w_ref[tc, hh] = _dup_bf16_words(n1)
            e2_ref[tc, hh] = pltpu.bitcast(e2.astype(BF16), jnp.int32)
            r2_ref[tc, hh] = pltpu.bitcast(r2.astype(BF16), jnp.int32)
        return carry

    for hh in range(PEER_HEADS):
        head_body(hh, 0)


def _stage_f(xs, mod, g2, wqt, keys, tb):
    b, l, d = xs.shape
    nblk = l // tb
    ntok = b * l
    nch = ntok // LANE
    kern = functools.partial(_stage_f_kernel, tb=tb)
    row_spec = pl.BlockSpec((tb // LANE, PEER_HEADS, PEER_NKEYS, LANE), lambda bi, i: (bi * nblk + i, 0, 0, 0))
    pair_spec = pl.BlockSpec((tb // LANE, PEER_HEADS, PEER_NKEYS // 2, LANE), lambda bi, i: (bi * nblk + i, 0, 0, 0))
    desc_shape = lambda rows: jax.ShapeDtypeStruct((nch, PEER_HEADS, rows, LANE), jnp.int32)
    return pl.pallas_call(
        kern,
        grid=(b, nblk),
        in_specs=[
            pl.BlockSpec((1, tb, d), lambda bi, i: (bi, i, 0)),
            pl.BlockSpec((1, 8, d), lambda bi, i: (bi, 0, 0)),
            _full_spec(g2.shape), _full_spec(wqt.shape), _full_spec(keys.shape),
        ],
        out_specs=[pl.BlockSpec((d // 4, tb), lambda bi, i: (0, bi * nblk + i)),
                   pl.BlockSpec((1, SUBLANES, LANE), lambda bi, i: (bi * nblk + i, 0, 0)),
                   row_spec, row_spec, pair_spec, pair_spec],
        out_shape=[jax.ShapeDtypeStruct((d // 4, ntok), jnp.int32),
                   jax.ShapeDtypeStruct((ntok // tb, SUBLANES, LANE), F32),
                   desc_shape(PEER_NKEYS), desc_shape(PEER_NKEYS),
                   desc_shape(PEER_NKEYS // 2), desc_shape(PEER_NKEYS // 2)],
        scratch_shapes=[
            pltpu.VMEM((PEER_HEADS * 2 * PEER_HALF, tb), F32),
            pltpu.VMEM((PEER_TOPK, tb), F32),
            pltpu.VMEM((PEER_TOPK, tb), F32),
        ],
        compiler_params=_cparams(("parallel", "parallel")),
    )(xs, mod, g2, wqt, keys)


PEER_TBG = 1024
PEER_TE = 1024
PEER_I1_PER_TILE = PEER_TE // PEER_NKEYS
PEER_N_TILES = PEER_NKEYS * PEER_NKEYS // PEER_TE
PEER_MXU_COLS = 256
PEER_G_FLAGS = None


def _stage_g_kernel(htw_ref, hinv_ref, e1w_ref, n1w_ref, e2_ref, r2_ref, uw_ref, uinv_ref, vtw_ref, x_ref, mod_ref,
                    o_ref, acc_ref, ata_ref, atb_ref, p_ref, *, tb):
    s = pl.program_id(1)

    @pl.when(s == 0)
    def _():
        acc_ref[...] = jnp.zeros_like(acc_ref)

    def step(at_cur_ref, at_next_ref):
        per_grp = PEER_MXU_COLS // LANE
        for grp in range(tb // PEER_MXU_COLS):
            cols = slice(grp * PEER_MXU_COLS, (grp + 1) * PEER_MXU_COLS)
            for tcl in range(per_grp if at_cur_ref is not None else 0):
                tc = grp * per_grp + tcl
                tcol = slice(tc * LANE, (tc + 1) * LANE)
                for i1l in range(PEER_I1_PER_TILE):
                    rows = slice(i1l * PEER_NKEYS, (i1l + 1) * PEER_NKEYS)
                    gate = jnp.zeros((PEER_NKEYS, LANE), BF16)
                    for hh in range(PEER_HEADS):
                        e1row = e1w_ref[tc, hh, i1l:i1l + 1, :]
                        n1row = n1w_ref[tc, hh, i1l:i1l + 1, :]
                        e1 = pltpu.bitcast(jnp.broadcast_to(e1row, (PEER_NKEYS // 2, LANE)), BF16)
                        n1 = pltpu.bitcast(jnp.broadcast_to(n1row, (PEER_NKEYS // 2, LANE)), BF16)
                        r2 = pltpu.bitcast(r2_ref[tc, hh], BF16)
                        e2 = pltpu.bitcast(e2_ref[tc, hh], BF16)
                        gate = gate + e2 * jnp.where(r2 < n1, e1, 0.0)
                    p_ref[rows, tcol] = gate * _gelu(at_cur_ref[rows, tcol])
            if at_cur_ref is not None:
                vt = pltpu.bitcast(vtw_ref[...], BF16)
                acc_ref[:, cols] += _dot(vt, p_ref[:, cols])
            if at_next_ref is not None:
                u = pltpu.bitcast(uw_ref[...], F8)
                ht = pltpu.bitcast(htw_ref[:, cols], F8)
                dequant = uinv_ref[0, 0:1, 0:1] * hinv_ref[grp, 0:1, 0:1]
                at_next_ref[:, cols] = (_dot(u, ht) * dequant).astype(BF16)

    last = pl.num_programs(1) - 1

    @pl.when(s == 0)
    def _():
        step(None, ata_ref)

    @pl.when(jnp.logical_and(s % 2 == 0, jnp.logical_and(s > 0, s < last)))
    def _():
        step(atb_ref, ata_ref)

    @pl.when(s % 2 == 1)
    def _():
        step(ata_ref, atb_ref)

    @pl.when(s == last)
    def _():
        step(atb_ref, None)
        o_ref[...] = x_ref[...] + mod_ref[0, 5:6, :] * acc_ref[...].T


def _stage_g(htw, hinv, e1w, n1w, e2, r2, uw, uinv, vtw, xflat, mod, tokens_per_batch, tb):
    ntok = htw.shape[1]
    d = 4 * htw.shape[0]
    assert 4 * uw.shape[0] == PEER_N_TILES * PEER_TE and PEER_N_TILES % 2 == 0 and PEER_TBF == PEER_MXU_COLS
    blocks_per_batch = tokens_per_batch // tb
    kern = functools.partial(_stage_g_kernel, tb=tb)
    last = PEER_N_TILES - 1
    desc_spec = pl.BlockSpec((tb // LANE, PEER_HEADS, PEER_NKEYS // 2, LANE), lambda j, i: (j, 0, 0, 0))
    row_spec = pl.BlockSpec((tb // LANE, PEER_HEADS, PEER_I1_PER_TILE, LANE),
                            lambda j, i: (j, 0, jnp.maximum(i - 1, 0), 0))
    return pl.pallas_call(
        kern,
        grid=(ntok // tb, PEER_N_TILES + 1),
        in_specs=[
            pl.BlockSpec((d // 4, tb), lambda j, i: (0, j)),
            pl.BlockSpec((tb // PEER_TBF, SUBLANES, LANE), lambda j, i: (j, 0, 0)),
            row_spec, row_spec, desc_spec, desc_spec,
            pl.BlockSpec((PEER_TE // 4, d), lambda j, i: (jnp.minimum(i, last), 0)),
            pl.BlockSpec((1, SUBLANES, LANE), lambda j, i: (jnp.minimum(i, last), 0, 0)),
            pl.BlockSpec((d // 2, PEER_TE), lambda j, i: (0, jnp.maximum(i - 1, 0))),
            pl.BlockSpec((tb, d), lambda j, i: (j, 0)),
            pl.BlockSpec((1, 8, d), lambda j, i: (j // blocks_per_batch, 0, 0)),
        ],
        out_specs=pl.BlockSpec((tb, d), lambda j, i: (j, 0)),
        out_shape=jax.ShapeDtypeStruct((ntok, d), F32),
        scratch_shapes=[
            pltpu.VMEM((d, tb), F32),
            pltpu.VMEM((PEER_TE, tb), BF16),
            pltpu.VMEM((PEER_TE, tb), BF16),
            pltpu.VMEM((PEER_TE, tb), BF16),
        ],
        compiler_params=_cparams(("parallel", "arbitrary"), PEER_G_FLAGS),
    )(htw, hinv, e1w, n1w, e2, r2, uw, uinv, vtw, xflat, mod)


def _final_norm_kernel(x_ref, g_ref, o_ref):
    x = x_ref[...]
    ms = jnp.mean(x * x, axis=-1, keepdims=True)
    o_ref[...] = x * lax.rsqrt(ms + EPS) * g_ref[...]


def _final_norm(xflat, g, tb=512):
    n, d = xflat.shape
    return pl.pallas_call(
        _final_norm_kernel,
        grid=(n // tb,),
        in_specs=[pl.BlockSpec((tb, d), lambda i: (i, 0)), _full_spec(g.shape)],
        out_specs=pl.BlockSpec((tb, d), lambda i: (i, 0)),
        out_shape=jax.ShapeDtypeStruct((n, d), F32),
        compiler_params=_cparams(("parallel",)),
    )(xflat, g)


def _pack_kernel(x_ref, o_ref, *, transpose):
    x = x_ref[...]
    if transpose:
        x = x.T
    o_ref[...] = pltpu.bitcast(x.astype(BF16), jnp.int32)


def _to_fp8_words(x):
    amax = jnp.max(jnp.abs(x), axis=(0, 1), keepdims=True)
    scale = FP8_TARGET / jnp.maximum(amax, FP8_TINY)
    words = pltpu.bitcast((x * scale).astype(F8), jnp.int32)
    return words, jnp.broadcast_to(1.0 / scale, (SUBLANES, LANE))


def _pack_fp8_kernel(x_ref, o_ref, inv_ref):
    o_ref[...], inv_ref[0] = _to_fp8_words(x_ref[...])


def _pack_fp8_tiles(x, layer, tile=1024):
    _, r, c = x.shape
    return pl.pallas_call(
        _pack_fp8_kernel,
        grid=(r // tile,),
        in_specs=[pl.BlockSpec((None, tile, c), lambda i: (layer, i, 0))],
        out_specs=[pl.BlockSpec((tile // 4, c), lambda i: (i, 0)),
                   pl.BlockSpec((1, SUBLANES, LANE), lambda i: (i, 0, 0))],
        out_shape=[jax.ShapeDtypeStruct((r // 4, c), jnp.int32),
                   jax.ShapeDtypeStruct((r // tile, SUBLANES, LANE), F32)],
        compiler_params=_cparams(("parallel",)),
    )(x)


def _pack_row_pairs(x, layer, transpose=False, tile=1024):
    _, r, c = x.shape
    if transpose:
        out_shape, out_spec = (c // 2, r), pl.BlockSpec((c // 2, tile), lambda i: (0, i))
    else:
        out_shape, out_spec = (r // 2, c), pl.BlockSpec((tile // 2, c), lambda i: (i, 0))
    return pl.pallas_call(
        functools.partial(_pack_kernel, transpose=transpose),
        grid=(r // tile,),
        in_specs=[pl.BlockSpec((None, tile, c), lambda i: (layer, i, 0))],
        out_specs=out_spec,
        out_shape=jax.ShapeDtypeStruct(out_shape, jnp.int32),
        compiler_params=_cparams(("parallel",)),
    )(x)


def _block_ones(n, blk):
    idx = np.arange(n) // blk
    return jnp.asarray((idx[:, None] == idx[None, :]).astype(np.float32), dtype=BF16)


def _s5_discretise(lam_re, lam_im, b_re, b_im, log_step):
    lam_re = jnp.minimum(lam_re.astype(F32), -1e-4)
    lam_im = lam_im.astype(F32)
    dt = jnp.exp(log_step.astype(F32))[:, None]
    mag = jnp.exp(lam_re * dt)
    a_re = mag * jnp.cos(lam_im * dt)
    a_im = mag * jnp.sin(lam_im * dt)
    den = lam_re * lam_re + lam_im * lam_im
    f_re = ((a_re - 1.0) * lam_re + a_im * lam_im) / den
    f_im = (a_im * lam_re - (a_re - 1.0) * lam_im) / den
    b_re = b_re.astype(F32)
    b_im = b_im.astype(F32)
    bb_re = f_re[..., None] * b_re - f_im[..., None] * b_im
    bb_im = f_re[..., None] * b_im + f_im[..., None] * b_re
    return a_re, a_im, bb_re, bb_im


def _group_block_diag(t):
    g, r, c = t.shape
    eye = jnp.eye(g, dtype=t.dtype)
    return (t[:, :, None, :] * eye[:, None, :, None]).reshape(g * r, g * c)


def _s5_params(lam_re, lam_im, b_re, b_im, c_re, c_im, log_step, nseq):
    b_rows, c_cols, ars, ais = [], [], [], []
    for d in range(2):
        a_re, a_im, bb_re, bb_im = _s5_discretise(lam_re[d], lam_im[d], b_re[d], b_im[d], log_step[d])
        bm = jnp.concatenate([_group_block_diag(jnp.swapaxes(bb_re, 1, 2)),
                              _group_block_diag(jnp.swapaxes(bb_im, 1, 2))], axis=1)
        b_rows.append(bm)
        cm = jnp.concatenate([_group_block_diag(jnp.swapaxes(c_re[d].astype(F32), 1, 2)),
                              -_group_block_diag(jnp.swapaxes(c_im[d].astype(F32), 1, 2))], axis=0)
        c_cols.append(cm)
        ars.append(jnp.broadcast_to(a_re.reshape(1, S5_LANES), (nseq, S5_LANES)))
        ais.append(jnp.broadcast_to(a_im.reshape(1, S5_LANES), (nseq, S5_LANES)))
    b2 = jnp.concatenate(b_rows, axis=0).astype(BF16)
    c2 = jnp.concatenate(c_cols, axis=1).astype(BF16)
    return b2, jnp.concatenate(ars, axis=0), jnp.concatenate(ais, axis=0), c2


def kernel(x, c, ctx, c_ctx, w_mod, b_mod, norm1_g, norm2_g, w_in, w_out, sgu_w, sgu_b, s5_lambda_re, s5_lambda_im, s5_b_re, s5_b_im, s5_c_re, s5_c_im, s5_log_step, s5_d, s5_w_glu, gla_w_gate, gla_b_gate, gla_norm_g, peer_w_query, peer_sub_keys, peer_expert_u, peer_expert_v, final_norm_g):
    nb, seq, d = x.shape
    c_len = ctx.shape[1]
    depth = w_mod.shape[0]

    cc = jnp.concatenate([c, c_ctx[None, :], jnp.zeros((8 - nb - 1, d), F32)], axis=0)
    mods = _mod_call(cc, w_mod, b_mod)

    ones_sgu = _block_ones(SGU_DIM, SGU_HEAD_DIM)
    ones_gla = _block_ones(GLA_DIM, GLA_DV)
    tri_np = np.tril(np.ones((GLA_CHUNK, GLA_CHUNK), np.float32))
    trif = jnp.asarray(tri_np, dtype=BF16)
    trib = jnp.asarray(tri_np.T, dtype=BF16)
    s_zero = jnp.zeros((nb, 2, GLA_DIM, GLA_KEY_DIM), F32)

    xl, xc = x, ctx
    for l in range(depth):
        ctx_out = l < depth - 1
        m6 = mods[l].reshape(8, N_MOD, d)
        mod_l = jnp.pad(m6[:nb], ((0, 0), (0, 2), (0, 0)))
        mod_c = jnp.broadcast_to(jnp.pad(m6[nb], ((0, 2), (0, 0)))[None], (nb, 8, d))

        win = jnp.pad(w_in[l], ((0, 0), (0, IN_PAD - IN_WIDTH))).astype(BF16)
        sguw = sgu_w[l].astype(BF16)
        sgub = jnp.repeat(jnp.swapaxes(sgu_b[l], 0, 1), SGU_HEAD_DIM, axis=1)
        gw = jnp.zeros((LANE, 2 * GLA_KEY_DIM), F32)
        gw = gw.at[0:GLA_RANK, 0:GLA_KEY_DIM].set(gla_w_gate[l, 0])
        gw = gw.at[GLA_RANK:2 * GLA_RANK, GLA_KEY_DIM:].set(gla_w_gate[l, 1]).astype(BF16)
        gb = gla_b_gate[l].reshape(1, 2 * GLA_KEY_DIM)
        g1 = norm1_g[l].reshape(1, d)

        s5_all = jnp.zeros((nb, seq + c_len, S5_DIM), F32)
        sgu_l, s5_all, qk_l, v_l, g_l, la_l = _stage_a(xl, mod_l, g1, win, sguw, sgub, ones_sgu, gw, gb,
                                                       s5_all, 0, tb=TB_LATENT, grid_layout=True)
        sgu_c, s5_all, qk_c, v_c, g_c, la_c = _stage_a(xc, mod_c, g1, win, sguw, sgub, ones_sgu, gw, gb,
                                                       s5_all, seq, tb=TB_CTX, grid_layout=False)

        b2, ar, ai, c2 = _s5_params(s5_lambda_re[l], s5_lambda_im[l], s5_b_re[l], s5_b_im[l],
                                    s5_c_re[l], s5_c_im[l], s5_log_step[l], nb)
        yf_all, yb_all = _s5_call(s5_all, seq, b2, ar, ai, c2)

        of_c, ob_c, s_ctx = _gla_call(qk_c, v_c, la_c, s_zero, trif, trib)
        of_l, ob_l, _ = _gla_call(qk_l, v_l, la_l, s_ctx, trif, trib)

        dskip = s5_d[l].reshape(1, S5_DIM)
        wglu = s5_w_glu[l].astype(BF16)
        normg = gla_norm_g[l].reshape(1, GLA_DIM)
        wout = w_out[l].astype(BF16)
        g2 = norm2_g[l].reshape(1, d)
        wqt = jnp.swapaxes(peer_w_query[l], 0, 1).astype(BF16)
        keys = peer_sub_keys[l].reshape(2 * PEER_HEADS, PEER_NKEYS, PEER_HALF).astype(BF16)
        uw, uinv = _pack_fp8_tiles(peer_expert_u, l, tile=PEER_TE)
        vtw = _pack_row_pairs(peer_expert_v, l, transpose=True)

        xl = _stage_e(xl, mod_l, sgu_l, yf_all, yb_all, s5_all, 0, of_l, ob_l, g_l,
                      dskip, wglu, normg, ones_gla, wout, tb=TB_LATENT)
        desc = _stage_f(xl, mod_l, g2, wqt, keys, tb=PEER_TBF)
        xl = _stage_g(*desc, uw, uinv, vtw, xl.reshape(nb * seq, d), mod_l, seq, PEER_TBG).reshape(nb, seq, d)

        if ctx_out:
            xc = _stage_e(xc, mod_c, sgu_c, yf_all, yb_all, s5_all, seq, of_c, ob_c, g_c,
                          dskip, wglu, normg, ones_gla, wout, tb=TB_CTX)
            desc = _stage_f(xc, mod_c, g2, wqt, keys, tb=PEER_TBF)
            xc = _stage_g(*desc, uw, uinv, vtw, xc.reshape(nb * c_len, d), mod_c, c_len,
                          min(PEER_TBG, c_len)).reshape(nb, c_len, d)

    return _final_norm(xl.reshape(nb * seq, d), final_norm_g.reshape(1, d)).reshape(nb, seq, d)
```

```python
import functools
import math

import numpy as np
import jax
import jax.numpy as jnp
from jax import lax
from jax.experimental import pallas as pl
from jax.experimental.pallas import tpu as pltpu

F32 = jnp.float32
BF16 = jnp.bfloat16
SUBLANES = 8

EPS = 1e-6
N_MOD = 6
GRID_W = 64

SGU_DIM = 256
SGU_HEADS = 4
SGU_HEAD_DIM = 64
SGU_CHUNK = 128

S5_DIM = 256
S5_GROUP = 16
S5_GROUPS = 16
S5_STATE = 64
S5_LANES = S5_GROUPS * S5_STATE

GLA_DIM = 512
GLA_HEADS = 8
GLA_DV = 64
GLA_DK = 32
GLA_KEY_DIM = 256
GLA_RANK = 16
GLA_GATE_TEMP = 16.0
GLA_CHUNK = 64

PEER_HEADS = 8
PEER_NKEYS = 128
PEER_HALF = 128
PEER_TOPK = 16

IN_WIDTH = 2336
IN_PAD = 2432
LANE = 128

VMEM_LIMIT = 56 * 1024 * 1024
TB_LATENT = 512
TB_CTX = 256

NEG_INF = float("-inf")
POS_INF = float("inf")


def _cparams(sem, flags=None):
    return pltpu.CompilerParams(dimension_semantics=sem, vmem_limit_bytes=VMEM_LIMIT, flags=flags)


def _gelu(x):
    c = math.sqrt(2.0 / math.pi)
    return 0.5 * x * (1.0 + jnp.tanh(c * (x + 0.044715 * (x * x * x))))


def _sigmoid(x):
    return 1.0 / (1.0 + jnp.exp(-x))


def _dot(a, b):
    return jnp.dot(a, b, preferred_element_type=F32)


def _dot_nt(a, b):
    return lax.dot_general(a, b, (((1,), (1,)), ((), ())), preferred_element_type=F32)


def _dot_tn(a, b):
    return lax.dot_general(a, b, (((0,), (0,)), ((), ())), preferred_element_type=F32)


def _split3(x):
    hi = x.astype(BF16)
    r = x - hi.astype(F32)
    mid = r.astype(BF16)
    lo = (r - mid.astype(F32)).astype(BF16)
    return hi, mid, lo


def _dot_x_exact(x, m):
    hi, mid, lo = _split3(x)
    return _dot(hi, m) + _dot(mid, m) + _dot(lo, m)


def _dot_m_exact(m, x):
    hi, mid, lo = _split3(x)
    return _dot(m, hi) + _dot(m, mid) + _dot(m, lo)


def _full_spec(shape):
    nd = len(shape)
    return pl.BlockSpec(shape, lambda *_: (0,) * nd)


MOD_TILE = 512


def _mod_kernel(c_ref, w_ref, b_ref, o_ref):
    c = c_ref[...]
    a = c * _sigmoid(c)
    o_ref[0] = jnp.dot(a, w_ref[0], preferred_element_type=F32,
                       precision=lax.Precision.HIGHEST) + b_ref[0]


def _mod_call(cc, w_mod, b_mod):
    depth, d, nd = w_mod.shape
    rows = cc.shape[0]
    return pl.pallas_call(
        _mod_kernel,
        grid=(depth, nd // MOD_TILE),
        in_specs=[
            pl.BlockSpec((rows, d), lambda l, j: (0, 0)),
            pl.BlockSpec((1, d, MOD_TILE), lambda l, j: (l, 0, j)),
            pl.BlockSpec((1, 1, MOD_TILE), lambda l, j: (l, 0, j)),
        ],
        out_specs=pl.BlockSpec((1, rows, MOD_TILE), lambda l, j: (l, 0, j)),
        out_shape=jax.ShapeDtypeStruct((depth, rows, nd), F32),
        compiler_params=_cparams(("parallel", "parallel")),
    )(cc, w_mod, b_mod.reshape(depth, 1, nd))


def _store_tokens(ref, val, grid_layout):
    if grid_layout:
        for r in range(val.shape[0] // GRID_W):
            ref[:, r, :] = val[r * GRID_W:(r + 1) * GRID_W]
    else:
        ref[0] = val


def _load_tokens(ref, grid_layout):
    if grid_layout:
        return jnp.concatenate([ref[:, r, :] for r in range(ref.shape[1])], axis=0)
    return ref[0]


def _stage_a_kernel(x_ref, mod_ref, g1_ref, win_ref, sguw_ref, sgub_ref, ones_ref, gw_ref, gb_ref, s5_in_ref,
                    sgu_ref, s5x_ref, qk_ref, v_ref, g_ref, la_ref, *, tb, grid_layout):
    del s5_in_ref
    x = x_ref[0]
    ms = jnp.mean(x * x, axis=-1, keepdims=True)
    xn = x * lax.rsqrt(ms + EPS) * g1_ref[...]
    h = xn * (1.0 + mod_ref[0, 1:2, :]) + mod_ref[0, 0:1, :]
    cols = _dot(h.astype(BF16), win_ref[...])

    u = _gelu(cols[:, 0:SGU_DIM])
    v = _gelu(cols[:, SGU_DIM:2 * SGU_DIM])
    msq = _dot_x_exact(v * v, ones_ref[...]) * (1.0 / SGU_HEAD_DIM)
    vn = (v * lax.rsqrt(msq + EPS)).astype(BF16)
    head_of_lane = lax.broadcasted_iota(jnp.int32, (SGU_CHUNK, SGU_DIM), 1) // SGU_HEAD_DIM
    for ci in range(tb // SGU_CHUNK):
        rows = slice(ci * SGU_CHUNK, (ci + 1) * SGU_CHUNK)
        vc = vn[rows]
        mixed = sgub_ref[...]
        for hh in range(SGU_HEADS):
            mixed = mixed + jnp.where(head_of_lane == hh, _dot(sguw_ref[hh], vc), 0.0)
        sgu_ref[0, rows, :] = u[rows] * mixed

    s5x_ref[0] = cols[:, 512:768]
    q = cols[:, 768:1024] * (GLA_DK ** -0.5)
    _store_tokens(qk_ref, jnp.concatenate([q, cols[:, 1024:1280]], axis=-1), grid_layout)
    _store_tokens(v_ref, cols[:, 1280:1792], grid_layout)
    g_ref[0] = cols[:, 1792:2304]

    z = cols[:, 2304:2432].astype(BF16)
    za = _dot(z, gw_ref[...]) + gb_ref[...]
    log_sig = jnp.minimum(za, 0.0) - jnp.log1p(jnp.exp(-jnp.abs(za)))
    _store_tokens(la_ref, log_sig * (1.0 / GLA_GATE_TEMP), grid_layout)


def _stage_a(xs, mod, g1, win, sguw, sgub, ones_sgu, gw, gb, s5_all, s5_row0, tb, grid_layout):
    b, l, d = xs.shape
    assert s5_row0 % tb == 0
    kern = functools.partial(_stage_a_kernel, tb=tb, grid_layout=grid_layout)
    tok = lambda w: pl.BlockSpec((1, tb, w), lambda bi, i: (bi, i, 0))
    if grid_layout:
        assert tb % GRID_W == 0 and l % GRID_W == 0
        gla = lambda w: pl.BlockSpec((GRID_W, None, tb // GRID_W, w), lambda bi, i: (0, bi, i, 0))
        gla_shape = lambda w: jax.ShapeDtypeStruct((GRID_W, b, l // GRID_W, w), F32)
    else:
        gla = tok
        gla_shape = lambda w: jax.ShapeDtypeStruct((b, l, w), F32)
    nat_shape = lambda w: jax.ShapeDtypeStruct((b, l, w), F32)
    return pl.pallas_call(
        kern,
        grid=(b, l // tb),
        in_specs=[
            tok(d),
            pl.BlockSpec((1, 8, d), lambda bi, i: (bi, 0, 0)),
            _full_spec(g1.shape), _full_spec(win.shape), _full_spec(sguw.shape), _full_spec(sgub.shape),
            _full_spec(ones_sgu.shape), _full_spec(gw.shape), _full_spec(gb.shape),
            pl.BlockSpec(memory_space=pl.ANY),
        ],
        out_specs=[tok(SGU_DIM),
                   pl.BlockSpec((1, tb, S5_DIM), lambda bi, i: (bi, s5_row0 // tb + i, 0)),
                   gla(2 * GLA_KEY_DIM), gla(GLA_DIM), tok(GLA_DIM), gla(2 * GLA_KEY_DIM)],
        out_shape=[nat_shape(SGU_DIM), jax.ShapeDtypeStruct(s5_all.shape, F32),
                   gla_shape(2 * GLA_KEY_DIM), gla_shape(GLA_DIM), nat_shape(GLA_DIM), gla_shape(2 * GLA_KEY_DIM)],
        input_output_aliases={9: 1},
        compiler_params=_cparams(("parallel", "parallel")),
    )(xs, mod, g1, win, sguw, sgub, ones_sgu, gw, gb, s5_all)


S5_TC = 128


def _s5_kernel(uf_ref, ub_ref, perm_ref, permt_ref, b2_ref, ar_ref, ai_ref, c2_ref, yf_ref, yb_ref,
               h_ref, buf0_ref, buf1_ref, buf2_ref, *, nseq):
    tc = S5_TC
    rows = tc * 2 * nseq
    s = pl.program_id(0)

    @pl.when(s == 0)
    def _():
        h_ref[...] = jnp.zeros_like(h_ref)
        buf0_ref[...] = jnp.zeros_like(buf0_ref)
        buf1_ref[...] = jnp.zeros_like(buf1_ref)
        buf2_ref[...] = jnp.zeros_like(buf2_ref)

    def phase(scan_ref, fill_ref, read_ref):
        ar = ar_ref[...]
        ai = ai_ref[...]
        hr, hi = h_ref[:, 0:S5_LANES], h_ref[:, S5_LANES:]
        for t in range(tc):
            bur = scan_ref[t, :, 0:S5_LANES]
            bui = scan_ref[t, :, S5_LANES:]
            hr, hi = ar * hr - ai * hi + bur, ar * hi + ai * hr + bui
            scan_ref[t, :, 0:S5_LANES] = hr
            scan_ref[t, :, S5_LANES:] = hi
        h_ref[:, 0:S5_LANES] = hr
        h_ref[:, S5_LANES:] = hi

        fwd_row = lax.broadcasted_iota(jnp.int32, (rows, S5_DIM), 0) % (2 * nseq) < nseq
        hs = read_ref[...].reshape(rows, 2 * S5_LANES).astype(BF16)
        y2 = _dot(hs, c2_ref[...])
        y = jnp.where(fwd_row, y2[:, 0:S5_DIM], y2[:, S5_DIM:])
        y_hi = y.astype(BF16)
        y_lo = (y - y_hi.astype(F32)).astype(BF16)
        y_nat = _dot(permt_ref[...], y_hi) + _dot(permt_ref[...], y_lo)
        for b in range(nseq):
            yf_ref[b] = y_nat[b * tc:(b + 1) * tc]
            yb_ref[b] = y_nat[(nseq + b) * tc:(nseq + b + 1) * tc]

        x = jnp.concatenate([uf_ref[b] for b in range(nseq)] + [ub_ref[b] for b in range(nseq)], axis=0)
        u_tm = _dot(perm_ref[...], x.astype(BF16))
        u = jnp.concatenate([jnp.where(fwd_row, u_tm, 0.0), jnp.where(fwd_row, 0.0, u_tm)], axis=-1).astype(BF16)
        fill_ref[...] = _dot(u, b2_ref[...]).reshape(tc, 2 * nseq, 2 * S5_LANES)

    bufs = (buf0_ref, buf1_ref, buf2_ref)
    for k in range(3):
        @pl.when(s % 3 == k)
        def _(k=k):
            phase(bufs[(k + 2) % 3], bufs[k], bufs[(k + 1) % 3])


def _s5_permutation(nseq):
    tc = S5_TC
    p = np.zeros((tc * 2 * nseq, tc * 2 * nseq), np.float32)
    for q in range(2 * nseq):
        for t in range(tc):
            p[t * 2 * nseq + q, q * tc + (t if q < nseq else tc - 1 - t)] = 1.0
    return p


def _s5_call(s5_all, n_lat, b2, ar, ai, c2):
    nseq, t, _ = s5_all.shape
    n = t // S5_TC
    n_l = n_lat // S5_TC
    n_c = n - n_l
    perm = _s5_permutation(nseq)
    kern = functools.partial(_s5_kernel, nseq=nseq)
    fwd_blk = lambda k: jnp.where(k < n_c, n_l + k, k - n_c)
    bwd_blk = lambda k: n - 1 - k
    in_k = lambda s: jnp.minimum(s, n - 1)
    out_k = lambda s: jnp.clip(s - 2, 0, n - 1)
    blk = (nseq, S5_TC, S5_DIM)
    rows = 2 * nseq
    buf = pltpu.VMEM((S5_TC, rows, 2 * S5_LANES), F32)
    return pl.pallas_call(
        kern,
        grid=(n + 2,),
        in_specs=[
            pl.BlockSpec(blk, lambda s: (0, fwd_blk(in_k(s)), 0)), pl.BlockSpec(blk, lambda s: (0, bwd_blk(in_k(s)), 0)),
            _full_spec(perm.shape), _full_spec(perm.shape),
            _full_spec(b2.shape), _full_spec(ar.shape), _full_spec(ai.shape), _full_spec(c2.shape),
        ],
        out_specs=[pl.BlockSpec(blk, lambda s: (0, fwd_blk(out_k(s)), 0)),
                   pl.BlockSpec(blk, lambda s: (0, bwd_blk(out_k(s)), 0))],
        out_shape=[jax.ShapeDtypeStruct(s5_all.shape, F32), jax.ShapeDtypeStruct(s5_all.shape, F32)],
        scratch_shapes=[pltpu.VMEM((rows, 2 * S5_LANES), F32), buf, buf, buf],
        compiler_params=_cparams(("arbitrary",)),
    )(s5_all, s5_all, jnp.asarray(perm, dtype=BF16), jnp.asarray(perm.T, dtype=BF16), b2, ar, ai, c2)


def _gla_kernel(qkf_ref, vf_ref, laf_ref, qkb_ref, vb_ref, lab_ref, s0_ref, trif_ref, trib_ref,
                of_ref, ob_ref, sout_ref, s_ref, *, nb):
    c = pl.program_id(0)

    @pl.when(c == 0)
    def _():
        s_ref[...] = s0_ref[...]

    ch = GLA_CHUNK
    r_k = lax.broadcasted_iota(jnp.int32, (GLA_HEADS * ch, GLA_KEY_DIM), 0) // ch
    c_k = lax.broadcasted_iota(jnp.int32, (GLA_HEADS * ch, GLA_KEY_DIM), 1) // GLA_DK
    hm_k = r_k == c_k
    r_v = lax.broadcasted_iota(jnp.int32, (GLA_HEADS * ch, GLA_DIM), 0) // ch
    c_v = lax.broadcasted_iota(jnp.int32, (GLA_HEADS * ch, GLA_DIM), 1) // GLA_DV
    hm_v = r_v == c_v
    r_s = lax.broadcasted_iota(jnp.int32, (GLA_DIM, GLA_KEY_DIM), 0) // GLA_DV
    c_s = lax.broadcasted_iota(jnp.int32, (GLA_DIM, GLA_KEY_DIM), 1) // GLA_DK
    hm_s = r_s == c_s
    t_i = lax.broadcasted_iota(jnp.int32, (ch, GLA_HEADS * ch), 0)
    s_i = lax.broadcasted_iota(jnp.int32, (ch, GLA_HEADS * ch), 1) % ch
    mask_f = t_i >= s_i
    mask_b = t_i <= s_i
    trif = trif_ref[...]
    trib = trib_ref[...]

    fwd = dict(qk=qkf_ref, v=vf_ref, la=laf_ref, o=of_ref, tri=trif, last=ch - 1, ref=ch // 2, mask=mask_f, d=0)
    bwd = dict(qk=qkb_ref, v=vb_ref, la=lab_ref, o=ob_ref, tri=trib, last=0, ref=ch - 1 - ch // 2, mask=mask_b, d=1)
    streams = [(b, p) for b in range(nb) for p in (fwd, bwd)]

    bcums = [_dot_m_exact(p["tri"], p["la"][b]) for b, p in streams]
    scaled = []
    for (b, p), bcum in zip(streams, bcums):
        qk = p["qk"][b]
        q, k = qk[:, 0:GLA_KEY_DIM], qk[:, GLA_KEY_DIM:]
        blast = bcum[p["last"]:p["last"] + 1]
        bref = bcum[p["ref"]:p["ref"] + 1]
        qe = (q * jnp.exp(bcum)).astype(BF16)
        qd = (q * jnp.exp(bcum - bref)).astype(BF16)
        kd = k * jnp.exp(bref - bcum)
        kdec = (k * jnp.exp(blast - bcum)).astype(BF16)
        kst = jnp.where(hm_k, jnp.concatenate([kd] * GLA_HEADS, axis=0), 0.0).astype(BF16)
        scaled.append((qe, qd, kdec, kst, jnp.exp(blast)))
    prods = []
    for (b, p), (qe, qd, kdec, kst, _) in zip(streams, scaled):
        v = p["v"][b]
        sc = _dot_nt(qd, kst)
        o_inter = _dot_nt(qe, s_ref[b, p["d"]].astype(BF16))
        kv_t = _dot_tn(v.astype(BF16), kdec)
        prods.append((sc, o_inter, kv_t))
    for (b, p), (_, _, _, _, decay), (sc, o_inter, kv_t) in zip(streams, scaled, prods):
        vbd = jnp.where(hm_v, jnp.concatenate([p["v"][b]] * GLA_HEADS, axis=0), 0.0).astype(BF16)
        p["o"][b] = _dot(jnp.where(p["mask"], sc, 0.0).astype(BF16), vbd) + o_inter
        s_ref[b, p["d"]] = s_ref[b, p["d"]] * decay + jnp.where(hm_s, kv_t, 0.0)

    @pl.when(c == pl.num_programs(0) - 1)
    def _():
        sout_ref[...] = s_ref[...]


def _gla_call(qk, v, la, s0, trif, trib):
    ch = GLA_CHUNK
    if qk.ndim == 4:
        n, b = qk.shape[0], qk.shape[1]
        assert qk.shape[2] == ch
        spec = lambda w, off, rev: pl.BlockSpec(
            (None, b, ch, w), (lambda c: (n - 1 - c, 0, 0, off)) if rev else (lambda c: (c, 0, 0, off)))
    else:
        b = qk.shape[0]
        n = qk.shape[1] // ch
        spec = lambda w, off, rev: pl.BlockSpec(
            (b, ch, w), (lambda c: (0, n - 1 - c, off)) if rev else (lambda c: (0, c, off)))
    kern = functools.partial(_gla_kernel, nb=b)
    o_shape = jax.ShapeDtypeStruct(v.shape, F32)
    return pl.pallas_call(
        kern,
        grid=(n,),
        in_specs=[
            spec(2 * GLA_KEY_DIM, 0, False), spec(GLA_DIM, 0, False), spec(GLA_KEY_DIM, 0, False),
            spec(2 * GLA_KEY_DIM, 0, True), spec(GLA_DIM, 0, True), spec(GLA_KEY_DIM, 1, True),
            _full_spec(s0.shape), _full_spec(trif.shape), _full_spec(trib.shape),
        ],
        out_specs=[spec(GLA_DIM, 0, False), spec(GLA_DIM, 0, True), _full_spec(s0.shape)],
        out_shape=[o_shape, o_shape, jax.ShapeDtypeStruct(s0.shape, F32)],
        scratch_shapes=[pltpu.VMEM(s0.shape, F32)],
        compiler_params=_cparams(("arbitrary",)),
    )(qk, v, la, qk, v, la, s0, trif, trib)


def _stage_e_kernel(x_ref, mod_ref, sgu_ref, yf_ref, yb_ref, s5x_ref, of_ref, ob_ref, g_ref,
                    dskip_ref, wglu_ref, normg_ref, ones_ref, wout_ref, o_ref, *, grid_layout):
    ys = yf_ref[0] + yb_ref[0] + dskip_ref[...] * s5x_ref[0]
    z = _dot(_gelu(ys).astype(BF16), wglu_ref[...])
    s5o = z[:, 0:S5_DIM] * _sigmoid(z[:, S5_DIM:])
    o = _load_tokens(of_ref, grid_layout) + _load_tokens(ob_ref, grid_layout)
    ms = _dot_x_exact(o * o, ones_ref[...]) * (1.0 / GLA_DV)
    g = g_ref[0]
    gl = o * lax.rsqrt(ms + EPS) * normg_ref[...] * (g * _sigmoid(g))
    y = (_dot(sgu_ref[0].astype(BF16), wout_ref[0:SGU_DIM, :])
         + _dot(s5o.astype(BF16), wout_ref[SGU_DIM:SGU_DIM + S5_DIM, :])
         + _dot(gl.astype(BF16), wout_ref[SGU_DIM + S5_DIM:, :]))
    o_ref[0] = x_ref[0] + mod_ref[0, 2:3, :] * y


def _stage_e(xs, mod, sgu, yf_all, yb_all, s5_all, s5_row0, of, ob, g, dskip, wglu, normg, ones_gla, wout, tb):
    b, l, d = xs.shape
    assert s5_row0 % tb == 0
    grid_layout = of.ndim == 4
    tok = lambda w: pl.BlockSpec((1, tb, w), lambda bi, i: (bi, i, 0))
    s5 = pl.BlockSpec((1, tb, S5_DIM), lambda bi, i: (bi, s5_row0 // tb + i, 0))
    if grid_layout:
        assert tb % GRID_W == 0
        gla = pl.BlockSpec((GRID_W, None, tb // GRID_W, GLA_DIM), lambda bi, i: (0, bi, i, 0))
    else:
        gla = tok(GLA_DIM)
    return pl.pallas_call(
        functools.partial(_stage_e_kernel, grid_layout=grid_layout),
        grid=(b, l // tb),
        in_specs=[
            tok(d), pl.BlockSpec((1, 8, d), lambda bi, i: (bi, 0, 0)),
            tok(SGU_DIM), s5, s5, s5, gla, gla, tok(GLA_DIM),
            _full_spec(dskip.shape), _full_spec(wglu.shape), _full_spec(normg.shape),
            _full_spec(ones_gla.shape), _full_spec(wout.shape),
        ],
        out_specs=tok(d),
        out_shape=jax.ShapeDtypeStruct((b, l, d), F32),
        compiler_params=_cparams(("parallel", "parallel")),
    )(xs, mod, sgu, yf_all, yb_all, s5_all, of, ob, g, dskip, wglu, normg, ones_gla, wout)


PEER_TBF = 256


def _sort_network_16():
    def merge(lo, hi, r):
        step = r * 2
        if step < hi - lo:
            yield from merge(lo, hi, step)
            yield from merge(lo + r, hi, step)
            for i in range(lo + r, hi - r, step):
                yield (i, i + r)
        else:
            yield (lo, lo + r)

    def sort(lo, hi):
        if hi - lo >= 1:
            mid = lo + (hi - lo) // 2
            yield from sort(lo, mid)
            yield from sort(mid + 1, hi)
            yield from merge(lo, hi, 1)

    return tuple(sort(0, PEER_TOPK - 1))


SORT16 = _sort_network_16()
BITONIC16 = tuple((k, k + s) for s in (8, 4, 2, 1) for k in range(PEER_TOPK) if not k & s)


def _compare_exchange(xs, pairs):
    xs = list(xs)
    for i, j in pairs:
        hi = jnp.maximum(xs[i], xs[j])
        lo = jnp.minimum(xs[i], xs[j])
        xs[i], xs[j] = hi, lo
    return xs


def _merge_sublanes(xs):
    for shift in (4, 6, 7):
        rolled = [pltpu.roll(x, shift, 0) for x in xs]
        xs = [jnp.maximum(xs[k], rolled[PEER_TOPK - 1 - k]) for k in range(PEER_TOPK)]
        xs = _compare_exchange(xs, BITONIC16)
    return xs


def _dup_bf16_words(x):
    bits = pltpu.bitcast(x.astype(BF16).astype(F32), jnp.int32)
    return bits | lax.shift_right_logical(bits, 16)


def _stage_f_kernel(x_ref, mod_ref, g2_ref, wqt_ref, keys_ref,
                    ht_ref, e1w_ref, n1w_ref, e2_ref, r2_ref, qt_ref, v1_ref, v2_ref, *, tb):
    x = x_ref[0]
    ms = jnp.mean(x * x, axis=-1, keepdims=True)
    xn = x * lax.rsqrt(ms + EPS) * g2_ref[...]
    h = xn * (1.0 + mod_ref[0, 4:5, :]) + mod_ref[0, 3:4, :]
    ht = h.T.astype(BF16)
    ht_ref[...] = pltpu.bitcast(ht, jnp.int32)
    qt_ref[...] = _dot(wqt_ref[...], ht)
    k_top = PEER_TOPK

    def tiles(s):
        return [s[SUBLANES * k:SUBLANES * (k + 1)] for k in range(PEER_NKEYS // SUBLANES)]

    def head_body(hh, carry):
        for tc in range(tb // LANE):
            tcol = slice(tc * LANE, (tc + 1) * LANE)
            r1 = hh * (2 * PEER_HALF)
            q1 = qt_ref[r1:r1 + PEER_HALF, tcol].astype(BF16)
            q2 = qt_ref[r1 + PEER_HALF:r1 + 2 * PEER_HALF, tcol].astype(BF16)
            s1 = _dot(keys_ref[hh], q1)
            s2 = _dot(keys_ref[PEER_HEADS + hh], q2)
            for s, v_ref in ((s1, v1_ref), (s2, v2_ref)):
                top = _merge_sublanes(_compare_exchange(tiles(s), SORT16))
                for k in range(k_top):
                    v_ref[k:k + 1, tcol] = top[k][0:1]
            v1row = lambda a: v1_ref[a:a + 1, tcol]
            v2row = lambda b: v2_ref[b:b + 1, tcol]
            v1lo = v1_ref[0:SUBLANES, tcol]
            v2lo = v2_ref[0:SUBLANES, tcol]
            cand = [v1lo + v2row(b) for b in range(k_top)]
            tail = [v1row(a) + v2lo for a in range(SUBLANES, k_top)]
            for k in range(SUBLANES, k_top):
                cand[k] = jnp.maximum(cand[k], tail[k_top - 1 - k])
            best = _merge_sublanes(_compare_exchange(cand, BITONIC16))
            theta = best[k_top - 1][0:1]
            cmax = best[0][0:1]
            zsum = jnp.zeros((1, LANE), F32)
            for k in range(k_top):
                zsum = zsum + jnp.exp(best[k][0:1] - cmax)
            rz = 1.0 / zsum
            n_top = jnp.zeros((1, LANE), F32)
            for b in range(k_top):
                n_top = jnp.where(v1row(0) + v2row(b) >= theta, float(b + 1), n_top)
            n1 = jnp.zeros(s1.shape, F32)
            for b in range(SUBLANES):
                n1 = jnp.where(s1 + v2row(b) >= theta, float(b + 1), n1)
            n1 = jnp.where(s1 >= v1row(0), n_top, n1)
            r2 = jnp.zeros(s2.shape, F32)
            for b in range(k_top):
                r2 = jnp.where(v2row(b) > s2, float(b + 1), r2)
            e1 = jnp.where(s1 >= v1row(k_top - 1), jnp.exp(s1 - v1row(0)), 0.0) * rz
            e2 = jnp.where(s2 >= v2row(k_top - 1), jnp.exp(s2 - v2row(0)), 0.0)
            e1w_ref[tc, hh] = _dup_bf16_words(e1)
            n1w_ref[tc, hh] = _dup_bf16_words(n1)
            e2_ref[tc, hh] = pltpu.bitcast(e2.astype(BF16), jnp.int32)
            r2_ref[tc, hh] = pltpu.bitcast(r2.astype(BF16), jnp.int32)
        return carry

    for hh in range(PEER_HEADS):
        head_body(hh, 0)


def _stage_f(xs, mod, g2, wqt, keys, tb):
    b, l, d = xs.shape
    nblk = l // tb
    ntok = b * l
    nch = ntok // LANE
    kern = functools.partial(_stage_f_kernel, tb=tb)
    row_spec = pl.BlockSpec((tb // LANE, PEER_HEADS, PEER_NKEYS, LANE), lambda bi, i: (bi * nblk + i, 0, 0, 0))
    pair_spec = pl.BlockSpec((tb // LANE, PEER_HEADS, PEER_NKEYS // 2, LANE), lambda bi, i: (bi * nblk + i, 0, 0, 0))
    desc_shape = lambda rows: jax.ShapeDtypeStruct((nch, PEER_HEADS, rows, LANE), jnp.int32)
    return pl.pallas_call(
        kern,
        grid=(b, nblk),
        in_specs=[
            pl.BlockSpec((1, tb, d), lambda bi, i: (bi, i, 0)),
            pl.BlockSpec((1, 8, d), lambda bi, i: (bi, 0, 0)),
            _full_spec(g2.shape), _full_spec(wqt.shape), _full_spec(keys.shape),
        ],
        out_specs=[pl.BlockSpec((d // 2, tb), lambda bi, i: (0, bi * nblk + i)),
                   row_spec, row_spec, pair_spec, pair_spec],
        out_shape=[jax.ShapeDtypeStruct((d // 2, ntok), jnp.int32), desc_shape(PEER_NKEYS), desc_shape(PEER_NKEYS),
                   desc_shape(PEER_NKEYS // 2), desc_shape(PEER_NKEYS // 2)],
        scratch_shapes=[
            pltpu.VMEM((PEER_HEADS * 2 * PEER_HALF, tb), F32),
            pltpu.VMEM((PEER_TOPK, tb), F32),
            pltpu.VMEM((PEER_TOPK, tb), F32),
        ],
        compiler_params=_cparams(("parallel", "parallel")),
    )(xs, mod, g2, wqt, keys)


PEER_TBG = 1024
PEER_TE = 1024
PEER_I1_PER_TILE = PEER_TE // PEER_NKEYS
PEER_N_TILES = PEER_NKEYS * PEER_NKEYS // PEER_TE
PEER_MXU_COLS = 256
PEER_G_FLAGS = None


def _stage_g_kernel(htw_ref, e1w_ref, n1w_ref, e2_ref, r2_ref, uw_ref, vtw_ref, x_ref, mod_ref,
                    o_ref, acc_ref, ata_ref, atb_ref, p_ref, *, tb):
    s = pl.program_id(1)

    @pl.when(s == 0)
    def _():
        acc_ref[...] = jnp.zeros_like(acc_ref)

    def step(at_cur_ref, at_next_ref):
        per_grp = PEER_MXU_COLS // LANE
        for grp in range(tb // PEER_MXU_COLS):
            cols = slice(grp * PEER_MXU_COLS, (grp + 1) * PEER_MXU_COLS)
            for tcl in range(per_grp if at_cur_ref is not None else 0):
                tc = grp * per_grp + tcl
                tcol = slice(tc * LANE, (tc + 1) * LANE)
                for i1l in range(PEER_I1_PER_TILE):
                    rows = slice(i1l * PEER_NKEYS, (i1l + 1) * PEER_NKEYS)
                    gate = jnp.zeros((PEER_NKEYS, LANE), BF16)
                    for hh in range(PEER_HEADS):
                        e1row = e1w_ref[tc, hh, i1l:i1l + 1, :]
                        n1row = n1w_ref[tc, hh, i1l:i1l + 1, :]
                        e1 = pltpu.bitcast(jnp.broadcast_to(e1row, (PEER_NKEYS // 2, LANE)), BF16)
                        n1 = pltpu.bitcast(jnp.broadcast_to(n1row, (PEER_NKEYS // 2, LANE)), BF16)
                        r2 = pltpu.bitcast(r2_ref[tc, hh], BF16)
                        e2 = pltpu.bitcast(e2_ref[tc, hh], BF16)
                        gate = gate + e2 * jnp.where(r2 < n1, e1, 0.0)
                    p_ref[rows, tcol] = gate * _gelu(at_cur_ref[rows, tcol])
            if at_cur_ref is not None:
                vt = pltpu.bitcast(vtw_ref[...], BF16)
                acc_ref[:, cols] += _dot(vt, p_ref[:, cols])
            if at_next_ref is not None:
                u = pltpu.bitcast(uw_ref[...], BF16)
                ht = pltpu.bitcast(htw_ref[:, cols], BF16)
                at_next_ref[:, cols] = _dot(u, ht).astype(BF16)

    last = pl.num_programs(1) - 1

    @pl.when(s == 0)
    def _():
        step(None, ata_ref)

    @pl.when(jnp.logical_and(s % 2 == 0, jnp.logical_and(s > 0, s < last)))
    def _():
        step(atb_ref, ata_ref)

    @pl.when(s % 2 == 1)
    def _():
        step(ata_ref, atb_ref)

    @pl.when(s == last)
    def _():
        step(atb_ref, None)
        o_ref[...] = x_ref[...] + mod_ref[0, 5:6, :] * acc_ref[...].T


def _stage_g(htw, e1w, n1w, e2, r2, uw, vtw, xflat, mod, tokens_per_batch, tb):
    ntok = htw.shape[1]
    d = 2 * htw.shape[0]
    assert 2 * uw.shape[0] == PEER_N_TILES * PEER_TE and PEER_N_TILES % 2 == 0
    blocks_per_batch = tokens_per_batch // tb
    kern = functools.partial(_stage_g_kernel, tb=tb)
    last = PEER_N_TILES - 1
    desc_spec = pl.BlockSpec((tb // LANE, PEER_HEADS, PEER_NKEYS // 2, LANE), lambda j, i: (j, 0, 0, 0))
    row_spec = pl.BlockSpec((tb // LANE, PEER_HEADS, PEER_I1_PER_TILE, LANE),
                            lambda j, i: (j, 0, jnp.maximum(i - 1, 0), 0))
    return pl.pallas_call(
        kern,
        grid=(ntok // tb, PEER_N_TILES + 1),
        in_specs=[
            pl.BlockSpec((d // 2, tb), lambda j, i: (0, j)),
            row_spec, row_spec, desc_spec, desc_spec,
            pl.BlockSpec((PEER_TE // 2, d), lambda j, i: (jnp.minimum(i, last), 0)),
            pl.BlockSpec((d // 2, PEER_TE), lambda j, i: (0, jnp.maximum(i - 1, 0))),
            pl.BlockSpec((tb, d), lambda j, i: (j, 0)),
            pl.BlockSpec((1, 8, d), lambda j, i: (j // blocks_per_batch, 0, 0)),
        ],
        out_specs=pl.BlockSpec((tb, d), lambda j, i: (j, 0)),
        out_shape=jax.ShapeDtypeStruct((ntok, d), F32),
        scratch_shapes=[
            pltpu.VMEM((d, tb), F32),
            pltpu.VMEM((PEER_TE, tb), BF16),
            pltpu.VMEM((PEER_TE, tb), BF16),
            pltpu.VMEM((PEER_TE, tb), BF16),
        ],
        compiler_params=_cparams(("parallel", "arbitrary"), PEER_G_FLAGS),
    )(htw, e1w, n1w, e2, r2, uw, vtw, xflat, mod)


def _final_norm_kernel(x_ref, g_ref, o_ref):
    x = x_ref[...]
    ms = jnp.mean(x * x, axis=-1, keepdims=True)
    o_ref[...] = x * lax.rsqrt(ms + EPS) * g_ref[...]


def _final_norm(xflat, g, tb=512):
    n, d = xflat.shape
    return pl.pallas_call(
        _final_norm_kernel,
        grid=(n // tb,),
        in_specs=[pl.BlockSpec((tb, d), lambda i: (i, 0)), _full_spec(g.shape)],
        out_specs=pl.BlockSpec((tb, d), lambda i: (i, 0)),
        out_shape=jax.ShapeDtypeStruct((n, d), F32),
        compiler_params=_cparams(("parallel",)),
    )(xflat, g)


def _pack_kernel(x_ref, o_ref, *, transpose):
    x = x_ref[...]
    if transpose:
        x = x.T
    o_ref[...] = pltpu.bitcast(x.astype(BF16), jnp.int32)


def _pack_row_pairs(x, layer, transpose=False, tile=1024):
    _, r, c = x.shape
    if transpose:
        out_shape, out_spec = (c // 2, r), pl.BlockSpec((c // 2, tile), lambda i: (0, i))
    else:
        out_shape, out_spec = (r // 2, c), pl.BlockSpec((tile // 2, c), lambda i: (i, 0))
    return pl.pallas_call(
        functools.partial(_pack_kernel, transpose=transpose),
        grid=(r // tile,),
        in_specs=[pl.BlockSpec((None, tile, c), lambda i: (layer, i, 0))],
        out_specs=out_spec,
        out_shape=jax.ShapeDtypeStruct(out_shape, jnp.int32),
        compiler_params=_cparams(("parallel",)),
    )(x)


def _block_ones(n, blk):
    idx = np.arange(n) // blk
    return jnp.asarray((idx[:, None] == idx[None, :]).astype(np.float32), dtype=BF16)


def _s5_discretise(lam_re, lam_im, b_re, b_im, log_step):
    lam_re = jnp.minimum(lam_re.astype(F32), -1e-4)
    lam_im = lam_im.astype(F32)
    dt = jnp.exp(log_step.astype(F32))[:, None]
    mag = jnp.exp(lam_re * dt)
    a_re = mag * jnp.cos(lam_im * dt)
    a_im = mag * jnp.sin(lam_im * dt)
    den = lam_re * lam_re + lam_im * lam_im
    f_re = ((a_re - 1.0) * lam_re + a_im * lam_im) / den
    f_im = (a_im * lam_re - (a_re - 1.0) * lam_im) / den
    b_re = b_re.astype(F32)
    b_im = b_im.astype(F32)
    bb_re = f_re[..., None] * b_re - f_im[..., None] * b_im
    bb_im = f_re[..., None] * b_im + f_im[..., None] * b_re
    return a_re, a_im, bb_re, bb_im


def _group_block_diag(t):
    g, r, c = t.shape
    eye = jnp.eye(g, dtype=t.dtype)
    return (t[:, :, None, :] * eye[:, None, :, None]).reshape(g * r, g * c)


def _s5_params(lam_re, lam_im, b_re, b_im, c_re, c_im, log_step, nseq):
    b_rows, c_cols, ars, ais = [], [], [], []
    for d in range(2):
        a_re, a_im, bb_re, bb_im = _s5_discretise(lam_re[d], lam_im[d], b_re[d], b_im[d], log_step[d])
        bm = jnp.concatenate([_group_block_diag(jnp.swapaxes(bb_re, 1, 2)),
                              _group_block_diag(jnp.swapaxes(bb_im, 1, 2))], axis=1)
        b_rows.append(bm)
        cm = jnp.concatenate([_group_block_diag(jnp.swapaxes(c_re[d].astype(F32), 1, 2)),
                              -_group_block_diag(jnp.swapaxes(c_im[d].astype(F32), 1, 2))], axis=0)
        c_cols.append(cm)
        ars.append(jnp.broadcast_to(a_re.reshape(1, S5_LANES), (nseq, S5_LANES)))
        ais.append(jnp.broadcast_to(a_im.reshape(1, S5_LANES), (nseq, S5_LANES)))
    b2 = jnp.concatenate(b_rows, axis=0).astype(BF16)
    c2 = jnp.concatenate(c_cols, axis=1).astype(BF16)
    return b2, jnp.concatenate(ars, axis=0), jnp.concatenate(ais, axis=0), c2


def kernel(x, c, ctx, c_ctx, w_mod, b_mod, norm1_g, norm2_g, w_in, w_out, sgu_w, sgu_b, s5_lambda_re, s5_lambda_im, s5_b_re, s5_b_im, s5_c_re, s5_c_im, s5_log_step, s5_d, s5_w_glu, gla_w_gate, gla_b_gate, gla_norm_g, peer_w_query, peer_sub_keys, peer_expert_u, peer_expert_v, final_norm_g):
    nb, seq, d = x.shape
    c_len = ctx.shape[1]
    depth = w_mod.shape[0]

    cc = jnp.concatenate([c, c_ctx[None, :], jnp.zeros((8 - nb - 1, d), F32)], axis=0)
    mods = _mod_call(cc, w_mod, b_mod)

    ones_sgu = _block_ones(SGU_DIM, SGU_HEAD_DIM)
    ones_gla = _block_ones(GLA_DIM, GLA_DV)
    tri_np = np.tril(np.ones((GLA_CHUNK, GLA_CHUNK), np.float32))
    trif = jnp.asarray(tri_np, dtype=BF16)
    trib = jnp.asarray(tri_np.T, dtype=BF16)
    s_zero = jnp.zeros((nb, 2, GLA_DIM, GLA_KEY_DIM), F32)

    xl, xc = x, ctx
    for l in range(depth):
        ctx_out = l < depth - 1
        m6 = mods[l].reshape(8, N_MOD, d)
        mod_l = jnp.pad(m6[:nb], ((0, 0), (0, 2), (0, 0)))
        mod_c = jnp.broadcast_to(jnp.pad(m6[nb], ((0, 2), (0, 0)))[None], (nb, 8, d))

        win = jnp.pad(w_in[l], ((0, 0), (0, IN_PAD - IN_WIDTH))).astype(BF16)
        sguw = sgu_w[l].astype(BF16)
        sgub = jnp.repeat(jnp.swapaxes(sgu_b[l], 0, 1), SGU_HEAD_DIM, axis=1)
        gw = jnp.zeros((LANE, 2 * GLA_KEY_DIM), F32)
        gw = gw.at[0:GLA_RANK, 0:GLA_KEY_DIM].set(gla_w_gate[l, 0])
        gw = gw.at[GLA_RANK:2 * GLA_RANK, GLA_KEY_DIM:].set(gla_w_gate[l, 1]).astype(BF16)
        gb = gla_b_gate[l].reshape(1, 2 * GLA_KEY_DIM)
        g1 = norm1_g[l].reshape(1, d)

        s5_all = jnp.zeros((nb, seq + c_len, S5_DIM), F32)
        sgu_l, s5_all, qk_l, v_l, g_l, la_l = _stage_a(xl, mod_l, g1, win, sguw, sgub, ones_sgu, gw, gb,
                                                       s5_all, 0, tb=TB_LATENT, grid_layout=True)
        sgu_c, s5_all, qk_c, v_c, g_c, la_c = _stage_a(xc, mod_c, g1, win, sguw, sgub, ones_sgu, gw, gb,
                                                       s5_all, seq, tb=TB_CTX, grid_layout=False)

        b2, ar, ai, c2 = _s5_params(s5_lambda_re[l], s5_lambda_im[l], s5_b_re[l], s5_b_im[l],
                                    s5_c_re[l], s5_c_im[l], s5_log_step[l], nb)
        yf_all, yb_all = _s5_call(s5_all, seq, b2, ar, ai, c2)

        of_c, ob_c, s_ctx = _gla_call(qk_c, v_c, la_c, s_zero, trif, trib)
        of_l, ob_l, _ = _gla_call(qk_l, v_l, la_l, s_ctx, trif, trib)

        dskip = s5_d[l].reshape(1, S5_DIM)
        wglu = s5_w_glu[l].astype(BF16)
        normg = gla_norm_g[l].reshape(1, GLA_DIM)
        wout = w_out[l].astype(BF16)
        g2 = norm2_g[l].reshape(1, d)
        wqt = jnp.swapaxes(peer_w_query[l], 0, 1).astype(BF16)
        keys = peer_sub_keys[l].reshape(2 * PEER_HEADS, PEER_NKEYS, PEER_HALF).astype(BF16)
        u_bf = _pack_row_pairs(peer_expert_u, l)
        vt_bf = _pack_row_pairs(peer_expert_v, l, transpose=True)

        xl = _stage_e(xl, mod_l, sgu_l, yf_all, yb_all, s5_all, 0, of_l, ob_l, g_l,
                      dskip, wglu, normg, ones_gla, wout, tb=TB_LATENT)
        desc = _stage_f(xl, mod_l, g2, wqt, keys, tb=PEER_TBF)
        xl = _stage_g(*desc, u_bf, vt_bf, xl.reshape(nb * seq, d), mod_l, seq, PEER_TBG).reshape(nb, seq, d)

        if ctx_out:
            xc = _stage_e(xc, mod_c, sgu_c, yf_all, yb_all, s5_all, seq, of_c, ob_c, g_c,
                          dskip, wglu, normg, ones_gla, wout, tb=TB_CTX)
            desc = _stage_f(xc, mod_c, g2, wqt, keys, tb=PEER_TBF)
            xc = _stage_g(*desc, u_bf, vt_bf, xc.reshape(nb * c_len, d), mod_c, c_len,
                          min(PEER_TBG, c_len)).reshape(nb, c_len, d)

    return _final_norm(xl.reshape(nb * seq, d), final_norm_g.reshape(1, d)).reshape(nb, seq, d)
```

```python
import functools
import math

import numpy as np
import jax
import jax.numpy as jnp
from jax import lax
from jax.experimental import pallas as pl
from jax.experimental.pallas import tpu as pltpu

F32 = jnp.float32
BF16 = jnp.bfloat16
SUBLANES = 8

EPS = 1e-6
N_MOD = 6
GRID_W = 64

SGU_DIM = 256
SGU_HEADS = 4
SGU_HEAD_DIM = 64
SGU_CHUNK = 128

S5_DIM = 256
S5_GROUP = 16
S5_GROUPS = 16
S5_STATE = 64
S5_LANES = S5_GROUPS * S5_STATE

GLA_DIM = 512
GLA_HEADS = 8
GLA_DV = 64
GLA_DK = 32
GLA_KEY_DIM = 256
GLA_RANK = 16
GLA_GATE_TEMP = 16.0
GLA_CHUNK = 64

PEER_HEADS = 8
PEER_NKEYS = 128
PEER_HALF = 128
PEER_TOPK = 16

IN_WIDTH = 2336
IN_PAD = 2432
LANE = 128

VMEM_LIMIT = 56 * 1024 * 1024
TB_LATENT = 512
TB_CTX = 256

NEG_INF = float("-inf")
POS_INF = float("inf")


def _cparams(sem, flags=None):
    return pltpu.CompilerParams(dimension_semantics=sem, vmem_limit_bytes=VMEM_LIMIT, flags=flags)


def _gelu(x):
    c = math.sqrt(2.0 / math.pi)
    return 0.5 * x * (1.0 + jnp.tanh(c * (x + 0.044715 * (x * x * x))))


def _sigmoid(x):
    return 1.0 / (1.0 + jnp.exp(-x))


def _dot(a, b):
    return jnp.dot(a, b, preferred_element_type=F32)


def _dot_nt(a, b):
    return lax.dot_general(a, b, (((1,), (1,)), ((), ())), preferred_element_type=F32)


def _dot_tn(a, b):
    return lax.dot_general(a, b, (((0,), (0,)), ((), ())), preferred_element_type=F32)


def _split3(x):
    hi = x.astype(BF16)
    r = x - hi.astype(F32)
    mid = r.astype(BF16)
    lo = (r - mid.astype(F32)).astype(BF16)
    return hi, mid, lo


def _dot_x_exact(x, m):
    hi, mid, lo = _split3(x)
    return _dot(hi, m) + _dot(mid, m) + _dot(lo, m)


def _dot_m_exact(m, x):
    hi, mid, lo = _split3(x)
    return _dot(m, hi) + _dot(m, mid) + _dot(m, lo)


def _full_spec(shape):
    nd = len(shape)
    return pl.BlockSpec(shape, lambda *_: (0,) * nd)


MOD_TILE = 512


def _mod_kernel(c_ref, w_ref, b_ref, o_ref):
    c = c_ref[...]
    a = c * _sigmoid(c)
    o_ref[0] = jnp.dot(a, w_ref[0], preferred_element_type=F32,
                       precision=lax.Precision.HIGHEST) + b_ref[0]


def _mod_call(cc, w_mod, b_mod):
    depth, d, nd = w_mod.shape
    rows = cc.shape[0]
    return pl.pallas_call(
        _mod_kernel,
        grid=(depth, nd // MOD_TILE),
        in_specs=[
            pl.BlockSpec((rows, d), lambda l, j: (0, 0)),
            pl.BlockSpec((1, d, MOD_TILE), lambda l, j: (l, 0, j)),
            pl.BlockSpec((1, 1, MOD_TILE), lambda l, j: (l, 0, j)),
        ],
        out_specs=pl.BlockSpec((1, rows, MOD_TILE), lambda l, j: (l, 0, j)),
        out_shape=jax.ShapeDtypeStruct((depth, rows, nd), F32),
        compiler_params=_cparams(("parallel", "parallel")),
    )(cc, w_mod, b_mod.reshape(depth, 1, nd))


def _store_tokens(ref, val, grid_layout):
    if grid_layout:
        for r in range(val.shape[0] // GRID_W):
            ref[:, r, :] = val[r * GRID_W:(r + 1) * GRID_W]
    else:
        ref[0] = val


def _load_tokens(ref, grid_layout):
    if grid_layout:
        return jnp.concatenate([ref[:, r, :] for r in range(ref.shape[1])], axis=0)
    return ref[0]


def _stage_a_kernel(x_ref, mod_ref, g1_ref, win_ref, sguw_ref, sgub_ref, ones_ref, gw_ref, gb_ref, s5_in_ref,
                    sgu_ref, s5x_ref, qk_ref, v_ref, g_ref, la_ref, *, tb, grid_layout):
    del s5_in_ref
    x = x_ref[0]
    ms = jnp.mean(x * x, axis=-1, keepdims=True)
    xn = x * lax.rsqrt(ms + EPS) * g1_ref[...]
    h = xn * (1.0 + mod_ref[0, 1:2, :]) + mod_ref[0, 0:1, :]
    cols = _dot(h.astype(BF16), win_ref[...])

    u = _gelu(cols[:, 0:SGU_DIM])
    v = _gelu(cols[:, SGU_DIM:2 * SGU_DIM])
    msq = _dot_x_exact(v * v, ones_ref[...]) * (1.0 / SGU_HEAD_DIM)
    vn = (v * lax.rsqrt(msq + EPS)).astype(BF16)
    head_of_lane = lax.broadcasted_iota(jnp.int32, (SGU_CHUNK, SGU_DIM), 1) // SGU_HEAD_DIM
    for ci in range(tb // SGU_CHUNK):
        rows = slice(ci * SGU_CHUNK, (ci + 1) * SGU_CHUNK)
        vc = vn[rows]
        mixed = sgub_ref[...]
        for hh in range(SGU_HEADS):
            mixed = mixed + jnp.where(head_of_lane == hh, _dot(sguw_ref[hh], vc), 0.0)
        sgu_ref[0, rows, :] = u[rows] * mixed

    s5x_ref[0] = cols[:, 512:768]
    q = cols[:, 768:1024] * (GLA_DK ** -0.5)
    _store_tokens(qk_ref, jnp.concatenate([q, cols[:, 1024:1280]], axis=-1), grid_layout)
    _store_tokens(v_ref, cols[:, 1280:1792], grid_layout)
    g_ref[0] = cols[:, 1792:2304]

    z = cols[:, 2304:2432].astype(BF16)
    za = _dot(z, gw_ref[...]) + gb_ref[...]
    log_sig = jnp.minimum(za, 0.0) - jnp.log1p(jnp.exp(-jnp.abs(za)))
    _store_tokens(la_ref, log_sig * (1.0 / GLA_GATE_TEMP), grid_layout)


def _stage_a(xs, mod, g1, win, sguw, sgub, ones_sgu, gw, gb, s5_all, s5_row0, tb, grid_layout):
    b, l, d = xs.shape
    assert s5_row0 % tb == 0
    kern = functools.partial(_stage_a_kernel, tb=tb, grid_layout=grid_layout)
    tok = lambda w: pl.BlockSpec((1, tb, w), lambda bi, i: (bi, i, 0))
    if grid_layout:
        assert tb % GRID_W == 0 and l % GRID_W == 0
        gla = lambda w: pl.BlockSpec((GRID_W, None, tb // GRID_W, w), lambda bi, i: (0, bi, i, 0))
        gla_shape = lambda w: jax.ShapeDtypeStruct((GRID_W, b, l // GRID_W, w), F32)
    else:
        gla = tok
        gla_shape = lambda w: jax.ShapeDtypeStruct((b, l, w), F32)
    nat_shape = lambda w: jax.ShapeDtypeStruct((b, l, w), F32)
    return pl.pallas_call(
        kern,
        grid=(b, l // tb),
        in_specs=[
            tok(d),
            pl.BlockSpec((1, 8, d), lambda bi, i: (bi, 0, 0)),
            _full_spec(g1.shape), _full_spec(win.shape), _full_spec(sguw.shape), _full_spec(sgub.shape),
            _full_spec(ones_sgu.shape), _full_spec(gw.shape), _full_spec(gb.shape),
            pl.BlockSpec(memory_space=pl.ANY),
        ],
        out_specs=[tok(SGU_DIM),
                   pl.BlockSpec((1, tb, S5_DIM), lambda bi, i: (bi, s5_row0 // tb + i, 0)),
                   gla(2 * GLA_KEY_DIM), gla(GLA_DIM), tok(GLA_DIM), gla(2 * GLA_KEY_DIM)],
        out_shape=[nat_shape(SGU_DIM), jax.ShapeDtypeStruct(s5_all.shape, F32),
                   gla_shape(2 * GLA_KEY_DIM), gla_shape(GLA_DIM), nat_shape(GLA_DIM), gla_shape(2 * GLA_KEY_DIM)],
        input_output_aliases={9: 1},
        compiler_params=_cparams(("parallel", "parallel")),
    )(xs, mod, g1, win, sguw, sgub, ones_sgu, gw, gb, s5_all)


S5_TC = 128


def _s5_kernel(uf_ref, ub_ref, perm_ref, permt_ref, b2_ref, ar_ref, ai_ref, c2_ref, yf_ref, yb_ref,
               h_ref, buf0_ref, buf1_ref, buf2_ref, *, nseq):
    tc = S5_TC
    rows = tc * 2 * nseq
    s = pl.program_id(0)

    @pl.when(s == 0)
    def _():
        h_ref[...] = jnp.zeros_like(h_ref)
        buf0_ref[...] = jnp.zeros_like(buf0_ref)
        buf1_ref[...] = jnp.zeros_like(buf1_ref)
        buf2_ref[...] = jnp.zeros_like(buf2_ref)

    def phase(scan_ref, fill_ref, read_ref):
        ar = ar_ref[...]
        ai = ai_ref[...]
        hr, hi = h_ref[:, 0:S5_LANES], h_ref[:, S5_LANES:]
        for t in range(tc):
            bur = scan_ref[t, :, 0:S5_LANES]
            bui = scan_ref[t, :, S5_LANES:]
            hr, hi = ar * hr - ai * hi + bur, ar * hi + ai * hr + bui
            scan_ref[t, :, 0:S5_LANES] = hr
            scan_ref[t, :, S5_LANES:] = hi
        h_ref[:, 0:S5_LANES] = hr
        h_ref[:, S5_LANES:] = hi

        fwd_row = lax.broadcasted_iota(jnp.int32, (rows, S5_DIM), 0) % (2 * nseq) < nseq
        hs = read_ref[...].reshape(rows, 2 * S5_LANES).astype(BF16)
        y2 = _dot(hs, c2_ref[...])
        y = jnp.where(fwd_row, y2[:, 0:S5_DIM], y2[:, S5_DIM:])
        y_hi = y.astype(BF16)
        y_lo = (y - y_hi.astype(F32)).astype(BF16)
        y_nat = _dot(permt_ref[...], y_hi) + _dot(permt_ref[...], y_lo)
        for b in range(nseq):
            yf_ref[b] = y_nat[b * tc:(b + 1) * tc]
            yb_ref[b] = y_nat[(nseq + b) * tc:(nseq + b + 1) * tc]

        x = jnp.concatenate([uf_ref[b] for b in range(nseq)] + [ub_ref[b] for b in range(nseq)], axis=0)
        u_tm = _dot(perm_ref[...], x.astype(BF16))
        u = jnp.concatenate([jnp.where(fwd_row, u_tm, 0.0), jnp.where(fwd_row, 0.0, u_tm)], axis=-1).astype(BF16)
        fill_ref[...] = _dot(u, b2_ref[...]).reshape(tc, 2 * nseq, 2 * S5_LANES)

    bufs = (buf0_ref, buf1_ref, buf2_ref)
    for k in range(3):
        @pl.when(s % 3 == k)
        def _(k=k):
            phase(bufs[(k + 2) % 3], bufs[k], bufs[(k + 1) % 3])


def _s5_permutation(nseq):
    tc = S5_TC
    p = np.zeros((tc * 2 * nseq, tc * 2 * nseq), np.float32)
    for q in range(2 * nseq):
        for t in range(tc):
            p[t * 2 * nseq + q, q * tc + (t if q < nseq else tc - 1 - t)] = 1.0
    return p


def _s5_call(s5_all, n_lat, b2, ar, ai, c2):
    nseq, t, _ = s5_all.shape
    n = t // S5_TC
    n_l = n_lat // S5_TC
    n_c = n - n_l
    perm = _s5_permutation(nseq)
    kern = functools.partial(_s5_kernel, nseq=nseq)
    fwd_blk = lambda k: jnp.where(k < n_c, n_l + k, k - n_c)
    bwd_blk = lambda k: n - 1 - k
    in_k = lambda s: jnp.minimum(s, n - 1)
    out_k = lambda s: jnp.clip(s - 2, 0, n - 1)
    blk = (nseq, S5_TC, S5_DIM)
    rows = 2 * nseq
    buf = pltpu.VMEM((S5_TC, rows, 2 * S5_LANES), F32)
    return pl.pallas_call(
        kern,
        grid=(n + 2,),
        in_specs=[
            pl.BlockSpec(blk, lambda s: (0, fwd_blk(in_k(s)), 0)), pl.BlockSpec(blk, lambda s: (0, bwd_blk(in_k(s)), 0)),
            _full_spec(perm.shape), _full_spec(perm.shape),
            _full_spec(b2.shape), _full_spec(ar.shape), _full_spec(ai.shape), _full_spec(c2.shape),
        ],
        out_specs=[pl.BlockSpec(blk, lambda s: (0, fwd_blk(out_k(s)), 0)),
                   pl.BlockSpec(blk, lambda s: (0, bwd_blk(out_k(s)), 0))],
        out_shape=[jax.ShapeDtypeStruct(s5_all.shape, F32), jax.ShapeDtypeStruct(s5_all.shape, F32)],
        scratch_shapes=[pltpu.VMEM((rows, 2 * S5_LANES), F32), buf, buf, buf],
        compiler_params=_cparams(("arbitrary",)),
    )(s5_all, s5_all, jnp.asarray(perm, dtype=BF16), jnp.asarray(perm.T, dtype=BF16), b2, ar, ai, c2)


def _gla_kernel(qkf_ref, vf_ref, laf_ref, qkb_ref, vb_ref, lab_ref, s0_ref, trif_ref, trib_ref,
                of_ref, ob_ref, sout_ref, s_ref, *, nb):
    c = pl.program_id(0)

    @pl.when(c == 0)
    def _():
        s_ref[...] = s0_ref[...]

    ch = GLA_CHUNK
    r_k = lax.broadcasted_iota(jnp.int32, (GLA_HEADS * ch, GLA_KEY_DIM), 0) // ch
    c_k = lax.broadcasted_iota(jnp.int32, (GLA_HEADS * ch, GLA_KEY_DIM), 1) // GLA_DK
    hm_k = r_k == c_k
    r_v = lax.broadcasted_iota(jnp.int32, (GLA_HEADS * ch, GLA_DIM), 0) // ch
    c_v = lax.broadcasted_iota(jnp.int32, (GLA_HEADS * ch, GLA_DIM), 1) // GLA_DV
    hm_v = r_v == c_v
    r_s = lax.broadcasted_iota(jnp.int32, (GLA_DIM, GLA_KEY_DIM), 0) // GLA_DV
    c_s = lax.broadcasted_iota(jnp.int32, (GLA_DIM, GLA_KEY_DIM), 1) // GLA_DK
    hm_s = r_s == c_s
    t_i = lax.broadcasted_iota(jnp.int32, (ch, GLA_HEADS * ch), 0)
    s_i = lax.broadcasted_iota(jnp.int32, (ch, GLA_HEADS * ch), 1) % ch
    mask_f = t_i >= s_i
    mask_b = t_i <= s_i
    trif = trif_ref[...]
    trib = trib_ref[...]

    fwd = dict(qk=qkf_ref, v=vf_ref, la=laf_ref, o=of_ref, tri=trif, last=ch - 1, ref=ch // 2, mask=mask_f, d=0)
    bwd = dict(qk=qkb_ref, v=vb_ref, la=lab_ref, o=ob_ref, tri=trib, last=0, ref=ch - 1 - ch // 2, mask=mask_b, d=1)
    streams = [(b, p) for b in range(nb) for p in (fwd, bwd)]

    bcums = [_dot_m_exact(p["tri"], p["la"][b]) for b, p in streams]
    scaled = []
    for (b, p), bcum in zip(streams, bcums):
        qk = p["qk"][b]
        q, k = qk[:, 0:GLA_KEY_DIM], qk[:, GLA_KEY_DIM:]
        blast = bcum[p["last"]:p["last"] + 1]
        bref = bcum[p["ref"]:p["ref"] + 1]
        qe = (q * jnp.exp(bcum)).astype(BF16)
        qd = (q * jnp.exp(bcum - bref)).astype(BF16)
        kd = k * jnp.exp(bref - bcum)
        kdec = (k * jnp.exp(blast - bcum)).astype(BF16)
        kst = jnp.where(hm_k, jnp.concatenate([kd] * GLA_HEADS, axis=0), 0.0).astype(BF16)
        scaled.append((qe, qd, kdec, kst, jnp.exp(blast)))
    prods = []
    for (b, p), (qe, qd, kdec, kst, _) in zip(streams, scaled):
        v = p["v"][b]
        sc = _dot_nt(qd, kst)
        o_inter = _dot_nt(qe, s_ref[b, p["d"]].astype(BF16))
        kv_t = _dot_tn(v.astype(BF16), kdec)
        prods.append((sc, o_inter, kv_t))
    for (b, p), (_, _, _, _, decay), (sc, o_inter, kv_t) in zip(streams, scaled, prods):
        vbd = jnp.where(hm_v, jnp.concatenate([p["v"][b]] * GLA_HEADS, axis=0), 0.0).astype(BF16)
        p["o"][b] = _dot(jnp.where(p["mask"], sc, 0.0).astype(BF16), vbd) + o_inter
        s_ref[b, p["d"]] = s_ref[b, p["d"]] * decay + jnp.where(hm_s, kv_t, 0.0)

    @pl.when(c == pl.num_programs(0) - 1)
    def _():
        sout_ref[...] = s_ref[...]


def _gla_call(qk, v, la, s0, trif, trib):
    ch = GLA_CHUNK
    if qk.ndim == 4:
        n, b = qk.shape[0], qk.shape[1]
        assert qk.shape[2] == ch
        spec = lambda w, off, rev: pl.BlockSpec(
            (None, b, ch, w), (lambda c: (n - 1 - c, 0, 0, off)) if rev else (lambda c: (c, 0, 0, off)))
    else:
        b = qk.shape[0]
        n = qk.shape[1] // ch
        spec = lambda w, off, rev: pl.BlockSpec(
            (b, ch, w), (lambda c: (0, n - 1 - c, off)) if rev else (lambda c: (0, c, off)))
    kern = functools.partial(_gla_kernel, nb=b)
    o_shape = jax.ShapeDtypeStruct(v.shape, F32)
    return pl.pallas_call(
        kern,
        grid=(n,),
        in_specs=[
            spec(2 * GLA_KEY_DIM, 0, False), spec(GLA_DIM, 0, False), spec(GLA_KEY_DIM, 0, False),
            spec(2 * GLA_KEY_DIM, 0, True), spec(GLA_DIM, 0, True), spec(GLA_KEY_DIM, 1, True),
            _full_spec(s0.shape), _full_spec(trif.shape), _full_spec(trib.shape),
        ],
        out_specs=[spec(GLA_DIM, 0, False), spec(GLA_DIM, 0, True), _full_spec(s0.shape)],
        out_shape=[o_shape, o_shape, jax.ShapeDtypeStruct(s0.shape, F32)],
        scratch_shapes=[pltpu.VMEM(s0.shape, F32)],
        compiler_params=_cparams(("arbitrary",)),
    )(qk, v, la, qk, v, la, s0, trif, trib)


def _stage_e_kernel(x_ref, mod_ref, sgu_ref, yf_ref, yb_ref, s5x_ref, of_ref, ob_ref, g_ref,
                    dskip_ref, wglu_ref, normg_ref, ones_ref, wout_ref, o_ref, *, grid_layout):
    ys = yf_ref[0] + yb_ref[0] + dskip_ref[...] * s5x_ref[0]
    z = _dot(_gelu(ys).astype(BF16), wglu_ref[...])
    s5o = z[:, 0:S5_DIM] * _sigmoid(z[:, S5_DIM:])
    o = _load_tokens(of_ref, grid_layout) + _load_tokens(ob_ref, grid_layout)
    ms = _dot_x_exact(o * o, ones_ref[...]) * (1.0 / GLA_DV)
    g = g_ref[0]
    gl = o * lax.rsqrt(ms + EPS) * normg_ref[...] * (g * _sigmoid(g))
    y = (_dot(sgu_ref[0].astype(BF16), wout_ref[0:SGU_DIM, :])
         + _dot(s5o.astype(BF16), wout_ref[SGU_DIM:SGU_DIM + S5_DIM, :])
         + _dot(gl.astype(BF16), wout_ref[SGU_DIM + S5_DIM:, :]))
    o_ref[0] = x_ref[0] + mod_ref[0, 2:3, :] * y


def _stage_e(xs, mod, sgu, yf_all, yb_all, s5_all, s5_row0, of, ob, g, dskip, wglu, normg, ones_gla, wout, tb):
    b, l, d = xs.shape
    assert s5_row0 % tb == 0
    grid_layout = of.ndim == 4
    tok = lambda w: pl.BlockSpec((1, tb, w), lambda bi, i: (bi, i, 0))
    s5 = pl.BlockSpec((1, tb, S5_DIM), lambda bi, i: (bi, s5_row0 // tb + i, 0))
    if grid_layout:
        assert tb % GRID_W == 0
        gla = pl.BlockSpec((GRID_W, None, tb // GRID_W, GLA_DIM), lambda bi, i: (0, bi, i, 0))
    else:
        gla = tok(GLA_DIM)
    return pl.pallas_call(
        functools.partial(_stage_e_kernel, grid_layout=grid_layout),
        grid=(b, l // tb),
        in_specs=[
            tok(d), pl.BlockSpec((1, 8, d), lambda bi, i: (bi, 0, 0)),
            tok(SGU_DIM), s5, s5, s5, gla, gla, tok(GLA_DIM),
            _full_spec(dskip.shape), _full_spec(wglu.shape), _full_spec(normg.shape),
            _full_spec(ones_gla.shape), _full_spec(wout.shape),
        ],
        out_specs=tok(d),
        out_shape=jax.ShapeDtypeStruct((b, l, d), F32),
        compiler_params=_cparams(("parallel", "parallel")),
    )(xs, mod, sgu, yf_all, yb_all, s5_all, of, ob, g, dskip, wglu, normg, ones_gla, wout)


PEER_TBF = 256


def _sort_network_16():
    def merge(lo, hi, r):
        step = r * 2
        if step < hi - lo:
            yield from merge(lo, hi, step)
            yield from merge(lo + r, hi, step)
            for i in range(lo + r, hi - r, step):
                yield (i, i + r)
        else:
            yield (lo, lo + r)

    def sort(lo, hi):
        if hi - lo >= 1:
            mid = lo + (hi - lo) // 2
            yield from sort(lo, mid)
            yield from sort(mid + 1, hi)
            yield from merge(lo, hi, 1)

    return tuple(sort(0, PEER_TOPK - 1))


SORT16 = _sort_network_16()
BITONIC16 = tuple((k, k + s) for s in (8, 4, 2, 1) for k in range(PEER_TOPK) if not k & s)


def _compare_exchange(xs, pairs):
    xs = list(xs)
    for i, j in pairs:
        hi = jnp.maximum(xs[i], xs[j])
        lo = jnp.minimum(xs[i], xs[j])
        xs[i], xs[j] = hi, lo
    return xs


def _merge_sublanes(xs):
    for shift in (4, 6, 7):
        rolled = [pltpu.roll(x, shift, 0) for x in xs]
        xs = [jnp.maximum(xs[k], rolled[PEER_TOPK - 1 - k]) for k in range(PEER_TOPK)]
        xs = _compare_exchange(xs, BITONIC16)
    return xs


def _dup_bf16_words(x):
    bits = pltpu.bitcast(x.astype(BF16).astype(F32), jnp.int32)
    return bits | lax.shift_right_logical(bits, 16)


def _stage_f_kernel(x_ref, mod_ref, g2_ref, wqt_ref, keys_ref,
                    ht_ref, e1w_ref, n1w_ref, e2_ref, r2_ref, qt_ref, v1_ref, v2_ref, *, tb):
    x = x_ref[0]
    ms = jnp.mean(x * x, axis=-1, keepdims=True)
    xn = x * lax.rsqrt(ms + EPS) * g2_ref[...]
    h = xn * (1.0 + mod_ref[0, 4:5, :]) + mod_ref[0, 3:4, :]
    ht = h.T.astype(BF16)
    ht_ref[...] = pltpu.bitcast(ht, jnp.int32)
    qt_ref[...] = _dot(wqt_ref[...], ht)
    k_top = PEER_TOPK

    def tiles(s):
        return [s[SUBLANES * k:SUBLANES * (k + 1)] for k in range(PEER_NKEYS // SUBLANES)]

    def head_body(hh, carry):
        for tc in range(tb // LANE):
            tcol = slice(tc * LANE, (tc + 1) * LANE)
            r1 = hh * (2 * PEER_HALF)
            q1 = qt_ref[r1:r1 + PEER_HALF, tcol].astype(BF16)
            q2 = qt_ref[r1 + PEER_HALF:r1 + 2 * PEER_HALF, tcol].astype(BF16)
            s1 = _dot(keys_ref[hh], q1)
            s2 = _dot(keys_ref[PEER_HEADS + hh], q2)
            for s, v_ref in ((s1, v1_ref), (s2, v2_ref)):
                top = _merge_sublanes(_compare_exchange(tiles(s), SORT16))
                for k in range(k_top):
                    v_ref[k:k + 1, tcol] = top[k][0:1]
            v1row = lambda a: v1_ref[a:a + 1, tcol]
            v2row = lambda b: v2_ref[b:b + 1, tcol]
            v1lo = v1_ref[0:SUBLANES, tcol]
            v2lo = v2_ref[0:SUBLANES, tcol]
            cand = [v1lo + v2row(b) for b in range(k_top)]
            tail = [v1row(a) + v2lo for a in range(SUBLANES, k_top)]
            for k in range(SUBLANES, k_top):
                cand[k] = jnp.maximum(cand[k], tail[k_top - 1 - k])
            best = _merge_sublanes(_compare_exchange(cand, BITONIC16))
            theta = best[k_top - 1][0:1]
            cmax = best[0][0:1]
            zsum = jnp.zeros((1, LANE), F32)
            for k in range(k_top):
                zsum = zsum + jnp.exp(best[k][0:1] - cmax)
            rz = 1.0 / zsum
            n_top = jnp.zeros((1, LANE), F32)
            for b in range(k_top):
                n_top = jnp.where(v1row(0) + v2row(b) >= theta, float(b + 1), n_top)
            n1 = jnp.zeros(s1.shape, F32)
            for b in range(SUBLANES):
                n1 = jnp.where(s1 + v2row(b) >= theta, float(b + 1), n1)
            n1 = jnp.where(s1 >= v1row(0), n_top, n1)
            r2 = jnp.zeros(s2.shape, F32)
            for b in range(k_top):
                r2 = jnp.where(v2row(b) > s2, float(b + 1), r2)
            e1 = jnp.where(s1 >= v1row(k_top - 1), jnp.exp(s1 - v1row(0)), 0.0) * rz
            e2 = jnp.where(s2 >= v2row(k_top - 1), jnp.exp(s2 - v2row(0)), 0.0)
            e1w_ref[tc, hh] = _dup_bf16_words(e1)
            n1w_ref[tc, hh] = _dup_bf16_words(n1)
            e2_ref[tc, hh] = pltpu.bitcast(e2.astype(BF16), jnp.int32)
            r2_ref[tc, hh] = pltpu.bitcast(r2.astype(BF16), jnp.int32)
        return carry

    for hh in range(PEER_HEADS):
        head_body(hh, 0)


def _stage_f(xs, mod, g2, wqt, keys, tb):
    b, l, d = xs.shape
    nblk = l // tb
    ntok = b * l
    nch = ntok // LANE
    kern = functools.partial(_stage_f_kernel, tb=tb)
    row_spec = pl.BlockSpec((tb // LANE, PEER_HEADS, PEER_NKEYS, LANE), lambda bi, i: (bi * nblk + i, 0, 0, 0))
    pair_spec = pl.BlockSpec((tb // LANE, PEER_HEADS, PEER_NKEYS // 2, LANE), lambda bi, i: (bi * nblk + i, 0, 0, 0))
    desc_shape = lambda rows: jax.ShapeDtypeStruct((nch, PEER_HEADS, rows, LANE), jnp.int32)
    return pl.pallas_call(
        kern,
        grid=(b, nblk),
        in_specs=[
            pl.BlockSpec((1, tb, d), lambda bi, i: (bi, i, 0)),
            pl.BlockSpec((1, 8, d), lambda bi, i: (bi, 0, 0)),
            _full_spec(g2.shape), _full_spec(wqt.shape), _full_spec(keys.shape),
        ],
        out_specs=[pl.BlockSpec((d // 2, tb), lambda bi, i: (0, bi * nblk + i)),
                   row_spec, row_spec, pair_spec, pair_spec],
        out_shape=[jax.ShapeDtypeStruct((d // 2, ntok), jnp.int32), desc_shape(PEER_NKEYS), desc_shape(PEER_NKEYS),
                   desc_shape(PEER_NKEYS // 2), desc_shape(PEER_NKEYS // 2)],
        scratch_shapes=[
            pltpu.VMEM((PEER_HEADS * 2 * PEER_HALF, tb), F32),
            pltpu.VMEM((PEER_TOPK, tb), F32),
            pltpu.VMEM((PEER_TOPK, tb), F32),
        ],
        compiler_params=_cparams(("parallel", "parallel")),
    )(xs, mod, g2, wqt, keys)


PEER_TBG = 1024
PEER_TE = 1024
PEER_I1_PER_TILE = PEER_TE // PEER_NKEYS
PEER_N_TILES = PEER_NKEYS * PEER_NKEYS // PEER_TE
PEER_MXU_COLS = 256
PEER_G_FLAGS = None


def _stage_g_kernel(htw_ref, e1w_ref, n1w_ref, e2_ref, r2_ref, uw_ref, vtw_ref, x_ref, mod_ref, fg_ref,
                    o_ref, acc_ref, ata_ref, atb_ref, p_ref, *, tb, final_norm):
    s = pl.program_id(1)

    @pl.when(s == 0)
    def _():
        acc_ref[...] = jnp.zeros_like(acc_ref)

    def step(at_cur_ref, at_next_ref):
        per_grp = PEER_MXU_COLS // LANE
        for grp in range(tb // PEER_MXU_COLS):
            cols = slice(grp * PEER_MXU_COLS, (grp + 1) * PEER_MXU_COLS)
            for tcl in range(per_grp if at_cur_ref is not None else 0):
                tc = grp * per_grp + tcl
                tcol = slice(tc * LANE, (tc + 1) * LANE)
                for i1l in range(PEER_I1_PER_TILE):
                    rows = slice(i1l * PEER_NKEYS, (i1l + 1) * PEER_NKEYS)
                    gate = jnp.zeros((PEER_NKEYS, LANE), BF16)
                    for hh in range(PEER_HEADS):
                        e1row = e1w_ref[tc, hh, i1l:i1l + 1, :]
                        n1row = n1w_ref[tc, hh, i1l:i1l + 1, :]
                        e1 = pltpu.bitcast(jnp.broadcast_to(e1row, (PEER_NKEYS // 2, LANE)), BF16)
                        n1 = pltpu.bitcast(jnp.broadcast_to(n1row, (PEER_NKEYS // 2, LANE)), BF16)
                        r2 = pltpu.bitcast(r2_ref[tc, hh], BF16)
                        e2 = pltpu.bitcast(e2_ref[tc, hh], BF16)
                        gate = gate + e2 * jnp.where(r2 < n1, e1, 0.0)
                    p_ref[rows, tcol] = gate * _gelu(at_cur_ref[rows, tcol])
            if at_cur_ref is not None:
                vt = pltpu.bitcast(vtw_ref[...], BF16)
                acc_ref[:, cols] += _dot(vt, p_ref[:, cols])
            if at_next_ref is not None:
                u = pltpu.bitcast(uw_ref[...], BF16)
                ht = pltpu.bitcast(htw_ref[:, cols], BF16)
                at_next_ref[:, cols] = _dot(u, ht).astype(BF16)

    last = pl.num_programs(1) - 1

    @pl.when(s == 0)
    def _():
        step(None, ata_ref)

    @pl.when(jnp.logical_and(s % 2 == 0, jnp.logical_and(s > 0, s < last)))
    def _():
        step(atb_ref, ata_ref)

    @pl.when(s % 2 == 1)
    def _():
        step(ata_ref, atb_ref)

    @pl.when(s == last)
    def _():
        step(atb_ref, None)
        xo = x_ref[...] + mod_ref[0, 5:6, :] * acc_ref[...].T
        if final_norm:
            ms = jnp.mean(xo * xo, axis=-1, keepdims=True)
            xo = xo * lax.rsqrt(ms + EPS) * fg_ref[...]
        o_ref[...] = xo


def _stage_g(htw, e1w, n1w, e2, r2, uw, vtw, xflat, mod, final_g, tokens_per_batch, tb, final_norm):
    ntok = htw.shape[1]
    d = 2 * htw.shape[0]
    assert 2 * uw.shape[0] == PEER_N_TILES * PEER_TE and PEER_N_TILES % 2 == 0
    blocks_per_batch = tokens_per_batch // tb
    kern = functools.partial(_stage_g_kernel, tb=tb, final_norm=final_norm)
    last = PEER_N_TILES - 1
    desc_spec = pl.BlockSpec((tb // LANE, PEER_HEADS, PEER_NKEYS // 2, LANE), lambda j, i: (j, 0, 0, 0))
    row_spec = pl.BlockSpec((tb // LANE, PEER_HEADS, PEER_I1_PER_TILE, LANE),
                            lambda j, i: (j, 0, jnp.maximum(i - 1, 0), 0))
    return pl.pallas_call(
        kern,
        grid=(ntok // tb, PEER_N_TILES + 1),
        in_specs=[
            pl.BlockSpec((d // 2, tb), lambda j, i: (0, j)),
            row_spec, row_spec, desc_spec, desc_spec,
            pl.BlockSpec((PEER_TE // 2, d), lambda j, i: (jnp.minimum(i, last), 0)),
            pl.BlockSpec((d // 2, PEER_TE), lambda j, i: (0, jnp.maximum(i - 1, 0))),
            pl.BlockSpec((tb, d), lambda j, i: (j, 0)),
            pl.BlockSpec((1, 8, d), lambda j, i: (j // blocks_per_batch, 0, 0)),
            _full_spec(final_g.shape),
        ],
        out_specs=pl.BlockSpec((tb, d), lambda j, i: (j, 0)),
        out_shape=jax.ShapeDtypeStruct((ntok, d), F32),
        scratch_shapes=[
            pltpu.VMEM((d, tb), F32),
            pltpu.VMEM((PEER_TE, tb), BF16),
            pltpu.VMEM((PEER_TE, tb), BF16),
            pltpu.VMEM((PEER_TE, tb), BF16),
        ],
        compiler_params=_cparams(("parallel", "arbitrary"), PEER_G_FLAGS),
    )(htw, e1w, n1w, e2, r2, uw, vtw, xflat, mod, final_g)


def _pack_kernel(x_ref, o_ref, *, transpose):
    x = x_ref[...]
    if transpose:
        x = x.T
    o_ref[...] = pltpu.bitcast(x.astype(BF16), jnp.int32)


def _pack_row_pairs(x, layer, transpose=False, tile=1024):
    _, r, c = x.shape
    if transpose:
        out_shape, out_spec = (c // 2, r), pl.BlockSpec((c // 2, tile), lambda i: (0, i))
    else:
        out_shape, out_spec = (r // 2, c), pl.BlockSpec((tile // 2, c), lambda i: (i, 0))
    return pl.pallas_call(
        functools.partial(_pack_kernel, transpose=transpose),
        grid=(r // tile,),
        in_specs=[pl.BlockSpec((None, tile, c), lambda i: (layer, i, 0))],
        out_specs=out_spec,
        out_shape=jax.ShapeDtypeStruct(out_shape, jnp.int32),
        compiler_params=_cparams(("parallel",)),
    )(x)


def _block_ones(n, blk):
    idx = np.arange(n) // blk
    return jnp.asarray((idx[:, None] == idx[None, :]).astype(np.float32), dtype=BF16)


def _s5_discretise(lam_re, lam_im, b_re, b_im, log_step):
    lam_re = jnp.minimum(lam_re.astype(F32), -1e-4)
    lam_im = lam_im.astype(F32)
    dt = jnp.exp(log_step.astype(F32))[:, None]
    mag = jnp.exp(lam_re * dt)
    a_re = mag * jnp.cos(lam_im * dt)
    a_im = mag * jnp.sin(lam_im * dt)
    den = lam_re * lam_re + lam_im * lam_im
    f_re = ((a_re - 1.0) * lam_re + a_im * lam_im) / den
    f_im = (a_im * lam_re - (a_re - 1.0) * lam_im) / den
    b_re = b_re.astype(F32)
    b_im = b_im.astype(F32)
    bb_re = f_re[..., None] * b_re - f_im[..., None] * b_im
    bb_im = f_re[..., None] * b_im + f_im[..., None] * b_re
    return a_re, a_im, bb_re, bb_im


def _group_block_diag(t):
    g, r, c = t.shape
    eye = jnp.eye(g, dtype=t.dtype)
    return (t[:, :, None, :] * eye[:, None, :, None]).reshape(g * r, g * c)


def _s5_params(lam_re, lam_im, b_re, b_im, c_re, c_im, log_step, nseq):
    b_rows, c_cols, ars, ais = [], [], [], []
    for d in range(2):
        a_re, a_im, bb_re, bb_im = _s5_discretise(lam_re[d], lam_im[d], b_re[d], b_im[d], log_step[d])
        bm = jnp.concatenate([_group_block_diag(jnp.swapaxes(bb_re, 1, 2)),
                              _group_block_diag(jnp.swapaxes(bb_im, 1, 2))], axis=1)
        b_rows.append(bm)
        cm = jnp.concatenate([_group_block_diag(jnp.swapaxes(c_re[d].astype(F32), 1, 2)),
                              -_group_block_diag(jnp.swapaxes(c_im[d].astype(F32), 1, 2))], axis=0)
        c_cols.append(cm)
        ars.append(jnp.broadcast_to(a_re.reshape(1, S5_LANES), (nseq, S5_LANES)))
        ais.append(jnp.broadcast_to(a_im.reshape(1, S5_LANES), (nseq, S5_LANES)))
    b2 = jnp.concatenate(b_rows, axis=0).astype(BF16)
    c2 = jnp.concatenate(c_cols, axis=1).astype(BF16)
    return b2, jnp.concatenate(ars, axis=0), jnp.concatenate(ais, axis=0), c2


def kernel(x, c, ctx, c_ctx, w_mod, b_mod, norm1_g, norm2_g, w_in, w_out, sgu_w, sgu_b, s5_lambda_re, s5_lambda_im, s5_b_re, s5_b_im, s5_c_re, s5_c_im, s5_log_step, s5_d, s5_w_glu, gla_w_gate, gla_b_gate, gla_norm_g, peer_w_query, peer_sub_keys, peer_expert_u, peer_expert_v, final_norm_g):
    nb, seq, d = x.shape
    c_len = ctx.shape[1]
    depth = w_mod.shape[0]

    cc = jnp.concatenate([c, c_ctx[None, :], jnp.zeros((8 - nb - 1, d), F32)], axis=0)
    mods = _mod_call(cc, w_mod, b_mod)

    ones_sgu = _block_ones(SGU_DIM, SGU_HEAD_DIM)
    ones_gla = _block_ones(GLA_DIM, GLA_DV)
    tri_np = np.tril(np.ones((GLA_CHUNK, GLA_CHUNK), np.float32))
    trif = jnp.asarray(tri_np, dtype=BF16)
    trib = jnp.asarray(tri_np.T, dtype=BF16)
    s_zero = jnp.zeros((nb, 2, GLA_DIM, GLA_KEY_DIM), F32)
    final_g = final_norm_g.reshape(1, d)

    xl, xc = x, ctx
    for l in range(depth):
        ctx_out = l < depth - 1
        m6 = mods[l].reshape(8, N_MOD, d)
        mod_l = jnp.pad(m6[:nb], ((0, 0), (0, 2), (0, 0)))
        mod_c = jnp.broadcast_to(jnp.pad(m6[nb], ((0, 2), (0, 0)))[None], (nb, 8, d))

        win = jnp.pad(w_in[l], ((0, 0), (0, IN_PAD - IN_WIDTH))).astype(BF16)
        sguw = sgu_w[l].astype(BF16)
        sgub = jnp.repeat(jnp.swapaxes(sgu_b[l], 0, 1), SGU_HEAD_DIM, axis=1)
        gw = jnp.zeros((LANE, 2 * GLA_KEY_DIM), F32)
        gw = gw.at[0:GLA_RANK, 0:GLA_KEY_DIM].set(gla_w_gate[l, 0])
        gw = gw.at[GLA_RANK:2 * GLA_RANK, GLA_KEY_DIM:].set(gla_w_gate[l, 1]).astype(BF16)
        gb = gla_b_gate[l].reshape(1, 2 * GLA_KEY_DIM)
        g1 = norm1_g[l].reshape(1, d)

        s5_all = jnp.zeros((nb, seq + c_len, S5_DIM), F32)
        sgu_l, s5_all, qk_l, v_l, g_l, la_l = _stage_a(xl, mod_l, g1, win, sguw, sgub, ones_sgu, gw, gb,
                                                       s5_all, 0, tb=TB_LATENT, grid_layout=True)
        sgu_c, s5_all, qk_c, v_c, g_c, la_c = _stage_a(xc, mod_c, g1, win, sguw, sgub, ones_sgu, gw, gb,
                                                       s5_all, seq, tb=TB_CTX, grid_layout=False)

        b2, ar, ai, c2 = _s5_params(s5_lambda_re[l], s5_lambda_im[l], s5_b_re[l], s5_b_im[l],
                                    s5_c_re[l], s5_c_im[l], s5_log_step[l], nb)
        yf_all, yb_all = _s5_call(s5_all, seq, b2, ar, ai, c2)

        of_c, ob_c, s_ctx = _gla_call(qk_c, v_c, la_c, s_zero, trif, trib)
        of_l, ob_l, _ = _gla_call(qk_l, v_l, la_l, s_ctx, trif, trib)

        dskip = s5_d[l].reshape(1, S5_DIM)
        wglu = s5_w_glu[l].astype(BF16)
        normg = gla_norm_g[l].reshape(1, GLA_DIM)
        wout = w_out[l].astype(BF16)
        g2 = norm2_g[l].reshape(1, d)
        wqt = jnp.swapaxes(peer_w_query[l], 0, 1).astype(BF16)
        keys = peer_sub_keys[l].reshape(2 * PEER_HEADS, PEER_NKEYS, PEER_HALF).astype(BF16)
        u_bf = _pack_row_pairs(peer_expert_u, l)
        vt_bf = _pack_row_pairs(peer_expert_v, l, transpose=True)

        xl = _stage_e(xl, mod_l, sgu_l, yf_all, yb_all, s5_all, 0, of_l, ob_l, g_l,
                      dskip, wglu, normg, ones_gla, wout, tb=TB_LATENT)
        desc = _stage_f(xl, mod_l, g2, wqt, keys, tb=PEER_TBF)
        xl = _stage_g(*desc, u_bf, vt_bf, xl.reshape(nb * seq, d), mod_l, final_g, seq, PEER_TBG,
                      final_norm=not ctx_out).reshape(nb, seq, d)

        if ctx_out:
            xc = _stage_e(xc, mod_c, sgu_c, yf_all, yb_all, s5_all, seq, of_c, ob_c, g_c,
                          dskip, wglu, normg, ones_gla, wout, tb=TB_CTX)
            desc = _stage_f(xc, mod_c, g2, wqt, keys, tb=PEER_TBF)
            xc = _stage_g(*desc, u_bf, vt_bf, xc.reshape(nb * c_len, d), mod_c, final_g, c_len,
                          min(PEER_TBG, c_len), final_norm=False).reshape(nb, c_len, d)

    return xl
```

```python
import functools
import math

import numpy as np
import jax
import jax.numpy as jnp
from jax import lax
from jax.experimental import pallas as pl
from jax.experimental.pallas import tpu as pltpu

F32 = jnp.float32
BF16 = jnp.bfloat16
SUBLANES = 8

EPS = 1e-6
N_MOD = 6
GRID_W = 64

SGU_DIM = 256
SGU_HEADS = 4
SGU_HEAD_DIM = 64
SGU_CHUNK = 128

S5_DIM = 256
S5_GROUP = 16
S5_GROUPS = 16
S5_STATE = 64
S5_LANES = S5_GROUPS * S5_STATE

GLA_DIM = 512
GLA_HEADS = 8
GLA_DV = 64
GLA_DK = 32
GLA_KEY_DIM = 256
GLA_RANK = 16
GLA_GATE_TEMP = 16.0
GLA_CHUNK = 64

PEER_HEADS = 8
PEER_NKEYS = 128
PEER_HALF = 128
PEER_TOPK = 16

IN_WIDTH = 2336
IN_PAD = 2432
LANE = 128

VMEM_LIMIT = 56 * 1024 * 1024
TB_LATENT = 512
TB_CTX = 256

NEG_INF = float("-inf")
POS_INF = float("inf")


def _cparams(sem, flags=None):
    return pltpu.CompilerParams(dimension_semantics=sem, vmem_limit_bytes=VMEM_LIMIT, flags=flags)


def _gelu(x):
    c = math.sqrt(2.0 / math.pi)
    return 0.5 * x * (1.0 + jnp.tanh(c * (x + 0.044715 * (x * x * x))))


def _sigmoid(x):
    return 1.0 / (1.0 + jnp.exp(-x))


def _dot(a, b):
    return jnp.dot(a, b, preferred_element_type=F32)


def _dot_nt(a, b):
    return lax.dot_general(a, b, (((1,), (1,)), ((), ())), preferred_element_type=F32)


def _dot_tn(a, b):
    return lax.dot_general(a, b, (((0,), (0,)), ((), ())), preferred_element_type=F32)


def _split3(x):
    hi = x.astype(BF16)
    r = x - hi.astype(F32)
    mid = r.astype(BF16)
    lo = (r - mid.astype(F32)).astype(BF16)
    return hi, mid, lo


def _dot_x_exact(x, m):
    hi, mid, lo = _split3(x)
    return _dot(hi, m) + _dot(mid, m) + _dot(lo, m)


def _dot_m_exact(m, x):
    hi, mid, lo = _split3(x)
    return _dot(m, hi) + _dot(m, mid) + _dot(m, lo)


def _full_spec(shape):
    nd = len(shape)
    return pl.BlockSpec(shape, lambda *_: (0,) * nd)


MOD_TILE = 512


def _mod_kernel(c_ref, w_ref, b_ref, o_ref):
    c = c_ref[...]
    a = c * _sigmoid(c)
    o_ref[0] = jnp.dot(a, w_ref[0], preferred_element_type=F32,
                       precision=lax.Precision.HIGHEST) + b_ref[0]


def _mod_call(cc, w_mod, b_mod):
    depth, d, nd = w_mod.shape
    rows = cc.shape[0]
    return pl.pallas_call(
        _mod_kernel,
        grid=(depth, nd // MOD_TILE),
        in_specs=[
            pl.BlockSpec((rows, d), lambda l, j: (0, 0)),
            pl.BlockSpec((1, d, MOD_TILE), lambda l, j: (l, 0, j)),
            pl.BlockSpec((1, 1, MOD_TILE), lambda l, j: (l, 0, j)),
        ],
        out_specs=pl.BlockSpec((1, rows, MOD_TILE), lambda l, j: (l, 0, j)),
        out_shape=jax.ShapeDtypeStruct((depth, rows, nd), F32),
        compiler_params=_cparams(("parallel", "parallel")),
    )(cc, w_mod, b_mod.reshape(depth, 1, nd))


def _store_tokens(ref, val, grid_layout):
    if grid_layout:
        for r in range(val.shape[0] // GRID_W):
            ref[:, r, :] = val[r * GRID_W:(r + 1) * GRID_W]
    else:
        ref[0] = val


def _load_tokens(ref, grid_layout):
    if grid_layout:
        return jnp.concatenate([ref[:, r, :] for r in range(ref.shape[1])], axis=0)
    return ref[0]


def _stage_a_kernel(x_ref, mod_ref, g1_ref, win_ref, sguw_ref, sgub_ref, ones_ref, gw_ref, gb_ref, s5_in_ref,
                    sgu_ref, s5x_ref, qk_ref, v_ref, g_ref, la_ref, *, tb, grid_layout):
    del s5_in_ref
    x = x_ref[0]
    ms = jnp.mean(x * x, axis=-1, keepdims=True)
    xn = x * lax.rsqrt(ms + EPS) * g1_ref[...]
    h = xn * (1.0 + mod_ref[0, 1:2, :]) + mod_ref[0, 0:1, :]
    cols = _dot(h.astype(BF16), win_ref[...])

    u = _gelu(cols[:, 0:SGU_DIM])
    v = _gelu(cols[:, SGU_DIM:2 * SGU_DIM])
    msq = _dot_x_exact(v * v, ones_ref[...]) * (1.0 / SGU_HEAD_DIM)
    vn = (v * lax.rsqrt(msq + EPS)).astype(BF16)
    head_of_lane = lax.broadcasted_iota(jnp.int32, (SGU_CHUNK, SGU_DIM), 1) // SGU_HEAD_DIM
    for ci in range(tb // SGU_CHUNK):
        rows = slice(ci * SGU_CHUNK, (ci + 1) * SGU_CHUNK)
        vc = vn[rows]
        mixed = sgub_ref[...]
        for hh in range(SGU_HEADS):
            mixed = mixed + jnp.where(head_of_lane == hh, _dot(sguw_ref[hh], vc), 0.0)
        sgu_ref[0, rows, :] = u[rows] * mixed

    s5x_ref[0] = cols[:, 512:768]
    q = cols[:, 768:1024] * (GLA_DK ** -0.5)
    _store_tokens(qk_ref, jnp.concatenate([q, cols[:, 1024:1280]], axis=-1), grid_layout)
    _store_tokens(v_ref, cols[:, 1280:1792], grid_layout)
    g_ref[0] = cols[:, 1792:2304]

    z = cols[:, 2304:2432].astype(BF16)
    za = _dot(z, gw_ref[...]) + gb_ref[...]
    log_sig = jnp.minimum(za, 0.0) - jnp.log1p(jnp.exp(-jnp.abs(za)))
    _store_tokens(la_ref, log_sig * (1.0 / GLA_GATE_TEMP), grid_layout)


def _stage_a(xs, mod, g1, win, sguw, sgub, ones_sgu, gw, gb, s5_all, s5_row0, tb, grid_layout):
    b, l, d = xs.shape
    assert s5_row0 % tb == 0
    kern = functools.partial(_stage_a_kernel, tb=tb, grid_layout=grid_layout)
    tok = lambda w: pl.BlockSpec((1, tb, w), lambda bi, i: (bi, i, 0))
    if grid_layout:
        assert tb % GRID_W == 0 and l % GRID_W == 0
        gla = lambda w: pl.BlockSpec((GRID_W, None, tb // GRID_W, w), lambda bi, i: (0, bi, i, 0))
        gla_shape = lambda w: jax.ShapeDtypeStruct((GRID_W, b, l // GRID_W, w), F32)
    else:
        gla = tok
        gla_shape = lambda w: jax.ShapeDtypeStruct((b, l, w), F32)
    nat_shape = lambda w: jax.ShapeDtypeStruct((b, l, w), F32)
    return pl.pallas_call(
        kern,
        grid=(b, l // tb),
        in_specs=[
            tok(d),
            pl.BlockSpec((1, 8, d), lambda bi, i: (bi, 0, 0)),
            _full_spec(g1.shape), _full_spec(win.shape), _full_spec(sguw.shape), _full_spec(sgub.shape),
            _full_spec(ones_sgu.shape), _full_spec(gw.shape), _full_spec(gb.shape),
            pl.BlockSpec(memory_space=pl.ANY),
        ],
        out_specs=[tok(SGU_DIM),
                   pl.BlockSpec((1, tb, S5_DIM), lambda bi, i: (bi, s5_row0 // tb + i, 0)),
                   gla(2 * GLA_KEY_DIM), gla(GLA_DIM), tok(GLA_DIM), gla(2 * GLA_KEY_DIM)],
        out_shape=[nat_shape(SGU_DIM), jax.ShapeDtypeStruct(s5_all.shape, F32),
                   gla_shape(2 * GLA_KEY_DIM), gla_shape(GLA_DIM), nat_shape(GLA_DIM), gla_shape(2 * GLA_KEY_DIM)],
        input_output_aliases={9: 1},
        compiler_params=_cparams(("parallel", "parallel")),
    )(xs, mod, g1, win, sguw, sgub, ones_sgu, gw, gb, s5_all)


S5_TC = 128


def _s5_kernel(uf_ref, ub_ref, perm_ref, permt_ref, b2_ref, ar_ref, ai_ref, c2_ref, yf_ref, yb_ref,
               h_ref, buf0_ref, buf1_ref, buf2_ref, *, nseq):
    tc = S5_TC
    rows = tc * 2 * nseq
    s = pl.program_id(0)

    @pl.when(s == 0)
    def _():
        h_ref[...] = jnp.zeros_like(h_ref)
        buf0_ref[...] = jnp.zeros_like(buf0_ref)
        buf1_ref[...] = jnp.zeros_like(buf1_ref)
        buf2_ref[...] = jnp.zeros_like(buf2_ref)

    def phase(scan_ref, fill_ref, read_ref):
        ar = ar_ref[...]
        ai = ai_ref[...]
        hr, hi = h_ref[:, 0:S5_LANES], h_ref[:, S5_LANES:]
        for t in range(tc):
            bur = scan_ref[t, :, 0:S5_LANES]
            bui = scan_ref[t, :, S5_LANES:]
            hr, hi = ar * hr - ai * hi + bur, ar * hi + ai * hr + bui
            scan_ref[t, :, 0:S5_LANES] = hr
            scan_ref[t, :, S5_LANES:] = hi
        h_ref[:, 0:S5_LANES] = hr
        h_ref[:, S5_LANES:] = hi

        fwd_row = lax.broadcasted_iota(jnp.int32, (rows, S5_DIM), 0) % (2 * nseq) < nseq
        hs = read_ref[...].reshape(rows, 2 * S5_LANES).astype(BF16)
        y2 = _dot(hs, c2_ref[...])
        y = jnp.where(fwd_row, y2[:, 0:S5_DIM], y2[:, S5_DIM:])
        y_hi = y.astype(BF16)
        y_lo = (y - y_hi.astype(F32)).astype(BF16)
        y_nat = _dot(permt_ref[...], y_hi) + _dot(permt_ref[...], y_lo)
        for b in range(nseq):
            yf_ref[b] = y_nat[b * tc:(b + 1) * tc]
            yb_ref[b] = y_nat[(nseq + b) * tc:(nseq + b + 1) * tc]

        x = jnp.concatenate([uf_ref[b] for b in range(nseq)] + [ub_ref[b] for b in range(nseq)], axis=0)
        u_tm = _dot(perm_ref[...], x.astype(BF16))
        u = jnp.concatenate([jnp.where(fwd_row, u_tm, 0.0), jnp.where(fwd_row, 0.0, u_tm)], axis=-1).astype(BF16)
        fill_ref[...] = _dot(u, b2_ref[...]).reshape(tc, 2 * nseq, 2 * S5_LANES)

    bufs = (buf0_ref, buf1_ref, buf2_ref)
    for k in range(3):
        @pl.when(s % 3 == k)
        def _(k=k):
            phase(bufs[(k + 2) % 3], bufs[k], bufs[(k + 1) % 3])


def _s5_permutation(nseq):
    tc = S5_TC
    p = np.zeros((tc * 2 * nseq, tc * 2 * nseq), np.float32)
    for q in range(2 * nseq):
        for t in range(tc):
            p[t * 2 * nseq + q, q * tc + (t if q < nseq else tc - 1 - t)] = 1.0
    return p


def _s5_call(s5_all, n_lat, b2, ar, ai, c2):
    nseq, t, _ = s5_all.shape
    n = t // S5_TC
    n_l = n_lat // S5_TC
    n_c = n - n_l
    perm = _s5_permutation(nseq)
    kern = functools.partial(_s5_kernel, nseq=nseq)
    fwd_blk = lambda k: jnp.where(k < n_c, n_l + k, k - n_c)
    bwd_blk = lambda k: n - 1 - k
    in_k = lambda s: jnp.minimum(s, n - 1)
    out_k = lambda s: jnp.clip(s - 2, 0, n - 1)
    blk = (nseq, S5_TC, S5_DIM)
    rows = 2 * nseq
    buf = pltpu.VMEM((S5_TC, rows, 2 * S5_LANES), F32)
    return pl.pallas_call(
        kern,
        grid=(n + 2,),
        in_specs=[
            pl.BlockSpec(blk, lambda s: (0, fwd_blk(in_k(s)), 0)), pl.BlockSpec(blk, lambda s: (0, bwd_blk(in_k(s)), 0)),
            _full_spec(perm.shape), _full_spec(perm.shape),
            _full_spec(b2.shape), _full_spec(ar.shape), _full_spec(ai.shape), _full_spec(c2.shape),
        ],
        out_specs=[pl.BlockSpec(blk, lambda s: (0, fwd_blk(out_k(s)), 0)),
                   pl.BlockSpec(blk, lambda s: (0, bwd_blk(out_k(s)), 0))],
        out_shape=[jax.ShapeDtypeStruct(s5_all.shape, F32), jax.ShapeDtypeStruct(s5_all.shape, F32)],
        scratch_shapes=[pltpu.VMEM((rows, 2 * S5_LANES), F32), buf, buf, buf],
        compiler_params=_cparams(("arbitrary",)),
    )(s5_all, s5_all, jnp.asarray(perm, dtype=BF16), jnp.asarray(perm.T, dtype=BF16), b2, ar, ai, c2)


def _gla_kernel(qkf_ref, vf_ref, laf_ref, qkb_ref, vb_ref, lab_ref, s0_ref, trif_ref, trib_ref,
                of_ref, ob_ref, sout_ref, s_ref, *, nb):
    c = pl.program_id(0)

    @pl.when(c == 0)
    def _():
        s_ref[...] = s0_ref[...]

    ch = GLA_CHUNK
    r_k = lax.broadcasted_iota(jnp.int32, (GLA_HEADS * ch, GLA_KEY_DIM), 0) // ch
    c_k = lax.broadcasted_iota(jnp.int32, (GLA_HEADS * ch, GLA_KEY_DIM), 1) // GLA_DK
    hm_k = r_k == c_k
    r_v = lax.broadcasted_iota(jnp.int32, (GLA_HEADS * ch, GLA_DIM), 0) // ch
    c_v = lax.broadcasted_iota(jnp.int32, (GLA_HEADS * ch, GLA_DIM), 1) // GLA_DV
    hm_v = r_v == c_v
    r_s = lax.broadcasted_iota(jnp.int32, (GLA_DIM, GLA_KEY_DIM), 0) // GLA_DV
    c_s = lax.broadcasted_iota(jnp.int32, (GLA_DIM, GLA_KEY_DIM), 1) // GLA_DK
    hm_s = r_s == c_s
    t_i = lax.broadcasted_iota(jnp.int32, (ch, GLA_HEADS * ch), 0)
    s_i = lax.broadcasted_iota(jnp.int32, (ch, GLA_HEADS * ch), 1) % ch
    mask_f = t_i >= s_i
    mask_b = t_i <= s_i
    trif = trif_ref[...]
    trib = trib_ref[...]

    fwd = dict(qk=qkf_ref, v=vf_ref, la=laf_ref, o=of_ref, tri=trif, last=ch - 1, ref=ch // 2, mask=mask_f, d=0)
    bwd = dict(qk=qkb_ref, v=vb_ref, la=lab_ref, o=ob_ref, tri=trib, last=0, ref=ch - 1 - ch // 2, mask=mask_b, d=1)
    streams = [(b, p) for b in range(nb) for p in (fwd, bwd)]

    bcums = [_dot_m_exact(p["tri"], p["la"][b]) for b, p in streams]
    scaled = []
    for (b, p), bcum in zip(streams, bcums):
        qk = p["qk"][b]
        q, k = qk[:, 0:GLA_KEY_DIM], qk[:, GLA_KEY_DIM:]
        blast = bcum[p["last"]:p["last"] + 1]
        bref = bcum[p["ref"]:p["ref"] + 1]
        qe = (q * jnp.exp(bcum)).astype(BF16)
        qd = (q * jnp.exp(bcum - bref)).astype(BF16)
        kd = k * jnp.exp(bref - bcum)
        kdec = (k * jnp.exp(blast - bcum)).astype(BF16)
        kst = jnp.where(hm_k, jnp.concatenate([kd] * GLA_HEADS, axis=0), 0.0).astype(BF16)
        scaled.append((qe, qd, kdec, kst, jnp.exp(blast)))
    prods = []
    for (b, p), (qe, qd, kdec, kst, _) in zip(streams, scaled):
        v = p["v"][b]
        sc = _dot_nt(qd, kst)
        o_inter = _dot_nt(qe, s_ref[b, p["d"]].astype(BF16))
        kv_t = _dot_tn(v.astype(BF16), kdec)
        prods.append((sc, o_inter, kv_t))
    for (b, p), (_, _, _, _, decay), (sc, o_inter, kv_t) in zip(streams, scaled, prods):
        vbd = jnp.where(hm_v, jnp.concatenate([p["v"][b]] * GLA_HEADS, axis=0), 0.0).astype(BF16)
        p["o"][b] = _dot(jnp.where(p["mask"], sc, 0.0).astype(BF16), vbd) + o_inter
        s_ref[b, p["d"]] = s_ref[b, p["d"]] * decay + jnp.where(hm_s, kv_t, 0.0)

    @pl.when(c == pl.num_programs(0) - 1)
    def _():
        sout_ref[...] = s_ref[...]


def _gla_call(qk, v, la, s0, trif, trib):
    ch = GLA_CHUNK
    if qk.ndim == 4:
        n, b = qk.shape[0], qk.shape[1]
        assert qk.shape[2] == ch
        spec = lambda w, off, rev: pl.BlockSpec(
            (None, b, ch, w), (lambda c: (n - 1 - c, 0, 0, off)) if rev else (lambda c: (c, 0, 0, off)))
    else:
        b = qk.shape[0]
        n = qk.shape[1] // ch
        spec = lambda w, off, rev: pl.BlockSpec(
            (b, ch, w), (lambda c: (0, n - 1 - c, off)) if rev else (lambda c: (0, c, off)))
    kern = functools.partial(_gla_kernel, nb=b)
    o_shape = jax.ShapeDtypeStruct(v.shape, F32)
    return pl.pallas_call(
        kern,
        grid=(n,),
        in_specs=[
            spec(2 * GLA_KEY_DIM, 0, False), spec(GLA_DIM, 0, False), spec(GLA_KEY_DIM, 0, False),
            spec(2 * GLA_KEY_DIM, 0, True), spec(GLA_DIM, 0, True), spec(GLA_KEY_DIM, 1, True),
            _full_spec(s0.shape), _full_spec(trif.shape), _full_spec(trib.shape),
        ],
        out_specs=[spec(GLA_DIM, 0, False), spec(GLA_DIM, 0, True), _full_spec(s0.shape)],
        out_shape=[o_shape, o_shape, jax.ShapeDtypeStruct(s0.shape, F32)],
        scratch_shapes=[pltpu.VMEM(s0.shape, F32)],
        compiler_params=_cparams(("arbitrary",)),
    )(qk, v, la, qk, v, la, s0, trif, trib)


def _stage_e_kernel(x_ref, mod_ref, sgu_ref, yf_ref, yb_ref, s5x_ref, of_ref, ob_ref, g_ref,
                    dskip_ref, wglu_ref, normg_ref, ones_ref, wout_ref, o_ref, *, grid_layout):
    ys = yf_ref[0] + yb_ref[0] + dskip_ref[...] * s5x_ref[0]
    z = _dot(_gelu(ys).astype(BF16), wglu_ref[...])
    s5o = z[:, 0:S5_DIM] * _sigmoid(z[:, S5_DIM:])
    o = _load_tokens(of_ref, grid_layout) + _load_tokens(ob_ref, grid_layout)
    ms = _dot_x_exact(o * o, ones_ref[...]) * (1.0 / GLA_DV)
    g = g_ref[0]
    gl = o * lax.rsqrt(ms + EPS) * normg_ref[...] * (g * _sigmoid(g))
    y = (_dot(sgu_ref[0].astype(BF16), wout_ref[0:SGU_DIM, :])
         + _dot(s5o.astype(BF16), wout_ref[SGU_DIM:SGU_DIM + S5_DIM, :])
         + _dot(gl.astype(BF16), wout_ref[SGU_DIM + S5_DIM:, :]))
    o_ref[0] = x_ref[0] + mod_ref[0, 2:3, :] * y


def _stage_e(xs, mod, sgu, yf_all, yb_all, s5_all, s5_row0, of, ob, g, dskip, wglu, normg, ones_gla, wout, tb):
    b, l, d = xs.shape
    assert s5_row0 % tb == 0
    grid_layout = of.ndim == 4
    tok = lambda w: pl.BlockSpec((1, tb, w), lambda bi, i: (bi, i, 0))
    s5 = pl.BlockSpec((1, tb, S5_DIM), lambda bi, i: (bi, s5_row0 // tb + i, 0))
    if grid_layout:
        assert tb % GRID_W == 0
        gla = pl.BlockSpec((GRID_W, None, tb // GRID_W, GLA_DIM), lambda bi, i: (0, bi, i, 0))
    else:
        gla = tok(GLA_DIM)
    return pl.pallas_call(
        functools.partial(_stage_e_kernel, grid_layout=grid_layout),
        grid=(b, l // tb),
        in_specs=[
            tok(d), pl.BlockSpec((1, 8, d), lambda bi, i: (bi, 0, 0)),
            tok(SGU_DIM), s5, s5, s5, gla, gla, tok(GLA_DIM),
            _full_spec(dskip.shape), _full_spec(wglu.shape), _full_spec(normg.shape),
            _full_spec(ones_gla.shape), _full_spec(wout.shape),
        ],
        out_specs=tok(d),
        out_shape=jax.ShapeDtypeStruct((b, l, d), F32),
        compiler_params=_cparams(("parallel", "parallel")),
    )(xs, mod, sgu, yf_all, yb_all, s5_all, of, ob, g, dskip, wglu, normg, ones_gla, wout)


PEER_TBF = 256
AUX_ROWS = 16
AUX_THETA, AUX_V1_TOP, AUX_V1_LAST, AUX_N_TOP, AUX_RZ = 8, 9, 10, 11, 12


def _sort_network_16():
    def merge(lo, hi, r):
        step = r * 2
        if step < hi - lo:
            yield from merge(lo, hi, step)
            yield from merge(lo + r, hi, step)
            for i in range(lo + r, hi - r, step):
                yield (i, i + r)
        else:
            yield (lo, lo + r)

    def sort(lo, hi):
        if hi - lo >= 1:
            mid = lo + (hi - lo) // 2
            yield from sort(lo, mid)
            yield from sort(mid + 1, hi)
            yield from merge(lo, hi, 1)

    return tuple(sort(0, PEER_TOPK - 1))


SORT16 = _sort_network_16()
BITONIC16 = tuple((k, k + s) for s in (8, 4, 2, 1) for k in range(PEER_TOPK) if not k & s)


def _compare_exchange(xs, pairs):
    xs = list(xs)
    for i, j in pairs:
        hi = jnp.maximum(xs[i], xs[j])
        lo = jnp.minimum(xs[i], xs[j])
        xs[i], xs[j] = hi, lo
    return xs


def _merge_sublanes(xs):
    for shift in (4, 6, 7):
        rolled = [pltpu.roll(x, shift, 0) for x in xs]
        xs = [jnp.maximum(xs[k], rolled[PEER_TOPK - 1 - k]) for k in range(PEER_TOPK)]
        xs = _compare_exchange(xs, BITONIC16)
    return xs


def _dup_bf16_words(x):
    bits = pltpu.bitcast(x.astype(BF16).astype(F32), jnp.int32)
    return bits | lax.shift_right_logical(bits, 16)


def _stage_f_kernel(x_ref, mod_ref, g2_ref, wqt_ref, keys_ref,
                    ht_ref, s1_ref, aux_ref, e2_ref, r2_ref, qt_ref, v1_ref, v2_ref, *, tb):
    x = x_ref[0]
    ms = jnp.mean(x * x, axis=-1, keepdims=True)
    xn = x * lax.rsqrt(ms + EPS) * g2_ref[...]
    h = xn * (1.0 + mod_ref[0, 4:5, :]) + mod_ref[0, 3:4, :]
    ht = h.T.astype(BF16)
    ht_ref[...] = pltpu.bitcast(ht, jnp.int32)
    qt_ref[...] = _dot(wqt_ref[...], ht)
    k_top = PEER_TOPK

    def tiles(s):
        return [s[SUBLANES * k:SUBLANES * (k + 1)] for k in range(PEER_NKEYS // SUBLANES)]

    def head_body(hh, carry):
        for tc in range(tb // LANE):
            tcol = slice(tc * LANE, (tc + 1) * LANE)
            r1 = hh * (2 * PEER_HALF)
            q1 = qt_ref[r1:r1 + PEER_HALF, tcol].astype(BF16)
            q2 = qt_ref[r1 + PEER_HALF:r1 + 2 * PEER_HALF, tcol].astype(BF16)
            s1 = _dot(keys_ref[hh], q1)
            s2 = _dot(keys_ref[PEER_HEADS + hh], q2)
            for s, v_ref in ((s1, v1_ref), (s2, v2_ref)):
                top = _merge_sublanes(_compare_exchange(tiles(s), SORT16))
                for k in range(k_top):
                    v_ref[k:k + 1, tcol] = top[k][0:1]
            v1row = lambda a: v1_ref[a:a + 1, tcol]
            v2row = lambda b: v2_ref[b:b + 1, tcol]
            v1lo = v1_ref[0:SUBLANES, tcol]
            v2lo = v2_ref[0:SUBLANES, tcol]
            cand = [v1lo + v2row(b) for b in range(k_top)]
            tail = [v1row(a) + v2lo for a in range(SUBLANES, k_top)]
            for k in range(SUBLANES, k_top):
                cand[k] = jnp.maximum(cand[k], tail[k_top - 1 - k])
            best = _merge_sublanes(_compare_exchange(cand, BITONIC16))
            theta = best[k_top - 1][0:1]
            cmax = best[0][0:1]
            zsum = jnp.zeros((1, LANE), F32)
            for k in range(k_top):
                zsum = zsum + jnp.exp(best[k][0:1] - cmax)
            rz = 1.0 / zsum
            n_top = jnp.zeros((1, LANE), F32)
            for b in range(k_top):
                n_top = jnp.where(v1row(0) + v2row(b) >= theta, float(b + 1), n_top)
            r2 = jnp.zeros(s2.shape, F32)
            for b in range(k_top):
                r2 = jnp.where(v2row(b) > s2, float(b + 1), r2)
            e2 = jnp.where(s2 >= v2row(k_top - 1), jnp.exp(s2 - v2row(0)), 0.0)
            s1_ref[tc, hh] = s1
            aux_ref[tc, hh, 0:SUBLANES, :] = v2lo
            aux_ref[tc, hh, SUBLANES:, :] = jnp.zeros((AUX_ROWS - SUBLANES, LANE), F32)
            for row, val in ((AUX_THETA, theta), (AUX_V1_TOP, v1row(0)), (AUX_V1_LAST, v1row(k_top - 1)),
                             (AUX_N_TOP, n_top), (AUX_RZ, rz)):
                aux_ref[tc, hh, row:row + 1, :] = val
            e2_ref[tc, hh] = pltpu.bitcast(e2.astype(BF16), jnp.int32)
            r2_ref[tc, hh] = pltpu.bitcast(r2.astype(BF16), jnp.int32)
        return carry

    for hh in range(PEER_HEADS):
        head_body(hh, 0)


def _stage_f(xs, mod, g2, wqt, keys, tb):
    b, l, d = xs.shape
    nblk = l // tb
    ntok = b * l
    nch = ntok // LANE
    kern = functools.partial(_stage_f_kernel, tb=tb)
    row_spec = pl.BlockSpec((tb // LANE, PEER_HEADS, PEER_NKEYS, LANE), lambda bi, i: (bi * nblk + i, 0, 0, 0))
    pair_spec = pl.BlockSpec((tb // LANE, PEER_HEADS, PEER_NKEYS // 2, LANE), lambda bi, i: (bi * nblk + i, 0, 0, 0))
    desc_shape = lambda rows: jax.ShapeDtypeStruct((nch, PEER_HEADS, rows, LANE), jnp.int32)
    return pl.pallas_call(
        kern,
        grid=(b, nblk),
        in_specs=[
            pl.BlockSpec((1, tb, d), lambda bi, i: (bi, i, 0)),
            pl.BlockSpec((1, 8, d), lambda bi, i: (bi, 0, 0)),
            _full_spec(g2.shape), _full_spec(wqt.shape), _full_spec(keys.shape),
        ],
        out_specs=[pl.BlockSpec((d // 2, tb), lambda bi, i: (0, bi * nblk + i)),
                   row_spec,
                   pl.BlockSpec((tb // LANE, PEER_HEADS, AUX_ROWS, LANE), lambda bi, i: (bi * nblk + i, 0, 0, 0)),
                   pair_spec, pair_spec],
        out_shape=[jax.ShapeDtypeStruct((d // 2, ntok), jnp.int32),
                   jax.ShapeDtypeStruct((nch, PEER_HEADS, PEER_NKEYS, LANE), F32),
                   jax.ShapeDtypeStruct((nch, PEER_HEADS, AUX_ROWS, LANE), F32),
                   desc_shape(PEER_NKEYS // 2), desc_shape(PEER_NKEYS // 2)],
        scratch_shapes=[
            pltpu.VMEM((PEER_HEADS * 2 * PEER_HALF, tb), F32),
            pltpu.VMEM((PEER_TOPK, tb), F32),
            pltpu.VMEM((PEER_TOPK, tb), F32),
        ],
        compiler_params=_cparams(("parallel", "parallel")),
    )(xs, mod, g2, wqt, keys)


PEER_TBG = 1024
PEER_TE = 1024
PEER_I1_PER_TILE = PEER_TE // PEER_NKEYS
PEER_N_TILES = PEER_NKEYS * PEER_NKEYS // PEER_TE
PEER_MXU_COLS = 256
PEER_G_FLAGS = None


def _stage_g_kernel(htw_ref, s1_ref, aux_ref, e2_ref, r2_ref, uw_ref, vtw_ref, x_ref, mod_ref, fg_ref,
                    o_ref, acc_ref, ata_ref, atb_ref, p_ref, e1w_ref, n1w_ref, *, tb, final_norm):
    s = pl.program_id(1)

    @pl.when(s == 0)
    def _():
        acc_ref[...] = jnp.zeros_like(acc_ref)

    def step(at_cur_ref, at_next_ref):
        per_grp = PEER_MXU_COLS // LANE
        for grp in range(tb // PEER_MXU_COLS):
            cols = slice(grp * PEER_MXU_COLS, (grp + 1) * PEER_MXU_COLS)
            for tcl in range(per_grp if at_cur_ref is not None else 0):
                tc = grp * per_grp + tcl
                tcol = slice(tc * LANE, (tc + 1) * LANE)
                for hh in range(PEER_HEADS):
                    s1 = s1_ref[tc, hh]
                    aux = lambda r: aux_ref[tc, hh, r:r + 1, :]
                    n1 = jnp.zeros(s1.shape, F32)
                    for b in range(SUBLANES):
                        n1 = jnp.where(s1 + aux(b) >= aux(AUX_THETA), float(b + 1), n1)
                    n1 = jnp.where(s1 >= aux(AUX_V1_TOP), aux(AUX_N_TOP), n1)
                    e1 = jnp.where(s1 >= aux(AUX_V1_LAST), jnp.exp(s1 - aux(AUX_V1_TOP)), 0.0) * aux(AUX_RZ)
                    e1w_ref[tc, hh] = _dup_bf16_words(e1)
                    n1w_ref[tc, hh] = _dup_bf16_words(n1)
                for i1l in range(PEER_I1_PER_TILE):
                    rows = slice(i1l * PEER_NKEYS, (i1l + 1) * PEER_NKEYS)
                    gate = jnp.zeros((PEER_NKEYS, LANE), BF16)
                    for hh in range(PEER_HEADS):
                        e1row = e1w_ref[tc, hh, i1l:i1l + 1, :]
                        n1row = n1w_ref[tc, hh, i1l:i1l + 1, :]
                        e1 = pltpu.bitcast(jnp.broadcast_to(e1row, (PEER_NKEYS // 2, LANE)), BF16)
                        n1 = pltpu.bitcast(jnp.broadcast_to(n1row, (PEER_NKEYS // 2, LANE)), BF16)
                        r2 = pltpu.bitcast(r2_ref[tc, hh], BF16)
                        e2 = pltpu.bitcast(e2_ref[tc, hh], BF16)
                        gate = gate + e2 * jnp.where(r2 < n1, e1, 0.0)
                    p_ref[rows, tcol] = gate * _gelu(at_cur_ref[rows, tcol])
            if at_cur_ref is not None:
                vt = pltpu.bitcast(vtw_ref[...], BF16)
                acc_ref[:, cols] += _dot(vt, p_ref[:, cols])
            if at_next_ref is not None:
                u = pltpu.bitcast(uw_ref[...], BF16)
                ht = pltpu.bitcast(htw_ref[:, cols], BF16)
                at_next_ref[:, cols] = _dot(u, ht).astype(BF16)

    last = pl.num_programs(1) - 1

    @pl.when(s == 0)
    def _():
        step(None, ata_ref)

    @pl.when(jnp.logical_and(s % 2 == 0, jnp.logical_and(s > 0, s < last)))
    def _():
        step(atb_ref, ata_ref)

    @pl.when(s % 2 == 1)
    def _():
        step(ata_ref, atb_ref)

    @pl.when(s == last)
    def _():
        step(atb_ref, None)
        xo = x_ref[...] + mod_ref[0, 5:6, :] * acc_ref[...].T
        if final_norm:
            ms = jnp.mean(xo * xo, axis=-1, keepdims=True)
            xo = xo * lax.rsqrt(ms + EPS) * fg_ref[...]
        o_ref[...] = xo


def _stage_g(htw, s1, aux, e2, r2, uw, vtw, xflat, mod, final_g, tokens_per_batch, tb, final_norm):
    ntok = htw.shape[1]
    d = 2 * htw.shape[0]
    assert 2 * uw.shape[0] == PEER_N_TILES * PEER_TE and PEER_N_TILES % 2 == 0
    blocks_per_batch = tokens_per_batch // tb
    kern = functools.partial(_stage_g_kernel, tb=tb, final_norm=final_norm)
    last = PEER_N_TILES - 1
    desc_spec = pl.BlockSpec((tb // LANE, PEER_HEADS, PEER_NKEYS // 2, LANE), lambda j, i: (j, 0, 0, 0))
    row_spec = pl.BlockSpec((tb // LANE, PEER_HEADS, PEER_I1_PER_TILE, LANE),
                            lambda j, i: (j, 0, jnp.maximum(i - 1, 0), 0))
    return pl.pallas_call(
        kern,
        grid=(ntok // tb, PEER_N_TILES + 1),
        in_specs=[
            pl.BlockSpec((d // 2, tb), lambda j, i: (0, j)),
            row_spec,
            pl.BlockSpec((tb // LANE, PEER_HEADS, AUX_ROWS, LANE), lambda j, i: (j, 0, 0, 0)),
            desc_spec, desc_spec,
            pl.BlockSpec((PEER_TE // 2, d), lambda j, i: (jnp.minimum(i, last), 0)),
            pl.BlockSpec((d // 2, PEER_TE), lambda j, i: (0, jnp.maximum(i - 1, 0))),
            pl.BlockSpec((tb, d), lambda j, i: (j, 0)),
            pl.BlockSpec((1, 8, d), lambda j, i: (j // blocks_per_batch, 0, 0)),
            _full_spec(final_g.shape),
        ],
        out_specs=pl.BlockSpec((tb, d), lambda j, i: (j, 0)),
        out_shape=jax.ShapeDtypeStruct((ntok, d), F32),
        scratch_shapes=[
            pltpu.VMEM((d, tb), F32),
            pltpu.VMEM((PEER_TE, tb), BF16),
            pltpu.VMEM((PEER_TE, tb), BF16),
            pltpu.VMEM((PEER_TE, tb), BF16),
            pltpu.VMEM((tb // LANE, PEER_HEADS, PEER_I1_PER_TILE, LANE), jnp.int32),
            pltpu.VMEM((tb // LANE, PEER_HEADS, PEER_I1_PER_TILE, LANE), jnp.int32),
        ],
        compiler_params=_cparams(("parallel", "arbitrary"), PEER_G_FLAGS),
    )(htw, s1, aux, e2, r2, uw, vtw, xflat, mod, final_g)


def _pack_kernel(x_ref, o_ref, *, transpose):
    x = x_ref[...]
    if transpose:
        x = x.T
    o_ref[...] = pltpu.bitcast(x.astype(BF16), jnp.int32)


def _pack_row_pairs(x, layer, transpose=False, tile=1024):
    _, r, c = x.shape
    if transpose:
        out_shape, out_spec = (c // 2, r), pl.BlockSpec((c // 2, tile), lambda i: (0, i))
    else:
        out_shape, out_spec = (r // 2, c), pl.BlockSpec((tile // 2, c), lambda i: (i, 0))
    return pl.pallas_call(
        functools.partial(_pack_kernel, transpose=transpose),
        grid=(r // tile,),
        in_specs=[pl.BlockSpec((None, tile, c), lambda i: (layer, i, 0))],
        out_specs=out_spec,
        out_shape=jax.ShapeDtypeStruct(out_shape, jnp.int32),
        compiler_params=_cparams(("parallel",)),
    )(x)


def _block_ones(n, blk):
    idx = np.arange(n) // blk
    return jnp.asarray((idx[:, None] == idx[None, :]).astype(np.float32), dtype=BF16)


def _s5_discretise(lam_re, lam_im, b_re, b_im, log_step):
    lam_re = jnp.minimum(lam_re.astype(F32), -1e-4)
    lam_im = lam_im.astype(F32)
    dt = jnp.exp(log_step.astype(F32))[:, None]
    mag = jnp.exp(lam_re * dt)
    a_re = mag * jnp.cos(lam_im * dt)
    a_im = mag * jnp.sin(lam_im * dt)
    den = lam_re * lam_re + lam_im * lam_im
    f_re = ((a_re - 1.0) * lam_re + a_im * lam_im) / den
    f_im = (a_im * lam_re - (a_re - 1.0) * lam_im) / den
    b_re = b_re.astype(F32)
    b_im = b_im.astype(F32)
    bb_re = f_re[..., None] * b_re - f_im[..., None] * b_im
    bb_im = f_re[..., None] * b_im + f_im[..., None] * b_re
    return a_re, a_im, bb_re, bb_im


def _group_block_diag(t):
    g, r, c = t.shape
    eye = jnp.eye(g, dtype=t.dtype)
    return (t[:, :, None, :] * eye[:, None, :, None]).reshape(g * r, g * c)


def _s5_params(lam_re, lam_im, b_re, b_im, c_re, c_im, log_step, nseq):
    b_rows, c_cols, ars, ais = [], [], [], []
    for d in range(2):
        a_re, a_im, bb_re, bb_im = _s5_discretise(lam_re[d], lam_im[d], b_re[d], b_im[d], log_step[d])
        bm = jnp.concatenate([_group_block_diag(jnp.swapaxes(bb_re, 1, 2)),
                              _group_block_diag(jnp.swapaxes(bb_im, 1, 2))], axis=1)
        b_rows.append(bm)
        cm = jnp.concatenate([_group_block_diag(jnp.swapaxes(c_re[d].astype(F32), 1, 2)),
                              -_group_block_diag(jnp.swapaxes(c_im[d].astype(F32), 1, 2))], axis=0)
        c_cols.append(cm)
        ars.append(jnp.broadcast_to(a_re.reshape(1, S5_LANES), (nseq, S5_LANES)))
        ais.append(jnp.broadcast_to(a_im.reshape(1, S5_LANES), (nseq, S5_LANES)))
    b2 = jnp.concatenate(b_rows, axis=0).astype(BF16)
    c2 = jnp.concatenate(c_cols, axis=1).astype(BF16)
    return b2, jnp.concatenate(ars, axis=0), jnp.concatenate(ais, axis=0), c2


def kernel(x, c, ctx, c_ctx, w_mod, b_mod, norm1_g, norm2_g, w_in, w_out, sgu_w, sgu_b, s5_lambda_re, s5_lambda_im, s5_b_re, s5_b_im, s5_c_re, s5_c_im, s5_log_step, s5_d, s5_w_glu, gla_w_gate, gla_b_gate, gla_norm_g, peer_w_query, peer_sub_keys, peer_expert_u, peer_expert_v, final_norm_g):
    nb, seq, d = x.shape
    c_len = ctx.shape[1]
    depth = w_mod.shape[0]

    cc = jnp.concatenate([c, c_ctx[None, :], jnp.zeros((8 - nb - 1, d), F32)], axis=0)
    mods = _mod_call(cc, w_mod, b_mod)

    ones_sgu = _block_ones(SGU_DIM, SGU_HEAD_DIM)
    ones_gla = _block_ones(GLA_DIM, GLA_DV)
    tri_np = np.tril(np.ones((GLA_CHUNK, GLA_CHUNK), np.float32))
    trif = jnp.asarray(tri_np, dtype=BF16)
    trib = jnp.asarray(tri_np.T, dtype=BF16)
    s_zero = jnp.zeros((nb, 2, GLA_DIM, GLA_KEY_DIM), F32)
    final_g = final_norm_g.reshape(1, d)

    xl, xc = x, ctx
    for l in range(depth):
        ctx_out = l < depth - 1
        m6 = mods[l].reshape(8, N_MOD, d)
        mod_l = jnp.pad(m6[:nb], ((0, 0), (0, 2), (0, 0)))
        mod_c = jnp.broadcast_to(jnp.pad(m6[nb], ((0, 2), (0, 0)))[None], (nb, 8, d))

        win = jnp.pad(w_in[l], ((0, 0), (0, IN_PAD - IN_WIDTH))).astype(BF16)
        sguw = sgu_w[l].astype(BF16)
        sgub = jnp.repeat(jnp.swapaxes(sgu_b[l], 0, 1), SGU_HEAD_DIM, axis=1)
        gw = jnp.zeros((LANE, 2 * GLA_KEY_DIM), F32)
        gw = gw.at[0:GLA_RANK, 0:GLA_KEY_DIM].set(gla_w_gate[l, 0])
        gw = gw.at[GLA_RANK:2 * GLA_RANK, GLA_KEY_DIM:].set(gla_w_gate[l, 1]).astype(BF16)
        gb = gla_b_gate[l].reshape(1, 2 * GLA_KEY_DIM)
        g1 = norm1_g[l].reshape(1, d)

        s5_all = jnp.zeros((nb, seq + c_len, S5_DIM), F32)
        sgu_l, s5_all, qk_l, v_l, g_l, la_l = _stage_a(xl, mod_l, g1, win, sguw, sgub, ones_sgu, gw, gb,
                                                       s5_all, 0, tb=TB_LATENT, grid_layout=True)
        sgu_c, s5_all, qk_c, v_c, g_c, la_c = _stage_a(xc, mod_c, g1, win, sguw, sgub, ones_sgu, gw, gb,
                                                       s5_all, seq, tb=TB_CTX, grid_layout=False)

        b2, ar, ai, c2 = _s5_params(s5_lambda_re[l], s5_lambda_im[l], s5_b_re[l], s5_b_im[l],
                                    s5_c_re[l], s5_c_im[l], s5_log_step[l], nb)
        yf_all, yb_all = _s5_call(s5_all, seq, b2, ar, ai, c2)

        of_c, ob_c, s_ctx = _gla_call(qk_c, v_c, la_c, s_zero, trif, trib)
        of_l, ob_l, _ = _gla_call(qk_l, v_l, la_l, s_ctx, trif, trib)

        dskip = s5_d[l].reshape(1, S5_DIM)
        wglu = s5_w_glu[l].astype(BF16)
        normg = gla_norm_g[l].reshape(1, GLA_DIM)
        wout = w_out[l].astype(BF16)
        g2 = norm2_g[l].reshape(1, d)
        wqt = jnp.swapaxes(peer_w_query[l], 0, 1).astype(BF16)
        keys = peer_sub_keys[l].reshape(2 * PEER_HEADS, PEER_NKEYS, PEER_HALF).astype(BF16)
        u_bf = _pack_row_pairs(peer_expert_u, l)
        vt_bf = _pack_row_pairs(peer_expert_v, l, transpose=True)

        xl = _stage_e(xl, mod_l, sgu_l, yf_all, yb_all, s5_all, 0, of_l, ob_l, g_l,
                      dskip, wglu, normg, ones_gla, wout, tb=TB_LATENT)
        desc = _stage_f(xl, mod_l, g2, wqt, keys, tb=PEER_TBF)
        xl = _stage_g(*desc, u_bf, vt_bf, xl.reshape(nb * seq, d), mod_l, final_g, seq, PEER_TBG,
                      final_norm=not ctx_out).reshape(nb, seq, d)

        if ctx_out:
            xc = _stage_e(xc, mod_c, sgu_c, yf_all, yb_all, s5_all, seq, of_c, ob_c, g_c,
                          dskip, wglu, normg, ones_gla, wout, tb=TB_CTX)
            desc = _stage_f(xc, mod_c, g2, wqt, keys, tb=PEER_TBF)
            xc = _stage_g(*desc, u_bf, vt_bf, xc.reshape(nb * c_len, d), mod_c, final_g, c_len,
                          min(PEER_TBG, c_len), final_norm=False).reshape(nb, c_len, d)

    return xl
```

```python
import functools
import math

import numpy as np
import jax
import jax.numpy as jnp
from jax import lax
from jax.experimental import pallas as pl
from jax.experimental.pallas import tpu as pltpu

F32 = jnp.float32
BF16 = jnp.bfloat16
SUBLANES = 8

EPS = 1e-6
N_MOD = 6
GRID_W = 64

SGU_DIM = 256
SGU_HEADS = 4
SGU_HEAD_DIM = 64
SGU_CHUNK = 128

S5_DIM = 256
S5_GROUP = 16
S5_GROUPS = 16
S5_STATE = 64
S5_LANES = S5_GROUPS * S5_STATE

GLA_DIM = 512
GLA_HEADS = 8
GLA_DV = 64
GLA_DK = 32
GLA_KEY_DIM = 256
GLA_RANK = 16
GLA_GATE_TEMP = 16.0
GLA_CHUNK = 64

PEER_HEADS = 8
PEER_NKEYS = 128
PEER_HALF = 128
PEER_TOPK = 16

IN_WIDTH = 2336
IN_PAD = 2432
LANE = 128

VMEM_LIMIT = 56 * 1024 * 1024
TB_LATENT = 512
TB_CTX = 256

NEG_INF = float("-inf")
POS_INF = float("inf")


def _cparams(sem, flags=None):
    return pltpu.CompilerParams(dimension_semantics=sem, vmem_limit_bytes=VMEM_LIMIT, flags=flags)


def _gelu(x):
    c = math.sqrt(2.0 / math.pi)
    return 0.5 * x * (1.0 + jnp.tanh(c * (x + 0.044715 * (x * x * x))))


def _sigmoid(x):
    return 1.0 / (1.0 + jnp.exp(-x))


def _dot(a, b):
    return jnp.dot(a, b, preferred_element_type=F32)


def _dot_nt(a, b):
    return lax.dot_general(a, b, (((1,), (1,)), ((), ())), preferred_element_type=F32)


def _dot_tn(a, b):
    return lax.dot_general(a, b, (((0,), (0,)), ((), ())), preferred_element_type=F32)


def _split3(x):
    hi = x.astype(BF16)
    r = x - hi.astype(F32)
    mid = r.astype(BF16)
    lo = (r - mid.astype(F32)).astype(BF16)
    return hi, mid, lo


def _dot_x_exact(x, m):
    hi, mid, lo = _split3(x)
    return _dot(hi, m) + _dot(mid, m) + _dot(lo, m)


def _dot_m_exact(m, x):
    hi, mid, lo = _split3(x)
    return _dot(m, hi) + _dot(m, mid) + _dot(m, lo)


def _full_spec(shape):
    nd = len(shape)
    return pl.BlockSpec(shape, lambda *_: (0,) * nd)


MOD_TILE = 512


def _mod_kernel(c_ref, w_ref, b_ref, o_ref):
    c = c_ref[...]
    a = c * _sigmoid(c)
    o_ref[0] = jnp.dot(a, w_ref[0], preferred_element_type=F32,
                       precision=lax.Precision.HIGHEST) + b_ref[0]


def _mod_call(cc, w_mod, b_mod):
    depth, d, nd = w_mod.shape
    rows = cc.shape[0]
    return pl.pallas_call(
        _mod_kernel,
        grid=(depth, nd // MOD_TILE),
        in_specs=[
            pl.BlockSpec((rows, d), lambda l, j: (0, 0)),
            pl.BlockSpec((1, d, MOD_TILE), lambda l, j: (l, 0, j)),
            pl.BlockSpec((1, 1, MOD_TILE), lambda l, j: (l, 0, j)),
        ],
        out_specs=pl.BlockSpec((1, rows, MOD_TILE), lambda l, j: (l, 0, j)),
        out_shape=jax.ShapeDtypeStruct((depth, rows, nd), F32),
        compiler_params=_cparams(("parallel", "parallel")),
    )(cc, w_mod, b_mod.reshape(depth, 1, nd))


def _store_tokens(ref, val, grid_layout):
    if grid_layout:
        for r in range(val.shape[0] // GRID_W):
            ref[:, r, :] = val[r * GRID_W:(r + 1) * GRID_W]
    else:
        ref[0] = val


def _load_tokens(ref, grid_layout):
    if grid_layout:
        return jnp.concatenate([ref[:, r, :] for r in range(ref.shape[1])], axis=0)
    return ref[0]


def _stage_a_kernel(x_ref, mod_ref, g1_ref, win_ref, sguw_ref, sgub_ref, ones_ref, gw_ref, gb_ref, s5_in_ref,
                    sgu_ref, s5x_ref, qk_ref, v_ref, g_ref, la_ref, *, tb, grid_layout):
    del s5_in_ref
    x = x_ref[0]
    ms = jnp.mean(x * x, axis=-1, keepdims=True)
    xn = x * lax.rsqrt(ms + EPS) * g1_ref[...]
    h = xn * (1.0 + mod_ref[0, 1:2, :]) + mod_ref[0, 0:1, :]
    cols = _dot(h.astype(BF16), win_ref[...])

    u = _gelu(cols[:, 0:SGU_DIM])
    v = _gelu(cols[:, SGU_DIM:2 * SGU_DIM])
    msq = _dot_x_exact(v * v, ones_ref[...]) * (1.0 / SGU_HEAD_DIM)
    vn = (v * lax.rsqrt(msq + EPS)).astype(BF16)
    head_of_lane = lax.broadcasted_iota(jnp.int32, (SGU_CHUNK, SGU_DIM), 1) // SGU_HEAD_DIM
    for ci in range(tb // SGU_CHUNK):
        rows = slice(ci * SGU_CHUNK, (ci + 1) * SGU_CHUNK)
        vc = vn[rows]
        mixed = sgub_ref[...]
        for hh in range(SGU_HEADS):
            mixed = mixed + jnp.where(head_of_lane == hh, _dot(sguw_ref[hh], vc), 0.0)
        sgu_ref[0, rows, :] = u[rows] * mixed

    s5x_ref[0] = cols[:, 512:768]
    q = cols[:, 768:1024] * (GLA_DK ** -0.5)
    _store_tokens(qk_ref, jnp.concatenate([q, cols[:, 1024:1280]], axis=-1), grid_layout)
    _store_tokens(v_ref, cols[:, 1280:1792], grid_layout)
    g_ref[0] = cols[:, 1792:2304]

    z = cols[:, 2304:2432].astype(BF16)
    za = _dot(z, gw_ref[...]) + gb_ref[...]
    log_sig = jnp.minimum(za, 0.0) - jnp.log1p(jnp.exp(-jnp.abs(za)))
    _store_tokens(la_ref, log_sig * (1.0 / GLA_GATE_TEMP), grid_layout)


def _stage_a(xs, mod, g1, win, sguw, sgub, ones_sgu, gw, gb, s5_all, s5_row0, tb, grid_layout):
    b, l, d = xs.shape
    assert s5_row0 % tb == 0
    kern = functools.partial(_stage_a_kernel, tb=tb, grid_layout=grid_layout)
    tok = lambda w: pl.BlockSpec((1, tb, w), lambda bi, i: (bi, i, 0))
    if grid_layout:
        assert tb % GRID_W == 0 and l % GRID_W == 0
        gla = lambda w: pl.BlockSpec((GRID_W, None, tb // GRID_W, w), lambda bi, i: (0, bi, i, 0))
        gla_shape = lambda w: jax.ShapeDtypeStruct((GRID_W, b, l // GRID_W, w), F32)
    else:
        gla = tok
        gla_shape = lambda w: jax.ShapeDtypeStruct((b, l, w), F32)
    nat_shape = lambda w: jax.ShapeDtypeStruct((b, l, w), F32)
    return pl.pallas_call(
        kern,
        grid=(b, l // tb),
        in_specs=[
            tok(d),
            pl.BlockSpec((1, 8, d), lambda bi, i: (bi, 0, 0)),
            _full_spec(g1.shape), _full_spec(win.shape), _full_spec(sguw.shape), _full_spec(sgub.shape),
            _full_spec(ones_sgu.shape), _full_spec(gw.shape), _full_spec(gb.shape),
            pl.BlockSpec(memory_space=pl.ANY),
        ],
        out_specs=[tok(SGU_DIM),
                   pl.BlockSpec((1, tb, S5_DIM), lambda bi, i: (bi, s5_row0 // tb + i, 0)),
                   gla(2 * GLA_KEY_DIM), gla(GLA_DIM), tok(GLA_DIM), gla(2 * GLA_KEY_DIM)],
        out_shape=[nat_shape(SGU_DIM), jax.ShapeDtypeStruct(s5_all.shape, F32),
                   gla_shape(2 * GLA_KEY_DIM), gla_shape(GLA_DIM), nat_shape(GLA_DIM), gla_shape(2 * GLA_KEY_DIM)],
        input_output_aliases={9: 1},
        compiler_params=_cparams(("parallel", "parallel")),
    )(xs, mod, g1, win, sguw, sgub, ones_sgu, gw, gb, s5_all)


S5_TC = 128


def _s5_kernel(uf_ref, ub_ref, perm_ref, permt_ref, b2_ref, ar_ref, ai_ref, c2_ref, yf_ref, yb_ref,
               h_ref, buf0_ref, buf1_ref, buf2_ref, *, nseq):
    tc = S5_TC
    rows = tc * 2 * nseq
    s = pl.program_id(0)

    @pl.when(s == 0)
    def _():
        h_ref[...] = jnp.zeros_like(h_ref)
        buf0_ref[...] = jnp.zeros_like(buf0_ref)
        buf1_ref[...] = jnp.zeros_like(buf1_ref)
        buf2_ref[...] = jnp.zeros_like(buf2_ref)

    def phase(scan_ref, fill_ref, read_ref):
        ar = ar_ref[...]
        ai = ai_ref[...]
        hr, hi = h_ref[:, 0:S5_LANES], h_ref[:, S5_LANES:]
        for t in range(tc):
            bur = scan_ref[t, :, 0:S5_LANES]
            bui = scan_ref[t, :, S5_LANES:]
            hr, hi = ar * hr - ai * hi + bur, ar * hi + ai * hr + bui
            scan_ref[t, :, 0:S5_LANES] = hr
            scan_ref[t, :, S5_LANES:] = hi
        h_ref[:, 0:S5_LANES] = hr
        h_ref[:, S5_LANES:] = hi

        fwd_row = lax.broadcasted_iota(jnp.int32, (rows, S5_DIM), 0) % (2 * nseq) < nseq
        hs = read_ref[...].reshape(rows, 2 * S5_LANES).astype(BF16)
        y2 = _dot(hs, c2_ref[...])
        y = jnp.where(fwd_row, y2[:, 0:S5_DIM], y2[:, S5_DIM:])
        y_hi = y.astype(BF16)
        y_lo = (y - y_hi.astype(F32)).astype(BF16)
        y_nat = _dot(permt_ref[...], y_hi) + _dot(permt_ref[...], y_lo)
        for b in range(nseq):
            yf_ref[b] = y_nat[b * tc:(b + 1) * tc]
            yb_ref[b] = y_nat[(nseq + b) * tc:(nseq + b + 1) * tc]

        x = jnp.concatenate([uf_ref[b] for b in range(nseq)] + [ub_ref[b] for b in range(nseq)], axis=0)
        u_tm = _dot(perm_ref[...], x.astype(BF16))
        u = jnp.concatenate([jnp.where(fwd_row, u_tm, 0.0), jnp.where(fwd_row, 0.0, u_tm)], axis=-1).astype(BF16)
        fill_ref[...] = _dot(u, b2_ref[...]).reshape(tc, 2 * nseq, 2 * S5_LANES)

    bufs = (buf0_ref, buf1_ref, buf2_ref)
    for k in range(3):
        @pl.when(s % 3 == k)
        def _(k=k):
            phase(bufs[(k + 2) % 3], bufs[k], bufs[(k + 1) % 3])


def _s5_permutation(nseq):
    tc = S5_TC
    p = np.zeros((tc * 2 * nseq, tc * 2 * nseq), np.float32)
    for q in range(2 * nseq):
        for t in range(tc):
            p[t * 2 * nseq + q, q * tc + (t if q < nseq else tc - 1 - t)] = 1.0
    return p


def _s5_call(s5_all, n_lat, b2, ar, ai, c2):
    nseq, t, _ = s5_all.shape
    n = t // S5_TC
    n_l = n_lat // S5_TC
    n_c = n - n_l
    perm = _s5_permutation(nseq)
    kern = functools.partial(_s5_kernel, nseq=nseq)
    fwd_blk = lambda k: jnp.where(k < n_c, n_l + k, k - n_c)
    bwd_blk = lambda k: n - 1 - k
    in_k = lambda s: jnp.minimum(s, n - 1)
    out_k = lambda s: jnp.clip(s - 2, 0, n - 1)
    blk = (nseq, S5_TC, S5_DIM)
    rows = 2 * nseq
    buf = pltpu.VMEM((S5_TC, rows, 2 * S5_LANES), F32)
    return pl.pallas_call(
        kern,
        grid=(n + 2,),
        in_specs=[
            pl.BlockSpec(blk, lambda s: (0, fwd_blk(in_k(s)), 0)), pl.BlockSpec(blk, lambda s: (0, bwd_blk(in_k(s)), 0)),
            _full_spec(perm.shape), _full_spec(perm.shape),
            _full_spec(b2.shape), _full_spec(ar.shape), _full_spec(ai.shape), _full_spec(c2.shape),
        ],
        out_specs=[pl.BlockSpec(blk, lambda s: (0, fwd_blk(out_k(s)), 0)),
                   pl.BlockSpec(blk, lambda s: (0, bwd_blk(out_k(s)), 0))],
        out_shape=[jax.ShapeDtypeStruct(s5_all.shape, F32), jax.ShapeDtypeStruct(s5_all.shape, F32)],
        scratch_shapes=[pltpu.VMEM((rows, 2 * S5_LANES), F32), buf, buf, buf],
        compiler_params=_cparams(("arbitrary",)),
    )(s5_all, s5_all, jnp.asarray(perm, dtype=BF16), jnp.asarray(perm.T, dtype=BF16), b2, ar, ai, c2)


def _gla_kernel(qkf_ref, vf_ref, laf_ref, qkb_ref, vb_ref, lab_ref, s0_ref, trif_ref, trib_ref,
                of_ref, ob_ref, sout_ref, s_ref, *, nb):
    c = pl.program_id(0)

    @pl.when(c == 0)
    def _():
        s_ref[...] = s0_ref[...]

    ch = GLA_CHUNK
    r_k = lax.broadcasted_iota(jnp.int32, (GLA_HEADS * ch, GLA_KEY_DIM), 0) // ch
    c_k = lax.broadcasted_iota(jnp.int32, (GLA_HEADS * ch, GLA_KEY_DIM), 1) // GLA_DK
    hm_k = r_k == c_k
    r_v = lax.broadcasted_iota(jnp.int32, (GLA_HEADS * ch, GLA_DIM), 0) // ch
    c_v = lax.broadcasted_iota(jnp.int32, (GLA_HEADS * ch, GLA_DIM), 1) // GLA_DV
    hm_v = r_v == c_v
    r_s = lax.broadcasted_iota(jnp.int32, (GLA_DIM, GLA_KEY_DIM), 0) // GLA_DV
    c_s = lax.broadcasted_iota(jnp.int32, (GLA_DIM, GLA_KEY_DIM), 1) // GLA_DK
    hm_s = r_s == c_s
    t_i = lax.broadcasted_iota(jnp.int32, (ch, GLA_HEADS * ch), 0)
    s_i = lax.broadcasted_iota(jnp.int32, (ch, GLA_HEADS * ch), 1) % ch
    mask_f = t_i >= s_i
    mask_b = t_i <= s_i
    trif = trif_ref[...]
    trib = trib_ref[...]

    fwd = dict(qk=qkf_ref, v=vf_ref, la=laf_ref, o=of_ref, tri=trif, last=ch - 1, ref=ch // 2, mask=mask_f, d=0)
    bwd = dict(qk=qkb_ref, v=vb_ref, la=lab_ref, o=ob_ref, tri=trib, last=0, ref=ch - 1 - ch // 2, mask=mask_b, d=1)
    streams = [(b, p) for b in range(nb) for p in (fwd, bwd)]

    bcums = [_dot_m_exact(p["tri"], p["la"][b]) for b, p in streams]
    scaled = []
    for (b, p), bcum in zip(streams, bcums):
        qk = p["qk"][b]
        q, k = qk[:, 0:GLA_KEY_DIM], qk[:, GLA_KEY_DIM:]
        blast = bcum[p["last"]:p["last"] + 1]
        bref = bcum[p["ref"]:p["ref"] + 1]
        qe = (q * jnp.exp(bcum)).astype(BF16)
        qd = (q * jnp.exp(bcum - bref)).astype(BF16)
        kd = k * jnp.exp(bref - bcum)
        kdec = (k * jnp.exp(blast - bcum)).astype(BF16)
        kst = jnp.where(hm_k, jnp.concatenate([kd] * GLA_HEADS, axis=0), 0.0).astype(BF16)
        scaled.append((qe, qd, kdec, kst, jnp.exp(blast)))
    prods = []
    for (b, p), (qe, qd, kdec, kst, _) in zip(streams, scaled):
        v = p["v"][b]
        sc = _dot_nt(qd, kst)
        o_inter = _dot_nt(qe, s_ref[b, p["d"]].astype(BF16))
        kv_t = _dot_tn(v.astype(BF16), kdec)
        prods.append((sc, o_inter, kv_t))
    for (b, p), (_, _, _, _, decay), (sc, o_inter, kv_t) in zip(streams, scaled, prods):
        vbd = jnp.where(hm_v, jnp.concatenate([p["v"][b]] * GLA_HEADS, axis=0), 0.0).astype(BF16)
        p["o"][b] = _dot(jnp.where(p["mask"], sc, 0.0).astype(BF16), vbd) + o_inter
        s_ref[b, p["d"]] = s_ref[b, p["d"]] * decay + jnp.where(hm_s, kv_t, 0.0)

    @pl.when(c == pl.num_programs(0) - 1)
    def _():
        sout_ref[...] = s_ref[...]


def _gla_call(qk, v, la, s0, trif, trib):
    ch = GLA_CHUNK
    if qk.ndim == 4:
        n, b = qk.shape[0], qk.shape[1]
        assert qk.shape[2] == ch
        spec = lambda w, off, rev: pl.BlockSpec(
            (None, b, ch, w), (lambda c: (n - 1 - c, 0, 0, off)) if rev else (lambda c: (c, 0, 0, off)))
    else:
        b = qk.shape[0]
        n = qk.shape[1] // ch
        spec = lambda w, off, rev: pl.BlockSpec(
            (b, ch, w), (lambda c: (0, n - 1 - c, off)) if rev else (lambda c: (0, c, off)))
    kern = functools.partial(_gla_kernel, nb=b)
    o_shape = jax.ShapeDtypeStruct(v.shape, F32)
    return pl.pallas_call(
        kern,
        grid=(n,),
        in_specs=[
            spec(2 * GLA_KEY_DIM, 0, False), spec(GLA_DIM, 0, False), spec(GLA_KEY_DIM, 0, False),
            spec(2 * GLA_KEY_DIM, 0, True), spec(GLA_DIM, 0, True), spec(GLA_KEY_DIM, 1, True),
            _full_spec(s0.shape), _full_spec(trif.shape), _full_spec(trib.shape),
        ],
        out_specs=[spec(GLA_DIM, 0, False), spec(GLA_DIM, 0, True), _full_spec(s0.shape)],
        out_shape=[o_shape, o_shape, jax.ShapeDtypeStruct(s0.shape, F32)],
        scratch_shapes=[pltpu.VMEM(s0.shape, F32)],
        compiler_params=_cparams(("arbitrary",)),
    )(qk, v, la, qk, v, la, s0, trif, trib)


def _stage_e_kernel(x_ref, mod_ref, sgu_ref, yf_ref, yb_ref, s5x_ref, of_ref, ob_ref, g_ref,
                    dskip_ref, wglu_ref, normg_ref, ones_ref, wout_ref, o_ref, *, grid_layout):
    ys = yf_ref[0] + yb_ref[0] + dskip_ref[...] * s5x_ref[0]
    z = _dot(_gelu(ys).astype(BF16), wglu_ref[...])
    s5o = z[:, 0:S5_DIM] * _sigmoid(z[:, S5_DIM:])
    o = _load_tokens(of_ref, grid_layout) + _load_tokens(ob_ref, grid_layout)
    ms = _dot_x_exact(o * o, ones_ref[...]) * (1.0 / GLA_DV)
    g = g_ref[0]
    gl = o * lax.rsqrt(ms + EPS) * normg_ref[...] * (g * _sigmoid(g))
    y = (_dot(sgu_ref[0].astype(BF16), wout_ref[0:SGU_DIM, :])
         + _dot(s5o.astype(BF16), wout_ref[SGU_DIM:SGU_DIM + S5_DIM, :])
         + _dot(gl.astype(BF16), wout_ref[SGU_DIM + S5_DIM:, :]))
    o_ref[0] = x_ref[0] + mod_ref[0, 2:3, :] * y


def _stage_e(xs, mod, sgu, yf_all, yb_all, s5_all, s5_row0, of, ob, g, dskip, wglu, normg, ones_gla, wout, tb):
    b, l, d = xs.shape
    assert s5_row0 % tb == 0
    grid_layout = of.ndim == 4
    tok = lambda w: pl.BlockSpec((1, tb, w), lambda bi, i: (bi, i, 0))
    s5 = pl.BlockSpec((1, tb, S5_DIM), lambda bi, i: (bi, s5_row0 // tb + i, 0))
    if grid_layout:
        assert tb % GRID_W == 0
        gla = pl.BlockSpec((GRID_W, None, tb // GRID_W, GLA_DIM), lambda bi, i: (0, bi, i, 0))
    else:
        gla = tok(GLA_DIM)
    return pl.pallas_call(
        functools.partial(_stage_e_kernel, grid_layout=grid_layout),
        grid=(b, l // tb),
        in_specs=[
            tok(d), pl.BlockSpec((1, 8, d), lambda bi, i: (bi, 0, 0)),
            tok(SGU_DIM), s5, s5, s5, gla, gla, tok(GLA_DIM),
            _full_spec(dskip.shape), _full_spec(wglu.shape), _full_spec(normg.shape),
            _full_spec(ones_gla.shape), _full_spec(wout.shape),
        ],
        out_specs=tok(d),
        out_shape=jax.ShapeDtypeStruct((b, l, d), F32),
        compiler_params=_cparams(("parallel", "parallel")),
    )(xs, mod, sgu, yf_all, yb_all, s5_all, of, ob, g, dskip, wglu, normg, ones_gla, wout)


PEER_TBF = 256
AUX_ROWS = 16
AUX_THETA, AUX_V1_TOP, AUX_V1_LAST, AUX_N_TOP, AUX_RZ = 8, 9, 10, 11, 12


def _sort_network_16():
    def merge(lo, hi, r):
        step = r * 2
        if step < hi - lo:
            yield from merge(lo, hi, step)
            yield from merge(lo + r, hi, step)
            for i in range(lo + r, hi - r, step):
                yield (i, i + r)
        else:
            yield (lo, lo + r)

    def sort(lo, hi):
        if hi - lo >= 1:
            mid = lo + (hi - lo) // 2
            yield from sort(lo, mid)
            yield from sort(mid + 1, hi)
            yield from merge(lo, hi, 1)

    return tuple(sort(0, PEER_TOPK - 1))


SORT16 = _sort_network_16()
BITONIC16 = tuple((k, k + s) for s in (8, 4, 2, 1) for k in range(PEER_TOPK) if not k & s)


def _compare_exchange(xs, pairs):
    xs = list(xs)
    for i, j in pairs:
        hi = jnp.maximum(xs[i], xs[j])
        lo = jnp.minimum(xs[i], xs[j])
        xs[i], xs[j] = hi, lo
    return xs


def _merge_sublanes(xs):
    for shift in (4, 6, 7):
        rolled = [pltpu.roll(x, shift, 0) for x in xs]
        xs = [jnp.maximum(xs[k], rolled[PEER_TOPK - 1 - k]) for k in range(PEER_TOPK)]
        xs = _compare_exchange(xs, BITONIC16)
    return xs


def _dup_bf16_words(x):
    bits = pltpu.bitcast(x.astype(BF16).astype(F32), jnp.int32)
    return bits | lax.shift_right_logical(bits, 16)


def _stage_f_kernel(x_ref, mod_ref, g2_ref, wqt_ref, keys_ref,
                    ht_ref, s1_ref, aux_ref, e2_ref, r2_ref, qt_ref, v1_ref, v2_ref, *, tb):
    x = x_ref[0]
    ms = jnp.mean(x * x, axis=-1, keepdims=True)
    xn = x * lax.rsqrt(ms + EPS) * g2_ref[...]
    h = xn * (1.0 + mod_ref[0, 4:5, :]) + mod_ref[0, 3:4, :]
    ht = h.T.astype(BF16)
    ht_ref[...] = pltpu.bitcast(ht, jnp.int32)
    qt_ref[...] = _dot(wqt_ref[...], ht)
    k_top = PEER_TOPK

    def tiles(s):
        return [s[SUBLANES * k:SUBLANES * (k + 1)] for k in range(PEER_NKEYS // SUBLANES)]

    def head_body(hh, carry):
        for tc in range(tb // LANE):
            tcol = slice(tc * LANE, (tc + 1) * LANE)
            r1 = hh * (2 * PEER_HALF)
            q1 = qt_ref[r1:r1 + PEER_HALF, tcol].astype(BF16)
            q2 = qt_ref[r1 + PEER_HALF:r1 + 2 * PEER_HALF, tcol].astype(BF16)
            s1 = _dot(keys_ref[hh], q1)
            s2 = _dot(keys_ref[PEER_HEADS + hh], q2)
            for s, v_ref in ((s1, v1_ref), (s2, v2_ref)):
                top = _merge_sublanes(_compare_exchange(tiles(s), SORT16))
                for k in range(k_top):
                    v_ref[k:k + 1, tcol] = top[k][0:1]
            v1row = lambda a: v1_ref[a:a + 1, tcol]
            v2row = lambda b: v2_ref[b:b + 1, tcol]
            v1lo = v1_ref[0:SUBLANES, tcol]
            v2lo = v2_ref[0:SUBLANES, tcol]
            cand = [v1lo + v2row(b) for b in range(k_top)]
            tail = [v1row(a) + v2lo for a in range(SUBLANES, k_top)]
            for k in range(SUBLANES, k_top):
                cand[k] = jnp.maximum(cand[k], tail[k_top - 1 - k])
            best = _merge_sublanes(_compare_exchange(cand, BITONIC16))
            theta = best[k_top - 1][0:1]
            cmax = best[0][0:1]
            zsum = jnp.zeros((1, LANE), F32)
            for k in range(k_top):
                zsum = zsum + jnp.exp(best[k][0:1] - cmax)
            rz = 1.0 / zsum
            n_top = jnp.zeros((1, LANE), F32)
            for b in range(k_top):
                n_top = jnp.where(v1row(0) + v2row(b) >= theta, float(b + 1), n_top)
            r2 = jnp.zeros(s2.shape, F32)
            for b in range(k_top):
                r2 = jnp.where(v2row(b) > s2, float(b + 1), r2)
            e2 = jnp.where(s2 >= v2row(k_top - 1), jnp.exp(s2 - v2row(0)), 0.0)
            s1_ref[tc, hh] = s1
            aux_ref[tc, hh, 0:SUBLANES, :] = v2lo
            aux_ref[tc, hh, SUBLANES:, :] = jnp.zeros((AUX_ROWS - SUBLANES, LANE), F32)
            for row, val in ((AUX_THETA, theta), (AUX_V1_TOP, v1row(0)), (AUX_V1_LAST, v1row(k_top - 1)),
                             (AUX_N_TOP, n_top), (AUX_RZ, rz)):
                aux_ref[tc, hh, row:row + 1, :] = val
            e2_ref[tc, hh] = pltpu.bitcast(e2.astype(BF16), jnp.int32)
            r2_ref[tc, hh] = pltpu.bitcast(r2.astype(BF16), jnp.int32)
        return carry

    for hh in range(PEER_HEADS):
        head_body(hh, 0)


def _stage_f(xs, mod, g2, wqt, keys, tb):
    b, l, d = xs.shape
    nblk = l // tb
    ntok = b * l
    nch = ntok // LANE
    kern = functools.partial(_stage_f_kernel, tb=tb)
    row_spec = pl.BlockSpec((tb // LANE, PEER_HEADS, PEER_NKEYS, LANE), lambda bi, i: (bi * nblk + i, 0, 0, 0))
    pair_spec = pl.BlockSpec((tb // LANE, PEER_HEADS, PEER_NKEYS // 2, LANE), lambda bi, i: (bi * nblk + i, 0, 0, 0))
    desc_shape = lambda rows: jax.ShapeDtypeStruct((nch, PEER_HEADS, rows, LANE), jnp.int32)
    return pl.pallas_call(
        kern,
        grid=(b, nblk),
        in_specs=[
            pl.BlockSpec((1, tb, d), lambda bi, i: (bi, i, 0)),
            pl.BlockSpec((1, 8, d), lambda bi, i: (bi, 0, 0)),
            _full_spec(g2.shape), _full_spec(wqt.shape), _full_spec(keys.shape),
        ],
        out_specs=[pl.BlockSpec((d // 2, tb), lambda bi, i: (0, bi * nblk + i)),
                   row_spec,
                   pl.BlockSpec((tb // LANE, PEER_HEADS, AUX_ROWS, LANE), lambda bi, i: (bi * nblk + i, 0, 0, 0)),
                   pair_spec, pair_spec],
        out_shape=[jax.ShapeDtypeStruct((d // 2, ntok), jnp.int32),
                   jax.ShapeDtypeStruct((nch, PEER_HEADS, PEER_NKEYS, LANE), F32),
                   jax.ShapeDtypeStruct((nch, PEER_HEADS, AUX_ROWS, LANE), F32),
                   desc_shape(PEER_NKEYS // 2), desc_shape(PEER_NKEYS // 2)],
        scratch_shapes=[
            pltpu.VMEM((PEER_HEADS * 2 * PEER_HALF, tb), F32),
            pltpu.VMEM((PEER_TOPK, tb), F32),
            pltpu.VMEM((PEER_TOPK, tb), F32),
        ],
        compiler_params=_cparams(("parallel", "parallel")),
    )(xs, mod, g2, wqt, keys)


PEER_TBG = 1024
PEER_TE = 1024
PEER_I1_PER_TILE = PEER_TE // PEER_NKEYS
PEER_N_TILES = PEER_NKEYS * PEER_NKEYS // PEER_TE
PEER_MXU_COLS = 256
PEER_G_FLAGS = None


def _stage_g_kernel(htw_ref, s1_ref, aux_ref, e2_ref, r2_ref, uw_ref, vtw_ref, x_ref, mod_ref, fg_ref,
                    o_ref, acc_ref, ata_ref, atb_ref, p_ref, e1w_ref, n1w_ref, *, tb, final_norm):
    s = pl.program_id(1)

    @pl.when(s == 0)
    def _():
        acc_ref[...] = jnp.zeros_like(acc_ref)

    def step(at_cur_ref, at_next_ref):
        per_grp = PEER_MXU_COLS // LANE
        for grp in range(tb // PEER_MXU_COLS):
            cols = slice(grp * PEER_MXU_COLS, (grp + 1) * PEER_MXU_COLS)
            for tcl in range(per_grp if at_cur_ref is not None else 0):
                tc = grp * per_grp + tcl
                tcol = slice(tc * LANE, (tc + 1) * LANE)
                for i1l in range(PEER_I1_PER_TILE):
                    rows = slice(i1l * PEER_NKEYS, (i1l + 1) * PEER_NKEYS)
                    gate = jnp.zeros((PEER_NKEYS, LANE), BF16)
                    for hh in range(PEER_HEADS):
                        e1row = e1w_ref[tc, hh, i1l:i1l + 1, :]
                        n1row = n1w_ref[tc, hh, i1l:i1l + 1, :]
                        e1 = pltpu.bitcast(jnp.broadcast_to(e1row, (PEER_NKEYS // 2, LANE)), BF16)
                        n1 = pltpu.bitcast(jnp.broadcast_to(n1row, (PEER_NKEYS // 2, LANE)), BF16)
                        r2 = pltpu.bitcast(r2_ref[tc, hh], BF16)
                        e2 = pltpu.bitcast(e2_ref[tc, hh], BF16)
                        gate = gate + e2 * jnp.where(r2 < n1, e1, 0.0)
                    p_ref[rows, tcol] = gate * _gelu(at_cur_ref[rows, tcol])
            if at_cur_ref is not None:
                vt = pltpu.bitcast(vtw_ref[...], BF16)
                acc_ref[:, cols] += _dot(vt, p_ref[:, cols])
            if at_next_ref is not None:
                u = pltpu.bitcast(uw_ref[...], BF16)
                ht = pltpu.bitcast(htw_ref[:, cols], BF16)
                at_next_ref[:, cols] = _dot(u, ht).astype(BF16)
        if at_next_ref is not None:
            for tc in range(tb // LANE):
                for hh in range(PEER_HEADS):
                    s1 = s1_ref[tc, hh]
                    aux = lambda r: aux_ref[tc, hh, r:r + 1, :]
                    n1 = jnp.zeros(s1.shape, F32)
                    for b in range(SUBLANES):
                        n1 = jnp.where(s1 + aux(b) >= aux(AUX_THETA), float(b + 1), n1)
                    n1 = jnp.where(s1 >= aux(AUX_V1_TOP), aux(AUX_N_TOP), n1)
                    e1 = jnp.where(s1 >= aux(AUX_V1_LAST), jnp.exp(s1 - aux(AUX_V1_TOP)), 0.0) * aux(AUX_RZ)
                    e1w_ref[tc, hh] = _dup_bf16_words(e1)
                    n1w_ref[tc, hh] = _dup_bf16_words(n1)

    last = pl.num_programs(1) - 1

    @pl.when(s == 0)
    def _():
        step(None, ata_ref)

    @pl.when(jnp.logical_and(s % 2 == 0, jnp.logical_and(s > 0, s < last)))
    def _():
        step(atb_ref, ata_ref)

    @pl.when(s % 2 == 1)
    def _():
        step(ata_ref, atb_ref)

    @pl.when(s == last)
    def _():
        step(atb_ref, None)
        xo = x_ref[...] + mod_ref[0, 5:6, :] * acc_ref[...].T
        if final_norm:
            ms = jnp.mean(xo * xo, axis=-1, keepdims=True)
            xo = xo * lax.rsqrt(ms + EPS) * fg_ref[...]
        o_ref[...] = xo


def _stage_g(htw, s1, aux, e2, r2, uw, vtw, xflat, mod, final_g, tokens_per_batch, tb, final_norm):
    ntok = htw.shape[1]
    d = 2 * htw.shape[0]
    assert 2 * uw.shape[0] == PEER_N_TILES * PEER_TE and PEER_N_TILES % 2 == 0
    blocks_per_batch = tokens_per_batch // tb
    kern = functools.partial(_stage_g_kernel, tb=tb, final_norm=final_norm)
    last = PEER_N_TILES - 1
    desc_spec = pl.BlockSpec((tb // LANE, PEER_HEADS, PEER_NKEYS // 2, LANE), lambda j, i: (j, 0, 0, 0))
    row_spec = pl.BlockSpec((tb // LANE, PEER_HEADS, PEER_I1_PER_TILE, LANE),
                            lambda j, i: (j, 0, jnp.minimum(i, last), 0))
    return pl.pallas_call(
        kern,
        grid=(ntok // tb, PEER_N_TILES + 1),
        in_specs=[
            pl.BlockSpec((d // 2, tb), lambda j, i: (0, j)),
            row_spec,
            pl.BlockSpec((tb // LANE, PEER_HEADS, AUX_ROWS, LANE), lambda j, i: (j, 0, 0, 0)),
            desc_spec, desc_spec,
            pl.BlockSpec((PEER_TE // 2, d), lambda j, i: (jnp.minimum(i, last), 0)),
            pl.BlockSpec((d // 2, PEER_TE), lambda j, i: (0, jnp.maximum(i - 1, 0))),
            pl.BlockSpec((tb, d), lambda j, i: (j, 0)),
            pl.BlockSpec((1, 8, d), lambda j, i: (j // blocks_per_batch, 0, 0)),
            _full_spec(final_g.shape),
        ],
        out_specs=pl.BlockSpec((tb, d), lambda j, i: (j, 0)),
        out_shape=jax.ShapeDtypeStruct((ntok, d), F32),
        scratch_shapes=[
            pltpu.VMEM((d, tb), F32),
            pltpu.VMEM((PEER_TE, tb), BF16),
            pltpu.VMEM((PEER_TE, tb), BF16),
            pltpu.VMEM((PEER_TE, tb), BF16),
            pltpu.VMEM((tb // LANE, PEER_HEADS, PEER_I1_PER_TILE, LANE), jnp.int32),
            pltpu.VMEM((tb // LANE, PEER_HEADS, PEER_I1_PER_TILE, LANE), jnp.int32),
        ],
        compiler_params=_cparams(("parallel", "arbitrary"), PEER_G_FLAGS),
    )(htw, s1, aux, e2, r2, uw, vtw, xflat, mod, final_g)


def _pack_kernel(x_ref, o_ref, *, transpose):
    x = x_ref[...]
    if transpose:
        x = x.T
    o_ref[...] = pltpu.bitcast(x.astype(BF16), jnp.int32)


def _pack_row_pairs(x, layer, transpose=False, tile=1024):
    _, r, c = x.shape
    if transpose:
        out_shape, out_spec = (c // 2, r), pl.BlockSpec((c // 2, tile), lambda i: (0, i))
    else:
        out_shape, out_spec = (r // 2, c), pl.BlockSpec((tile // 2, c), lambda i: (i, 0))
    return pl.pallas_call(
        functools.partial(_pack_kernel, transpose=transpose),
        grid=(r // tile,),
        in_specs=[pl.BlockSpec((None, tile, c), lambda i: (layer, i, 0))],
        out_specs=out_spec,
        out_shape=jax.ShapeDtypeStruct(out_shape, jnp.int32),
        compiler_params=_cparams(("parallel",)),
    )(x)


def _block_ones(n, blk):
    idx = np.arange(n) // blk
    return jnp.asarray((idx[:, None] == idx[None, :]).astype(np.float32), dtype=BF16)


def _s5_discretise(lam_re, lam_im, b_re, b_im, log_step):
    lam_re = jnp.minimum(lam_re.astype(F32), -1e-4)
    lam_im = lam_im.astype(F32)
    dt = jnp.exp(log_step.astype(F32))[:, None]
    mag = jnp.exp(lam_re * dt)
    a_re = mag * jnp.cos(lam_im * dt)
    a_im = mag * jnp.sin(lam_im * dt)
    den = lam_re * lam_re + lam_im * lam_im
    f_re = ((a_re - 1.0) * lam_re + a_im * lam_im) / den
    f_im = (a_im * lam_re - (a_re - 1.0) * lam_im) / den
    b_re = b_re.astype(F32)
    b_im = b_im.astype(F32)
    bb_re = f_re[..., None] * b_re - f_im[..., None] * b_im
    bb_im = f_re[..., None] * b_im + f_im[..., None] * b_re
    return a_re, a_im, bb_re, bb_im


def _group_block_diag(t):
    g, r, c = t.shape
    eye = jnp.eye(g, dtype=t.dtype)
    return (t[:, :, None, :] * eye[:, None, :, None]).reshape(g * r, g * c)


def _s5_params(lam_re, lam_im, b_re, b_im, c_re, c_im, log_step, nseq):
    b_rows, c_cols, ars, ais = [], [], [], []
    for d in range(2):
        a_re, a_im, bb_re, bb_im = _s5_discretise(lam_re[d], lam_im[d], b_re[d], b_im[d], log_step[d])
        bm = jnp.concatenate([_group_block_diag(jnp.swapaxes(bb_re, 1, 2)),
                              _group_block_diag(jnp.swapaxes(bb_im, 1, 2))], axis=1)
        b_rows.append(bm)
        cm = jnp.concatenate([_group_block_diag(jnp.swapaxes(c_re[d].astype(F32), 1, 2)),
                              -_group_block_diag(jnp.swapaxes(c_im[d].astype(F32), 1, 2))], axis=0)
        c_cols.append(cm)
        ars.append(jnp.broadcast_to(a_re.reshape(1, S5_LANES), (nseq, S5_LANES)))
        ais.append(jnp.broadcast_to(a_im.reshape(1, S5_LANES), (nseq, S5_LANES)))
    b2 = jnp.concatenate(b_rows, axis=0).astype(BF16)
    c2 = jnp.concatenate(c_cols, axis=1).astype(BF16)
    return b2, jnp.concatenate(ars, axis=0), jnp.concatenate(ais, axis=0), c2


def kernel(x, c, ctx, c_ctx, w_mod, b_mod, norm1_g, norm2_g, w_in, w_out, sgu_w, sgu_b, s5_lambda_re, s5_lambda_im, s5_b_re, s5_b_im, s5_c_re, s5_c_im, s5_log_step, s5_d, s5_w_glu, gla_w_gate, gla_b_gate, gla_norm_g, peer_w_query, peer_sub_keys, peer_expert_u, peer_expert_v, final_norm_g):
    nb, seq, d = x.shape
    c_len = ctx.shape[1]
    depth = w_mod.shape[0]

    cc = jnp.concatenate([c, c_ctx[None, :], jnp.zeros((8 - nb - 1, d), F32)], axis=0)
    mods = _mod_call(cc, w_mod, b_mod)

    ones_sgu = _block_ones(SGU_DIM, SGU_HEAD_DIM)
    ones_gla = _block_ones(GLA_DIM, GLA_DV)
    tri_np = np.tril(np.ones((GLA_CHUNK, GLA_CHUNK), np.float32))
    trif = jnp.asarray(tri_np, dtype=BF16)
    trib = jnp.asarray(tri_np.T, dtype=BF16)
    s_zero = jnp.zeros((nb, 2, GLA_DIM, GLA_KEY_DIM), F32)
    final_g = final_norm_g.reshape(1, d)

    xl, xc = x, ctx
    for l in range(depth):
        ctx_out = l < depth - 1
        m6 = mods[l].reshape(8, N_MOD, d)
        mod_l = jnp.pad(m6[:nb], ((0, 0), (0, 2), (0, 0)))
        mod_c = jnp.broadcast_to(jnp.pad(m6[nb], ((0, 2), (0, 0)))[None], (nb, 8, d))

        win = jnp.pad(w_in[l], ((0, 0), (0, IN_PAD - IN_WIDTH))).astype(BF16)
        sguw = sgu_w[l].astype(BF16)
        sgub = jnp.repeat(jnp.swapaxes(sgu_b[l], 0, 1), SGU_HEAD_DIM, axis=1)
        gw = jnp.zeros((LANE, 2 * GLA_KEY_DIM), F32)
        gw = gw.at[0:GLA_RANK, 0:GLA_KEY_DIM].set(gla_w_gate[l, 0])
        gw = gw.at[GLA_RANK:2 * GLA_RANK, GLA_KEY_DIM:].set(gla_w_gate[l, 1]).astype(BF16)
        gb = gla_b_gate[l].reshape(1, 2 * GLA_KEY_DIM)
        g1 = norm1_g[l].reshape(1, d)

        s5_all = jnp.zeros((nb, seq + c_len, S5_DIM), F32)
        sgu_l, s5_all, qk_l, v_l, g_l, la_l = _stage_a(xl, mod_l, g1, win, sguw, sgub, ones_sgu, gw, gb,
                                                       s5_all, 0, tb=TB_LATENT, grid_layout=True)
        sgu_c, s5_all, qk_c, v_c, g_c, la_c = _stage_a(xc, mod_c, g1, win, sguw, sgub, ones_sgu, gw, gb,
                                                       s5_all, seq, tb=TB_CTX, grid_layout=False)

        b2, ar, ai, c2 = _s5_params(s5_lambda_re[l], s5_lambda_im[l], s5_b_re[l], s5_b_im[l],
                                    s5_c_re[l], s5_c_im[l], s5_log_step[l], nb)
        yf_all, yb_all = _s5_call(s5_all, seq, b2, ar, ai, c2)

        of_c, ob_c, s_ctx = _gla_call(qk_c, v_c, la_c, s_zero, trif, trib)
        of_l, ob_l, _ = _gla_call(qk_l, v_l, la_l, s_ctx, trif, trib)

        dskip = s5_d[l].reshape(1, S5_DIM)
        wglu = s5_w_glu[l].astype(BF16)
        normg = gla_norm_g[l].reshape(1, GLA_DIM)
        wout = w_out[l].astype(BF16)
        g2 = norm2_g[l].reshape(1, d)
        wqt = jnp.swapaxes(peer_w_query[l], 0, 1).astype(BF16)
        keys = peer_sub_keys[l].reshape(2 * PEER_HEADS, PEER_NKEYS, PEER_HALF).astype(BF16)
        u_bf = _pack_row_pairs(peer_expert_u, l)
        vt_bf = _pack_row_pairs(peer_expert_v, l, transpose=True)

        xl = _stage_e(xl, mod_l, sgu_l, yf_all, yb_all, s5_all, 0, of_l, ob_l, g_l,
                      dskip, wglu, normg, ones_gla, wout, tb=TB_LATENT)
        desc = _stage_f(xl, mod_l, g2, wqt, keys, tb=PEER_TBF)
        xl = _stage_g(*desc, u_bf, vt_bf, xl.reshape(nb * seq, d), mod_l, final_g, seq, PEER_TBG,
                      final_norm=not ctx_out).reshape(nb, seq, d)

        if ctx_out:
            xc = _stage_e(xc, mod_c, sgu_c, yf_all, yb_all, s5_all, seq, of_c, ob_c, g_c,
                          dskip, wglu, normg, ones_gla, wout, tb=TB_CTX)
            desc = _stage_f(xc, mod_c, g2, wqt, keys, tb=PEER_TBF)
            xc = _stage_g(*desc, u_bf, vt_bf, xc.reshape(nb * c_len, d), mod_c, final_g, c_len,
                          min(PEER_TBG, c_len), final_norm=False).reshape(nb, c_len, d)

    return xl
```

```python
import functools
import math

import numpy as np
import jax
import jax.numpy as jnp
from jax import lax
from jax.experimental import pallas as pl
from jax.experimental.pallas import tpu as pltpu

F32 = jnp.float32
BF16 = jnp.bfloat16
SUBLANES = 8

EPS = 1e-6
N_MOD = 6
GRID_W = 64

SGU_DIM = 256
SGU_HEADS = 4
SGU_HEAD_DIM = 64
SGU_CHUNK = 128

S5_DIM = 256
S5_GROUP = 16
S5_GROUPS = 16
S5_STATE = 64
S5_LANES = S5_GROUPS * S5_STATE

GLA_DIM = 512
GLA_HEADS = 8
GLA_DV = 64
GLA_DK = 32
GLA_KEY_DIM = 256
GLA_RANK = 16
GLA_GATE_TEMP = 16.0
GLA_CHUNK = 64

PEER_HEADS = 8
PEER_NKEYS = 128
PEER_HALF = 128
PEER_TOPK = 16

IN_WIDTH = 2336
IN_PAD = 2432
LANE = 128

VMEM_LIMIT = 56 * 1024 * 1024
TB_LATENT = 512
TB_CTX = 256

NEG_INF = float("-inf")
POS_INF = float("inf")


def _cparams(sem, flags=None):
    return pltpu.CompilerParams(dimension_semantics=sem, vmem_limit_bytes=VMEM_LIMIT, flags=flags)


def _gelu(x):
    c = math.sqrt(2.0 / math.pi)
    return 0.5 * x * (1.0 + jnp.tanh(c * (x + 0.044715 * (x * x * x))))


def _sigmoid(x):
    return 1.0 / (1.0 + jnp.exp(-x))


def _dot(a, b):
    return jnp.dot(a, b, preferred_element_type=F32)


def _dot_nt(a, b):
    return lax.dot_general(a, b, (((1,), (1,)), ((), ())), preferred_element_type=F32)


def _dot_tn(a, b):
    return lax.dot_general(a, b, (((0,), (0,)), ((), ())), preferred_element_type=F32)


def _split3(x):
    hi = x.astype(BF16)
    r = x - hi.astype(F32)
    mid = r.astype(BF16)
    lo = (r - mid.astype(F32)).astype(BF16)
    return hi, mid, lo


def _dot_x_exact(x, m):
    hi, mid, lo = _split3(x)
    return _dot(hi, m) + _dot(mid, m) + _dot(lo, m)


def _dot_m_exact(m, x):
    hi, mid, lo = _split3(x)
    return _dot(m, hi) + _dot(m, mid) + _dot(m, lo)


def _full_spec(shape):
    nd = len(shape)
    return pl.BlockSpec(shape, lambda *_: (0,) * nd)


MOD_TILE = 512


def _mod_kernel(c_ref, w_ref, b_ref, o_ref):
    c = c_ref[...]
    a = c * _sigmoid(c)
    o_ref[0] = jnp.dot(a, w_ref[0], preferred_element_type=F32,
                       precision=lax.Precision.HIGHEST) + b_ref[0]


def _mod_call(cc, w_mod, b_mod):
    depth, d, nd = w_mod.shape
    rows = cc.shape[0]
    return pl.pallas_call(
        _mod_kernel,
        grid=(depth, nd // MOD_TILE),
        in_specs=[
            pl.BlockSpec((rows, d), lambda l, j: (0, 0)),
            pl.BlockSpec((1, d, MOD_TILE), lambda l, j: (l, 0, j)),
            pl.BlockSpec((1, 1, MOD_TILE), lambda l, j: (l, 0, j)),
        ],
        out_specs=pl.BlockSpec((1, rows, MOD_TILE), lambda l, j: (l, 0, j)),
        out_shape=jax.ShapeDtypeStruct((depth, rows, nd), F32),
        compiler_params=_cparams(("parallel", "parallel")),
    )(cc, w_mod, b_mod.reshape(depth, 1, nd))


def _store_tokens(ref, val, grid_layout):
    if grid_layout:
        for r in range(val.shape[0] // GRID_W):
            ref[:, r, :] = val[r * GRID_W:(r + 1) * GRID_W]
    else:
        ref[0] = val


def _load_tokens(ref, grid_layout):
    if grid_layout:
        return jnp.concatenate([ref[:, r, :] for r in range(ref.shape[1])], axis=0)
    return ref[0]


def _stage_a_kernel(x_ref, mod_ref, g1_ref, win_ref, sguw_ref, sgub_ref, ones_ref, gw_ref, gb_ref, s5_in_ref,
                    sgu_ref, s5x_ref, qk_ref, v_ref, g_ref, la_ref, *, tb, grid_layout):
    del s5_in_ref
    x = x_ref[0]
    ms = jnp.mean(x * x, axis=-1, keepdims=True)
    xn = x * lax.rsqrt(ms + EPS) * g1_ref[...]
    h = xn * (1.0 + mod_ref[0, 1:2, :]) + mod_ref[0, 0:1, :]
    cols = _dot(h.astype(BF16), win_ref[...])

    u = _gelu(cols[:, 0:SGU_DIM])
    v = _gelu(cols[:, SGU_DIM:2 * SGU_DIM])
    msq = _dot_x_exact(v * v, ones_ref[...]) * (1.0 / SGU_HEAD_DIM)
    vn = (v * lax.rsqrt(msq + EPS)).astype(BF16)
    head_of_lane = lax.broadcasted_iota(jnp.int32, (SGU_CHUNK, SGU_DIM), 1) // SGU_HEAD_DIM
    for ci in range(tb // SGU_CHUNK):
        rows = slice(ci * SGU_CHUNK, (ci + 1) * SGU_CHUNK)
        vc = vn[rows]
        mixed = sgub_ref[...]
        for hh in range(SGU_HEADS):
            mixed = mixed + jnp.where(head_of_lane == hh, _dot(sguw_ref[hh], vc), 0.0)
        sgu_ref[0, rows, :] = u[rows] * mixed

    s5x_ref[0] = cols[:, 512:768]
    q = cols[:, 768:1024] * (GLA_DK ** -0.5)
    _store_tokens(qk_ref, jnp.concatenate([q, cols[:, 1024:1280]], axis=-1), grid_layout)
    _store_tokens(v_ref, cols[:, 1280:1792], grid_layout)
    g_ref[0] = cols[:, 1792:2304]

    z = cols[:, 2304:2432].astype(BF16)
    za = _dot(z, gw_ref[...]) + gb_ref[...]
    log_sig = jnp.minimum(za, 0.0) - jnp.log1p(jnp.exp(-jnp.abs(za)))
    _store_tokens(la_ref, log_sig * (1.0 / GLA_GATE_TEMP), grid_layout)


def _stage_a(xs, mod, g1, win, sguw, sgub, ones_sgu, gw, gb, s5_all, s5_row0, tb, grid_layout):
    b, l, d = xs.shape
    assert s5_row0 % tb == 0
    kern = functools.partial(_stage_a_kernel, tb=tb, grid_layout=grid_layout)
    tok = lambda w: pl.BlockSpec((1, tb, w), lambda bi, i: (bi, i, 0))
    if grid_layout:
        assert tb % GRID_W == 0 and l % GRID_W == 0
        gla = lambda w: pl.BlockSpec((GRID_W, None, tb // GRID_W, w), lambda bi, i: (0, bi, i, 0))
        gla_shape = lambda w: jax.ShapeDtypeStruct((GRID_W, b, l // GRID_W, w), F32)
    else:
        gla = tok
        gla_shape = lambda w: jax.ShapeDtypeStruct((b, l, w), F32)
    nat_shape = lambda w: jax.ShapeDtypeStruct((b, l, w), F32)
    return pl.pallas_call(
        kern,
        grid=(b, l // tb),
        in_specs=[
            tok(d),
            pl.BlockSpec((1, 8, d), lambda bi, i: (bi, 0, 0)),
            _full_spec(g1.shape), _full_spec(win.shape), _full_spec(sguw.shape), _full_spec(sgub.shape),
            _full_spec(ones_sgu.shape), _full_spec(gw.shape), _full_spec(gb.shape),
            pl.BlockSpec(memory_space=pl.ANY),
        ],
        out_specs=[tok(SGU_DIM),
                   pl.BlockSpec((1, tb, S5_DIM), lambda bi, i: (bi, s5_row0 // tb + i, 0)),
                   gla(2 * GLA_KEY_DIM), gla(GLA_DIM), tok(GLA_DIM), gla(2 * GLA_KEY_DIM)],
        out_shape=[nat_shape(SGU_DIM), jax.ShapeDtypeStruct(s5_all.shape, F32),
                   gla_shape(2 * GLA_KEY_DIM), gla_shape(GLA_DIM), nat_shape(GLA_DIM), gla_shape(2 * GLA_KEY_DIM)],
        input_output_aliases={9: 1},
        compiler_params=_cparams(("parallel", "parallel")),
    )(xs, mod, g1, win, sguw, sgub, ones_sgu, gw, gb, s5_all)


S5_TC = 128


def _s5_kernel(uf_ref, ub_ref, perm_ref, permt_ref, b2_ref, ar_ref, ai_ref, c2_ref, yf_ref, yb_ref,
               h_ref, buf0_ref, buf1_ref, buf2_ref, *, nseq):
    tc = S5_TC
    rows = tc * 2 * nseq
    s = pl.program_id(0)

    @pl.when(s == 0)
    def _():
        h_ref[...] = jnp.zeros_like(h_ref)
        buf0_ref[...] = jnp.zeros_like(buf0_ref)
        buf1_ref[...] = jnp.zeros_like(buf1_ref)
        buf2_ref[...] = jnp.zeros_like(buf2_ref)

    def phase(scan_ref, fill_ref, read_ref):
        ar = ar_ref[...]
        ai = ai_ref[...]
        hr, hi = h_ref[:, 0:S5_LANES], h_ref[:, S5_LANES:]
        for t in range(tc):
            bur = scan_ref[t, :, 0:S5_LANES]
            bui = scan_ref[t, :, S5_LANES:]
            hr, hi = ar * hr - ai * hi + bur, ar * hi + ai * hr + bui
            scan_ref[t, :, 0:S5_LANES] = hr
            scan_ref[t, :, S5_LANES:] = hi
        h_ref[:, 0:S5_LANES] = hr
        h_ref[:, S5_LANES:] = hi

        fwd_row = lax.broadcasted_iota(jnp.int32, (rows, S5_DIM), 0) % (2 * nseq) < nseq
        hs = read_ref[...].reshape(rows, 2 * S5_LANES).astype(BF16)
        y2 = _dot(hs, c2_ref[...])
        y = jnp.where(fwd_row, y2[:, 0:S5_DIM], y2[:, S5_DIM:])
        y_hi = y.astype(BF16)
        y_lo = (y - y_hi.astype(F32)).astype(BF16)
        y_nat = _dot(permt_ref[...], y_hi) + _dot(permt_ref[...], y_lo)
        for b in range(nseq):
            yf_ref[b] = y_nat[b * tc:(b + 1) * tc]
            yb_ref[b] = y_nat[(nseq + b) * tc:(nseq + b + 1) * tc]

        x = jnp.concatenate([uf_ref[b] for b in range(nseq)] + [ub_ref[b] for b in range(nseq)], axis=0)
        u_tm = _dot(perm_ref[...], x.astype(BF16))
        u = jnp.concatenate([jnp.where(fwd_row, u_tm, 0.0), jnp.where(fwd_row, 0.0, u_tm)], axis=-1).astype(BF16)
        fill_ref[...] = _dot(u, b2_ref[...]).reshape(tc, 2 * nseq, 2 * S5_LANES)

    bufs = (buf0_ref, buf1_ref, buf2_ref)
    for k in range(3):
        @pl.when(s % 3 == k)
        def _(k=k):
            phase(bufs[(k + 2) % 3], bufs[k], bufs[(k + 1) % 3])


def _s5_permutation(nseq):
    tc = S5_TC
    p = np.zeros((tc * 2 * nseq, tc * 2 * nseq), np.float32)
    for q in range(2 * nseq):
        for t in range(tc):
            p[t * 2 * nseq + q, q * tc + (t if q < nseq else tc - 1 - t)] = 1.0
    return p


def _s5_call(s5_all, n_lat, b2, ar, ai, c2):
    nseq, t, _ = s5_all.shape
    n = t // S5_TC
    n_l = n_lat // S5_TC
    n_c = n - n_l
    perm = _s5_permutation(nseq)
    kern = functools.partial(_s5_kernel, nseq=nseq)
    fwd_blk = lambda k: jnp.where(k < n_c, n_l + k, k - n_c)
    bwd_blk = lambda k: n - 1 - k
    in_k = lambda s: jnp.minimum(s, n - 1)
    out_k = lambda s: jnp.clip(s - 2, 0, n - 1)
    blk = (nseq, S5_TC, S5_DIM)
    rows = 2 * nseq
    buf = pltpu.VMEM((S5_TC, rows, 2 * S5_LANES), F32)
    return pl.pallas_call(
        kern,
        grid=(n + 2,),
        in_specs=[
            pl.BlockSpec(blk, lambda s: (0, fwd_blk(in_k(s)), 0)), pl.BlockSpec(blk, lambda s: (0, bwd_blk(in_k(s)), 0)),
            _full_spec(perm.shape), _full_spec(perm.shape),
            _full_spec(b2.shape), _full_spec(ar.shape), _full_spec(ai.shape), _full_spec(c2.shape),
        ],
        out_specs=[pl.BlockSpec(blk, lambda s: (0, fwd_blk(out_k(s)), 0)),
                   pl.BlockSpec(blk, lambda s: (0, bwd_blk(out_k(s)), 0))],
        out_shape=[jax.ShapeDtypeStruct(s5_all.shape, F32), jax.ShapeDtypeStruct(s5_all.shape, F32)],
        scratch_shapes=[pltpu.VMEM((rows, 2 * S5_LANES), F32), buf, buf, buf],
        compiler_params=_cparams(("arbitrary",)),
    )(s5_all, s5_all, jnp.asarray(perm, dtype=BF16), jnp.asarray(perm.T, dtype=BF16), b2, ar, ai, c2)


def _gla_kernel(qkf_ref, vf_ref, laf_ref, qkb_ref, vb_ref, lab_ref, s0_ref, trif_ref, trib_ref,
                of_ref, ob_ref, sout_ref, s_ref, *, nb):
    c = pl.program_id(0)

    @pl.when(c == 0)
    def _():
        s_ref[...] = s0_ref[...]

    ch = GLA_CHUNK
    r_k = lax.broadcasted_iota(jnp.int32, (GLA_HEADS * ch, GLA_KEY_DIM), 0) // ch
    c_k = lax.broadcasted_iota(jnp.int32, (GLA_HEADS * ch, GLA_KEY_DIM), 1) // GLA_DK
    hm_k = r_k == c_k
    r_v = lax.broadcasted_iota(jnp.int32, (GLA_HEADS * ch, GLA_DIM), 0) // ch
    c_v = lax.broadcasted_iota(jnp.int32, (GLA_HEADS * ch, GLA_DIM), 1) // GLA_DV
    hm_v = r_v == c_v
    r_s = lax.broadcasted_iota(jnp.int32, (GLA_DIM, GLA_KEY_DIM), 0) // GLA_DV
    c_s = lax.broadcasted_iota(jnp.int32, (GLA_DIM, GLA_KEY_DIM), 1) // GLA_DK
    hm_s = r_s == c_s
    t_i = lax.broadcasted_iota(jnp.int32, (ch, GLA_HEADS * ch), 0)
    s_i = lax.broadcasted_iota(jnp.int32, (ch, GLA_HEADS * ch), 1) % ch
    mask_f = t_i >= s_i
    mask_b = t_i <= s_i
    trif = trif_ref[...]
    trib = trib_ref[...]

    fwd = dict(qk=qkf_ref, v=vf_ref, la=laf_ref, o=of_ref, tri=trif, last=ch - 1, ref=ch // 2, mask=mask_f, d=0)
    bwd = dict(qk=qkb_ref, v=vb_ref, la=lab_ref, o=ob_ref, tri=trib, last=0, ref=ch - 1 - ch // 2, mask=mask_b, d=1)
    streams = [(b, p) for b in range(nb) for p in (fwd, bwd)]

    bcums = [_dot_m_exact(p["tri"], p["la"][b]) for b, p in streams]
    scaled = []
    for (b, p), bcum in zip(streams, bcums):
        qk = p["qk"][b]
        q, k = qk[:, 0:GLA_KEY_DIM], qk[:, GLA_KEY_DIM:]
        blast = bcum[p["last"]:p["last"] + 1]
        bref = bcum[p["ref"]:p["ref"] + 1]
        qe = (q * jnp.exp(bcum)).astype(BF16)
        qd = (q * jnp.exp(bcum - bref)).astype(BF16)
        kd = k * jnp.exp(bref - bcum)
        kdec = (k * jnp.exp(blast - bcum)).astype(BF16)
        kst = jnp.where(hm_k, jnp.concatenate([kd] * GLA_HEADS, axis=0), 0.0).astype(BF16)
        scaled.append((qe, qd, kdec, kst, jnp.exp(blast)))
    prods = []
    for (b, p), (qe, qd, kdec, kst, _) in zip(streams, scaled):
        v = p["v"][b]
        sc = _dot_nt(qd, kst)
        o_inter = _dot_nt(qe, s_ref[b, p["d"]].astype(BF16))
        kv_t = _dot_tn(v.astype(BF16), kdec)
        prods.append((sc, o_inter, kv_t))
    for (b, p), (_, _, _, _, decay), (sc, o_inter, kv_t) in zip(streams, scaled, prods):
        vbd = jnp.where(hm_v, jnp.concatenate([p["v"][b]] * GLA_HEADS, axis=0), 0.0).astype(BF16)
        p["o"][b] = _dot(jnp.where(p["mask"], sc, 0.0).astype(BF16), vbd) + o_inter
        s_ref[b, p["d"]] = s_ref[b, p["d"]] * decay + jnp.where(hm_s, kv_t, 0.0)

    @pl.when(c == pl.num_programs(0) - 1)
    def _():
        sout_ref[...] = s_ref[...]


def _gla_call(qk, v, la, s0, trif, trib):
    ch = GLA_CHUNK
    if qk.ndim == 4:
        n, b = qk.shape[0], qk.shape[1]
        assert qk.shape[2] == ch
        spec = lambda w, off, rev: pl.BlockSpec(
            (None, b, ch, w), (lambda c: (n - 1 - c, 0, 0, off)) if rev else (lambda c: (c, 0, 0, off)))
    else:
        b = qk.shape[0]
        n = qk.shape[1] // ch
        spec = lambda w, off, rev: pl.BlockSpec(
            (b, ch, w), (lambda c: (0, n - 1 - c, off)) if rev else (lambda c: (0, c, off)))
    kern = functools.partial(_gla_kernel, nb=b)
    o_shape = jax.ShapeDtypeStruct(v.shape, F32)
    return pl.pallas_call(
        kern,
        grid=(n,),
        in_specs=[
            spec(2 * GLA_KEY_DIM, 0, False), spec(GLA_DIM, 0, False), spec(GLA_KEY_DIM, 0, False),
            spec(2 * GLA_KEY_DIM, 0, True), spec(GLA_DIM, 0, True), spec(GLA_KEY_DIM, 1, True),
            _full_spec(s0.shape), _full_spec(trif.shape), _full_spec(trib.shape),
        ],
        out_specs=[spec(GLA_DIM, 0, False), spec(GLA_DIM, 0, True), _full_spec(s0.shape)],
        out_shape=[o_shape, o_shape, jax.ShapeDtypeStruct(s0.shape, F32)],
        scratch_shapes=[pltpu.VMEM(s0.shape, F32)],
        compiler_params=_cparams(("arbitrary",)),
    )(qk, v, la, qk, v, la, s0, trif, trib)


def _stage_e_kernel(x_ref, mod_ref, sgu_ref, yf_ref, yb_ref, s5x_ref, of_ref, ob_ref, g_ref,
                    dskip_ref, wglu_ref, normg_ref, ones_ref, wout_ref, o_ref, *, grid_layout):
    ys = yf_ref[0] + yb_ref[0] + dskip_ref[...] * s5x_ref[0]
    z = _dot(_gelu(ys).astype(BF16), wglu_ref[...])
    s5o = z[:, 0:S5_DIM] * _sigmoid(z[:, S5_DIM:])
    o = _load_tokens(of_ref, grid_layout) + _load_tokens(ob_ref, grid_layout)
    ms = _dot_x_exact(o * o, ones_ref[...]) * (1.0 / GLA_DV)
    g = g_ref[0]
    gl = o * lax.rsqrt(ms + EPS) * normg_ref[...] * (g * _sigmoid(g))
    y = (_dot(sgu_ref[0].astype(BF16), wout_ref[0:SGU_DIM, :])
         + _dot(s5o.astype(BF16), wout_ref[SGU_DIM:SGU_DIM + S5_DIM, :])
         + _dot(gl.astype(BF16), wout_ref[SGU_DIM + S5_DIM:, :]))
    o_ref[0] = x_ref[0] + mod_ref[0, 2:3, :] * y


def _stage_e(xs, mod, sgu, yf_all, yb_all, s5_all, s5_row0, of, ob, g, dskip, wglu, normg, ones_gla, wout, tb):
    b, l, d = xs.shape
    assert s5_row0 % tb == 0
    grid_layout = of.ndim == 4
    tok = lambda w: pl.BlockSpec((1, tb, w), lambda bi, i: (bi, i, 0))
    s5 = pl.BlockSpec((1, tb, S5_DIM), lambda bi, i: (bi, s5_row0 // tb + i, 0))
    if grid_layout:
        assert tb % GRID_W == 0
        gla = pl.BlockSpec((GRID_W, None, tb // GRID_W, GLA_DIM), lambda bi, i: (0, bi, i, 0))
    else:
        gla = tok(GLA_DIM)
    return pl.pallas_call(
        functools.partial(_stage_e_kernel, grid_layout=grid_layout),
        grid=(b, l // tb),
        in_specs=[
            tok(d), pl.BlockSpec((1, 8, d), lambda bi, i: (bi, 0, 0)),
            tok(SGU_DIM), s5, s5, s5, gla, gla, tok(GLA_DIM),
            _full_spec(dskip.shape), _full_spec(wglu.shape), _full_spec(normg.shape),
            _full_spec(ones_gla.shape), _full_spec(wout.shape),
        ],
        out_specs=tok(d),
        out_shape=jax.ShapeDtypeStruct((b, l, d), F32),
        compiler_params=_cparams(("parallel", "parallel")),
    )(xs, mod, sgu, yf_all, yb_all, s5_all, of, ob, g, dskip, wglu, normg, ones_gla, wout)


PEER_TBF = 256
AUX_ROWS = 16
AUX_THETA, AUX_V1_TOP, AUX_V1_LAST, AUX_N_TOP, AUX_RZ = 8, 9, 10, 11, 12


def _sort_network_16():
    def merge(lo, hi, r):
        step = r * 2
        if step < hi - lo:
            yield from merge(lo, hi, step)
            yield from merge(lo + r, hi, step)
            for i in range(lo + r, hi - r, step):
                yield (i, i + r)
        else:
            yield (lo, lo + r)

    def sort(lo, hi):
        if hi - lo >= 1:
            mid = lo + (hi - lo) // 2
            yield from sort(lo, mid)
            yield from sort(mid + 1, hi)
            yield from merge(lo, hi, 1)

    return tuple(sort(0, PEER_TOPK - 1))


SORT16 = _sort_network_16()
BITONIC16 = tuple((k, k + s) for s in (8, 4, 2, 1) for k in range(PEER_TOPK) if not k & s)


def _compare_exchange(xs, pairs):
    xs = list(xs)
    for i, j in pairs:
        hi = jnp.maximum(xs[i], xs[j])
        lo = jnp.minimum(xs[i], xs[j])
        xs[i], xs[j] = hi, lo
    return xs


def _merge_sublanes(xs):
    for shift in (4, 6, 7):
        rolled = [pltpu.roll(x, shift, 0) for x in xs]
        xs = [jnp.maximum(xs[k], rolled[PEER_TOPK - 1 - k]) for k in range(PEER_TOPK)]
        xs = _compare_exchange(xs, BITONIC16)
    return xs


def _dup_bf16_words(x):
    bits = pltpu.bitcast(x.astype(BF16).astype(F32), jnp.int32)
    return bits | lax.shift_right_logical(bits, 16)


def _stage_f_kernel(x_ref, mod_ref, g2_ref, wqt_ref, keys_ref,
                    ht_ref, s1_ref, aux_ref, e2_ref, r2_ref, qt_ref, v1_ref, v2_ref, *, tb):
    x = x_ref[0]
    ms = jnp.mean(x * x, axis=-1, keepdims=True)
    xn = x * lax.rsqrt(ms + EPS) * g2_ref[...]
    h = xn * (1.0 + mod_ref[0, 4:5, :]) + mod_ref[0, 3:4, :]
    ht = h.T.astype(BF16)
    ht_ref[...] = pltpu.bitcast(ht, jnp.int32)
    qt_ref[...] = _dot(wqt_ref[...], ht)
    k_top = PEER_TOPK

    def tiles(s):
        return [s[SUBLANES * k:SUBLANES * (k + 1)] for k in range(PEER_NKEYS // SUBLANES)]

    def head_body(hh, carry):
        for tc in range(tb // LANE):
            tcol = slice(tc * LANE, (tc + 1) * LANE)
            r1 = hh * (2 * PEER_HALF)
            q1 = qt_ref[r1:r1 + PEER_HALF, tcol].astype(BF16)
            q2 = qt_ref[r1 + PEER_HALF:r1 + 2 * PEER_HALF, tcol].astype(BF16)
            s1 = _dot(keys_ref[hh], q1)
            s2 = _dot(keys_ref[PEER_HEADS + hh], q2)
            for s, v_ref in ((s1, v1_ref), (s2, v2_ref)):
                top = _merge_sublanes(_compare_exchange(tiles(s), SORT16))
                for k in range(k_top):
                    v_ref[k:k + 1, tcol] = top[k][0:1]
            v1row = lambda a: v1_ref[a:a + 1, tcol]
            v2row = lambda b: v2_ref[b:b + 1, tcol]
            v1lo = v1_ref[0:SUBLANES, tcol]
            v2lo = v2_ref[0:SUBLANES, tcol]
            cand = [v1lo + v2row(b) for b in range(k_top)]
            tail = [v1row(a) + v2lo for a in range(SUBLANES, k_top)]
            for k in range(SUBLANES, k_top):
                cand[k] = jnp.maximum(cand[k], tail[k_top - 1 - k])
            best = _merge_sublanes(_compare_exchange(cand, BITONIC16))
            theta = best[k_top - 1][0:1]
            cmax = best[0][0:1]
            zsum = jnp.zeros((1, LANE), F32)
            for k in range(k_top):
                zsum = zsum + jnp.exp(best[k][0:1] - cmax)
            rz = 1.0 / zsum
            n_top = jnp.zeros((1, LANE), F32)
            for b in range(k_top):
                n_top = jnp.where(v1row(0) + v2row(b) >= theta, float(b + 1), n_top)
            r2 = jnp.zeros(s2.shape, F32)
            for b in range(k_top):
                r2 = jnp.where(v2row(b) > s2, float(b + 1), r2)
            e2 = jnp.where(s2 >= v2row(k_top - 1), jnp.exp(s2 - v2row(0)), 0.0)
            s1_ref[tc, hh] = s1
            aux_ref[tc, hh, 0:SUBLANES, :] = v2lo
            aux_ref[tc, hh, SUBLANES:, :] = jnp.zeros((AUX_ROWS - SUBLANES, LANE), F32)
            for row, val in ((AUX_THETA, theta), (AUX_V1_TOP, v1row(0)), (AUX_V1_LAST, v1row(k_top - 1)),
                             (AUX_N_TOP, n_top), (AUX_RZ, rz)):
                aux_ref[tc, hh, row:row + 1, :] = val
            e2_ref[tc, hh] = pltpu.bitcast(e2.astype(BF16), jnp.int32)
            r2_ref[tc, hh] = pltpu.bitcast(r2.astype(BF16), jnp.int32)
        return carry

    for hh in range(PEER_HEADS):
        head_body(hh, 0)


def _stage_f(xs, mod, g2, wqt, keys, tb):
    b, l, d = xs.shape
    nblk = l // tb
    ntok = b * l
    nch = ntok // LANE
    kern = functools.partial(_stage_f_kernel, tb=tb)
    row_spec = pl.BlockSpec((tb // LANE, PEER_HEADS, PEER_NKEYS, LANE), lambda bi, i: (bi * nblk + i, 0, 0, 0))
    pair_spec = pl.BlockSpec((tb // LANE, PEER_HEADS, PEER_NKEYS // 2, LANE), lambda bi, i: (bi * nblk + i, 0, 0, 0))
    desc_shape = lambda rows: jax.ShapeDtypeStruct((nch, PEER_HEADS, rows, LANE), jnp.int32)
    return pl.pallas_call(
        kern,
        grid=(b, nblk),
        in_specs=[
            pl.BlockSpec((1, tb, d), lambda bi, i: (bi, i, 0)),
            pl.BlockSpec((1, 8, d), lambda bi, i: (bi, 0, 0)),
            _full_spec(g2.shape), _full_spec(wqt.shape), _full_spec(keys.shape),
        ],
        out_specs=[pl.BlockSpec((d // 2, tb), lambda bi, i: (0, bi * nblk + i)),
                   row_spec,
                   pl.BlockSpec((tb // LANE, PEER_HEADS, AUX_ROWS, LANE), lambda bi, i: (bi * nblk + i, 0, 0, 0)),
                   pair_spec, pair_spec],
        out_shape=[jax.ShapeDtypeStruct((d // 2, ntok), jnp.int32),
                   jax.ShapeDtypeStruct((nch, PEER_HEADS, PEER_NKEYS, LANE), F32),
                   jax.ShapeDtypeStruct((nch, PEER_HEADS, AUX_ROWS, LANE), F32),
                   desc_shape(PEER_NKEYS // 2), desc_shape(PEER_NKEYS // 2)],
        scratch_shapes=[
            pltpu.VMEM((PEER_HEADS * 2 * PEER_HALF, tb), F32),
            pltpu.VMEM((PEER_TOPK, tb), F32),
            pltpu.VMEM((PEER_TOPK, tb), F32),
        ],
        compiler_params=_cparams(("parallel", "parallel")),
    )(xs, mod, g2, wqt, keys)


PEER_TBG = 1024
PEER_TE = 1024
PEER_I1_PER_TILE = PEER_TE // PEER_NKEYS
PEER_N_TILES = PEER_NKEYS * PEER_NKEYS // PEER_TE
PEER_MXU_COLS = 256
PEER_G_FLAGS = None


def _stage_g_kernel(htw_ref, s1_ref, aux_ref, e2_ref, r2_ref, uw_ref, vtw_ref, x_ref, mod_ref, fg_ref,
                    o_ref, acc_ref, ata_ref, atb_ref, p_ref, e1w_ref, n1w_ref, *, tb, final_norm):
    s = pl.program_id(1)

    @pl.when(s == 0)
    def _():
        acc_ref[...] = jnp.zeros_like(acc_ref)

    def step(at_cur_ref, at_next_ref):
        per_grp = PEER_MXU_COLS // LANE
        for grp in range(tb // PEER_MXU_COLS):
            cols = slice(grp * PEER_MXU_COLS, (grp + 1) * PEER_MXU_COLS)
            for tcl in range(per_grp if at_cur_ref is not None else 0):
                tc = grp * per_grp + tcl
                tcol = slice(tc * LANE, (tc + 1) * LANE)
                for i1l in range(PEER_I1_PER_TILE):
                    rows = slice(i1l * PEER_NKEYS, (i1l + 1) * PEER_NKEYS)
                    gate = jnp.zeros((PEER_NKEYS, LANE), BF16)
                    for hh in range(PEER_HEADS):
                        e1row = e1w_ref[tc, hh, i1l:i1l + 1, :]
                        n1row = n1w_ref[tc, hh, i1l:i1l + 1, :]
                        e1 = pltpu.bitcast(jnp.broadcast_to(e1row, (PEER_NKEYS // 2, LANE)), BF16)
                        n1 = pltpu.bitcast(jnp.broadcast_to(n1row, (PEER_NKEYS // 2, LANE)), BF16)
                        r2 = pltpu.bitcast(r2_ref[tc, hh], BF16)
                        e2 = pltpu.bitcast(e2_ref[tc, hh], BF16)
                        gate = gate + e2 * jnp.where(r2 < n1, e1, 0.0)
                    p_ref[rows, tcol] = gate * _gelu(at_cur_ref[rows, tcol])
            for tcl in range(per_grp if at_next_ref is not None else 0):
                tc = grp * per_grp + tcl
                for hh in range(PEER_HEADS):
                    s1 = s1_ref[tc, hh]
                    aux = lambda r: aux_ref[tc, hh, r:r + 1, :]
                    n1 = jnp.zeros(s1.shape, F32)
                    for b in range(SUBLANES):
                        n1 = jnp.where(s1 + aux(b) >= aux(AUX_THETA), float(b + 1), n1)
                    n1 = jnp.where(s1 >= aux(AUX_V1_TOP), aux(AUX_N_TOP), n1)
                    e1 = jnp.where(s1 >= aux(AUX_V1_LAST), jnp.exp(s1 - aux(AUX_V1_TOP)), 0.0) * aux(AUX_RZ)
                    e1w_ref[tc, hh] = _dup_bf16_words(e1)
                    n1w_ref[tc, hh] = _dup_bf16_words(n1)
            if at_cur_ref is not None:
                vt = pltpu.bitcast(vtw_ref[...], BF16)
                acc_ref[:, cols] += _dot(vt, p_ref[:, cols])
            if at_next_ref is not None:
                u = pltpu.bitcast(uw_ref[...], BF16)
                ht = pltpu.bitcast(htw_ref[:, cols], BF16)
                at_next_ref[:, cols] = _dot(u, ht).astype(BF16)

    last = pl.num_programs(1) - 1

    @pl.when(s == 0)
    def _():
        step(None, ata_ref)

    @pl.when(jnp.logical_and(s % 2 == 0, jnp.logical_and(s > 0, s < last)))
    def _():
        step(atb_ref, ata_ref)

    @pl.when(s % 2 == 1)
    def _():
        step(ata_ref, atb_ref)

    @pl.when(s == last)
    def _():
        step(atb_ref, None)
        xo = x_ref[...] + mod_ref[0, 5:6, :] * acc_ref[...].T
        if final_norm:
            ms = jnp.mean(xo * xo, axis=-1, keepdims=True)
            xo = xo * lax.rsqrt(ms + EPS) * fg_ref[...]
        o_ref[...] = xo


def _stage_g(htw, s1, aux, e2, r2, uw, vtw, xflat, mod, final_g, tokens_per_batch, tb, final_norm):
    ntok = htw.shape[1]
    d = 2 * htw.shape[0]
    assert 2 * uw.shape[0] == PEER_N_TILES * PEER_TE and PEER_N_TILES % 2 == 0
    blocks_per_batch = tokens_per_batch // tb
    kern = functools.partial(_stage_g_kernel, tb=tb, final_norm=final_norm)
    last = PEER_N_TILES - 1
    desc_spec = pl.BlockSpec((tb // LANE, PEER_HEADS, PEER_NKEYS // 2, LANE), lambda j, i: (j, 0, 0, 0))
    row_spec = pl.BlockSpec((tb // LANE, PEER_HEADS, PEER_I1_PER_TILE, LANE),
                            lambda j, i: (j, 0, jnp.minimum(i, last), 0))
    return pl.pallas_call(
        kern,
        grid=(ntok // tb, PEER_N_TILES + 1),
        in_specs=[
            pl.BlockSpec((d // 2, tb), lambda j, i: (0, j)),
            row_spec,
            pl.BlockSpec((tb // LANE, PEER_HEADS, AUX_ROWS, LANE), lambda j, i: (j, 0, 0, 0)),
            desc_spec, desc_spec,
            pl.BlockSpec((PEER_TE // 2, d), lambda j, i: (jnp.minimum(i, last), 0)),
            pl.BlockSpec((d // 2, PEER_TE), lambda j, i: (0, jnp.maximum(i - 1, 0))),
            pl.BlockSpec((tb, d), lambda j, i: (j, 0)),
            pl.BlockSpec((1, 8, d), lambda j, i: (j // blocks_per_batch, 0, 0)),
            _full_spec(final_g.shape),
        ],
        out_specs=pl.BlockSpec((tb, d), lambda j, i: (j, 0)),
        out_shape=jax.ShapeDtypeStruct((ntok, d), F32),
        scratch_shapes=[
            pltpu.VMEM((d, tb), F32),
            pltpu.VMEM((PEER_TE, tb), BF16),
            pltpu.VMEM((PEER_TE, tb), BF16),
            pltpu.VMEM((PEER_TE, tb), BF16),
            pltpu.VMEM((tb // LANE, PEER_HEADS, PEER_I1_PER_TILE, LANE), jnp.int32),
            pltpu.VMEM((tb // LANE, PEER_HEADS, PEER_I1_PER_TILE, LANE), jnp.int32),
        ],
        compiler_params=_cparams(("parallel", "arbitrary"), PEER_G_FLAGS),
    )(htw, s1, aux, e2, r2, uw, vtw, xflat, mod, final_g)


def _pack_kernel(x_ref, o_ref, *, transpose):
    x = x_ref[...]
    if transpose:
        x = x.T
    o_ref[...] = pltpu.bitcast(x.astype(BF16), jnp.int32)


def _pack_row_pairs(x, layer, transpose=False, tile=1024):
    _, r, c = x.shape
    if transpose:
        out_shape, out_spec = (c // 2, r), pl.BlockSpec((c // 2, tile), lambda i: (0, i))
    else:
        out_shape, out_spec = (r // 2, c), pl.BlockSpec((tile // 2, c), lambda i: (i, 0))
    return pl.pallas_call(
        functools.partial(_pack_kernel, transpose=transpose),
        grid=(r // tile,),
        in_specs=[pl.BlockSpec((None, tile, c), lambda i: (layer, i, 0))],
        out_specs=out_spec,
        out_shape=jax.ShapeDtypeStruct(out_shape, jnp.int32),
        compiler_params=_cparams(("parallel",)),
    )(x)


def _block_ones(n, blk):
    idx = np.arange(n) // blk
    return jnp.asarray((idx[:, None] == idx[None, :]).astype(np.float32), dtype=BF16)


def _s5_discretise(lam_re, lam_im, b_re, b_im, log_step):
    lam_re = jnp.minimum(lam_re.astype(F32), -1e-4)
    lam_im = lam_im.astype(F32)
    dt = jnp.exp(log_step.astype(F32))[:, None]
    mag = jnp.exp(lam_re * dt)
    a_re = mag * jnp.cos(lam_im * dt)
    a_im = mag * jnp.sin(lam_im * dt)
    den = lam_re * lam_re + lam_im * lam_im
    f_re = ((a_re - 1.0) * lam_re + a_im * lam_im) / den
    f_im = (a_im * lam_re - (a_re - 1.0) * lam_im) / den
    b_re = b_re.astype(F32)
    b_im = b_im.astype(F32)
    bb_re = f_re[..., None] * b_re - f_im[..., None] * b_im
    bb_im = f_re[..., None] * b_im + f_im[..., None] * b_re
    return a_re, a_im, bb_re, bb_im


def _group_block_diag(t):
    g, r, c = t.shape
    eye = jnp.eye(g, dtype=t.dtype)
    return (t[:, :, None, :] * eye[:, None, :, None]).reshape(g * r, g * c)


def _s5_params(lam_re, lam_im, b_re, b_im, c_re, c_im, log_step, nseq):
    b_rows, c_cols, ars, ais = [], [], [], []
    for d in range(2):
        a_re, a_im, bb_re, bb_im = _s5_discretise(lam_re[d], lam_im[d], b_re[d], b_im[d], log_step[d])
        bm = jnp.concatenate([_group_block_diag(jnp.swapaxes(bb_re, 1, 2)),
                              _group_block_diag(jnp.swapaxes(bb_im, 1, 2))], axis=1)
        b_rows.append(bm)
        cm = jnp.concatenate([_group_block_diag(jnp.swapaxes(c_re[d].astype(F32), 1, 2)),
                              -_group_block_diag(jnp.swapaxes(c_im[d].astype(F32), 1, 2))], axis=0)
        c_cols.append(cm)
        ars.append(jnp.broadcast_to(a_re.reshape(1, S5_LANES), (nseq, S5_LANES)))
        ais.append(jnp.broadcast_to(a_im.reshape(1, S5_LANES), (nseq, S5_LANES)))
    b2 = jnp.concatenate(b_rows, axis=0).astype(BF16)
    c2 = jnp.concatenate(c_cols, axis=1).astype(BF16)
    return b2, jnp.concatenate(ars, axis=0), jnp.concatenate(ais, axis=0), c2


def kernel(x, c, ctx, c_ctx, w_mod, b_mod, norm1_g, norm2_g, w_in, w_out, sgu_w, sgu_b, s5_lambda_re, s5_lambda_im, s5_b_re, s5_b_im, s5_c_re, s5_c_im, s5_log_step, s5_d, s5_w_glu, gla_w_gate, gla_b_gate, gla_norm_g, peer_w_query, peer_sub_keys, peer_expert_u, peer_expert_v, final_norm_g):
    nb, seq, d = x.shape
    c_len = ctx.shape[1]
    depth = w_mod.shape[0]

    cc = jnp.concatenate([c, c_ctx[None, :], jnp.zeros((8 - nb - 1, d), F32)], axis=0)
    mods = _mod_call(cc, w_mod, b_mod)

    ones_sgu = _block_ones(SGU_DIM, SGU_HEAD_DIM)
    ones_gla = _block_ones(GLA_DIM, GLA_DV)
    tri_np = np.tril(np.ones((GLA_CHUNK, GLA_CHUNK), np.float32))
    trif = jnp.asarray(tri_np, dtype=BF16)
    trib = jnp.asarray(tri_np.T, dtype=BF16)
    s_zero = jnp.zeros((nb, 2, GLA_DIM, GLA_KEY_DIM), F32)
    final_g = final_norm_g.reshape(1, d)

    xl, xc = x, ctx
    for l in range(depth):
        ctx_out = l < depth - 1
        m6 = mods[l].reshape(8, N_MOD, d)
        mod_l = jnp.pad(m6[:nb], ((0, 0), (0, 2), (0, 0)))
        mod_c = jnp.broadcast_to(jnp.pad(m6[nb], ((0, 2), (0, 0)))[None], (nb, 8, d))

        win = jnp.pad(w_in[l], ((0, 0), (0, IN_PAD - IN_WIDTH))).astype(BF16)
        sguw = sgu_w[l].astype(BF16)
        sgub = jnp.repeat(jnp.swapaxes(sgu_b[l], 0, 1), SGU_HEAD_DIM, axis=1)
        gw = jnp.zeros((LANE, 2 * GLA_KEY_DIM), F32)
        gw = gw.at[0:GLA_RANK, 0:GLA_KEY_DIM].set(gla_w_gate[l, 0])
        gw = gw.at[GLA_RANK:2 * GLA_RANK, GLA_KEY_DIM:].set(gla_w_gate[l, 1]).astype(BF16)
        gb = gla_b_gate[l].reshape(1, 2 * GLA_KEY_DIM)
        g1 = norm1_g[l].reshape(1, d)

        s5_all = jnp.zeros((nb, seq + c_len, S5_DIM), F32)
        sgu_l, s5_all, qk_l, v_l, g_l, la_l = _stage_a(xl, mod_l, g1, win, sguw, sgub, ones_sgu, gw, gb,
                                                       s5_all, 0, tb=TB_LATENT, grid_layout=True)
        sgu_c, s5_all, qk_c, v_c, g_c, la_c = _stage_a(xc, mod_c, g1, win, sguw, sgub, ones_sgu, gw, gb,
                                                       s5_all, seq, tb=TB_CTX, grid_layout=False)

        b2, ar, ai, c2 = _s5_params(s5_lambda_re[l], s5_lambda_im[l], s5_b_re[l], s5_b_im[l],
                                    s5_c_re[l], s5_c_im[l], s5_log_step[l], nb)
        yf_all, yb_all = _s5_call(s5_all, seq, b2, ar, ai, c2)

        of_c, ob_c, s_ctx = _gla_call(qk_c, v_c, la_c, s_zero, trif, trib)
        of_l, ob_l, _ = _gla_call(qk_l, v_l, la_l, s_ctx, trif, trib)

        dskip = s5_d[l].reshape(1, S5_DIM)
        wglu = s5_w_glu[l].astype(BF16)
        normg = gla_norm_g[l].reshape(1, GLA_DIM)
        wout = w_out[l].astype(BF16)
        g2 = norm2_g[l].reshape(1, d)
        wqt = jnp.swapaxes(peer_w_query[l], 0, 1).astype(BF16)
        keys = peer_sub_keys[l].reshape(2 * PEER_HEADS, PEER_NKEYS, PEER_HALF).astype(BF16)
        u_bf = _pack_row_pairs(peer_expert_u, l)
        vt_bf = _pack_row_pairs(peer_expert_v, l, transpose=True)

        xl = _stage_e(xl, mod_l, sgu_l, yf_all, yb_all, s5_all, 0, of_l, ob_l, g_l,
                      dskip, wglu, normg, ones_gla, wout, tb=TB_LATENT)
        desc = _stage_f(xl, mod_l, g2, wqt, keys, tb=PEER_TBF)
        xl = _stage_g(*desc, u_bf, vt_bf, xl.reshape(nb * seq, d), mod_l, final_g, seq, PEER_TBG,
                      final_norm=not ctx_out).reshape(nb, seq, d)

        if ctx_out:
            xc = _stage_e(xc, mod_c, sgu_c, yf_all, yb_all, s5_all, seq, of_c, ob_c, g_c,
                          dskip, wglu, normg, ones_gla, wout, tb=TB_CTX)
            desc = _stage_f(xc, mod_c, g2, wqt, keys, tb=PEER_TBF)
            xc = _stage_g(*desc, u_bf, vt_bf, xc.reshape(nb * c_len, d), mod_c, final_g, c_len,
                          min(PEER_TBG, c_len), final_norm=False).reshape(nb, c_len, d)

    return xl
```

```python
import functools
import math

import numpy as np
import jax
import jax.numpy as jnp
from jax import lax
from jax.experimental import pallas as pl
from jax.experimental.pallas import tpu as pltpu

F32 = jnp.float32
BF16 = jnp.bfloat16
SUBLANES = 8

EPS = 1e-6
N_MOD = 6
GRID_W = 64

SGU_DIM = 256
SGU_HEADS = 4
SGU_HEAD_DIM = 64
SGU_CHUNK = 128

S5_DIM = 256
S5_GROUP = 16
S5_GROUPS = 16
S5_STATE = 64
S5_LANES = S5_GROUPS * S5_STATE

GLA_DIM = 512
GLA_HEADS = 8
GLA_DV = 64
GLA_DK = 32
GLA_KEY_DIM = 256
GLA_RANK = 16
GLA_GATE_TEMP = 16.0
GLA_CHUNK = 64

PEER_HEADS = 8
PEER_NKEYS = 128
PEER_HALF = 128
PEER_TOPK = 16

IN_WIDTH = 2336
IN_PAD = 2432
LANE = 128

VMEM_LIMIT = 56 * 1024 * 1024
TB_LATENT = 1024
TB_CTX = 256

NEG_INF = float("-inf")
POS_INF = float("inf")


def _cparams(sem, flags=None):
    return pltpu.CompilerParams(dimension_semantics=sem, vmem_limit_bytes=VMEM_LIMIT, flags=flags)


def _gelu(x):
    c = math.sqrt(2.0 / math.pi)
    return 0.5 * x * (1.0 + jnp.tanh(c * (x + 0.044715 * (x * x * x))))


def _sigmoid(x):
    return 1.0 / (1.0 + jnp.exp(-x))


def _dot(a, b):
    return jnp.dot(a, b, preferred_element_type=F32)


def _dot_nt(a, b):
    return lax.dot_general(a, b, (((1,), (1,)), ((), ())), preferred_element_type=F32)


def _dot_tn(a, b):
    return lax.dot_general(a, b, (((0,), (0,)), ((), ())), preferred_element_type=F32)


def _split3(x):
    hi = x.astype(BF16)
    r = x - hi.astype(F32)
    mid = r.astype(BF16)
    lo = (r - mid.astype(F32)).astype(BF16)
    return hi, mid, lo


def _dot_x_exact(x, m):
    hi, mid, lo = _split3(x)
    return _dot(hi, m) + _dot(mid, m) + _dot(lo, m)


def _dot_m_exact(m, x):
    hi, mid, lo = _split3(x)
    return _dot(m, hi) + _dot(m, mid) + _dot(m, lo)


def _full_spec(shape):
    nd = len(shape)
    return pl.BlockSpec(shape, lambda *_: (0,) * nd)


MOD_TILE = 512


def _mod_kernel(c_ref, w_ref, b_ref, o_ref):
    c = c_ref[...]
    a = c * _sigmoid(c)
    o_ref[0] = jnp.dot(a, w_ref[0], preferred_element_type=F32,
                       precision=lax.Precision.HIGHEST) + b_ref[0]


def _mod_call(cc, w_mod, b_mod):
    depth, d, nd = w_mod.shape
    rows = cc.shape[0]
    return pl.pallas_call(
        _mod_kernel,
        grid=(depth, nd // MOD_TILE),
        in_specs=[
            pl.BlockSpec((rows, d), lambda l, j: (0, 0)),
            pl.BlockSpec((1, d, MOD_TILE), lambda l, j: (l, 0, j)),
            pl.BlockSpec((1, 1, MOD_TILE), lambda l, j: (l, 0, j)),
        ],
        out_specs=pl.BlockSpec((1, rows, MOD_TILE), lambda l, j: (l, 0, j)),
        out_shape=jax.ShapeDtypeStruct((depth, rows, nd), F32),
        compiler_params=_cparams(("parallel", "parallel")),
    )(cc, w_mod, b_mod.reshape(depth, 1, nd))


def _store_tokens(ref, val, grid_layout):
    if grid_layout:
        for r in range(val.shape[0] // GRID_W):
            ref[:, r, :] = val[r * GRID_W:(r + 1) * GRID_W]
    else:
        ref[0] = val


def _load_tokens(ref, grid_layout):
    if grid_layout:
        return jnp.concatenate([ref[:, r, :] for r in range(ref.shape[1])], axis=0)
    return ref[0]


def _stage_a_kernel(x_ref, mod_ref, g1_ref, win_ref, sguw_ref, sgub_ref, ones_ref, gw_ref, gb_ref, s5_in_ref,
                    sgu_ref, s5x_ref, qk_ref, v_ref, g_ref, la_ref, *, tb, grid_layout):
    del s5_in_ref
    x = x_ref[0]
    ms = jnp.mean(x * x, axis=-1, keepdims=True)
    xn = x * lax.rsqrt(ms + EPS) * g1_ref[...]
    h = xn * (1.0 + mod_ref[0, 1:2, :]) + mod_ref[0, 0:1, :]
    cols = _dot(h.astype(BF16), win_ref[...])

    u = _gelu(cols[:, 0:SGU_DIM])
    v = _gelu(cols[:, SGU_DIM:2 * SGU_DIM])
    msq = _dot_x_exact(v * v, ones_ref[...]) * (1.0 / SGU_HEAD_DIM)
    vn = (v * lax.rsqrt(msq + EPS)).astype(BF16)
    head_of_lane = lax.broadcasted_iota(jnp.int32, (SGU_CHUNK, SGU_DIM), 1) // SGU_HEAD_DIM
    for ci in range(tb // SGU_CHUNK):
        rows = slice(ci * SGU_CHUNK, (ci + 1) * SGU_CHUNK)
        vc = vn[rows]
        mixed = sgub_ref[...]
        for hh in range(SGU_HEADS):
            mixed = mixed + jnp.where(head_of_lane == hh, _dot(sguw_ref[hh], vc), 0.0)
        sgu_ref[0, rows, :] = u[rows] * mixed

    s5x_ref[0] = cols[:, 512:768]
    q = cols[:, 768:1024] * (GLA_DK ** -0.5)
    _store_tokens(qk_ref, jnp.concatenate([q, cols[:, 1024:1280]], axis=-1), grid_layout)
    _store_tokens(v_ref, cols[:, 1280:1792], grid_layout)
    g_ref[0] = cols[:, 1792:2304]

    z = cols[:, 2304:2432].astype(BF16)
    za = _dot(z, gw_ref[...]) + gb_ref[...]
    log_sig = jnp.minimum(za, 0.0) - jnp.log1p(jnp.exp(-jnp.abs(za)))
    _store_tokens(la_ref, log_sig * (1.0 / GLA_GATE_TEMP), grid_layout)


def _stage_a(xs, mod, g1, win, sguw, sgub, ones_sgu, gw, gb, s5_all, s5_row0, tb, grid_layout):
    b, l, d = xs.shape
    assert s5_row0 % tb == 0
    kern = functools.partial(_stage_a_kernel, tb=tb, grid_layout=grid_layout)
    tok = lambda w: pl.BlockSpec((1, tb, w), lambda bi, i: (bi, i, 0))
    if grid_layout:
        assert tb % GRID_W == 0 and l % GRID_W == 0
        gla = lambda w: pl.BlockSpec((GRID_W, None, tb // GRID_W, w), lambda bi, i: (0, bi, i, 0))
        gla_shape = lambda w: jax.ShapeDtypeStruct((GRID_W, b, l // GRID_W, w), F32)
    else:
        gla = tok
        gla_shape = lambda w: jax.ShapeDtypeStruct((b, l, w), F32)
    nat_shape = lambda w: jax.ShapeDtypeStruct((b, l, w), F32)
    return pl.pallas_call(
        kern,
        grid=(b, l // tb),
        in_specs=[
            tok(d),
            pl.BlockSpec((1, 8, d), lambda bi, i: (bi, 0, 0)),
            _full_spec(g1.shape), _full_spec(win.shape), _full_spec(sguw.shape), _full_spec(sgub.shape),
            _full_spec(ones_sgu.shape), _full_spec(gw.shape), _full_spec(gb.shape),
            pl.BlockSpec(memory_space=pl.ANY),
        ],
        out_specs=[tok(SGU_DIM),
                   pl.BlockSpec((1, tb, S5_DIM), lambda bi, i: (bi, s5_row0 // tb + i, 0)),
                   gla(2 * GLA_KEY_DIM), gla(GLA_DIM), tok(GLA_DIM), gla(2 * GLA_KEY_DIM)],
        out_shape=[nat_shape(SGU_DIM), jax.ShapeDtypeStruct(s5_all.shape, F32),
                   gla_shape(2 * GLA_KEY_DIM), gla_shape(GLA_DIM), nat_shape(GLA_DIM), gla_shape(2 * GLA_KEY_DIM)],
        input_output_aliases={9: 1},
        compiler_params=_cparams(("parallel", "parallel")),
    )(xs, mod, g1, win, sguw, sgub, ones_sgu, gw, gb, s5_all)


S5_TC = 128


def _s5_kernel(uf_ref, ub_ref, perm_ref, permt_ref, b2_ref, ar_ref, ai_ref, c2_ref, yf_ref, yb_ref,
               h_ref, buf0_ref, buf1_ref, buf2_ref, *, nseq):
    tc = S5_TC
    rows = tc * 2 * nseq
    s = pl.program_id(0)

    @pl.when(s == 0)
    def _():
        h_ref[...] = jnp.zeros_like(h_ref)
        buf0_ref[...] = jnp.zeros_like(buf0_ref)
        buf1_ref[...] = jnp.zeros_like(buf1_ref)
        buf2_ref[...] = jnp.zeros_like(buf2_ref)

    def phase(scan_ref, fill_ref, read_ref):
        ar = ar_ref[...]
        ai = ai_ref[...]
        hr, hi = h_ref[:, 0:S5_LANES], h_ref[:, S5_LANES:]
        for t in range(tc):
            bur = scan_ref[t, :, 0:S5_LANES]
            bui = scan_ref[t, :, S5_LANES:]
            hr, hi = ar * hr - ai * hi + bur, ar * hi + ai * hr + bui
            scan_ref[t, :, 0:S5_LANES] = hr
            scan_ref[t, :, S5_LANES:] = hi
        h_ref[:, 0:S5_LANES] = hr
        h_ref[:, S5_LANES:] = hi

        fwd_row = lax.broadcasted_iota(jnp.int32, (rows, S5_DIM), 0) % (2 * nseq) < nseq
        hs = read_ref[...].reshape(rows, 2 * S5_LANES).astype(BF16)
        y2 = _dot(hs, c2_ref[...])
        y = jnp.where(fwd_row, y2[:, 0:S5_DIM], y2[:, S5_DIM:])
        y_nat = _dot(permt_ref[...], y.astype(BF16))
        for b in range(nseq):
            yf_ref[b] = y_nat[b * tc:(b + 1) * tc]
            yb_ref[b] = y_nat[(nseq + b) * tc:(nseq + b + 1) * tc]

        x = jnp.concatenate([uf_ref[b] for b in range(nseq)] + [ub_ref[b] for b in range(nseq)], axis=0)
        u_tm = _dot(perm_ref[...], x.astype(BF16))
        u = jnp.concatenate([jnp.where(fwd_row, u_tm, 0.0), jnp.where(fwd_row, 0.0, u_tm)], axis=-1).astype(BF16)
        fill_ref[...] = _dot(u, b2_ref[...]).reshape(tc, 2 * nseq, 2 * S5_LANES)

    bufs = (buf0_ref, buf1_ref, buf2_ref)
    for k in range(3):
        @pl.when(s % 3 == k)
        def _(k=k):
            phase(bufs[(k + 2) % 3], bufs[k], bufs[(k + 1) % 3])


def _s5_permutation(nseq):
    tc = S5_TC
    p = np.zeros((tc * 2 * nseq, tc * 2 * nseq), np.float32)
    for q in range(2 * nseq):
        for t in range(tc):
            p[t * 2 * nseq + q, q * tc + (t if q < nseq else tc - 1 - t)] = 1.0
    return p


def _s5_call(s5_all, n_lat, b2, ar, ai, c2):
    nseq, t, _ = s5_all.shape
    n = t // S5_TC
    n_l = n_lat // S5_TC
    n_c = n - n_l
    perm = _s5_permutation(nseq)
    kern = functools.partial(_s5_kernel, nseq=nseq)
    fwd_blk = lambda k: jnp.where(k < n_c, n_l + k, k - n_c)
    bwd_blk = lambda k: n - 1 - k
    in_k = lambda s: jnp.minimum(s, n - 1)
    out_k = lambda s: jnp.clip(s - 2, 0, n - 1)
    blk = (nseq, S5_TC, S5_DIM)
    rows = 2 * nseq
    buf = pltpu.VMEM((S5_TC, rows, 2 * S5_LANES), F32)
    return pl.pallas_call(
        kern,
        grid=(n + 2,),
        in_specs=[
            pl.BlockSpec(blk, lambda s: (0, fwd_blk(in_k(s)), 0)), pl.BlockSpec(blk, lambda s: (0, bwd_blk(in_k(s)), 0)),
            _full_spec(perm.shape), _full_spec(perm.shape),
            _full_spec(b2.shape), _full_spec(ar.shape), _full_spec(ai.shape), _full_spec(c2.shape),
        ],
        out_specs=[pl.BlockSpec(blk, lambda s: (0, fwd_blk(out_k(s)), 0)),
                   pl.BlockSpec(blk, lambda s: (0, bwd_blk(out_k(s)), 0))],
        out_shape=[jax.ShapeDtypeStruct(s5_all.shape, F32), jax.ShapeDtypeStruct(s5_all.shape, F32)],
        scratch_shapes=[pltpu.VMEM((rows, 2 * S5_LANES), F32), buf, buf, buf],
        compiler_params=_cparams(("arbitrary",)),
    )(s5_all, s5_all, jnp.asarray(perm, dtype=BF16), jnp.asarray(perm.T, dtype=BF16), b2, ar, ai, c2)


def _gla_kernel(qkf_ref, vf_ref, laf_ref, qkb_ref, vb_ref, lab_ref, s0_ref, trif_ref, trib_ref,
                of_ref, ob_ref, sout_ref, s_ref, *, nb):
    c = pl.program_id(0)

    @pl.when(c == 0)
    def _():
        s_ref[...] = s0_ref[...]

    ch = GLA_CHUNK
    r_k = lax.broadcasted_iota(jnp.int32, (GLA_HEADS * ch, GLA_KEY_DIM), 0) // ch
    c_k = lax.broadcasted_iota(jnp.int32, (GLA_HEADS * ch, GLA_KEY_DIM), 1) // GLA_DK
    hm_k = r_k == c_k
    r_v = lax.broadcasted_iota(jnp.int32, (GLA_HEADS * ch, GLA_DIM), 0) // ch
    c_v = lax.broadcasted_iota(jnp.int32, (GLA_HEADS * ch, GLA_DIM), 1) // GLA_DV
    hm_v = r_v == c_v
    r_s = lax.broadcasted_iota(jnp.int32, (GLA_DIM, GLA_KEY_DIM), 0) // GLA_DV
    c_s = lax.broadcasted_iota(jnp.int32, (GLA_DIM, GLA_KEY_DIM), 1) // GLA_DK
    hm_s = r_s == c_s
    t_i = lax.broadcasted_iota(jnp.int32, (ch, GLA_HEADS * ch), 0)
    s_i = lax.broadcasted_iota(jnp.int32, (ch, GLA_HEADS * ch), 1) % ch
    mask_f = t_i >= s_i
    mask_b = t_i <= s_i
    trif = trif_ref[...]
    trib = trib_ref[...]

    fwd = dict(qk=qkf_ref, v=vf_ref, la=laf_ref, o=of_ref, tri=trif, last=ch - 1, ref=ch // 2, mask=mask_f, d=0)
    bwd = dict(qk=qkb_ref, v=vb_ref, la=lab_ref, o=ob_ref, tri=trib, last=0, ref=ch - 1 - ch // 2, mask=mask_b, d=1)
    streams = [(b, p) for b in range(nb) for p in (fwd, bwd)]

    bcums = [_dot_m_exact(p["tri"], p["la"][b]) for b, p in streams]
    scaled = []
    for (b, p), bcum in zip(streams, bcums):
        qk = p["qk"][b]
        q, k = qk[:, 0:GLA_KEY_DIM], qk[:, GLA_KEY_DIM:]
        blast = bcum[p["last"]:p["last"] + 1]
        bref = bcum[p["ref"]:p["ref"] + 1]
        qe = (q * jnp.exp(bcum)).astype(BF16)
        qd = (q * jnp.exp(bcum - bref)).astype(BF16)
        kd = k * jnp.exp(bref - bcum)
        kdec = (k * jnp.exp(blast - bcum)).astype(BF16)
        kst = jnp.where(hm_k, jnp.concatenate([kd] * GLA_HEADS, axis=0), 0.0).astype(BF16)
        scaled.append((qe, qd, kdec, kst, jnp.exp(blast)))
    prods = []
    for (b, p), (qe, qd, kdec, kst, _) in zip(streams, scaled):
        v = p["v"][b]
        sc = _dot_nt(qd, kst)
        o_inter = _dot_nt(qe, s_ref[b, p["d"]].astype(BF16))
        kv_t = _dot_tn(v.astype(BF16), kdec)
        prods.append((sc, o_inter, kv_t))
    for (b, p), (_, _, _, _, decay), (sc, o_inter, kv_t) in zip(streams, scaled, prods):
        vbd = jnp.where(hm_v, jnp.concatenate([p["v"][b]] * GLA_HEADS, axis=0), 0.0).astype(BF16)
        p["o"][b] = _dot(jnp.where(p["mask"], sc, 0.0).astype(BF16), vbd) + o_inter
        s_ref[b, p["d"]] = s_ref[b, p["d"]] * decay + jnp.where(hm_s, kv_t, 0.0)

    @pl.when(c == pl.num_programs(0) - 1)
    def _():
        sout_ref[...] = s_ref[...]


def _gla_call(qk, v, la, s0, trif, trib):
    ch = GLA_CHUNK
    if qk.ndim == 4:
        n, b = qk.shape[0], qk.shape[1]
        assert qk.shape[2] == ch
        spec = lambda w, off, rev: pl.BlockSpec(
            (None, b, ch, w), (lambda c: (n - 1 - c, 0, 0, off)) if rev else (lambda c: (c, 0, 0, off)))
    else:
        b = qk.shape[0]
        n = qk.shape[1] // ch
        spec = lambda w, off, rev: pl.BlockSpec(
            (b, ch, w), (lambda c: (0, n - 1 - c, off)) if rev else (lambda c: (0, c, off)))
    kern = functools.partial(_gla_kernel, nb=b)
    o_shape = jax.ShapeDtypeStruct(v.shape, F32)
    return pl.pallas_call(
        kern,
        grid=(n,),
        in_specs=[
            spec(2 * GLA_KEY_DIM, 0, False), spec(GLA_DIM, 0, False), spec(GLA_KEY_DIM, 0, False),
            spec(2 * GLA_KEY_DIM, 0, True), spec(GLA_DIM, 0, True), spec(GLA_KEY_DIM, 1, True),
            _full_spec(s0.shape), _full_spec(trif.shape), _full_spec(trib.shape),
        ],
        out_specs=[spec(GLA_DIM, 0, False), spec(GLA_DIM, 0, True), _full_spec(s0.shape)],
        out_shape=[o_shape, o_shape, jax.ShapeDtypeStruct(s0.shape, F32)],
        scratch_shapes=[pltpu.VMEM(s0.shape, F32)],
        compiler_params=_cparams(("arbitrary",)),
    )(qk, v, la, qk, v, la, s0, trif, trib)


def _stage_e_kernel(x_ref, mod_ref, sgu_ref, yf_ref, yb_ref, s5x_ref, of_ref, ob_ref, g_ref,
                    dskip_ref, wglu_ref, normg_ref, ones_ref, wout_ref, o_ref, *, grid_layout):
    ys = yf_ref[0] + yb_ref[0] + dskip_ref[...] * s5x_ref[0]
    z = _dot(_gelu(ys).astype(BF16), wglu_ref[...])
    s5o = z[:, 0:S5_DIM] * _sigmoid(z[:, S5_DIM:])
    o = _load_tokens(of_ref, grid_layout) + _load_tokens(ob_ref, grid_layout)
    ms = _dot_x_exact(o * o, ones_ref[...]) * (1.0 / GLA_DV)
    g = g_ref[0]
    gl = o * lax.rsqrt(ms + EPS) * normg_ref[...] * (g * _sigmoid(g))
    y = (_dot(sgu_ref[0].astype(BF16), wout_ref[0:SGU_DIM, :])
         + _dot(s5o.astype(BF16), wout_ref[SGU_DIM:SGU_DIM + S5_DIM, :])
         + _dot(gl.astype(BF16), wout_ref[SGU_DIM + S5_DIM:, :]))
    o_ref[0] = x_ref[0] + mod_ref[0, 2:3, :] * y


def _stage_e(xs, mod, sgu, yf_all, yb_all, s5_all, s5_row0, of, ob, g, dskip, wglu, normg, ones_gla, wout, tb):
    b, l, d = xs.shape
    assert s5_row0 % tb == 0
    grid_layout = of.ndim == 4
    tok = lambda w: pl.BlockSpec((1, tb, w), lambda bi, i: (bi, i, 0))
    s5 = pl.BlockSpec((1, tb, S5_DIM), lambda bi, i: (bi, s5_row0 // tb + i, 0))
    if grid_layout:
        assert tb % GRID_W == 0
        gla = pl.BlockSpec((GRID_W, None, tb // GRID_W, GLA_DIM), lambda bi, i: (0, bi, i, 0))
    else:
        gla = tok(GLA_DIM)
    return pl.pallas_call(
        functools.partial(_stage_e_kernel, grid_layout=grid_layout),
        grid=(b, l // tb),
        in_specs=[
            tok(d), pl.BlockSpec((1, 8, d), lambda bi, i: (bi, 0, 0)),
            tok(SGU_DIM), s5, s5, s5, gla, gla, tok(GLA_DIM),
            _full_spec(dskip.shape), _full_spec(wglu.shape), _full_spec(normg.shape),
            _full_spec(ones_gla.shape), _full_spec(wout.shape),
        ],
        out_specs=tok(d),
        out_shape=jax.ShapeDtypeStruct((b, l, d), F32),
        compiler_params=_cparams(("parallel", "parallel")),
    )(xs, mod, sgu, yf_all, yb_all, s5_all, of, ob, g, dskip, wglu, normg, ones_gla, wout)


PEER_TBF = 256
AUX_ROWS = 16
AUX_THETA, AUX_V1_TOP, AUX_V1_LAST, AUX_N_TOP, AUX_RZ = 8, 9, 10, 11, 12


def _sort_network_16():
    def merge(lo, hi, r):
        step = r * 2
        if step < hi - lo:
            yield from merge(lo, hi, step)
            yield from merge(lo + r, hi, step)
            for i in range(lo + r, hi - r, step):
                yield (i, i + r)
        else:
            yield (lo, lo + r)

    def sort(lo, hi):
        if hi - lo >= 1:
            mid = lo + (hi - lo) // 2
            yield from sort(lo, mid)
            yield from sort(mid + 1, hi)
            yield from merge(lo, hi, 1)

    return tuple(sort(0, PEER_TOPK - 1))


SORT16 = _sort_network_16()
BITONIC16 = tuple((k, k + s) for s in (8, 4, 2, 1) for k in range(PEER_TOPK) if not k & s)


def _compare_exchange(xs, pairs):
    xs = list(xs)
    for i, j in pairs:
        hi = jnp.maximum(xs[i], xs[j])
        lo = jnp.minimum(xs[i], xs[j])
        xs[i], xs[j] = hi, lo
    return xs


def _merge_sublanes(xs):
    for shift in (4, 6, 7):
        rolled = [pltpu.roll(x, shift, 0) for x in xs]
        xs = [jnp.maximum(xs[k], rolled[PEER_TOPK - 1 - k]) for k in range(PEER_TOPK)]
        xs = _compare_exchange(xs, BITONIC16)
    return xs


def _dup_bf16_words(x):
    bits = pltpu.bitcast(x.astype(BF16).astype(F32), jnp.int32)
    return bits | lax.shift_right_logical(bits, 16)


def _stage_f_kernel(x_ref, mod_ref, g2_ref, wqt_ref, keys_ref,
                    ht_ref, s1_ref, aux_ref, e2_ref, r2_ref, qt_ref, v1_ref, v2_ref, *, tb):
    x = x_ref[0]
    ms = jnp.mean(x * x, axis=-1, keepdims=True)
    xn = x * lax.rsqrt(ms + EPS) * g2_ref[...]
    h = xn * (1.0 + mod_ref[0, 4:5, :]) + mod_ref[0, 3:4, :]
    ht = h.T.astype(BF16)
    ht_ref[...] = pltpu.bitcast(ht, jnp.int32)
    qt_ref[...] = _dot(wqt_ref[...], ht)
    k_top = PEER_TOPK

    def tiles(s):
        return [s[SUBLANES * k:SUBLANES * (k + 1)] for k in range(PEER_NKEYS // SUBLANES)]

    def head_body(hh, carry):
        for tc in range(tb // LANE):
            tcol = slice(tc * LANE, (tc + 1) * LANE)
            r1 = hh * (2 * PEER_HALF)
            q1 = qt_ref[r1:r1 + PEER_HALF, tcol].astype(BF16)
            q2 = qt_ref[r1 + PEER_HALF:r1 + 2 * PEER_HALF, tcol].astype(BF16)
            s1 = _dot(keys_ref[hh], q1)
            s2 = _dot(keys_ref[PEER_HEADS + hh], q2)
            for s, v_ref in ((s1, v1_ref), (s2, v2_ref)):
                top = _merge_sublanes(_compare_exchange(tiles(s), SORT16))
                for k in range(k_top):
                    v_ref[k:k + 1, tcol] = top[k][0:1]
            v1row = lambda a: v1_ref[a:a + 1, tcol]
            v2row = lambda b: v2_ref[b:b + 1, tcol]
            v1lo = v1_ref[0:SUBLANES, tcol]
            v2lo = v2_ref[0:SUBLANES, tcol]
            cand = [v1lo + v2row(b) for b in range(k_top)]
            tail = [v1row(a) + v2lo for a in range(SUBLANES, k_top)]
            for k in range(SUBLANES, k_top):
                cand[k] = jnp.maximum(cand[k], tail[k_top - 1 - k])
            best = _merge_sublanes(_compare_exchange(cand, BITONIC16))
            theta = best[k_top - 1][0:1]
            cmax = best[0][0:1]
            zsum = jnp.zeros((1, LANE), F32)
            for k in range(k_top):
                zsum = zsum + jnp.exp(best[k][0:1] - cmax)
            rz = 1.0 / zsum
            n_top = jnp.zeros((1, LANE), F32)
            for b in range(k_top):
                n_top = jnp.where(v1row(0) + v2row(b) >= theta, float(b + 1), n_top)
            r2 = jnp.zeros(s2.shape, F32)
            for b in range(k_top):
                r2 = jnp.where(v2row(b) > s2, float(b + 1), r2)
            e2 = jnp.where(s2 >= v2row(k_top - 1), jnp.exp(s2 - v2row(0)), 0.0)
            s1_ref[tc, hh] = s1
            aux_ref[tc, hh, 0:SUBLANES, :] = v2lo
            aux_ref[tc, hh, SUBLANES:, :] = jnp.zeros((AUX_ROWS - SUBLANES, LANE), F32)
            for row, val in ((AUX_THETA, theta), (AUX_V1_TOP, v1row(0)), (AUX_V1_LAST, v1row(k_top - 1)),
                             (AUX_N_TOP, n_top), (AUX_RZ, rz)):
                aux_ref[tc, hh, row:row + 1, :] = val
            e2_ref[tc, hh] = pltpu.bitcast(e2.astype(BF16), jnp.int32)
            r2_ref[tc, hh] = pltpu.bitcast(r2.astype(BF16), jnp.int32)
        return carry

    for hh in range(PEER_HEADS):
        head_body(hh, 0)


def _stage_f(xs, mod, g2, wqt, keys, tb):
    b, l, d = xs.shape
    nblk = l // tb
    ntok = b * l
    nch = ntok // LANE
    kern = functools.partial(_stage_f_kernel, tb=tb)
    row_spec = pl.BlockSpec((tb // LANE, PEER_HEADS, PEER_NKEYS, LANE), lambda bi, i: (bi * nblk + i, 0, 0, 0))
    pair_spec = pl.BlockSpec((tb // LANE, PEER_HEADS, PEER_NKEYS // 2, LANE), lambda bi, i: (bi * nblk + i, 0, 0, 0))
    desc_shape = lambda rows: jax.ShapeDtypeStruct((nch, PEER_HEADS, rows, LANE), jnp.int32)
    return pl.pallas_call(
        kern,
        grid=(b, nblk),
        in_specs=[
            pl.BlockSpec((1, tb, d), lambda bi, i: (bi, i, 0)),
            pl.BlockSpec((1, 8, d), lambda bi, i: (bi, 0, 0)),
            _full_spec(g2.shape), _full_spec(wqt.shape), _full_spec(keys.shape),
        ],
        out_specs=[pl.BlockSpec((d // 2, tb), lambda bi, i: (0, bi * nblk + i)),
                   row_spec,
                   pl.BlockSpec((tb // LANE, PEER_HEADS, AUX_ROWS, LANE), lambda bi, i: (bi * nblk + i, 0, 0, 0)),
                   pair_spec, pair_spec],
        out_shape=[jax.ShapeDtypeStruct((d // 2, ntok), jnp.int32),
                   jax.ShapeDtypeStruct((nch, PEER_HEADS, PEER_NKEYS, LANE), F32),
                   jax.ShapeDtypeStruct((nch, PEER_HEADS, AUX_ROWS, LANE), F32),
                   desc_shape(PEER_NKEYS // 2), desc_shape(PEER_NKEYS // 2)],
        scratch_shapes=[
            pltpu.VMEM((PEER_HEADS * 2 * PEER_HALF, tb), F32),
            pltpu.VMEM((PEER_TOPK, tb), F32),
            pltpu.VMEM((PEER_TOPK, tb), F32),
        ],
        compiler_params=_cparams(("parallel", "parallel")),
    )(xs, mod, g2, wqt, keys)


PEER_TBG = 1024
PEER_TE = 1024
PEER_I1_PER_TILE = PEER_TE // PEER_NKEYS
PEER_N_TILES = PEER_NKEYS * PEER_NKEYS // PEER_TE
PEER_MXU_COLS = 256
PEER_G_FLAGS = None


def _stage_g_kernel(htw_ref, s1_ref, aux_ref, e2_ref, r2_ref, uw_ref, vtw_ref, x_ref, mod_ref, fg_ref,
                    o_ref, acc_ref, ata_ref, atb_ref, p_ref, e1w_ref, n1w_ref, *, tb, final_norm):
    s = pl.program_id(1)

    @pl.when(s == 0)
    def _():
        acc_ref[...] = jnp.zeros_like(acc_ref)

    def step(at_cur_ref, at_next_ref):
        per_grp = PEER_MXU_COLS // LANE
        for grp in range(tb // PEER_MXU_COLS):
            cols = slice(grp * PEER_MXU_COLS, (grp + 1) * PEER_MXU_COLS)
            for tcl in range(per_grp if at_cur_ref is not None else 0):
                tc = grp * per_grp + tcl
                tcol = slice(tc * LANE, (tc + 1) * LANE)
                for i1l in range(PEER_I1_PER_TILE):
                    rows = slice(i1l * PEER_NKEYS, (i1l + 1) * PEER_NKEYS)
                    gate = jnp.zeros((PEER_NKEYS, LANE), BF16)
                    for hh in range(PEER_HEADS):
                        e1row = e1w_ref[tc, hh, i1l:i1l + 1, :]
                        n1row = n1w_ref[tc, hh, i1l:i1l + 1, :]
                        e1 = pltpu.bitcast(jnp.broadcast_to(e1row, (PEER_NKEYS // 2, LANE)), BF16)
                        n1 = pltpu.bitcast(jnp.broadcast_to(n1row, (PEER_NKEYS // 2, LANE)), BF16)
                        r2 = pltpu.bitcast(r2_ref[tc, hh], BF16)
                        e2 = pltpu.bitcast(e2_ref[tc, hh], BF16)
                        gate = gate + e2 * jnp.where(r2 < n1, e1, 0.0)
                    p_ref[rows, tcol] = gate * _gelu(at_cur_ref[rows, tcol])
            if at_cur_ref is not None:
                vt = pltpu.bitcast(vtw_ref[...], BF16)
                acc_ref[:, cols] += _dot(vt, p_ref[:, cols])
            if at_next_ref is not None:
                u = pltpu.bitcast(uw_ref[...], BF16)
                ht = pltpu.bitcast(htw_ref[:, cols], BF16)
                at_next_ref[:, cols] = _dot(u, ht).astype(BF16)
        if at_next_ref is not None:
            for tc in range(tb // LANE):
                for hh in range(PEER_HEADS):
                    s1 = s1_ref[tc, hh]
                    aux = lambda r: aux_ref[tc, hh, r:r + 1, :]
                    n1 = jnp.zeros(s1.shape, F32)
                    for b in range(SUBLANES):
                        n1 = jnp.where(s1 + aux(b) >= aux(AUX_THETA), float(b + 1), n1)
                    n1 = jnp.where(s1 >= aux(AUX_V1_TOP), aux(AUX_N_TOP), n1)
                    e1 = jnp.where(s1 >= aux(AUX_V1_LAST), jnp.exp(s1 - aux(AUX_V1_TOP)), 0.0) * aux(AUX_RZ)
                    e1w_ref[tc, hh] = _dup_bf16_words(e1)
                    n1w_ref[tc, hh] = _dup_bf16_words(n1)

    last = pl.num_programs(1) - 1

    @pl.when(s == 0)
    def _():
        step(None, ata_ref)

    @pl.when(jnp.logical_and(s % 2 == 0, jnp.logical_and(s > 0, s < last)))
    def _():
        step(atb_ref, ata_ref)

    @pl.when(s % 2 == 1)
    def _():
        step(ata_ref, atb_ref)

    @pl.when(s == last)
    def _():
        step(atb_ref, None)
        xo = x_ref[...] + mod_ref[0, 5:6, :] * acc_ref[...].T
        if final_norm:
            ms = jnp.mean(xo * xo, axis=-1, keepdims=True)
            xo = xo * lax.rsqrt(ms + EPS) * fg_ref[...]
        o_ref[...] = xo


def _stage_g(htw, s1, aux, e2, r2, uw, vtw, xflat, mod, final_g, tokens_per_batch, tb, final_norm):
    ntok = htw.shape[1]
    d = 2 * htw.shape[0]
    assert 2 * uw.shape[0] == PEER_N_TILES * PEER_TE and PEER_N_TILES % 2 == 0
    blocks_per_batch = tokens_per_batch // tb
    kern = functools.partial(_stage_g_kernel, tb=tb, final_norm=final_norm)
    last = PEER_N_TILES - 1
    desc_spec = pl.BlockSpec((tb // LANE, PEER_HEADS, PEER_NKEYS // 2, LANE), lambda j, i: (j, 0, 0, 0))
    row_spec = pl.BlockSpec((tb // LANE, PEER_HEADS, PEER_I1_PER_TILE, LANE),
                            lambda j, i: (j, 0, jnp.minimum(i, last), 0))
    return pl.pallas_call(
        kern,
        grid=(ntok // tb, PEER_N_TILES + 1),
        in_specs=[
            pl.BlockSpec((d // 2, tb), lambda j, i: (0, j)),
            row_spec,
            pl.BlockSpec((tb // LANE, PEER_HEADS, AUX_ROWS, LANE), lambda j, i: (j, 0, 0, 0)),
            desc_spec, desc_spec,
            pl.BlockSpec((PEER_TE // 2, d), lambda j, i: (jnp.minimum(i, last), 0)),
            pl.BlockSpec((d // 2, PEER_TE), lambda j, i: (0, jnp.maximum(i - 1, 0))),
            pl.BlockSpec((tb, d), lambda j, i: (j, 0)),
            pl.BlockSpec((1, 8, d), lambda j, i: (j // blocks_per_batch, 0, 0)),
            _full_spec(final_g.shape),
        ],
        out_specs=pl.BlockSpec((tb, d), lambda j, i: (j, 0)),
        out_shape=jax.ShapeDtypeStruct((ntok, d), F32),
        scratch_shapes=[
            pltpu.VMEM((d, tb), F32),
            pltpu.VMEM((PEER_TE, tb), BF16),
            pltpu.VMEM((PEER_TE, tb), BF16),
            pltpu.VMEM((PEER_TE, tb), BF16),
            pltpu.VMEM((tb // LANE, PEER_HEADS, PEER_I1_PER_TILE, LANE), jnp.int32),
            pltpu.VMEM((tb // LANE, PEER_HEADS, PEER_I1_PER_TILE, LANE), jnp.int32),
        ],
        compiler_params=_cparams(("parallel", "arbitrary"), PEER_G_FLAGS),
    )(htw, s1, aux, e2, r2, uw, vtw, xflat, mod, final_g)


def _pack_kernel(x_ref, o_ref, *, transpose):
    x = x_ref[...]
    if transpose:
        x = x.T
    o_ref[...] = pltpu.bitcast(x.astype(BF16), jnp.int32)


def _pack_row_pairs(x, layer, transpose=False, tile=1024):
    _, r, c = x.shape
    if transpose:
        out_shape, out_spec = (c // 2, r), pl.BlockSpec((c // 2, tile), lambda i: (0, i))
    else:
        out_shape, out_spec = (r // 2, c), pl.BlockSpec((tile // 2, c), lambda i: (i, 0))
    return pl.pallas_call(
        functools.partial(_pack_kernel, transpose=transpose),
        grid=(r // tile,),
        in_specs=[pl.BlockSpec((None, tile, c), lambda i: (layer, i, 0))],
        out_specs=out_spec,
        out_shape=jax.ShapeDtypeStruct(out_shape, jnp.int32),
        compiler_params=_cparams(("parallel",)),
    )(x)


def _block_ones(n, blk):
    idx = np.arange(n) // blk
    return jnp.asarray((idx[:, None] == idx[None, :]).astype(np.float32), dtype=BF16)


def _s5_discretise(lam_re, lam_im, b_re, b_im, log_step):
    lam_re = jnp.minimum(lam_re.astype(F32), -1e-4)
    lam_im = lam_im.astype(F32)
    dt = jnp.exp(log_step.astype(F32))[:, None]
    mag = jnp.exp(lam_re * dt)
    a_re = mag * jnp.cos(lam_im * dt)
    a_im = mag * jnp.sin(lam_im * dt)
    den = lam_re * lam_re + lam_im * lam_im
    f_re = ((a_re - 1.0) * lam_re + a_im * lam_im) / den
    f_im = (a_im * lam_re - (a_re - 1.0) * lam_im) / den
    b_re = b_re.astype(F32)
    b_im = b_im.astype(F32)
    bb_re = f_re[..., None] * b_re - f_im[..., None] * b_im
    bb_im = f_re[..., None] * b_im + f_im[..., None] * b_re
    return a_re, a_im, bb_re, bb_im


def _group_block_diag(t):
    g, r, c = t.shape
    eye = jnp.eye(g, dtype=t.dtype)
    return (t[:, :, None, :] * eye[:, None, :, None]).reshape(g * r, g * c)


def _s5_params(lam_re, lam_im, b_re, b_im, c_re, c_im, log_step, nseq):
    b_rows, c_cols, ars, ais = [], [], [], []
    for d in range(2):
        a_re, a_im, bb_re, bb_im = _s5_discretise(lam_re[d], lam_im[d], b_re[d], b_im[d], log_step[d])
        bm = jnp.concatenate([_group_block_diag(jnp.swapaxes(bb_re, 1, 2)),
                              _group_block_diag(jnp.swapaxes(bb_im, 1, 2))], axis=1)
        b_rows.append(bm)
        cm = jnp.concatenate([_group_block_diag(jnp.swapaxes(c_re[d].astype(F32), 1, 2)),
                              -_group_block_diag(jnp.swapaxes(c_im[d].astype(F32), 1, 2))], axis=0)
        c_cols.append(cm)
        ars.append(jnp.broadcast_to(a_re.reshape(1, S5_LANES), (nseq, S5_LANES)))
        ais.append(jnp.broadcast_to(a_im.reshape(1, S5_LANES), (nseq, S5_LANES)))
    b2 = jnp.concatenate(b_rows, axis=0).astype(BF16)
    c2 = jnp.concatenate(c_cols, axis=1).astype(BF16)
    return b2, jnp.concatenate(ars, axis=0), jnp.concatenate(ais, axis=0), c2


def kernel(x, c, ctx, c_ctx, w_mod, b_mod, norm1_g, norm2_g, w_in, w_out, sgu_w, sgu_b, s5_lambda_re, s5_lambda_im, s5_b_re, s5_b_im, s5_c_re, s5_c_im, s5_log_step, s5_d, s5_w_glu, gla_w_gate, gla_b_gate, gla_norm_g, peer_w_query, peer_sub_keys, peer_expert_u, peer_expert_v, final_norm_g):
    nb, seq, d = x.shape
    c_len = ctx.shape[1]
    depth = w_mod.shape[0]

    cc = jnp.concatenate([c, c_ctx[None, :], jnp.zeros((8 - nb - 1, d), F32)], axis=0)
    mods = _mod_call(cc, w_mod, b_mod)

    ones_sgu = _block_ones(SGU_DIM, SGU_HEAD_DIM)
    ones_gla = _block_ones(GLA_DIM, GLA_DV)
    tri_np = np.tril(np.ones((GLA_CHUNK, GLA_CHUNK), np.float32))
    trif = jnp.asarray(tri_np, dtype=BF16)
    trib = jnp.asarray(tri_np.T, dtype=BF16)
    s_zero = jnp.zeros((nb, 2, GLA_DIM, GLA_KEY_DIM), F32)
    final_g = final_norm_g.reshape(1, d)

    xl, xc = x, ctx
    for l in range(depth):
        ctx_out = l < depth - 1
        m6 = mods[l].reshape(8, N_MOD, d)
        mod_l = jnp.pad(m6[:nb], ((0, 0), (0, 2), (0, 0)))
        mod_c = jnp.broadcast_to(jnp.pad(m6[nb], ((0, 2), (0, 0)))[None], (nb, 8, d))

        win = jnp.pad(w_in[l], ((0, 0), (0, IN_PAD - IN_WIDTH))).astype(BF16)
        sguw = sgu_w[l].astype(BF16)
        sgub = jnp.repeat(jnp.swapaxes(sgu_b[l], 0, 1), SGU_HEAD_DIM, axis=1)
        gw = jnp.zeros((LANE, 2 * GLA_KEY_DIM), F32)
        gw = gw.at[0:GLA_RANK, 0:GLA_KEY_DIM].set(gla_w_gate[l, 0])
        gw = gw.at[GLA_RANK:2 * GLA_RANK, GLA_KEY_DIM:].set(gla_w_gate[l, 1]).astype(BF16)
        gb = gla_b_gate[l].reshape(1, 2 * GLA_KEY_DIM)
        g1 = norm1_g[l].reshape(1, d)

        s5_all = jnp.zeros((nb, seq + c_len, S5_DIM), F32)
        sgu_l, s5_all, qk_l, v_l, g_l, la_l = _stage_a(xl, mod_l, g1, win, sguw, sgub, ones_sgu, gw, gb,
                                                       s5_all, 0, tb=TB_LATENT, grid_layout=True)
        sgu_c, s5_all, qk_c, v_c, g_c, la_c = _stage_a(xc, mod_c, g1, win, sguw, sgub, ones_sgu, gw, gb,
                                                       s5_all, seq, tb=TB_CTX, grid_layout=False)

        b2, ar, ai, c2 = _s5_params(s5_lambda_re[l], s5_lambda_im[l], s5_b_re[l], s5_b_im[l],
                                    s5_c_re[l], s5_c_im[l], s5_log_step[l], nb)
        yf_all, yb_all = _s5_call(s5_all, seq, b2, ar, ai, c2)

        of_c, ob_c, s_ctx = _gla_call(qk_c, v_c, la_c, s_zero, trif, trib)
        of_l, ob_l, _ = _gla_call(qk_l, v_l, la_l, s_ctx, trif, trib)

        dskip = s5_d[l].reshape(1, S5_DIM)
        wglu = s5_w_glu[l].astype(BF16)
        normg = gla_norm_g[l].reshape(1, GLA_DIM)
        wout = w_out[l].astype(BF16)
        g2 = norm2_g[l].reshape(1, d)
        wqt = jnp.swapaxes(peer_w_query[l], 0, 1).astype(BF16)
        keys = peer_sub_keys[l].reshape(2 * PEER_HEADS, PEER_NKEYS, PEER_HALF).astype(BF16)
        u_bf = _pack_row_pairs(peer_expert_u, l)
        vt_bf = _pack_row_pairs(peer_expert_v, l, transpose=True)

        xl = _stage_e(xl, mod_l, sgu_l, yf_all, yb_all, s5_all, 0, of_l, ob_l, g_l,
                      dskip, wglu, normg, ones_gla, wout, tb=TB_LATENT)
        desc = _stage_f(xl, mod_l, g2, wqt, keys, tb=PEER_TBF)
        xl = _stage_g(*desc, u_bf, vt_bf, xl.reshape(nb * seq, d), mod_l, final_g, seq, PEER_TBG,
                      final_norm=not ctx_out).reshape(nb, seq, d)

        if ctx_out:
            xc = _stage_e(xc, mod_c, sgu_c, yf_all, yb_all, s5_all, seq, of_c, ob_c, g_c,
                          dskip, wglu, normg, ones_gla, wout, tb=TB_CTX)
            desc = _stage_f(xc, mod_c, g2, wqt, keys, tb=PEER_TBF)
            xc = _stage_g(*desc, u_bf, vt_bf, xc.reshape(nb * c_len, d), mod_c, final_g, c_len,
                          min(PEER_TBG, c_len), final_norm=False).reshape(nb, c_len, d)

    return xl
```

```python
import functools
import math

import numpy as np
import jax
import jax.numpy as jnp
from jax import lax
from jax.experimental import pallas as pl
from jax.experimental.pallas import tpu as pltpu

F32 = jnp.float32
BF16 = jnp.bfloat16
SUBLANES = 8

EPS = 1e-6
N_MOD = 6
GRID_W = 64

SGU_DIM = 256
SGU_HEADS = 4
SGU_HEAD_DIM = 64
SGU_CHUNK = 128

S5_DIM = 256
S5_GROUP = 16
S5_GROUPS = 16
S5_STATE = 64
S5_LANES = S5_GROUPS * S5_STATE

GLA_DIM = 512
GLA_HEADS = 8
GLA_DV = 64
GLA_DK = 32
GLA_KEY_DIM = 256
GLA_RANK = 16
GLA_GATE_TEMP = 16.0
GLA_CHUNK = 64

PEER_HEADS = 8
PEER_NKEYS = 128
PEER_HALF = 128
PEER_TOPK = 16

IN_WIDTH = 2336
IN_PAD = 2432
LANE = 128

VMEM_LIMIT = 56 * 1024 * 1024
TB_LATENT = 1024
TB_CTX = 256

NEG_INF = float("-inf")
POS_INF = float("inf")


def _cparams(sem):
    return pltpu.CompilerParams(dimension_semantics=sem, vmem_limit_bytes=VMEM_LIMIT)


def _gelu(x):
    c = math.sqrt(2.0 / math.pi)
    return 0.5 * x * (1.0 + jnp.tanh(c * (x + 0.044715 * (x * x * x))))


def _sigmoid(x):
    return 1.0 / (1.0 + jnp.exp(-x))


def _dot(a, b):
    return jnp.dot(a, b, preferred_element_type=F32)


def _dot_nt(a, b):
    return lax.dot_general(a, b, (((1,), (1,)), ((), ())), preferred_element_type=F32)


def _dot_tn(a, b):
    return lax.dot_general(a, b, (((0,), (0,)), ((), ())), preferred_element_type=F32)


def _split3(x):
    hi = x.astype(BF16)
    r = x - hi.astype(F32)
    mid = r.astype(BF16)
    lo = (r - mid.astype(F32)).astype(BF16)
    return hi, mid, lo


def _dot_x_exact(x, m):
    hi, mid, lo = _split3(x)
    return _dot(hi, m) + _dot(mid, m) + _dot(lo, m)


def _dot_m_exact(m, x):
    hi, mid, lo = _split3(x)
    return _dot(m, hi) + _dot(m, mid) + _dot(m, lo)


def _full_spec(shape):
    nd = len(shape)
    return pl.BlockSpec(shape, lambda *_: (0,) * nd)


MOD_TILE = 512


def _mod_kernel(c_ref, w_ref, b_ref, o_ref):
    c = c_ref[...]
    a = c * _sigmoid(c)
    o_ref[0] = jnp.dot(a, w_ref[0], preferred_element_type=F32,
                       precision=lax.Precision.HIGHEST) + b_ref[0]


def _mod_call(cc, w_mod, b_mod):
    depth, d, nd = w_mod.shape
    rows = cc.shape[0]
    return pl.pallas_call(
        _mod_kernel,
        grid=(depth, nd // MOD_TILE),
        in_specs=[
            pl.BlockSpec((rows, d), lambda l, j: (0, 0)),
            pl.BlockSpec((1, d, MOD_TILE), lambda l, j: (l, 0, j)),
            pl.BlockSpec((1, 1, MOD_TILE), lambda l, j: (l, 0, j)),
        ],
        out_specs=pl.BlockSpec((1, rows, MOD_TILE), lambda l, j: (l, 0, j)),
        out_shape=jax.ShapeDtypeStruct((depth, rows, nd), F32),
        compiler_params=_cparams(("parallel", "parallel")),
    )(cc, w_mod, b_mod.reshape(depth, 1, nd))


def _store_tokens(ref, val, grid_layout):
    if grid_layout:
        for r in range(val.shape[0] // GRID_W):
            ref[:, r, :] = val[r * GRID_W:(r + 1) * GRID_W]
    else:
        ref[0] = val


def _load_tokens(ref, grid_layout):
    if grid_layout:
        return jnp.concatenate([ref[:, r, :] for r in range(ref.shape[1])], axis=0)
    return ref[0]


def _stage_a_kernel(x_ref, mod_ref, g1_ref, win_ref, sguw_ref, sgub_ref, ones_ref, gw_ref, gb_ref, s5_in_ref,
                    sgu_ref, s5x_ref, qk_ref, v_ref, g_ref, la_ref, *, tb, grid_layout):
    del s5_in_ref
    x = x_ref[0]
    ms = jnp.mean(x * x, axis=-1, keepdims=True)
    xn = x * lax.rsqrt(ms + EPS) * g1_ref[...]
    h = xn * (1.0 + mod_ref[0, 1:2, :]) + mod_ref[0, 0:1, :]
    cols = _dot(h.astype(BF16), win_ref[...])

    u = _gelu(cols[:, 0:SGU_DIM])
    v = _gelu(cols[:, SGU_DIM:2 * SGU_DIM])
    msq = _dot_x_exact(v * v, ones_ref[...]) * (1.0 / SGU_HEAD_DIM)
    vn = (v * lax.rsqrt(msq + EPS)).astype(BF16)
    head_of_lane = lax.broadcasted_iota(jnp.int32, (SGU_CHUNK, SGU_DIM), 1) // SGU_HEAD_DIM
    for ci in range(tb // SGU_CHUNK):
        rows = slice(ci * SGU_CHUNK, (ci + 1) * SGU_CHUNK)
        vc = vn[rows]
        mixed = sgub_ref[...]
        for hh in range(SGU_HEADS):
            mixed = mixed + jnp.where(head_of_lane == hh, _dot(sguw_ref[hh], vc), 0.0)
        sgu_ref[0, rows, :] = u[rows] * mixed

    s5x_ref[0] = cols[:, 512:768]
    q = cols[:, 768:1024] * (GLA_DK ** -0.5)
    _store_tokens(qk_ref, jnp.concatenate([q, cols[:, 1024:1280]], axis=-1), grid_layout)
    _store_tokens(v_ref, cols[:, 1280:1792], grid_layout)
    g_ref[0] = cols[:, 1792:2304]

    z = cols[:, 2304:2432].astype(BF16)
    za = _dot(z, gw_ref[...]) + gb_ref[...]
    log_sig = jnp.minimum(za, 0.0) - jnp.log1p(jnp.exp(-jnp.abs(za)))
    _store_tokens(la_ref, log_sig * (1.0 / GLA_GATE_TEMP), grid_layout)


def _stage_a(xs, mod, g1, win, sguw, sgub, ones_sgu, gw, gb, s5_all, s5_row0, tb, grid_layout):
    b, l, d = xs.shape
    assert s5_row0 % tb == 0
    kern = functools.partial(_stage_a_kernel, tb=tb, grid_layout=grid_layout)
    tok = lambda w: pl.BlockSpec((1, tb, w), lambda bi, i: (bi, i, 0))
    if grid_layout:
        assert tb % GRID_W == 0 and l % GRID_W == 0
        gla = lambda w: pl.BlockSpec((GRID_W, None, tb // GRID_W, w), lambda bi, i: (0, bi, i, 0))
        gla_shape = lambda w: jax.ShapeDtypeStruct((GRID_W, b, l // GRID_W, w), F32)
    else:
        gla = tok
        gla_shape = lambda w: jax.ShapeDtypeStruct((b, l, w), F32)
    nat_shape = lambda w: jax.ShapeDtypeStruct((b, l, w), F32)
    return pl.pallas_call(
        kern,
        grid=(b, l // tb),
        in_specs=[
            tok(d),
            pl.BlockSpec((1, 8, d), lambda bi, i: (bi, 0, 0)),
            _full_spec(g1.shape), _full_spec(win.shape), _full_spec(sguw.shape), _full_spec(sgub.shape),
            _full_spec(ones_sgu.shape), _full_spec(gw.shape), _full_spec(gb.shape),
            pl.BlockSpec(memory_space=pl.ANY),
        ],
        out_specs=[tok(SGU_DIM),
                   pl.BlockSpec((1, tb, S5_DIM), lambda bi, i: (bi, s5_row0 // tb + i, 0)),
                   gla(2 * GLA_KEY_DIM), gla(GLA_DIM), tok(GLA_DIM), gla(2 * GLA_KEY_DIM)],
        out_shape=[nat_shape(SGU_DIM), jax.ShapeDtypeStruct(s5_all.shape, F32),
                   gla_shape(2 * GLA_KEY_DIM), gla_shape(GLA_DIM), nat_shape(GLA_DIM), gla_shape(2 * GLA_KEY_DIM)],
        input_output_aliases={9: 1},
        compiler_params=_cparams(("parallel", "parallel")),
    )(xs, mod, g1, win, sguw, sgub, ones_sgu, gw, gb, s5_all)


S5_TC = 128


def _s5_kernel(uf_ref, ub_ref, perm_ref, permt_ref, b2_ref, ar_ref, ai_ref, c2_ref, yf_ref, yb_ref,
               h_ref, buf0_ref, buf1_ref, buf2_ref, *, nseq):
    tc = S5_TC
    rows = tc * 2 * nseq
    s = pl.program_id(0)

    @pl.when(s == 0)
    def _():
        h_ref[...] = jnp.zeros_like(h_ref)
        buf0_ref[...] = jnp.zeros_like(buf0_ref)
        buf1_ref[...] = jnp.zeros_like(buf1_ref)
        buf2_ref[...] = jnp.zeros_like(buf2_ref)

    def phase(scan_ref, fill_ref, read_ref):
        ar = ar_ref[...]
        ai = ai_ref[...]
        hr, hi = h_ref[:, 0:S5_LANES], h_ref[:, S5_LANES:]
        for t in range(tc):
            bur = scan_ref[t, :, 0:S5_LANES]
            bui = scan_ref[t, :, S5_LANES:]
            hr, hi = ar * hr - ai * hi + bur, ar * hi + ai * hr + bui
            scan_ref[t, :, 0:S5_LANES] = hr
            scan_ref[t, :, S5_LANES:] = hi
        h_ref[:, 0:S5_LANES] = hr
        h_ref[:, S5_LANES:] = hi

        fwd_row = lax.broadcasted_iota(jnp.int32, (rows, S5_DIM), 0) % (2 * nseq) < nseq
        hs = read_ref[...].reshape(rows, 2 * S5_LANES).astype(BF16)
        y2 = _dot(hs, c2_ref[...])
        y = jnp.where(fwd_row, y2[:, 0:S5_DIM], y2[:, S5_DIM:])
        y_nat = _dot(permt_ref[...], y.astype(BF16))
        for b in range(nseq):
            yf_ref[b] = y_nat[b * tc:(b + 1) * tc]
            yb_ref[b] = y_nat[(nseq + b) * tc:(nseq + b + 1) * tc]

        x = jnp.concatenate([uf_ref[b] for b in range(nseq)] + [ub_ref[b] for b in range(nseq)], axis=0)
        u_tm = _dot(perm_ref[...], x.astype(BF16))
        u = jnp.concatenate([jnp.where(fwd_row, u_tm, 0.0), jnp.where(fwd_row, 0.0, u_tm)], axis=-1).astype(BF16)
        fill_ref[...] = _dot(u, b2_ref[...]).reshape(tc, 2 * nseq, 2 * S5_LANES)

    bufs = (buf0_ref, buf1_ref, buf2_ref)
    for k in range(3):
        @pl.when(s % 3 == k)
        def _(k=k):
            phase(bufs[(k + 2) % 3], bufs[k], bufs[(k + 1) % 3])


def _s5_permutation(nseq):
    tc = S5_TC
    p = np.zeros((tc * 2 * nseq, tc * 2 * nseq), np.float32)
    for q in range(2 * nseq):
        for t in range(tc):
            p[t * 2 * nseq + q, q * tc + (t if q < nseq else tc - 1 - t)] = 1.0
    return p


def _s5_call(s5_all, n_lat, b2, ar, ai, c2):
    nseq, t, _ = s5_all.shape
    n = t // S5_TC
    n_l = n_lat // S5_TC
    n_c = n - n_l
    perm = _s5_permutation(nseq)
    kern = functools.partial(_s5_kernel, nseq=nseq)
    fwd_blk = lambda k: jnp.where(k < n_c, n_l + k, k - n_c)
    bwd_blk = lambda k: n - 1 - k
    in_k = lambda s: jnp.minimum(s, n - 1)
    out_k = lambda s: jnp.clip(s - 2, 0, n - 1)
    blk = (nseq, S5_TC, S5_DIM)
    rows = 2 * nseq
    buf = pltpu.VMEM((S5_TC, rows, 2 * S5_LANES), F32)
    return pl.pallas_call(
        kern,
        grid=(n + 2,),
        in_specs=[
            pl.BlockSpec(blk, lambda s: (0, fwd_blk(in_k(s)), 0)), pl.BlockSpec(blk, lambda s: (0, bwd_blk(in_k(s)), 0)),
            _full_spec(perm.shape), _full_spec(perm.shape),
            _full_spec(b2.shape), _full_spec(ar.shape), _full_spec(ai.shape), _full_spec(c2.shape),
        ],
        out_specs=[pl.BlockSpec(blk, lambda s: (0, fwd_blk(out_k(s)), 0)),
                   pl.BlockSpec(blk, lambda s: (0, bwd_blk(out_k(s)), 0))],
        out_shape=[jax.ShapeDtypeStruct(s5_all.shape, F32), jax.ShapeDtypeStruct(s5_all.shape, F32)],
        scratch_shapes=[pltpu.VMEM((rows, 2 * S5_LANES), F32), buf, buf, buf],
        compiler_params=_cparams(("arbitrary",)),
    )(s5_all, s5_all, jnp.asarray(perm, dtype=BF16), jnp.asarray(perm.T, dtype=BF16), b2, ar, ai, c2)


def _gla_kernel(qkf_ref, vf_ref, laf_ref, qkb_ref, vb_ref, lab_ref, s0_ref, trif_ref, trib_ref,
                of_ref, ob_ref, sout_ref, s_ref, *, nb):
    c = pl.program_id(0)

    @pl.when(c == 0)
    def _():
        s_ref[...] = s0_ref[...]

    ch = GLA_CHUNK
    r_k = lax.broadcasted_iota(jnp.int32, (GLA_HEADS * ch, GLA_KEY_DIM), 0) // ch
    c_k = lax.broadcasted_iota(jnp.int32, (GLA_HEADS * ch, GLA_KEY_DIM), 1) // GLA_DK
    hm_k = r_k == c_k
    r_v = lax.broadcasted_iota(jnp.int32, (GLA_HEADS * ch, GLA_DIM), 0) // ch
    c_v = lax.broadcasted_iota(jnp.int32, (GLA_HEADS * ch, GLA_DIM), 1) // GLA_DV
    hm_v = r_v == c_v
    r_s = lax.broadcasted_iota(jnp.int32, (GLA_DIM, GLA_KEY_DIM), 0) // GLA_DV
    c_s = lax.broadcasted_iota(jnp.int32, (GLA_DIM, GLA_KEY_DIM), 1) // GLA_DK
    hm_s = r_s == c_s
    t_i = lax.broadcasted_iota(jnp.int32, (ch, GLA_HEADS * ch), 0)
    s_i = lax.broadcasted_iota(jnp.int32, (ch, GLA_HEADS * ch), 1) % ch
    mask_f = t_i >= s_i
    mask_b = t_i <= s_i
    trif = trif_ref[...]
    trib = trib_ref[...]

    fwd = dict(qk=qkf_ref, v=vf_ref, la=laf_ref, o=of_ref, tri=trif, last=ch - 1, ref=ch // 2, mask=mask_f, d=0)
    bwd = dict(qk=qkb_ref, v=vb_ref, la=lab_ref, o=ob_ref, tri=trib, last=0, ref=ch - 1 - ch // 2, mask=mask_b, d=1)
    streams = [(b, p) for b in range(nb) for p in (fwd, bwd)]

    bcums = [_dot_m_exact(p["tri"], p["la"][b]) for b, p in streams]
    scaled = []
    for (b, p), bcum in zip(streams, bcums):
        qk = p["qk"][b]
        q, k = qk[:, 0:GLA_KEY_DIM], qk[:, GLA_KEY_DIM:]
        blast = bcum[p["last"]:p["last"] + 1]
        bref = bcum[p["ref"]:p["ref"] + 1]
        qe = (q * jnp.exp(bcum)).astype(BF16)
        qd = (q * jnp.exp(bcum - bref)).astype(BF16)
        kd = k * jnp.exp(bref - bcum)
        kdec = (k * jnp.exp(blast - bcum)).astype(BF16)
        kst = jnp.where(hm_k, jnp.concatenate([kd] * GLA_HEADS, axis=0), 0.0).astype(BF16)
        scaled.append((qe, qd, kdec, kst, jnp.exp(blast)))
    prods = []
    for (b, p), (qe, qd, kdec, kst, _) in zip(streams, scaled):
        v = p["v"][b]
        sc = _dot_nt(qd, kst)
        o_inter = _dot_nt(qe, s_ref[b, p["d"]].astype(BF16))
        kv_t = _dot_tn(v.astype(BF16), kdec)
        prods.append((sc, o_inter, kv_t))
    for (b, p), (_, _, _, _, decay), (sc, o_inter, kv_t) in zip(streams, scaled, prods):
        vbd = jnp.where(hm_v, jnp.concatenate([p["v"][b]] * GLA_HEADS, axis=0), 0.0).astype(BF16)
        p["o"][b] = _dot(jnp.where(p["mask"], sc, 0.0).astype(BF16), vbd) + o_inter
        s_ref[b, p["d"]] = s_ref[b, p["d"]] * decay + jnp.where(hm_s, kv_t, 0.0)

    @pl.when(c == pl.num_programs(0) - 1)
    def _():
        sout_ref[...] = s_ref[...]


def _gla_call(qk, v, la, s0, trif, trib):
    ch = GLA_CHUNK
    if qk.ndim == 4:
        n, b = qk.shape[0], qk.shape[1]
        assert qk.shape[2] == ch
        spec = lambda w, off, rev: pl.BlockSpec(
            (None, b, ch, w), (lambda c: (n - 1 - c, 0, 0, off)) if rev else (lambda c: (c, 0, 0, off)))
    else:
        b = qk.shape[0]
        n = qk.shape[1] // ch
        spec = lambda w, off, rev: pl.BlockSpec(
            (b, ch, w), (lambda c: (0, n - 1 - c, off)) if rev else (lambda c: (0, c, off)))
    kern = functools.partial(_gla_kernel, nb=b)
    o_shape = jax.ShapeDtypeStruct(v.shape, F32)
    return pl.pallas_call(
        kern,
        grid=(n,),
        in_specs=[
            spec(2 * GLA_KEY_DIM, 0, False), spec(GLA_DIM, 0, False), spec(GLA_KEY_DIM, 0, False),
            spec(2 * GLA_KEY_DIM, 0, True), spec(GLA_DIM, 0, True), spec(GLA_KEY_DIM, 1, True),
            _full_spec(s0.shape), _full_spec(trif.shape), _full_spec(trib.shape),
        ],
        out_specs=[spec(GLA_DIM, 0, False), spec(GLA_DIM, 0, True), _full_spec(s0.shape)],
        out_shape=[o_shape, o_shape, jax.ShapeDtypeStruct(s0.shape, F32)],
        scratch_shapes=[pltpu.VMEM(s0.shape, F32)],
        compiler_params=_cparams(("arbitrary",)),
    )(qk, v, la, qk, v, la, s0, trif, trib)


def _stage_e_kernel(x_ref, mod_ref, sgu_ref, yf_ref, yb_ref, s5x_ref, of_ref, ob_ref, g_ref,
                    dskip_ref, wglu_ref, normg_ref, ones_ref, wout_ref, o_ref, *, grid_layout):
    ys = yf_ref[0] + yb_ref[0] + dskip_ref[...] * s5x_ref[0]
    z = _dot(_gelu(ys).astype(BF16), wglu_ref[...])
    s5o = z[:, 0:S5_DIM] * _sigmoid(z[:, S5_DIM:])
    o = _load_tokens(of_ref, grid_layout) + _load_tokens(ob_ref, grid_layout)
    ms = _dot_x_exact(o * o, ones_ref[...]) * (1.0 / GLA_DV)
    g = g_ref[0]
    gl = o * lax.rsqrt(ms + EPS) * normg_ref[...] * (g * _sigmoid(g))
    y = (_dot(sgu_ref[0].astype(BF16), wout_ref[0:SGU_DIM, :])
         + _dot(s5o.astype(BF16), wout_ref[SGU_DIM:SGU_DIM + S5_DIM, :])
         + _dot(gl.astype(BF16), wout_ref[SGU_DIM + S5_DIM:, :]))
    o_ref[0] = x_ref[0] + mod_ref[0, 2:3, :] * y


def _stage_e(xs, mod, sgu, yf_all, yb_all, s5_all, s5_row0, of, ob, g, dskip, wglu, normg, ones_gla, wout, tb):
    b, l, d = xs.shape
    assert s5_row0 % tb == 0
    grid_layout = of.ndim == 4
    tok = lambda w: pl.BlockSpec((1, tb, w), lambda bi, i: (bi, i, 0))
    s5 = pl.BlockSpec((1, tb, S5_DIM), lambda bi, i: (bi, s5_row0 // tb + i, 0))
    if grid_layout:
        assert tb % GRID_W == 0
        gla = pl.BlockSpec((GRID_W, None, tb // GRID_W, GLA_DIM), lambda bi, i: (0, bi, i, 0))
    else:
        gla = tok(GLA_DIM)
    return pl.pallas_call(
        functools.partial(_stage_e_kernel, grid_layout=grid_layout),
        grid=(b, l // tb),
        in_specs=[
            tok(d), pl.BlockSpec((1, 8, d), lambda bi, i: (bi, 0, 0)),
            tok(SGU_DIM), s5, s5, s5, gla, gla, tok(GLA_DIM),
            _full_spec(dskip.shape), _full_spec(wglu.shape), _full_spec(normg.shape),
            _full_spec(ones_gla.shape), _full_spec(wout.shape),
        ],
        out_specs=tok(d),
        out_shape=jax.ShapeDtypeStruct((b, l, d), F32),
        compiler_params=_cparams(("parallel", "parallel")),
    )(xs, mod, sgu, yf_all, yb_all, s5_all, of, ob, g, dskip, wglu, normg, ones_gla, wout)


PEER_TBF = 256
AUX_ROWS = 16
AUX_THETA, AUX_V1_TOP, AUX_V1_LAST, AUX_N_TOP, AUX_RZ = 8, 9, 10, 11, 12


def _sort_network_16():
    def merge(lo, hi, r):
        step = r * 2
        if step < hi - lo:
            yield from merge(lo, hi, step)
            yield from merge(lo + r, hi, step)
            for i in range(lo + r, hi - r, step):
                yield (i, i + r)
        else:
            yield (lo, lo + r)

    def sort(lo, hi):
        if hi - lo >= 1:
            mid = lo + (hi - lo) // 2
            yield from sort(lo, mid)
            yield from sort(mid + 1, hi)
            yield from merge(lo, hi, 1)

    return tuple(sort(0, PEER_TOPK - 1))


SORT16 = _sort_network_16()
BITONIC16 = tuple((k, k + s) for s in (8, 4, 2, 1) for k in range(PEER_TOPK) if not k & s)


def _compare_exchange(xs, pairs):
    xs = list(xs)
    for i, j in pairs:
        hi = jnp.maximum(xs[i], xs[j])
        lo = jnp.minimum(xs[i], xs[j])
        xs[i], xs[j] = hi, lo
    return xs


def _merge_sublanes(xs):
    for shift in (4, 6, 7):
        rolled = [pltpu.roll(x, shift, 0) for x in xs]
        xs = [jnp.maximum(xs[k], rolled[PEER_TOPK - 1 - k]) for k in range(PEER_TOPK)]
        xs = _compare_exchange(xs, BITONIC16)
    return xs


def _dup_bf16_words(x):
    bits = pltpu.bitcast(x.astype(BF16).astype(F32), jnp.int32)
    return bits | lax.shift_right_logical(bits, 16)


def _stage_f_kernel(x_ref, mod_ref, g2_ref, wqt_ref, keys_ref,
                    ht_ref, s1_ref, aux_ref, e2_ref, r2_ref, qt_ref, v1_ref, v2_ref, *, tb):
    x = x_ref[0]
    ms = jnp.mean(x * x, axis=-1, keepdims=True)
    xn = x * lax.rsqrt(ms + EPS) * g2_ref[...]
    h = xn * (1.0 + mod_ref[0, 4:5, :]) + mod_ref[0, 3:4, :]
    ht = h.T.astype(BF16)
    ht_ref[...] = pltpu.bitcast(ht, jnp.int32)
    qt_ref[...] = _dot(wqt_ref[...], ht)
    k_top = PEER_TOPK

    def tiles(s):
        return [s[SUBLANES * k:SUBLANES * (k + 1)] for k in range(PEER_NKEYS // SUBLANES)]

    def head_body(hh, carry):
        for tc in range(tb // LANE):
            tcol = slice(tc * LANE, (tc + 1) * LANE)
            r1 = hh * (2 * PEER_HALF)
            q1 = qt_ref[r1:r1 + PEER_HALF, tcol].astype(BF16)
            q2 = qt_ref[r1 + PEER_HALF:r1 + 2 * PEER_HALF, tcol].astype(BF16)
            s1 = _dot(keys_ref[hh], q1)
            s2 = _dot(keys_ref[PEER_HEADS + hh], q2)
            for s, v_ref in ((s1, v1_ref), (s2, v2_ref)):
                top = _merge_sublanes(_compare_exchange(tiles(s), SORT16))
                for k in range(k_top):
                    v_ref[k:k + 1, tcol] = top[k][0:1]
            v1row = lambda a: v1_ref[a:a + 1, tcol]
            v2row = lambda b: v2_ref[b:b + 1, tcol]
            v1lo = v1_ref[0:SUBLANES, tcol]
            v2lo = v2_ref[0:SUBLANES, tcol]
            cand = [v1lo + v2row(b) for b in range(k_top)]
            tail = [v1row(a) + v2lo for a in range(SUBLANES, k_top)]
            for k in range(SUBLANES, k_top):
                cand[k] = jnp.maximum(cand[k], tail[k_top - 1 - k])
            best = _merge_sublanes(_compare_exchange(cand, BITONIC16))
            theta = best[k_top - 1][0:1]
            cmax = best[0][0:1]
            zsum = jnp.zeros((1, LANE), F32)
            for k in range(k_top):
                zsum = zsum + jnp.exp(best[k][0:1] - cmax)
            rz = 1.0 / zsum
            n_top = jnp.zeros((1, LANE), F32)
            for b in range(k_top):
                n_top = jnp.where(v1row(0) + v2row(b) >= theta, float(b + 1), n_top)
            r2 = jnp.zeros(s2.shape, F32)
            for b in range(k_top):
                r2 = jnp.where(v2row(b) > s2, float(b + 1), r2)
            e2 = jnp.where(s2 >= v2row(k_top - 1), jnp.exp(s2 - v2row(0)), 0.0)
            s1_ref[tc, hh] = s1
            aux_ref[tc, hh, 0:SUBLANES, :] = v2lo
            aux_ref[tc, hh, SUBLANES:, :] = jnp.zeros((AUX_ROWS - SUBLANES, LANE), F32)
            for row, val in ((AUX_THETA, theta), (AUX_V1_TOP, v1row(0)), (AUX_V1_LAST, v1row(k_top - 1)),
                             (AUX_N_TOP, n_top), (AUX_RZ, rz)):
                aux_ref[tc, hh, row:row + 1, :] = val
            e2_ref[tc, hh] = pltpu.bitcast(e2.astype(BF16), jnp.int32)
            r2_ref[tc, hh] = pltpu.bitcast(r2.astype(BF16), jnp.int32)
        return carry

    for hh in range(PEER_HEADS):
        head_body(hh, 0)


def _stage_f(xs, mod, g2, wqt, keys, tb):
    b, l, d = xs.shape
    nblk = l // tb
    ntok = b * l
    nch = ntok // LANE
    kern = functools.partial(_stage_f_kernel, tb=tb)
    row_spec = pl.BlockSpec((tb // LANE, PEER_HEADS, PEER_NKEYS, LANE), lambda bi, i: (bi * nblk + i, 0, 0, 0))
    pair_spec = pl.BlockSpec((tb // LANE, PEER_HEADS, PEER_NKEYS // 2, LANE), lambda bi, i: (bi * nblk + i, 0, 0, 0))
    desc_shape = lambda rows: jax.ShapeDtypeStruct((nch, PEER_HEADS, rows, LANE), jnp.int32)
    return pl.pallas_call(
        kern,
        grid=(b, nblk),
        in_specs=[
            pl.BlockSpec((1, tb, d), lambda bi, i: (bi, i, 0)),
            pl.BlockSpec((1, 8, d), lambda bi, i: (bi, 0, 0)),
            _full_spec(g2.shape), _full_spec(wqt.shape), _full_spec(keys.shape),
        ],
        out_specs=[pl.BlockSpec((d // 2, tb), lambda bi, i: (0, bi * nblk + i)),
                   row_spec,
                   pl.BlockSpec((tb // LANE, PEER_HEADS, AUX_ROWS, LANE), lambda bi, i: (bi * nblk + i, 0, 0, 0)),
                   pair_spec, pair_spec],
        out_shape=[jax.ShapeDtypeStruct((d // 2, ntok), jnp.int32),
                   jax.ShapeDtypeStruct((nch, PEER_HEADS, PEER_NKEYS, LANE), F32),
                   jax.ShapeDtypeStruct((nch, PEER_HEADS, AUX_ROWS, LANE), F32),
                   desc_shape(PEER_NKEYS // 2), desc_shape(PEER_NKEYS // 2)],
        scratch_shapes=[
            pltpu.VMEM((PEER_HEADS * 2 * PEER_HALF, tb), F32),
            pltpu.VMEM((PEER_TOPK, tb), F32),
            pltpu.VMEM((PEER_TOPK, tb), F32),
        ],
        compiler_params=_cparams(("parallel", "parallel")),
    )(xs, mod, g2, wqt, keys)


PEER_TBG = 1024
PEER_TE = 1024
PEER_I1_PER_TILE = PEER_TE // PEER_NKEYS
PEER_N_TILES = PEER_NKEYS * PEER_NKEYS // PEER_TE
PEER_MXU_COLS = 256


def _stage_g_kernel(htw_ref, s1_ref, aux_ref, e2_ref, r2_ref, uw_ref, vtw_ref, x_ref, mod_ref, fg_ref,
                    o_ref, acc_ref, ata_ref, atb_ref, p_ref, e1w_ref, n1w_ref, *, tb, final_norm):
    s = pl.program_id(1)

    @pl.when(s == 0)
    def _():
        acc_ref[...] = jnp.zeros_like(acc_ref)

    def step(at_cur_ref, at_next_ref):
        per_grp = PEER_MXU_COLS // LANE
        for grp in range(tb // PEER_MXU_COLS):
            cols = slice(grp * PEER_MXU_COLS, (grp + 1) * PEER_MXU_COLS)
            for tcl in range(per_grp if at_cur_ref is not None else 0):
                tc = grp * per_grp + tcl
                tcol = slice(tc * LANE, (tc + 1) * LANE)
                for i1l in range(PEER_I1_PER_TILE):
                    rows = slice(i1l * PEER_NKEYS, (i1l + 1) * PEER_NKEYS)
                    gate = jnp.zeros((PEER_NKEYS, LANE), BF16)
                    for hh in range(PEER_HEADS):
                        e1row = e1w_ref[tc, hh, i1l:i1l + 1, :]
                        n1row = n1w_ref[tc, hh, i1l:i1l + 1, :]
                        e1 = pltpu.bitcast(jnp.broadcast_to(e1row, (PEER_NKEYS // 2, LANE)), BF16)
                        n1 = pltpu.bitcast(jnp.broadcast_to(n1row, (PEER_NKEYS // 2, LANE)), BF16)
                        r2 = pltpu.bitcast(r2_ref[tc, hh], BF16)
                        e2 = pltpu.bitcast(e2_ref[tc, hh], BF16)
                        gate = gate + e2 * jnp.where(r2 < n1, e1, 0.0)
                    p_ref[rows, tcol] = gate * _gelu(at_cur_ref[rows, tcol])
            if at_cur_ref is not None:
                vt = pltpu.bitcast(vtw_ref[...], BF16)
                acc_ref[:, cols] += _dot(vt, p_ref[:, cols])
            if at_next_ref is not None:
                u = pltpu.bitcast(uw_ref[...], BF16)
                ht = pltpu.bitcast(htw_ref[:, cols], BF16)
                at_next_ref[:, cols] = _dot(u, ht).astype(BF16)
        if at_next_ref is not None:
            for tc in range(tb // LANE):
                for hh in range(PEER_HEADS):
                    s1 = s1_ref[tc, hh]
                    aux = lambda r: aux_ref[tc, hh, r:r + 1, :]
                    n1 = jnp.zeros(s1.shape, F32)
                    for b in range(SUBLANES):
                        n1 = jnp.where(s1 + aux(b) >= aux(AUX_THETA), float(b + 1), n1)
                    n1 = jnp.where(s1 >= aux(AUX_V1_TOP), aux(AUX_N_TOP), n1)
                    e1 = jnp.where(s1 >= aux(AUX_V1_LAST), jnp.exp(s1 - aux(AUX_V1_TOP)), 0.0) * aux(AUX_RZ)
                    e1w_ref[tc, hh] = _dup_bf16_words(e1)
                    n1w_ref[tc, hh] = _dup_bf16_words(n1)

    last = pl.num_programs(1) - 1

    @pl.when(s == 0)
    def _():
        step(None, ata_ref)

    @pl.when(jnp.logical_and(s % 2 == 0, jnp.logical_and(s > 0, s < last)))
    def _():
        step(atb_ref, ata_ref)

    @pl.when(s % 2 == 1)
    def _():
        step(ata_ref, atb_ref)

    @pl.when(s == last)
    def _():
        step(atb_ref, None)
        xo = x_ref[...] + mod_ref[0, 5:6, :] * acc_ref[...].T
        if final_norm:
            ms = jnp.mean(xo * xo, axis=-1, keepdims=True)
            xo = xo * lax.rsqrt(ms + EPS) * fg_ref[...]
        o_ref[...] = xo


def _stage_g(htw, s1, aux, e2, r2, uw, vtw, xflat, mod, final_g, tokens_per_batch, tb, final_norm):
    ntok = htw.shape[1]
    d = 2 * htw.shape[0]
    assert 2 * uw.shape[0] == PEER_N_TILES * PEER_TE and PEER_N_TILES % 2 == 0
    blocks_per_batch = tokens_per_batch // tb
    kern = functools.partial(_stage_g_kernel, tb=tb, final_norm=final_norm)
    last = PEER_N_TILES - 1
    desc_spec = pl.BlockSpec((tb // LANE, PEER_HEADS, PEER_NKEYS // 2, LANE), lambda j, i: (j, 0, 0, 0))
    row_spec = pl.BlockSpec((tb // LANE, PEER_HEADS, PEER_I1_PER_TILE, LANE),
                            lambda j, i: (j, 0, jnp.minimum(i, last), 0))
    return pl.pallas_call(
        kern,
        grid=(ntok // tb, PEER_N_TILES + 1),
        in_specs=[
            pl.BlockSpec((d // 2, tb), lambda j, i: (0, j)),
            row_spec,
            pl.BlockSpec((tb // LANE, PEER_HEADS, AUX_ROWS, LANE), lambda j, i: (j, 0, 0, 0)),
            desc_spec, desc_spec,
            pl.BlockSpec((PEER_TE // 2, d), lambda j, i: (jnp.minimum(i, last), 0)),
            pl.BlockSpec((d // 2, PEER_TE), lambda j, i: (0, jnp.maximum(i - 1, 0))),
            pl.BlockSpec((tb, d), lambda j, i: (j, 0)),
            pl.BlockSpec((1, 8, d), lambda j, i: (j // blocks_per_batch, 0, 0)),
            _full_spec(final_g.shape),
        ],
        out_specs=pl.BlockSpec((tb, d), lambda j, i: (j, 0)),
        out_shape=jax.ShapeDtypeStruct((ntok, d), F32),
        scratch_shapes=[
            pltpu.VMEM((d, tb), F32),
            pltpu.VMEM((PEER_TE, tb), BF16),
            pltpu.VMEM((PEER_TE, tb), BF16),
            pltpu.VMEM((PEER_TE, tb), BF16),
            pltpu.VMEM((tb // LANE, PEER_HEADS, PEER_I1_PER_TILE, LANE), jnp.int32),
            pltpu.VMEM((tb // LANE, PEER_HEADS, PEER_I1_PER_TILE, LANE), jnp.int32),
        ],
        compiler_params=_cparams(("parallel", "arbitrary")),
    )(htw, s1, aux, e2, r2, uw, vtw, xflat, mod, final_g)


def _pack_kernel(x_ref, o_ref, *, transpose):
    x = x_ref[...]
    if transpose:
        x = x.T
    o_ref[...] = pltpu.bitcast(x.astype(BF16), jnp.int32)


def _pack_row_pairs(x, layer, transpose=False, tile=1024):
    _, r, c = x.shape
    if transpose:
        out_shape, out_spec = (c // 2, r), pl.BlockSpec((c // 2, tile), lambda i: (0, i))
    else:
        out_shape, out_spec = (r // 2, c), pl.BlockSpec((tile // 2, c), lambda i: (i, 0))
    return pl.pallas_call(
        functools.partial(_pack_kernel, transpose=transpose),
        grid=(r // tile,),
        in_specs=[pl.BlockSpec((None, tile, c), lambda i: (layer, i, 0))],
        out_specs=out_spec,
        out_shape=jax.ShapeDtypeStruct(out_shape, jnp.int32),
        compiler_params=_cparams(("parallel",)),
    )(x)


def _block_ones(n, blk):
    idx = np.arange(n) // blk
    return jnp.asarray((idx[:, None] == idx[None, :]).astype(np.float32), dtype=BF16)


def _s5_discretise(lam_re, lam_im, b_re, b_im, log_step):
    lam_re = jnp.minimum(lam_re.astype(F32), -1e-4)
    lam_im = lam_im.astype(F32)
    dt = jnp.exp(log_step.astype(F32))[:, None]
    mag = jnp.exp(lam_re * dt)
    a_re = mag * jnp.cos(lam_im * dt)
    a_im = mag * jnp.sin(lam_im * dt)
    den = lam_re * lam_re + lam_im * lam_im
    f_re = ((a_re - 1.0) * lam_re + a_im * lam_im) / den
    f_im = (a_im * lam_re - (a_re - 1.0) * lam_im) / den
    b_re = b_re.astype(F32)
    b_im = b_im.astype(F32)
    bb_re = f_re[..., None] * b_re - f_im[..., None] * b_im
    bb_im = f_re[..., None] * b_im + f_im[..., None] * b_re
    return a_re, a_im, bb_re, bb_im


def _group_block_diag(t):
    g, r, c = t.shape
    eye = jnp.eye(g, dtype=t.dtype)
    return (t[:, :, None, :] * eye[:, None, :, None]).reshape(g * r, g * c)


def _s5_params(lam_re, lam_im, b_re, b_im, c_re, c_im, log_step, nseq):
    b_rows, c_cols, ars, ais = [], [], [], []
    for d in range(2):
        a_re, a_im, bb_re, bb_im = _s5_discretise(lam_re[d], lam_im[d], b_re[d], b_im[d], log_step[d])
        bm = jnp.concatenate([_group_block_diag(jnp.swapaxes(bb_re, 1, 2)),
                              _group_block_diag(jnp.swapaxes(bb_im, 1, 2))], axis=1)
        b_rows.append(bm)
        cm = jnp.concatenate([_group_block_diag(jnp.swapaxes(c_re[d].astype(F32), 1, 2)),
                              -_group_block_diag(jnp.swapaxes(c_im[d].astype(F32), 1, 2))], axis=0)
        c_cols.append(cm)
        ars.append(jnp.broadcast_to(a_re.reshape(1, S5_LANES), (nseq, S5_LANES)))
        ais.append(jnp.broadcast_to(a_im.reshape(1, S5_LANES), (nseq, S5_LANES)))
    b2 = jnp.concatenate(b_rows, axis=0).astype(BF16)
    c2 = jnp.concatenate(c_cols, axis=1).astype(BF16)
    return b2, jnp.concatenate(ars, axis=0), jnp.concatenate(ais, axis=0), c2


def kernel(x, c, ctx, c_ctx, w_mod, b_mod, norm1_g, norm2_g, w_in, w_out, sgu_w, sgu_b, s5_lambda_re, s5_lambda_im, s5_b_re, s5_b_im, s5_c_re, s5_c_im, s5_log_step, s5_d, s5_w_glu, gla_w_gate, gla_b_gate, gla_norm_g, peer_w_query, peer_sub_keys, peer_expert_u, peer_expert_v, final_norm_g):
    nb, seq, d = x.shape
    c_len = ctx.shape[1]
    depth = w_mod.shape[0]

    cc = jnp.concatenate([c, c_ctx[None, :], jnp.zeros((8 - nb - 1, d), F32)], axis=0)
    mods = _mod_call(cc, w_mod, b_mod)

    ones_sgu = _block_ones(SGU_DIM, SGU_HEAD_DIM)
    ones_gla = _block_ones(GLA_DIM, GLA_DV)
    tri_np = np.tril(np.ones((GLA_CHUNK, GLA_CHUNK), np.float32))
    trif = jnp.asarray(tri_np, dtype=BF16)
    trib = jnp.asarray(tri_np.T, dtype=BF16)
    s_zero = jnp.zeros((nb, 2, GLA_DIM, GLA_KEY_DIM), F32)
    final_g = final_norm_g.reshape(1, d)

    xl, xc = x, ctx
    for l in range(depth):
        ctx_out = l < depth - 1
        m6 = mods[l].reshape(8, N_MOD, d)
        mod_l = jnp.pad(m6[:nb], ((0, 0), (0, 2), (0, 0)))
        mod_c = jnp.broadcast_to(jnp.pad(m6[nb], ((0, 2), (0, 0)))[None], (nb, 8, d))

        win = jnp.pad(w_in[l], ((0, 0), (0, IN_PAD - IN_WIDTH))).astype(BF16)
        sguw = sgu_w[l].astype(BF16)
        sgub = jnp.repeat(jnp.swapaxes(sgu_b[l], 0, 1), SGU_HEAD_DIM, axis=1)
        gw = jnp.zeros((LANE, 2 * GLA_KEY_DIM), F32)
        gw = gw.at[0:GLA_RANK, 0:GLA_KEY_DIM].set(gla_w_gate[l, 0])
        gw = gw.at[GLA_RANK:2 * GLA_RANK, GLA_KEY_DIM:].set(gla_w_gate[l, 1]).astype(BF16)
        gb = gla_b_gate[l].reshape(1, 2 * GLA_KEY_DIM)
        g1 = norm1_g[l].reshape(1, d)

        s5_all = jnp.zeros((nb, seq + c_len, S5_DIM), F32)
        sgu_l, s5_all, qk_l, v_l, g_l, la_l = _stage_a(xl, mod_l, g1, win, sguw, sgub, ones_sgu, gw, gb,
                                                       s5_all, 0, tb=TB_LATENT, grid_layout=True)
        sgu_c, s5_all, qk_c, v_c, g_c, la_c = _stage_a(xc, mod_c, g1, win, sguw, sgub, ones_sgu, gw, gb,
                                                       s5_all, seq, tb=TB_CTX, grid_layout=False)

        b2, ar, ai, c2 = _s5_params(s5_lambda_re[l], s5_lambda_im[l], s5_b_re[l], s5_b_im[l],
                                    s5_c_re[l], s5_c_im[l], s5_log_step[l], nb)
        yf_all, yb_all = _s5_call(s5_all, seq, b2, ar, ai, c2)

        of_c, ob_c, s_ctx = _gla_call(qk_c, v_c, la_c, s_zero, trif, trib)
        of_l, ob_l, _ = _gla_call(qk_l, v_l, la_l, s_ctx, trif, trib)

        dskip = s5_d[l].reshape(1, S5_DIM)
        wglu = s5_w_glu[l].astype(BF16)
        normg = gla_norm_g[l].reshape(1, GLA_DIM)
        wout = w_out[l].astype(BF16)
        g2 = norm2_g[l].reshape(1, d)
        wqt = jnp.swapaxes(peer_w_query[l], 0, 1).astype(BF16)
        keys = peer_sub_keys[l].reshape(2 * PEER_HEADS, PEER_NKEYS, PEER_HALF).astype(BF16)
        u_bf = _pack_row_pairs(peer_expert_u, l)
        vt_bf = _pack_row_pairs(peer_expert_v, l, transpose=True)

        xl = _stage_e(xl, mod_l, sgu_l, yf_all, yb_all, s5_all, 0, of_l, ob_l, g_l,
                      dskip, wglu, normg, ones_gla, wout, tb=TB_LATENT)
        desc = _stage_f(xl, mod_l, g2, wqt, keys, tb=PEER_TBF)
        xl = _stage_g(*desc, u_bf, vt_bf, xl.reshape(nb * seq, d), mod_l, final_g, seq, PEER_TBG,
                      final_norm=not ctx_out).reshape(nb, seq, d)

        if ctx_out:
            xc = _stage_e(xc, mod_c, sgu_c, yf_all, yb_all, s5_all, seq, of_c, ob_c, g_c,
                          dskip, wglu, normg, ones_gla, wout, tb=TB_CTX)
            desc = _stage_f(xc, mod_c, g2, wqt, keys, tb=PEER_TBF)
            xc = _stage_g(*desc, u_bf, vt_bf, xc.reshape(nb * c_len, d), mod_c, final_g, c_len,
                          min(PEER_TBG, c_len), final_norm=False).reshape(nb, c_len, d)

    return xl
```

```python
import functools
import math

import numpy as np
import jax
import jax.numpy as jnp
from jax import lax
from jax.experimental import pallas as pl
from jax.experimental.pallas import tpu as pltpu

F32 = jnp.float32
BF16 = jnp.bfloat16
SUBLANES = 8

EPS = 1e-6
N_MOD = 6
GRID_W = 64

SGU_DIM = 256
SGU_HEADS = 4
SGU_HEAD_DIM = 64
SGU_CHUNK = 128

S5_DIM = 256
S5_GROUP = 16
S5_GROUPS = 16
S5_STATE = 64
S5_LANES = S5_GROUPS * S5_STATE

GLA_DIM = 512
GLA_HEADS = 8
GLA_DV = 64
GLA_DK = 32
GLA_KEY_DIM = 256
GLA_RANK = 16
GLA_GATE_TEMP = 16.0
GLA_CHUNK = 64

PEER_HEADS = 8
PEER_NKEYS = 128
PEER_HALF = 128
PEER_TOPK = 16

IN_WIDTH = 2336
IN_PAD = 2432
LANE = 128

VMEM_LIMIT = 56 * 1024 * 1024
TB_LATENT = 1024
TB_CTX = 256

NEG_INF = float("-inf")
POS_INF = float("inf")


def _cparams(sem):
    return pltpu.CompilerParams(dimension_semantics=sem, vmem_limit_bytes=VMEM_LIMIT)


def _gelu(x):
    c = math.sqrt(2.0 / math.pi)
    return 0.5 * x * (1.0 + jnp.tanh(c * (x + 0.044715 * (x * x * x))))


def _sigmoid(x):
    return 1.0 / (1.0 + jnp.exp(-x))


def _dot(a, b):
    return jnp.dot(a, b, preferred_element_type=F32)


def _dot_nt(a, b):
    return lax.dot_general(a, b, (((1,), (1,)), ((), ())), preferred_element_type=F32)


def _dot_tn(a, b):
    return lax.dot_general(a, b, (((0,), (0,)), ((), ())), preferred_element_type=F32)


def _split3(x):
    hi = x.astype(BF16)
    r = x - hi.astype(F32)
    mid = r.astype(BF16)
    lo = (r - mid.astype(F32)).astype(BF16)
    return hi, mid, lo


def _dot_x_exact(x, m):
    hi, mid, lo = _split3(x)
    return _dot(hi, m) + _dot(mid, m) + _dot(lo, m)


def _dot_m_exact(m, x):
    hi, mid, lo = _split3(x)
    return _dot(m, hi) + _dot(m, mid) + _dot(m, lo)


def _full_spec(shape):
    nd = len(shape)
    return pl.BlockSpec(shape, lambda *_: (0,) * nd)


MOD_TILE = 512


def _mod_kernel(c_ref, w_ref, b_ref, o_ref):
    c = c_ref[...]
    a = c * _sigmoid(c)
    o_ref[0] = jnp.dot(a, w_ref[0], preferred_element_type=F32,
                       precision=lax.Precision.HIGHEST) + b_ref[0]


def _mod_call(cc, w_mod, b_mod):
    depth, d, nd = w_mod.shape
    rows = cc.shape[0]
    return pl.pallas_call(
        _mod_kernel,
        grid=(depth, nd // MOD_TILE),
        in_specs=[
            pl.BlockSpec((rows, d), lambda l, j: (0, 0)),
            pl.BlockSpec((1, d, MOD_TILE), lambda l, j: (l, 0, j)),
            pl.BlockSpec((1, 1, MOD_TILE), lambda l, j: (l, 0, j)),
        ],
        out_specs=pl.BlockSpec((1, rows, MOD_TILE), lambda l, j: (l, 0, j)),
        out_shape=jax.ShapeDtypeStruct((depth, rows, nd), F32),
        compiler_params=_cparams(("parallel", "parallel")),
    )(cc, w_mod, b_mod.reshape(depth, 1, nd))


def _store_tokens(ref, val, grid_layout):
    if grid_layout:
        for r in range(val.shape[0] // GRID_W):
            ref[:, r, :] = val[r * GRID_W:(r + 1) * GRID_W]
    else:
        ref[0] = val


def _load_tokens(ref, grid_layout):
    if grid_layout:
        return jnp.concatenate([ref[:, r, :] for r in range(ref.shape[1])], axis=0)
    return ref[0]


def _stage_a_kernel(x_ref, mod_ref, g1_ref, win_ref, sguw_ref, sgub_ref, ones_ref, gw_ref, gb_ref, s5_in_ref,
                    sgu_ref, s5x_ref, qk_ref, v_ref, g_ref, la_ref, *, tb, grid_layout):
    del s5_in_ref
    x = x_ref[0]
    ms = jnp.mean(x * x, axis=-1, keepdims=True)
    xn = x * lax.rsqrt(ms + EPS) * g1_ref[...]
    h = xn * (1.0 + mod_ref[0, 1:2, :]) + mod_ref[0, 0:1, :]
    cols = _dot(h.astype(BF16), win_ref[...])

    u = _gelu(cols[:, 0:SGU_DIM])
    v = _gelu(cols[:, SGU_DIM:2 * SGU_DIM])
    msq = _dot_x_exact(v * v, ones_ref[...]) * (1.0 / SGU_HEAD_DIM)
    vn = (v * lax.rsqrt(msq + EPS)).astype(BF16)
    head_of_lane = lax.broadcasted_iota(jnp.int32, (SGU_CHUNK, SGU_DIM), 1) // SGU_HEAD_DIM
    for ci in range(tb // SGU_CHUNK):
        rows = slice(ci * SGU_CHUNK, (ci + 1) * SGU_CHUNK)
        vc = vn[rows]
        mixed = sgub_ref[...]
        for hh in range(SGU_HEADS):
            mixed = mixed + jnp.where(head_of_lane == hh, _dot(sguw_ref[hh], vc), 0.0)
        sgu_ref[0, rows, :] = u[rows] * mixed

    s5x_ref[0] = cols[:, 512:768]
    q = cols[:, 768:1024] * (GLA_DK ** -0.5)
    _store_tokens(qk_ref, jnp.concatenate([q, cols[:, 1024:1280]], axis=-1), grid_layout)
    _store_tokens(v_ref, cols[:, 1280:1792], grid_layout)
    g_ref[0] = cols[:, 1792:2304]

    z = cols[:, 2304:2432].astype(BF16)
    za = _dot(z, gw_ref[...]) + gb_ref[...]
    log_sig = jnp.minimum(za, 0.0) - jnp.log1p(jnp.exp(-jnp.abs(za)))
    _store_tokens(la_ref, log_sig * (1.0 / GLA_GATE_TEMP), grid_layout)


def _stage_a(xs, mod, g1, win, sguw, sgub, ones_sgu, gw, gb, s5_all, s5_row0, tb, grid_layout):
    b, l, d = xs.shape
    assert s5_row0 % tb == 0
    kern = functools.partial(_stage_a_kernel, tb=tb, grid_layout=grid_layout)
    tok = lambda w: pl.BlockSpec((1, tb, w), lambda bi, i: (bi, i, 0))
    if grid_layout:
        assert tb % GRID_W == 0 and l % GRID_W == 0
        gla = lambda w: pl.BlockSpec((GRID_W, None, tb // GRID_W, w), lambda bi, i: (0, bi, i, 0))
        gla_shape = lambda w: jax.ShapeDtypeStruct((GRID_W, b, l // GRID_W, w), F32)
    else:
        gla = tok
        gla_shape = lambda w: jax.ShapeDtypeStruct((b, l, w), F32)
    nat_shape = lambda w: jax.ShapeDtypeStruct((b, l, w), F32)
    return pl.pallas_call(
        kern,
        grid=(b, l // tb),
        in_specs=[
            tok(d),
            pl.BlockSpec((1, 8, d), lambda bi, i: (bi, 0, 0)),
            _full_spec(g1.shape), _full_spec(win.shape), _full_spec(sguw.shape), _full_spec(sgub.shape),
            _full_spec(ones_sgu.shape), _full_spec(gw.shape), _full_spec(gb.shape),
            pl.BlockSpec(memory_space=pl.ANY),
        ],
        out_specs=[tok(SGU_DIM),
                   pl.BlockSpec((1, tb, S5_DIM), lambda bi, i: (bi, s5_row0 // tb + i, 0)),
                   gla(2 * GLA_KEY_DIM), gla(GLA_DIM), tok(GLA_DIM), gla(2 * GLA_KEY_DIM)],
        out_shape=[nat_shape(SGU_DIM), jax.ShapeDtypeStruct(s5_all.shape, F32),
                   gla_shape(2 * GLA_KEY_DIM), gla_shape(GLA_DIM), nat_shape(GLA_DIM), gla_shape(2 * GLA_KEY_DIM)],
        input_output_aliases={9: 1},
        compiler_params=_cparams(("parallel", "parallel")),
    )(xs, mod, g1, win, sguw, sgub, ones_sgu, gw, gb, s5_all)


S5_TC = 128


def _s5_kernel(uf_ref, ub_ref, perm_ref, permt_ref, b2_ref, ar_ref, ai_ref, c2_ref, yf_ref, yb_ref,
               h_ref, buf0_ref, buf1_ref, buf2_ref, *, nseq):
    tc = S5_TC
    rows = tc * 2 * nseq
    s = pl.program_id(0)

    @pl.when(s == 0)
    def _():
        h_ref[...] = jnp.zeros_like(h_ref)
        buf0_ref[...] = jnp.zeros_like(buf0_ref)
        buf1_ref[...] = jnp.zeros_like(buf1_ref)
        buf2_ref[...] = jnp.zeros_like(buf2_ref)

    def phase(scan_ref, fill_ref, read_ref):
        ar = ar_ref[...]
        ai = ai_ref[...]
        hr, hi = h_ref[:, 0:S5_LANES], h_ref[:, S5_LANES:]
        for t in range(tc):
            bur = scan_ref[t, :, 0:S5_LANES]
            bui = scan_ref[t, :, S5_LANES:]
            hr, hi = ar * hr - ai * hi + bur, ar * hi + ai * hr + bui
            scan_ref[t, :, 0:S5_LANES] = hr
            scan_ref[t, :, S5_LANES:] = hi
        h_ref[:, 0:S5_LANES] = hr
        h_ref[:, S5_LANES:] = hi

        fwd_row = lax.broadcasted_iota(jnp.int32, (rows, S5_DIM), 0) % (2 * nseq) < nseq
        hs = read_ref[...].reshape(rows, 2 * S5_LANES).astype(BF16)
        y2 = _dot(hs, c2_ref[...])
        y = jnp.where(fwd_row, y2[:, 0:S5_DIM], y2[:, S5_DIM:])
        y_nat = _dot(permt_ref[...], y.astype(BF16))
        for b in range(nseq):
            yf_ref[b] = y_nat[b * tc:(b + 1) * tc]
            yb_ref[b] = y_nat[(nseq + b) * tc:(nseq + b + 1) * tc]

        x = jnp.concatenate([uf_ref[b] for b in range(nseq)] + [ub_ref[b] for b in range(nseq)], axis=0)
        u_tm = _dot(perm_ref[...], x.astype(BF16))
        u = jnp.concatenate([jnp.where(fwd_row, u_tm, 0.0), jnp.where(fwd_row, 0.0, u_tm)], axis=-1).astype(BF16)
        fill_ref[...] = _dot(u, b2_ref[...]).reshape(tc, 2 * nseq, 2 * S5_LANES)

    bufs = (buf0_ref, buf1_ref, buf2_ref)
    for k in range(3):
        @pl.when(s % 3 == k)
        def _(k=k):
            phase(bufs[(k + 2) % 3], bufs[k], bufs[(k + 1) % 3])


def _s5_permutation(nseq):
    tc = S5_TC
    p = np.zeros((tc * 2 * nseq, tc * 2 * nseq), np.float32)
    for q in range(2 * nseq):
        for t in range(tc):
            p[t * 2 * nseq + q, q * tc + (t if q < nseq else tc - 1 - t)] = 1.0
    return p


def _s5_call(s5_all, n_lat, b2, ar, ai, c2):
    nseq, t, _ = s5_all.shape
    n = t // S5_TC
    n_l = n_lat // S5_TC
    n_c = n - n_l
    perm = _s5_permutation(nseq)
    kern = functools.partial(_s5_kernel, nseq=nseq)
    fwd_blk = lambda k: jnp.where(k < n_c, n_l + k, k - n_c)
    bwd_blk = lambda k: n - 1 - k
    in_k = lambda s: jnp.minimum(s, n - 1)
    out_k = lambda s: jnp.clip(s - 2, 0, n - 1)
    blk = (nseq, S5_TC, S5_DIM)
    rows = 2 * nseq
    buf = pltpu.VMEM((S5_TC, rows, 2 * S5_LANES), F32)
    return pl.pallas_call(
        kern,
        grid=(n + 2,),
        in_specs=[
            pl.BlockSpec(blk, lambda s: (0, fwd_blk(in_k(s)), 0)), pl.BlockSpec(blk, lambda s: (0, bwd_blk(in_k(s)), 0)),
            _full_spec(perm.shape), _full_spec(perm.shape),
            _full_spec(b2.shape), _full_spec(ar.shape), _full_spec(ai.shape), _full_spec(c2.shape),
        ],
        out_specs=[pl.BlockSpec(blk, lambda s: (0, fwd_blk(out_k(s)), 0)),
                   pl.BlockSpec(blk, lambda s: (0, bwd_blk(out_k(s)), 0))],
        out_shape=[jax.ShapeDtypeStruct(s5_all.shape, F32), jax.ShapeDtypeStruct(s5_all.shape, F32)],
        scratch_shapes=[pltpu.VMEM((rows, 2 * S5_LANES), F32), buf, buf, buf],
        compiler_params=_cparams(("arbitrary",)),
    )(s5_all, s5_all, jnp.asarray(perm, dtype=BF16), jnp.asarray(perm.T, dtype=BF16), b2, ar, ai, c2)


def _gla_kernel(qkf_ref, vf_ref, laf_ref, qkb_ref, vb_ref, lab_ref, s0_ref, trif_ref, trib_ref,
                of_ref, ob_ref, sout_ref, s_ref, *, nb):
    c = pl.program_id(0)

    @pl.when(c == 0)
    def _():
        s_ref[...] = s0_ref[...]

    ch = GLA_CHUNK
    r_k = lax.broadcasted_iota(jnp.int32, (GLA_HEADS * ch, GLA_KEY_DIM), 0) // ch
    c_k = lax.broadcasted_iota(jnp.int32, (GLA_HEADS * ch, GLA_KEY_DIM), 1) // GLA_DK
    hm_k = r_k == c_k
    r_v = lax.broadcasted_iota(jnp.int32, (GLA_HEADS * ch, GLA_DIM), 0) // ch
    c_v = lax.broadcasted_iota(jnp.int32, (GLA_HEADS * ch, GLA_DIM), 1) // GLA_DV
    hm_v = r_v == c_v
    r_s = lax.broadcasted_iota(jnp.int32, (GLA_DIM, GLA_KEY_DIM), 0) // GLA_DV
    c_s = lax.broadcasted_iota(jnp.int32, (GLA_DIM, GLA_KEY_DIM), 1) // GLA_DK
    hm_s = r_s == c_s
    t_i = lax.broadcasted_iota(jnp.int32, (ch, GLA_HEADS * ch), 0)
    s_i = lax.broadcasted_iota(jnp.int32, (ch, GLA_HEADS * ch), 1) % ch
    mask_f = t_i >= s_i
    mask_b = t_i <= s_i
    trif = trif_ref[...]
    trib = trib_ref[...]

    fwd = dict(qk=qkf_ref, v=vf_ref, la=laf_ref, o=of_ref, tri=trif, last=ch - 1, ref=ch // 2, mask=mask_f, d=0)
    bwd = dict(qk=qkb_ref, v=vb_ref, la=lab_ref, o=ob_ref, tri=trib, last=0, ref=ch - 1 - ch // 2, mask=mask_b, d=1)
    streams = [(b, p) for b in range(nb) for p in (fwd, bwd)]

    bcums = [_dot_m_exact(p["tri"], p["la"][b]) for b, p in streams]
    scaled = []
    for (b, p), bcum in zip(streams, bcums):
        qk = p["qk"][b]
        q, k = qk[:, 0:GLA_KEY_DIM], qk[:, GLA_KEY_DIM:]
        blast = bcum[p["last"]:p["last"] + 1]
        bref = bcum[p["ref"]:p["ref"] + 1]
        qe = (q * jnp.exp(bcum)).astype(BF16)
        qd = (q * jnp.exp(bcum - bref)).astype(BF16)
        kd = k * jnp.exp(bref - bcum)
        kdec = (k * jnp.exp(blast - bcum)).astype(BF16)
        kst = jnp.where(hm_k, jnp.concatenate([kd] * GLA_HEADS, axis=0), 0.0).astype(BF16)
        scaled.append((qe, qd, kdec, kst, jnp.exp(blast)))
    prods = []
    for (b, p), (qe, qd, kdec, kst, _) in zip(streams, scaled):
        v = p["v"][b]
        sc = _dot_nt(qd, kst)
        o_inter = _dot_nt(qe, s_ref[b, p["d"]].astype(BF16))
        kv_t = _dot_tn(v.astype(BF16), kdec)
        prods.append((sc, o_inter, kv_t))
    for (b, p), (_, _, _, _, decay), (sc, o_inter, kv_t) in zip(streams, scaled, prods):
        vbd = jnp.where(hm_v, jnp.concatenate([p["v"][b]] * GLA_HEADS, axis=0), 0.0).astype(BF16)
        p["o"][b] = _dot(jnp.where(p["mask"], sc, 0.0).astype(BF16), vbd) + o_inter
        s_ref[b, p["d"]] = s_ref[b, p["d"]] * decay + jnp.where(hm_s, kv_t, 0.0)

    @pl.when(c == pl.num_programs(0) - 1)
    def _():
        sout_ref[...] = s_ref[...]


def _gla_call(qk, v, la, s0, trif, trib):
    ch = GLA_CHUNK
    if qk.ndim == 4:
        n, b = qk.shape[0], qk.shape[1]
        assert qk.shape[2] == ch
        spec = lambda w, off, rev: pl.BlockSpec(
            (None, b, ch, w), (lambda c: (n - 1 - c, 0, 0, off)) if rev else (lambda c: (c, 0, 0, off)))
    else:
        b = qk.shape[0]
        n = qk.shape[1] // ch
        spec = lambda w, off, rev: pl.BlockSpec(
            (b, ch, w), (lambda c: (0, n - 1 - c, off)) if rev else (lambda c: (0, c, off)))
    kern = functools.partial(_gla_kernel, nb=b)
    o_shape = jax.ShapeDtypeStruct(v.shape, F32)
    return pl.pallas_call(
        kern,
        grid=(n,),
        in_specs=[
            spec(2 * GLA_KEY_DIM, 0, False), spec(GLA_DIM, 0, False), spec(GLA_KEY_DIM, 0, False),
            spec(2 * GLA_KEY_DIM, 0, True), spec(GLA_DIM, 0, True), spec(GLA_KEY_DIM, 1, True),
            _full_spec(s0.shape), _full_spec(trif.shape), _full_spec(trib.shape),
        ],
        out_specs=[spec(GLA_DIM, 0, False), spec(GLA_DIM, 0, True), _full_spec(s0.shape)],
        out_shape=[o_shape, o_shape, jax.ShapeDtypeStruct(s0.shape, F32)],
        scratch_shapes=[pltpu.VMEM(s0.shape, F32)],
        compiler_params=_cparams(("arbitrary",)),
    )(qk, v, la, qk, v, la, s0, trif, trib)


def _stage_e_kernel(x_ref, mod_ref, sgu_ref, yf_ref, yb_ref, s5x_ref, of_ref, ob_ref, g_ref,
                    dskip_ref, wglu_ref, normg_ref, ones_ref, wout_ref, o_ref, *, grid_layout):
    ys = yf_ref[0] + yb_ref[0] + dskip_ref[...] * s5x_ref[0]
    z = _dot(_gelu(ys).astype(BF16), wglu_ref[...])
    s5o = z[:, 0:S5_DIM] * _sigmoid(z[:, S5_DIM:])
    o = _load_tokens(of_ref, grid_layout) + _load_tokens(ob_ref, grid_layout)
    ms = _dot_x_exact(o * o, ones_ref[...]) * (1.0 / GLA_DV)
    g = g_ref[0]
    gl = o * lax.rsqrt(ms + EPS) * normg_ref[...] * (g * _sigmoid(g))
    y = (_dot(sgu_ref[0].astype(BF16), wout_ref[0:SGU_DIM, :])
         + _dot(s5o.astype(BF16), wout_ref[SGU_DIM:SGU_DIM + S5_DIM, :])
         + _dot(gl.astype(BF16), wout_ref[SGU_DIM + S5_DIM:, :]))
    o_ref[0] = x_ref[0] + mod_ref[0, 2:3, :] * y


def _stage_e(xs, mod, sgu, yf_all, yb_all, s5_all, s5_row0, of, ob, g, dskip, wglu, normg, ones_gla, wout, tb):
    b, l, d = xs.shape
    assert s5_row0 % tb == 0
    grid_layout = of.ndim == 4
    tok = lambda w: pl.BlockSpec((1, tb, w), lambda bi, i: (bi, i, 0))
    s5 = pl.BlockSpec((1, tb, S5_DIM), lambda bi, i: (bi, s5_row0 // tb + i, 0))
    if grid_layout:
        assert tb % GRID_W == 0
        gla = pl.BlockSpec((GRID_W, None, tb // GRID_W, GLA_DIM), lambda bi, i: (0, bi, i, 0))
    else:
        gla = tok(GLA_DIM)
    return pl.pallas_call(
        functools.partial(_stage_e_kernel, grid_layout=grid_layout),
        grid=(b, l // tb),
        in_specs=[
            tok(d), pl.BlockSpec((1, 8, d), lambda bi, i: (bi, 0, 0)),
            tok(SGU_DIM), s5, s5, s5, gla, gla, tok(GLA_DIM),
            _full_spec(dskip.shape), _full_spec(wglu.shape), _full_spec(normg.shape),
            _full_spec(ones_gla.shape), _full_spec(wout.shape),
        ],
        out_specs=tok(d),
        out_shape=jax.ShapeDtypeStruct((b, l, d), F32),
        compiler_params=_cparams(("parallel", "parallel")),
    )(xs, mod, sgu, yf_all, yb_all, s5_all, of, ob, g, dskip, wglu, normg, ones_gla, wout)


PEER_TBF = 512
AUX_ROWS = 16
AUX_THETA, AUX_V1_TOP, AUX_V1_LAST, AUX_N_TOP, AUX_RZ = 8, 9, 10, 11, 12


def _sort_network_16():
    def merge(lo, hi, r):
        step = r * 2
        if step < hi - lo:
            yield from merge(lo, hi, step)
            yield from merge(lo + r, hi, step)
            for i in range(lo + r, hi - r, step):
                yield (i, i + r)
        else:
            yield (lo, lo + r)

    def sort(lo, hi):
        if hi - lo >= 1:
            mid = lo + (hi - lo) // 2
            yield from sort(lo, mid)
            yield from sort(mid + 1, hi)
            yield from merge(lo, hi, 1)

    return tuple(sort(0, PEER_TOPK - 1))


SORT16 = _sort_network_16()
BITONIC16 = tuple((k, k + s) for s in (8, 4, 2, 1) for k in range(PEER_TOPK) if not k & s)


def _compare_exchange(xs, pairs):
    xs = list(xs)
    for i, j in pairs:
        hi = jnp.maximum(xs[i], xs[j])
        lo = jnp.minimum(xs[i], xs[j])
        xs[i], xs[j] = hi, lo
    return xs


def _merge_sublanes(xs):
    for shift in (4, 6, 7):
        rolled = [pltpu.roll(x, shift, 0) for x in xs]
        xs = [jnp.maximum(xs[k], rolled[PEER_TOPK - 1 - k]) for k in range(PEER_TOPK)]
        xs = _compare_exchange(xs, BITONIC16)
    return xs


def _dup_bf16_words(x):
    bits = pltpu.bitcast(x.astype(BF16).astype(F32), jnp.int32)
    return bits | lax.shift_right_logical(bits, 16)


def _stage_f_kernel(x_ref, mod_ref, g2_ref, wqt_ref, keys_ref,
                    ht_ref, s1_ref, aux_ref, e2_ref, r2_ref, qt_ref, v1_ref, v2_ref, *, tb):
    x = x_ref[0]
    ms = jnp.mean(x * x, axis=-1, keepdims=True)
    xn = x * lax.rsqrt(ms + EPS) * g2_ref[...]
    h = xn * (1.0 + mod_ref[0, 4:5, :]) + mod_ref[0, 3:4, :]
    ht = h.T.astype(BF16)
    ht_ref[...] = pltpu.bitcast(ht, jnp.int32)
    qt_ref[...] = _dot(wqt_ref[...], ht)
    k_top = PEER_TOPK

    def tiles(s):
        return [s[SUBLANES * k:SUBLANES * (k + 1)] for k in range(PEER_NKEYS // SUBLANES)]

    def head_body(hh, carry):
        for tc in range(tb // LANE):
            tcol = slice(tc * LANE, (tc + 1) * LANE)
            r1 = hh * (2 * PEER_HALF)
            q1 = qt_ref[r1:r1 + PEER_HALF, tcol].astype(BF16)
            q2 = qt_ref[r1 + PEER_HALF:r1 + 2 * PEER_HALF, tcol].astype(BF16)
            s1 = _dot(keys_ref[hh], q1)
            s2 = _dot(keys_ref[PEER_HEADS + hh], q2)
            for s, v_ref in ((s1, v1_ref), (s2, v2_ref)):
                top = _merge_sublanes(_compare_exchange(tiles(s), SORT16))
                for k in range(k_top):
                    v_ref[k:k + 1, tcol] = top[k][0:1]
            v1row = lambda a: v1_ref[a:a + 1, tcol]
            v2row = lambda b: v2_ref[b:b + 1, tcol]
            v1lo = v1_ref[0:SUBLANES, tcol]
            v2lo = v2_ref[0:SUBLANES, tcol]
            cand = [v1lo + v2row(b) for b in range(k_top)]
            tail = [v1row(a) + v2lo for a in range(SUBLANES, k_top)]
            for k in range(SUBLANES, k_top):
                cand[k] = jnp.maximum(cand[k], tail[k_top - 1 - k])
            best = _merge_sublanes(_compare_exchange(cand, BITONIC16))
            theta = best[k_top - 1][0:1]
            cmax = best[0][0:1]
            zsum = jnp.zeros((1, LANE), F32)
            for k in range(k_top):
                zsum = zsum + jnp.exp(best[k][0:1] - cmax)
            rz = 1.0 / zsum
            n_top = jnp.zeros((1, LANE), F32)
            for b in range(k_top):
                n_top = jnp.where(v1row(0) + v2row(b) >= theta, float(b + 1), n_top)
            r2 = jnp.zeros(s2.shape, F32)
            for b in range(k_top):
                r2 = jnp.where(v2row(b) > s2, float(b + 1), r2)
            e2 = jnp.where(s2 >= v2row(k_top - 1), jnp.exp(s2 - v2row(0)), 0.0)
            s1_ref[tc, hh] = s1
            aux_ref[tc, hh, 0:SUBLANES, :] = v2lo
            aux_ref[tc, hh, SUBLANES:, :] = jnp.zeros((AUX_ROWS - SUBLANES, LANE), F32)
            for row, val in ((AUX_THETA, theta), (AUX_V1_TOP, v1row(0)), (AUX_V1_LAST, v1row(k_top - 1)),
                             (AUX_N_TOP, n_top), (AUX_RZ, rz)):
                aux_ref[tc, hh, row:row + 1, :] = val
            e2_ref[tc, hh] = pltpu.bitcast(e2.astype(BF16), jnp.int32)
            r2_ref[tc, hh] = pltpu.bitcast(r2.astype(BF16), jnp.int32)
        return carry

    for hh in range(PEER_HEADS):
        head_body(hh, 0)


def _stage_f(xs, mod, g2, wqt, keys, tb):
    b, l, d = xs.shape
    nblk = l // tb
    ntok = b * l
    nch = ntok // LANE
    kern = functools.partial(_stage_f_kernel, tb=tb)
    row_spec = pl.BlockSpec((tb // LANE, PEER_HEADS, PEER_NKEYS, LANE), lambda bi, i: (bi * nblk + i, 0, 0, 0))
    pair_spec = pl.BlockSpec((tb // LANE, PEER_HEADS, PEER_NKEYS // 2, LANE), lambda bi, i: (bi * nblk + i, 0, 0, 0))
    desc_shape = lambda rows: jax.ShapeDtypeStruct((nch, PEER_HEADS, rows, LANE), jnp.int32)
    return pl.pallas_call(
        kern,
        grid=(b, nblk),
        in_specs=[
            pl.BlockSpec((1, tb, d), lambda bi, i: (bi, i, 0)),
            pl.BlockSpec((1, 8, d), lambda bi, i: (bi, 0, 0)),
            _full_spec(g2.shape), _full_spec(wqt.shape), _full_spec(keys.shape),
        ],
        out_specs=[pl.BlockSpec((d // 2, tb), lambda bi, i: (0, bi * nblk + i)),
                   row_spec,
                   pl.BlockSpec((tb // LANE, PEER_HEADS, AUX_ROWS, LANE), lambda bi, i: (bi * nblk + i, 0, 0, 0)),
                   pair_spec, pair_spec],
        out_shape=[jax.ShapeDtypeStruct((d // 2, ntok), jnp.int32),
                   jax.ShapeDtypeStruct((nch, PEER_HEADS, PEER_NKEYS, LANE), F32),
                   jax.ShapeDtypeStruct((nch, PEER_HEADS, AUX_ROWS, LANE), F32),
                   desc_shape(PEER_NKEYS // 2), desc_shape(PEER_NKEYS // 2)],
        scratch_shapes=[
            pltpu.VMEM((PEER_HEADS * 2 * PEER_HALF, tb), F32),
            pltpu.VMEM((PEER_TOPK, tb), F32),
            pltpu.VMEM((PEER_TOPK, tb), F32),
        ],
        compiler_params=_cparams(("parallel", "parallel")),
    )(xs, mod, g2, wqt, keys)


PEER_TBG = 1024
PEER_TE = 1024
PEER_I1_PER_TILE = PEER_TE // PEER_NKEYS
PEER_N_TILES = PEER_NKEYS * PEER_NKEYS // PEER_TE
PEER_MXU_COLS = 256


def _stage_g_kernel(htw_ref, s1_ref, aux_ref, e2_ref, r2_ref, uw_ref, vtw_ref, x_ref, mod_ref, fg_ref,
                    o_ref, acc_ref, ata_ref, atb_ref, p_ref, e1w_ref, n1w_ref, *, tb, final_norm):
    s = pl.program_id(1)

    @pl.when(s == 0)
    def _():
        acc_ref[...] = jnp.zeros_like(acc_ref)

    def step(at_cur_ref, at_next_ref):
        per_grp = PEER_MXU_COLS // LANE
        for grp in range(tb // PEER_MXU_COLS):
            cols = slice(grp * PEER_MXU_COLS, (grp + 1) * PEER_MXU_COLS)
            for tcl in range(per_grp if at_cur_ref is not None else 0):
                tc = grp * per_grp + tcl
                tcol = slice(tc * LANE, (tc + 1) * LANE)
                for i1l in range(PEER_I1_PER_TILE):
                    rows = slice(i1l * PEER_NKEYS, (i1l + 1) * PEER_NKEYS)
                    gate = jnp.zeros((PEER_NKEYS, LANE), BF16)
                    for hh in range(PEER_HEADS):
                        e1row = e1w_ref[tc, hh, i1l:i1l + 1, :]
                        n1row = n1w_ref[tc, hh, i1l:i1l + 1, :]
                        e1 = pltpu.bitcast(jnp.broadcast_to(e1row, (PEER_NKEYS // 2, LANE)), BF16)
                        n1 = pltpu.bitcast(jnp.broadcast_to(n1row, (PEER_NKEYS // 2, LANE)), BF16)
                        r2 = pltpu.bitcast(r2_ref[tc, hh], BF16)
                        e2 = pltpu.bitcast(e2_ref[tc, hh], BF16)
                        gate = gate + e2 * jnp.where(r2 < n1, e1, 0.0)
                    p_ref[rows, tcol] = gate * _gelu(at_cur_ref[rows, tcol])
            if at_cur_ref is not None:
                vt = pltpu.bitcast(vtw_ref[...], BF16)
                acc_ref[:, cols] += _dot(vt, p_ref[:, cols])
            if at_next_ref is not None:
                u = pltpu.bitcast(uw_ref[...], BF16)
                ht = pltpu.bitcast(htw_ref[:, cols], BF16)
                at_next_ref[:, cols] = _dot(u, ht).astype(BF16)
        if at_next_ref is not None:
            for tc in range(tb // LANE):
                for hh in range(PEER_HEADS):
                    s1 = s1_ref[tc, hh]
                    aux = lambda r: aux_ref[tc, hh, r:r + 1, :]
                    n1 = jnp.zeros(s1.shape, F32)
                    for b in range(SUBLANES):
                        n1 = jnp.where(s1 + aux(b) >= aux(AUX_THETA), float(b + 1), n1)
                    n1 = jnp.where(s1 >= aux(AUX_V1_TOP), aux(AUX_N_TOP), n1)
                    e1 = jnp.where(s1 >= aux(AUX_V1_LAST), jnp.exp(s1 - aux(AUX_V1_TOP)), 0.0) * aux(AUX_RZ)
                    e1w_ref[tc, hh] = _dup_bf16_words(e1)
                    n1w_ref[tc, hh] = _dup_bf16_words(n1)

    last = pl.num_programs(1) - 1

    @pl.when(s == 0)
    def _():
        step(None, ata_ref)

    @pl.when(jnp.logical_and(s % 2 == 0, jnp.logical_and(s > 0, s < last)))
    def _():
        step(atb_ref, ata_ref)

    @pl.when(s % 2 == 1)
    def _():
        step(ata_ref, atb_ref)

    @pl.when(s == last)
    def _():
        step(atb_ref, None)
        xo = x_ref[...] + mod_ref[0, 5:6, :] * acc_ref[...].T
        if final_norm:
            ms = jnp.mean(xo * xo, axis=-1, keepdims=True)
            xo = xo * lax.rsqrt(ms + EPS) * fg_ref[...]
        o_ref[...] = xo


def _stage_g(htw, s1, aux, e2, r2, uw, vtw, xflat, mod, final_g, tokens_per_batch, tb, final_norm):
    ntok = htw.shape[1]
    d = 2 * htw.shape[0]
    assert 2 * uw.shape[0] == PEER_N_TILES * PEER_TE and PEER_N_TILES % 2 == 0
    blocks_per_batch = tokens_per_batch // tb
    kern = functools.partial(_stage_g_kernel, tb=tb, final_norm=final_norm)
    last = PEER_N_TILES - 1
    desc_spec = pl.BlockSpec((tb // LANE, PEER_HEADS, PEER_NKEYS // 2, LANE), lambda j, i: (j, 0, 0, 0))
    row_spec = pl.BlockSpec((tb // LANE, PEER_HEADS, PEER_I1_PER_TILE, LANE),
                            lambda j, i: (j, 0, jnp.minimum(i, last), 0))
    return pl.pallas_call(
        kern,
        grid=(ntok // tb, PEER_N_TILES + 1),
        in_specs=[
            pl.BlockSpec((d // 2, tb), lambda j, i: (0, j)),
            row_spec,
            pl.BlockSpec((tb // LANE, PEER_HEADS, AUX_ROWS, LANE), lambda j, i: (j, 0, 0, 0)),
            desc_spec, desc_spec,
            pl.BlockSpec((PEER_TE // 2, d), lambda j, i: (jnp.minimum(i, last), 0)),
            pl.BlockSpec((d // 2, PEER_TE), lambda j, i: (0, jnp.maximum(i - 1, 0))),
            pl.BlockSpec((tb, d), lambda j, i: (j, 0)),
            pl.BlockSpec((1, 8, d), lambda j, i: (j // blocks_per_batch, 0, 0)),
            _full_spec(final_g.shape),
        ],
        out_specs=pl.BlockSpec((tb, d), lambda j, i: (j, 0)),
        out_shape=jax.ShapeDtypeStruct((ntok, d), F32),
        scratch_shapes=[
            pltpu.VMEM((d, tb), F32),
            pltpu.VMEM((PEER_TE, tb), BF16),
            pltpu.VMEM((PEER_TE, tb), BF16),
            pltpu.VMEM((PEER_TE, tb), BF16),
            pltpu.VMEM((tb // LANE, PEER_HEADS, PEER_I1_PER_TILE, LANE), jnp.int32),
            pltpu.VMEM((tb // LANE, PEER_HEADS, PEER_I1_PER_TILE, LANE), jnp.int32),
        ],
        compiler_params=_cparams(("parallel", "arbitrary")),
    )(htw, s1, aux, e2, r2, uw, vtw, xflat, mod, final_g)


def _pack_kernel(x_ref, o_ref, *, transpose):
    x = x_ref[...]
    if transpose:
        x = x.T
    o_ref[...] = pltpu.bitcast(x.astype(BF16), jnp.int32)


def _pack_row_pairs(x, layer, transpose=False, tile=1024):
    _, r, c = x.shape
    if transpose:
        out_shape, out_spec = (c // 2, r), pl.BlockSpec((c // 2, tile), lambda i: (0, i))
    else:
        out_shape, out_spec = (r // 2, c), pl.BlockSpec((tile // 2, c), lambda i: (i, 0))
    return pl.pallas_call(
        functools.partial(_pack_kernel, transpose=transpose),
        grid=(r // tile,),
        in_specs=[pl.BlockSpec((None, tile, c), lambda i: (layer, i, 0))],
        out_specs=out_spec,
        out_shape=jax.ShapeDtypeStruct(out_shape, jnp.int32),
        compiler_params=_cparams(("parallel",)),
    )(x)


def _block_ones(n, blk):
    idx = np.arange(n) // blk
    return jnp.asarray((idx[:, None] == idx[None, :]).astype(np.float32), dtype=BF16)


def _s5_discretise(lam_re, lam_im, b_re, b_im, log_step):
    lam_re = jnp.minimum(lam_re.astype(F32), -1e-4)
    lam_im = lam_im.astype(F32)
    dt = jnp.exp(log_step.astype(F32))[:, None]
    mag = jnp.exp(lam_re * dt)
    a_re = mag * jnp.cos(lam_im * dt)
    a_im = mag * jnp.sin(lam_im * dt)
    den = lam_re * lam_re + lam_im * lam_im
    f_re = ((a_re - 1.0) * lam_re + a_im * lam_im) / den
    f_im = (a_im * lam_re - (a_re - 1.0) * lam_im) / den
    b_re = b_re.astype(F32)
    b_im = b_im.astype(F32)
    bb_re = f_re[..., None] * b_re - f_im[..., None] * b_im
    bb_im = f_re[..., None] * b_im + f_im[..., None] * b_re
    return a_re, a_im, bb_re, bb_im


def _group_block_diag(t):
    g, r, c = t.shape
    eye = jnp.eye(g, dtype=t.dtype)
    return (t[:, :, None, :] * eye[:, None, :, None]).reshape(g * r, g * c)


def _s5_params(lam_re, lam_im, b_re, b_im, c_re, c_im, log_step, nseq):
    b_rows, c_cols, ars, ais = [], [], [], []
    for d in range(2):
        a_re, a_im, bb_re, bb_im = _s5_discretise(lam_re[d], lam_im[d], b_re[d], b_im[d], log_step[d])
        bm = jnp.concatenate([_group_block_diag(jnp.swapaxes(bb_re, 1, 2)),
                              _group_block_diag(jnp.swapaxes(bb_im, 1, 2))], axis=1)
        b_rows.append(bm)
        cm = jnp.concatenate([_group_block_diag(jnp.swapaxes(c_re[d].astype(F32), 1, 2)),
                              -_group_block_diag(jnp.swapaxes(c_im[d].astype(F32), 1, 2))], axis=0)
        c_cols.append(cm)
        ars.append(jnp.broadcast_to(a_re.reshape(1, S5_LANES), (nseq, S5_LANES)))
        ais.append(jnp.broadcast_to(a_im.reshape(1, S5_LANES), (nseq, S5_LANES)))
    b2 = jnp.concatenate(b_rows, axis=0).astype(BF16)
    c2 = jnp.concatenate(c_cols, axis=1).astype(BF16)
    return b2, jnp.concatenate(ars, axis=0), jnp.concatenate(ais, axis=0), c2


def kernel(x, c, ctx, c_ctx, w_mod, b_mod, norm1_g, norm2_g, w_in, w_out, sgu_w, sgu_b, s5_lambda_re, s5_lambda_im, s5_b_re, s5_b_im, s5_c_re, s5_c_im, s5_log_step, s5_d, s5_w_glu, gla_w_gate, gla_b_gate, gla_norm_g, peer_w_query, peer_sub_keys, peer_expert_u, peer_expert_v, final_norm_g):
    nb, seq, d = x.shape
    c_len = ctx.shape[1]
    depth = w_mod.shape[0]

    cc = jnp.concatenate([c, c_ctx[None, :], jnp.zeros((8 - nb - 1, d), F32)], axis=0)
    mods = _mod_call(cc, w_mod, b_mod)

    ones_sgu = _block_ones(SGU_DIM, SGU_HEAD_DIM)
    ones_gla = _block_ones(GLA_DIM, GLA_DV)
    tri_np = np.tril(np.ones((GLA_CHUNK, GLA_CHUNK), np.float32))
    trif = jnp.asarray(tri_np, dtype=BF16)
    trib = jnp.asarray(tri_np.T, dtype=BF16)
    s_zero = jnp.zeros((nb, 2, GLA_DIM, GLA_KEY_DIM), F32)
    final_g = final_norm_g.reshape(1, d)

    xl, xc = x, ctx
    for l in range(depth):
        ctx_out = l < depth - 1
        m6 = mods[l].reshape(8, N_MOD, d)
        mod_l = jnp.pad(m6[:nb], ((0, 0), (0, 2), (0, 0)))
        mod_c = jnp.broadcast_to(jnp.pad(m6[nb], ((0, 2), (0, 0)))[None], (nb, 8, d))

        win = jnp.pad(w_in[l], ((0, 0), (0, IN_PAD - IN_WIDTH))).astype(BF16)
        sguw = sgu_w[l].astype(BF16)
        sgub = jnp.repeat(jnp.swapaxes(sgu_b[l], 0, 1), SGU_HEAD_DIM, axis=1)
        gw = jnp.zeros((LANE, 2 * GLA_KEY_DIM), F32)
        gw = gw.at[0:GLA_RANK, 0:GLA_KEY_DIM].set(gla_w_gate[l, 0])
        gw = gw.at[GLA_RANK:2 * GLA_RANK, GLA_KEY_DIM:].set(gla_w_gate[l, 1]).astype(BF16)
        gb = gla_b_gate[l].reshape(1, 2 * GLA_KEY_DIM)
        g1 = norm1_g[l].reshape(1, d)

        s5_all = jnp.zeros((nb, seq + c_len, S5_DIM), F32)
        sgu_l, s5_all, qk_l, v_l, g_l, la_l = _stage_a(xl, mod_l, g1, win, sguw, sgub, ones_sgu, gw, gb,
                                                       s5_all, 0, tb=TB_LATENT, grid_layout=True)
        sgu_c, s5_all, qk_c, v_c, g_c, la_c = _stage_a(xc, mod_c, g1, win, sguw, sgub, ones_sgu, gw, gb,
                                                       s5_all, seq, tb=TB_CTX, grid_layout=False)

        b2, ar, ai, c2 = _s5_params(s5_lambda_re[l], s5_lambda_im[l], s5_b_re[l], s5_b_im[l],
                                    s5_c_re[l], s5_c_im[l], s5_log_step[l], nb)
        yf_all, yb_all = _s5_call(s5_all, seq, b2, ar, ai, c2)

        of_c, ob_c, s_ctx = _gla_call(qk_c, v_c, la_c, s_zero, trif, trib)
        of_l, ob_l, _ = _gla_call(qk_l, v_l, la_l, s_ctx, trif, trib)

        dskip = s5_d[l].reshape(1, S5_DIM)
        wglu = s5_w_glu[l].astype(BF16)
        normg = gla_norm_g[l].reshape(1, GLA_DIM)
        wout = w_out[l].astype(BF16)
        g2 = norm2_g[l].reshape(1, d)
        wqt = jnp.swapaxes(peer_w_query[l], 0, 1).astype(BF16)
        keys = peer_sub_keys[l].reshape(2 * PEER_HEADS, PEER_NKEYS, PEER_HALF).astype(BF16)
        u_bf = _pack_row_pairs(peer_expert_u, l)
        vt_bf = _pack_row_pairs(peer_expert_v, l, transpose=True)

        xl = _stage_e(xl, mod_l, sgu_l, yf_all, yb_all, s5_all, 0, of_l, ob_l, g_l,
                      dskip, wglu, normg, ones_gla, wout, tb=TB_LATENT)
        desc = _stage_f(xl, mod_l, g2, wqt, keys, tb=PEER_TBF)
        xl = _stage_g(*desc, u_bf, vt_bf, xl.reshape(nb * seq, d), mod_l, final_g, seq, PEER_TBG,
                      final_norm=not ctx_out).reshape(nb, seq, d)

        if ctx_out:
            xc = _stage_e(xc, mod_c, sgu_c, yf_all, yb_all, s5_all, seq, of_c, ob_c, g_c,
                          dskip, wglu, normg, ones_gla, wout, tb=TB_CTX)
            desc = _stage_f(xc, mod_c, g2, wqt, keys, tb=min(PEER_TBF, c_len))
            xc = _stage_g(*desc, u_bf, vt_bf, xc.reshape(nb * c_len, d), mod_c, final_g, c_len,
                          min(PEER_TBG, c_len), final_norm=False).reshape(nb, c_len, d)

    return xl
```
